```python
import math
import jax, jax.numpy as jnp
from jax import lax
import numpy as np


D_MODEL = 1024
BATCH = 16
SEQ = 256
DEPTH = 2
DEC_BATCH = 4
DEC_SEQ = 2048
PAST_LEN = 256

GRID_W = 64
EPS = 1e-6
RET_HEADS = 4
RET_HEAD_DIM = 128
RET_WIDTH = RET_HEADS * RET_HEAD_DIM
RET_CHUNK = 128
ROPE_BASE = 10000.0
HY_WIDTH = 512
HY_ORDER = 2
HY_SHORT = 3
HY_EMB = 33
HY_FFN = 64
HY_SHORT_DECAY_PCT = 0.3
HY_LONG_DECAY_PCT = 1.5
HY_TARGET = 1e-2
LRU_WIDTH = 512
LRU_BLOCKS = 8
LRU_BLOCK_DIM = LRU_WIDTH // LRU_BLOCKS
LRU_CONV = 4
LRU_C = 8.0
D_MIX = RET_WIDTH + HY_WIDTH + LRU_WIDTH
D_IN = 4 * RET_WIDTH + 4 * HY_WIDTH + 2 * LRU_WIDTH

kernel_name = "hybrid_ret_hyena_rglru_diffusion_step"

F32 = jnp.float32


def rmsnorm(x, g):
    xf = x.astype(F32)
    y = xf * lax.rsqrt(jnp.mean(xf * xf, axis=-1, keepdims=True) + EPS)
    return (y * g.astype(F32)).astype(x.dtype)


def grid_rope(rows):
    row = jnp.repeat(jnp.arange(rows, dtype=F32), GRID_W)
    col = jnp.broadcast_to(jnp.arange(GRID_W, dtype=F32)[None, :], (rows, GRID_W)).reshape(-1)
    n_f = RET_HEAD_DIM // 4
    inv = ROPE_BASE ** (-jnp.arange(n_f, dtype=F32) / n_f)
    ang = jnp.concatenate([row[:, None] * inv[None], col[:, None] * inv[None]], axis=-1)
    return jnp.cos(ang), jnp.sin(ang)


def apply_rope(x, cos, sin):
    half = RET_HEAD_DIM // 2
    xa, xb = x[..., :half], x[..., half:]
    c = cos[None, :, None, :]
    s = sin[None, :, None, :]
    return jnp.concatenate([xa * c - xb * s, xa * s + xb * c], axis=-1)


def depthwise_conv(u, w, b, left):
    K, C = w.shape
    y = lax.conv_general_dilated(u, w[:, None, :].astype(u.dtype), (1,), [(left, K - 1 - left)],
                                 dimension_numbers=('NWC', 'WIO', 'NWC'), feature_group_count=C)
    return y + b.astype(u.dtype)


def retention_dir(q, k, v, log_gamma, state0):
    B, L, H, dk = q.shape
    C = RET_CHUNK
    n = L // C
    qc = q.reshape(B, n, C, H, dk)
    kc = k.reshape(B, n, C, H, dk)
    vc = v.reshape(B, n, C, H, -1)
    idx = jnp.arange(C, dtype=F32)
    diff = idx[:, None] - idx[None, :]
    dmat = jnp.where(diff >= 0, jnp.exp(log_gamma[:, None, None] * jnp.maximum(diff, 0.0)[None]), 0.0)
    scores = jnp.einsum('bnihd,bnjhd->bnhij', qc, kc) * dmat[None, None]
    inner = jnp.einsum('bnhij,bnjhe->bnihe', scores, vc)
    zeta = jnp.exp(log_gamma[:, None] * (C - 1 - idx)[None, :])
    kv = jnp.einsum('bnjhd,hj,bnjhe->bnhde', kc, zeta, vc)
    chunk_decay = jnp.exp(log_gamma * C)[None, :, None, None]

    def step(R, kv_i):
        return chunk_decay * R + kv_i, R

    R_final, R_prev = lax.scan(step, state0, jnp.moveaxis(kv, 1, 0))
    R_prev = jnp.moveaxis(R_prev, 0, 1)
    xi = jnp.exp(log_gamma[:, None] * (idx + 1.0)[None, :])
    cross = jnp.einsum('bnihd,hi,bnhde->bnihe', qc, xi, R_prev)
    return (inner + cross).reshape(B, L, H, -1), R_final


def retention_mixer(q, k, v, decay_logit, state0):
    B, L, H, _ = q.shape
    log_g = jax.nn.log_sigmoid(decay_logit.astype(F32))
    s0 = state0.astype(F32)
    of, sf = retention_dir(q, k, v, log_g[0], s0[:, 0])
    ob, sb = retention_dir(q[:, ::-1], k[:, ::-1], v[:, ::-1], log_g[1], s0[:, 1])
    out = of + ob[:, ::-1]
    out = out * lax.rsqrt(jnp.mean(out * out, axis=-1, keepdims=True) + EPS)
    return out.reshape(B, L, RET_WIDTH), jnp.stack([sf, sb], axis=1)


def hyena_filters(L, w1, b1, w2, b2, w3, freq):
    t = jnp.linspace(0.0, 1.0, L, dtype=F32)[:, None]
    n_bands = (HY_EMB - 1) // 2
    f = jnp.linspace(1e-4, n_bands - 1, n_bands, dtype=F32)
    ang = (2.0 * math.pi / L) * jnp.arange(L, dtype=F32)[:, None] * f[None, :]
    z = jnp.concatenate([t, jnp.cos(ang), -jnp.sin(ang)], axis=-1)
    fr = freq.astype(F32)
    hid = jnp.sin(fr[0] * (z @ w1.astype(F32) + b1.astype(F32)))
    hid = jnp.sin(fr[1] * (hid @ w2.astype(F32) + b2.astype(F32)))
    h = (hid @ w3.astype(F32)).reshape(L, HY_ORDER, 2, HY_WIDTH)
    min_decay = math.log(HY_TARGET) / HY_LONG_DECAY_PCT
    max_decay = math.log(HY_TARGET) / HY_SHORT_DECAY_PCT
    deltas = jnp.abs(jnp.linspace(min_decay, max_decay, HY_WIDTH, dtype=F32))
    h = h * jnp.exp(-t * deltas[None, :])[:, None, None, :]
    return h / (jnp.sum(jnp.abs(h), axis=0, keepdims=True) + EPS)


def long_conv_bidir(u, hf, hb, bias):
    L = u.shape[1]
    n = 2 * L
    U = jnp.fft.rfft(u, n=n, axis=1)
    Hs = jnp.fft.rfft(hf, n=n, axis=0) + jnp.conj(jnp.fft.rfft(hb, n=n, axis=0))
    y = jnp.fft.irfft(U * Hs[None], n=n, axis=1)[:, :L]
    return y + u * bias


def hyena_mixer(u3, conv_w, conv_b, filt, bias):
    u = depthwise_conv(u3, conv_w, conv_b, (HY_SHORT - 1) // 2).astype(F32)
    v, x1, x2 = jnp.split(u, 3, axis=-1)
    bias = bias.astype(F32)
    z = x1 * long_conv_bidir(v, filt[:, 0, 0], filt[:, 0, 1], bias[0])
    z = x2 * long_conv_bidir(z, filt[:, 1, 0], filt[:, 1, 1], bias[1])
    return z


def _lru_combine(left, right):
    a1, b1 = left
    a2, b2 = right
    return a1 * a2, a2 * b1 + b2


def rglru_dir(u, gw, gb, lam, h0):
    B, L, W = u.shape
    ub = u.reshape(B, L, LRU_BLOCKS, LRU_BLOCK_DIM)
    g = jnp.einsum('blnd,knde->kblne', ub, gw).reshape(2, B, L, W) + gb[:, None, None, :]
    r = jax.nn.sigmoid(g[0])
    i = jax.nn.sigmoid(g[1])
    log_a = -LRU_C * r * jax.nn.softplus(-lam)[None, None, :]
    a = jnp.exp(log_a)
    b = jnp.sqrt(-jnp.expm1(2.0 * log_a)) * (i * u)
    a_cum, b_cum = lax.associative_scan(_lru_combine, (a, b), axis=1)
    h = b_cum + a_cum * h0[:, None, :]
    return h, h[:, -1]


def rglru_mixer(x_in, conv_w, conv_b, gw, gb, lam, h0):
    u = depthwise_conv(x_in, conv_w, conv_b, LRU_CONV // 2).astype(F32)
    gw = gw.astype(F32)
    gb = gb.astype(F32)
    lam = lam.astype(F32)
    h0 = h0.astype(F32)
    hf, sf = rglru_dir(u, gw[0], gb[0], lam[0], h0[:, 0])
    hb, sb = rglru_dir(u[:, ::-1], gw[1], gb[1], lam[1], h0[:, 1])
    return hf + hb[:, ::-1], jnp.stack([sf, sb], axis=1)


def mixer_layer(x, mod, ret_s0, lru_s0, rope, filt, norm_g, w_in, ret_logit, hy_conv_w, hy_conv_b,
                hy_bias, lru_conv_w, lru_conv_b, lru_gw, lru_gb, lru_lam, w_out):
    B, L, _ = x.shape
    shift, scale, gate = jnp.split(mod, 3, axis=-1)
    h = rmsnorm(x, norm_g) * (1 + scale) + shift
    z = h @ w_in
    o1 = 4 * RET_WIDTH
    o2 = o1 + 4 * HY_WIDTH
    zr, zh, zl = z[..., :o1], z[..., o1:o2], z[..., o2:]
    q, k, v, g_r = jnp.split(zr.astype(F32), 4, axis=-1)
    q = q.reshape(B, L, RET_HEADS, RET_HEAD_DIM)
    k = k.reshape(B, L, RET_HEADS, RET_HEAD_DIM) * (RET_HEAD_DIM ** -0.5)
    v = v.reshape(B, L, RET_HEADS, RET_HEAD_DIM)
    if rope is not None:
        q = apply_rope(q, rope[0], rope[1])
        k = apply_rope(k, rope[0], rope[1])
    ret_out, ret_s = retention_mixer(q, k, v, ret_logit, ret_s0)
    hy_out = hyena_mixer(zh[..., :3 * HY_WIDTH], hy_conv_w, hy_conv_b, filt, hy_bias)
    g_h = zh[..., 3 * HY_WIDTH:].astype(F32)
    lru_out, lru_s = rglru_mixer(zl[..., :LRU_WIDTH], lru_conv_w, lru_conv_b, lru_gw, lru_gb, lru_lam, lru_s0)
    g_l = zl[..., LRU_WIDTH:].astype(F32)
    y = jnp.concatenate([ret_out * jax.nn.silu(g_r), hy_out * jax.nn.silu(g_h),
                         lru_out * jax.nn.silu(g_l)], axis=-1).astype(x.dtype)
    return x + gate * (y @ w_out), ret_s, lru_s


def setup_inputs(seed: int = 0) -> dict:
    key = jax.random.key(seed)
    ks = jax.random.split(key, 32)
    nrm = lambda i, shape: jax.random.normal(ks[i], shape, dtype=F32)
    base_logit = jnp.log(2.0 ** (5.0 + jnp.arange(RET_HEADS, dtype=F32)) - 1.0)
    ret_decay_logit = base_logit[None, None, :] + 0.1 * nrm(10, (DEPTH, 2, RET_HEADS))
    a_target = jax.random.uniform(ks[11], (DEPTH, 2, LRU_WIDTH), dtype=F32, minval=0.9, maxval=0.999)
    sp = -jnp.log(a_target) / LRU_C
    lru_lambda = -jnp.log(jnp.expm1(sp))
    return {
        "x_prompt": nrm(0, (BATCH, SEQ, D_MODEL)),
        "x_sample": nrm(1, (DEC_BATCH, DEC_SEQ, D_MODEL)),
        "state_ret": 0.5 * nrm(2, (DEC_BATCH, DEPTH, 2, RET_HEADS, RET_HEAD_DIM, RET_HEAD_DIM)),
        "state_lru": 0.5 * nrm(3, (DEC_BATCH, DEPTH, 2, LRU_WIDTH)),
        "c": nrm(4, (DEC_BATCH, D_MODEL)),
        "c_ctx": nrm(5, (D_MODEL,)),
        "norm_g": 1.0 + 0.1 * nrm(6, (DEPTH, D_MODEL)),
        "ada_w": 0.5 * D_MODEL ** -0.5 * nrm(7, (DEPTH, D_MODEL, 3 * D_MODEL)),
        "ada_b": 0.02 * nrm(8, (DEPTH, 3 * D_MODEL)),
        "w_in": D_MODEL ** -0.5 * nrm(9, (DEPTH, D_MODEL, D_IN)),
        "ret_decay_logit": ret_decay_logit,
        "hy_conv_w": HY_SHORT ** -0.5 * nrm(12, (DEPTH, HY_SHORT, 3 * HY_WIDTH)),
        "hy_conv_b": 0.02 * nrm(13, (DEPTH, 3 * HY_WIDTH)),
        "hy_ffn_w1": HY_EMB ** -0.5 * nrm(14, (DEPTH, HY_EMB, HY_FFN)),
        "hy_ffn_b1": 0.02 * nrm(15, (DEPTH, HY_FFN)),
        "hy_ffn_w2": HY_FFN ** -0.5 * nrm(16, (DEPTH, HY_FFN, HY_FFN)),
        "hy_ffn_b2": 0.02 * nrm(17, (DEPTH, HY_FFN)),
        "hy_ffn_w3": HY_FFN ** -0.5 * nrm(18, (DEPTH, HY_FFN, HY_ORDER * 2 * HY_WIDTH)),
        "hy_freq": 1.0 + 0.1 * nrm(19, (DEPTH, 2, HY_FFN)),
        "hy_bias": 0.1 * nrm(20, (DEPTH, HY_ORDER, HY_WIDTH)),
        "lru_conv_w": 0.5 * nrm(21, (DEPTH, LRU_CONV, LRU_WIDTH)),
        "lru_conv_b": 0.02 * nrm(22, (DEPTH, LRU_WIDTH)),
        "lru_gate_w": LRU_BLOCK_DIM ** -0.5 * nrm(23, (DEPTH, 2, 2, LRU_BLOCKS, LRU_BLOCK_DIM, LRU_BLOCK_DIM)),
        "lru_gate_b": 0.02 * nrm(24, (DEPTH, 2, 2, LRU_WIDTH)),
        "lru_lambda": lru_lambda,
        "w_out": D_MIX ** -0.5 * nrm(25, (DEPTH, D_MIX, D_MODEL)),
        "final_g": 1.0 + 0.1 * nrm(26, (D_MODEL,)),
    }


def reference(x_prompt, x_sample, state_ret, state_lru, c, c_ctx, norm_g, ada_w, ada_b, w_in,
              ret_decay_logit, hy_conv_w, hy_conv_b, hy_ffn_w1, hy_ffn_b1, hy_ffn_w2, hy_ffn_b2,
              hy_ffn_w3, hy_freq, hy_bias, lru_conv_w, lru_conv_b, lru_gate_w, lru_gate_b,
              lru_lambda, w_out, final_g):
    Bp, Lp, _ = x_prompt.shape
    Bs, Ls, _ = x_sample.shape
    rows = Ls // GRID_W
    rope = grid_rope(rows)
    zero_ret = jnp.zeros((Bp, 2, RET_HEADS, RET_HEAD_DIM, RET_HEAD_DIM), F32)
    zero_lru = jnp.zeros((Bp, 2, LRU_WIDTH), F32)
    xc, xl = x_prompt, x_sample
    new_ret, new_lru = [], []
    for l in range(DEPTH):
        p = (norm_g[l], w_in[l], ret_decay_logit[l], hy_conv_w[l], hy_conv_b[l], hy_bias[l],
             lru_conv_w[l], lru_conv_b[l], lru_gate_w[l], lru_gate_b[l], lru_lambda[l], w_out[l])
        filt_c = hyena_filters(Lp, hy_ffn_w1[l], hy_ffn_b1[l], hy_ffn_w2[l], hy_ffn_b2[l], hy_ffn_w3[l], hy_freq[l])
        filt_l = hyena_filters(Ls, hy_ffn_w1[l], hy_ffn_b1[l], hy_ffn_w2[l], hy_ffn_b2[l], hy_ffn_w3[l], hy_freq[l])
        mod_c = (jax.nn.silu(c_ctx) @ ada_w[l] + ada_b[l])[None, None, :]
        mod_l = (jax.nn.silu(c) @ ada_w[l] + ada_b[l])[:, None, :]
        xc, rs, ls = mixer_layer(xc, mod_c, zero_ret, zero_lru, None, filt_c, *p)
        new_ret.append(rs)
        new_lru.append(ls)
        xl, _, _ = mixer_layer(xl, mod_l, state_ret[:, l], state_lru[:, l], rope, filt_l, *p)
    y_prompt = rmsnorm(xc, final_g)
    y_sample = rmsnorm(xl, final_g)
    new_state_ret = jnp.stack(new_ret, axis=1).astype(x_prompt.dtype)
    new_state_lru = jnp.stack(new_lru, axis=1).astype(x_prompt.dtype)
    return (y_prompt, y_sample, new_state_ret, new_state_lru)
```

```python
import functools
import math

import numpy as np
import jax
import jax.numpy as jnp
from jax import lax
from jax.experimental import pallas as pl
from jax.experimental.pallas import tpu as pltpu

F32 = jnp.float32
BF16 = jnp.bfloat16
HI = lax.Precision.HIGHEST

D_MODEL = 1024
DEPTH = 2
GRID_W = 64
EPS = 1e-6
RET_HEADS = 4
RET_HEAD_DIM = 128
RET_WIDTH = RET_HEADS * RET_HEAD_DIM
RET_CHUNK = 128
ROPE_BASE = 10000.0
HY_WIDTH = 512
HY_ORDER = 2
HY_SHORT = 3
HY_EMB = 33
HY_FFN = 64
HY_SHORT_DECAY_PCT = 0.3
HY_LONG_DECAY_PCT = 1.5
HY_TARGET = 1e-2
LRU_WIDTH = 512
LRU_BLOCKS = 8
LRU_BLOCK_DIM = LRU_WIDTH // LRU_BLOCKS
LRU_CONV = 4
LRU_C = 8.0
D_MIX = RET_WIDTH + HY_WIDTH + LRU_WIDTH
D_IN = 4 * RET_WIDTH + 4 * HY_WIDTH + 2 * LRU_WIDTH

LANES = 128
SUBLANES = 8
DFT_T = 256
ROW_TILE = 256
MAC_ROWS = 64
VMEM_LIMIT = 56 * 1024 * 1024

_Q0, _K0, _V0, _GR0 = 0, 4, 8, 12
_HV0, _HX10, _HX20, _GH0 = 16, 20, 24, 28
_LX0, _GL0 = 32, 36


def _sigmoid(x):
    return 1.0 / (1.0 + jnp.exp(-x))


def _silu(x):
    return x * _sigmoid(x)


def _softplus(x):
    return jnp.maximum(x, 0.0) + jnp.log1p(jnp.exp(-jnp.abs(x)))


def _params(sem):
    return pltpu.CompilerParams(dimension_semantics=sem, vmem_limit_bytes=VMEM_LIMIT)


def _mod_kernel(c_ref, w_ref, b_ref, o_ref):
    c = c_ref[...]
    o_ref[0] = jnp.dot(_silu(c), w_ref[0], precision=HI, preferred_element_type=F32) + b_ref[0]


def _modulation(cvec, ada_w, ada_b):
    tn = 768
    return pl.pallas_call(
        _mod_kernel,
        grid=(DEPTH, 3 * D_MODEL // tn),
        in_specs=[pl.BlockSpec((SUBLANES, D_MODEL), lambda l, j: (0, 0)),
                  pl.BlockSpec((1, D_MODEL, tn), lambda l, j: (l, 0, j)),
                  pl.BlockSpec((1, 1, tn), lambda l, j: (l, 0, j))],
        out_specs=pl.BlockSpec((1, SUBLANES, tn), lambda l, j: (l, 0, j)),
        out_shape=jax.ShapeDtypeStruct((DEPTH, SUBLANES, 3 * D_MODEL), F32),
        compiler_params=_params(("arbitrary", "arbitrary")),
        name="modulation",
    )(cvec, ada_w, ada_b.reshape(DEPTH, 1, 3 * D_MODEL))


def _in_kernel(x_ref, sh_ref, sc_ref, g_ref, w_ref, o_ref):
    x = x_ref[0]
    ms = jnp.mean(x * x, axis=-1, keepdims=True)
    y = x * lax.rsqrt(ms + EPS) * g_ref[...]
    h = (y * (1.0 + sc_ref[0]) + sh_ref[0]).astype(BF16)
    tn = 1024
    for n in range(D_IN // tn):
        o_ref[0, :, n * tn:(n + 1) * tn] = jnp.dot(h, w_ref[:, n * tn:(n + 1) * tn],
                                                   preferred_element_type=F32)


def _in_proj(x, shift, scale, g, w_bf16):
    B, L, _ = x.shape
    per_batch = shift.shape[0] > 1
    mod_map = (lambda b, i: (b, 0, 0)) if per_batch else (lambda b, i: (0, 0, 0))
    return pl.pallas_call(
        _in_kernel,
        grid=(B, L // ROW_TILE),
        in_specs=[pl.BlockSpec((1, ROW_TILE, D_MODEL), lambda b, i: (b, i, 0)),
                  pl.BlockSpec((1, 1, D_MODEL), mod_map),
                  pl.BlockSpec((1, 1, D_MODEL), mod_map),
                  pl.BlockSpec((1, D_MODEL), lambda b, i: (0, 0)),
                  pl.BlockSpec((D_MODEL, D_IN), lambda b, i: (0, 0), pipeline_mode=pl.Buffered(1))],
        out_specs=pl.BlockSpec((1, ROW_TILE, D_IN), lambda b, i: (b, i, 0)),
        out_shape=jax.ShapeDtypeStruct((B, L, D_IN), F32),
        compiler_params=_params(("arbitrary", "arbitrary")),
        name="in_proj",
    )(x, shift, scale, g, w_bf16)


def _out_kernel(yr_ref, yh_ref, yl_ref, x_ref, gate_ref, w_ref, fg_ref, o_ref, *, final):
    acc = jnp.dot(yr_ref[0], w_ref[0:RET_WIDTH], preferred_element_type=F32)
    acc = acc + jnp.dot(yh_ref[0], w_ref[RET_WIDTH:RET_WIDTH + HY_WIDTH], preferred_element_type=F32)
    acc = acc + jnp.dot(yl_ref[0], w_ref[RET_WIDTH + HY_WIDTH:D_MIX], preferred_element_type=F32)
    x = x_ref[0] + gate_ref[0] * acc
    if final:
        ms = jnp.mean(x * x, axis=-1, keepdims=True)
        x = x * lax.rsqrt(ms + EPS) * fg_ref[...]
    o_ref[0] = x


def _out_proj(y_ret, y_hy, y_lru, x, gate, w_bf16, final_g, final):
    B, L, _ = x.shape
    per_batch = gate.shape[0] > 1
    mod_map = (lambda b, i: (b, 0, 0)) if per_batch else (lambda b, i: (0, 0, 0))
    yspec = pl.BlockSpec((1, ROW_TILE, 512), lambda b, i: (b, i, 0))
    return pl.pallas_call(
        functools.partial(_out_kernel, final=final),
        grid=(B, L // ROW_TILE),
        in_specs=[yspec, yspec, yspec,
                  pl.BlockSpec((1, ROW_TILE, D_MODEL), lambda b, i: (b, i, 0)),
                  pl.BlockSpec((1, 1, D_MODEL), mod_map),
                  pl.BlockSpec((D_MIX, D_MODEL), lambda b, i: (0, 0), pipeline_mode=pl.Buffered(1)),
                  pl.BlockSpec((1, D_MODEL), lambda b, i: (0, 0))],
        out_specs=pl.BlockSpec((1, ROW_TILE, D_MODEL), lambda b, i: (b, i, 0)),
        out_shape=jax.ShapeDtypeStruct((B, L, D_MODEL), F32),
        compiler_params=_params(("arbitrary", "arbitrary")),
        name="out_proj",
    )(y_ret, y_hy, y_lru, x, gate, w_bf16, final_g)


def _ret_kernel(*refs, L, use_rope, has_state):
    it = iter(refs)
    q_ref, k_ref, v_ref, g_ref = next(it), next(it), next(it), next(it)
    cos_ref = sin_ref = s0_ref = None
    if use_rope:
        cos_ref, sin_ref = next(it), next(it)
    lg_ref = next(it)
    if has_state:
        s0_ref = next(it)
    y_ref, sN_ref, acc_ref, R_ref = next(it), next(it), next(it), next(it)

    C = RET_CHUNK
    n = L // C
    ii = lax.broadcasted_iota(jnp.int32, (C, C), 0).astype(F32)
    jj = lax.broadcasted_iota(jnp.int32, (C, C), 1).astype(F32)
    kscale = RET_HEAD_DIM ** -0.5

    for d in (0, 1):
        logit = jnp.broadcast_to(lg_ref[d, 0], (C, C))
        lg = -_softplus(-logit)
        diff = (ii - jj) if d == 0 else (jj - ii)
        dmat = jnp.where(diff >= 0, jnp.exp(lg * jnp.maximum(diff, 0.0)), 0.0)
        if d == 0:
            zeta = jnp.exp(lg * (C - 1.0 - ii))
            xi = jnp.exp(lg * (ii + 1.0))
        else:
            zeta = jnp.exp(lg * ii)
            xi = jnp.exp(lg * (C - ii))
        cdecay = jnp.exp(lg * C)
        if has_state:
            R_ref[...] = s0_ref[0, d, 0]
        else:
            R_ref[...] = jnp.zeros((C, C), F32)

        def body(t, carry, d=d, dmat=dmat, zeta=zeta, xi=xi, cdecay=cdecay):
            c = t if d == 0 else n - 1 - t
            r0 = pl.multiple_of(c * C, C)
            q = q_ref[0, pl.ds(r0, C), :]
            k = k_ref[0, pl.ds(r0, C), :] * kscale
            v = v_ref[0, pl.ds(r0, C), :]
            if use_rope:
                cs = cos_ref[pl.ds(r0, C), :]
                sn = sin_ref[pl.ds(r0, C), :]
                q = q * cs + pltpu.roll(q, 64, axis=1) * sn
                k = k * cs + pltpu.roll(k, 64, axis=1) * sn
            s = lax.dot_general(q, k, (((1,), (1,)), ((), ())), precision=HI,
                                preferred_element_type=F32) * dmat
            R = R_ref[...]
            o = jnp.dot(s, v, precision=HI, preferred_element_type=F32)
            o = o + jnp.dot(q * xi, R, precision=HI, preferred_element_type=F32)
            kv = jnp.dot((k * zeta).T, v, precision=HI, preferred_element_type=F32)
            R_ref[...] = cdecay * R + kv
            if d == 0:
                acc_ref[pl.ds(r0, C), :] = o
            else:
                tot = acc_ref[pl.ds(r0, C), :] + o
                nrm = tot * lax.rsqrt(jnp.mean(tot * tot, axis=-1, keepdims=True) + EPS)
                y_ref[0, pl.ds(r0, C), :] = (nrm * _silu(g_ref[0, pl.ds(r0, C), :])).astype(BF16)
            return carry

        lax.fori_loop(0, n, body, 0)
        sN_ref[0, d, 0] = R_ref[...]


def _retention(z, decay_logit, rope, state0):
    B, L, _ = z.shape
    use_rope = rope is not None
    has_state = state0 is not None
    H = RET_HEADS
    col = lambda off: pl.BlockSpec((1, L, LANES), lambda b, h, off=off: (b, 0, off + h))
    in_specs = [col(_Q0), col(_K0), col(_V0), col(_GR0)]
    args = [z, z, z, z]
    if use_rope:
        in_specs += [pl.BlockSpec((L, LANES), lambda b, h: (0, 0))] * 2
        args += list(rope)
    in_specs.append(pl.BlockSpec((2, 1, 1, LANES), lambda b, h: (0, h, 0, 0)))
    args.append(jnp.broadcast_to(decay_logit[:, :, None, None], (2, H, 1, LANES)))
    if has_state:
        in_specs.append(pl.BlockSpec((1, 2, 1, RET_HEAD_DIM, RET_HEAD_DIM), lambda b, h: (b, 0, h, 0, 0)))
        args.append(state0)
    return pl.pallas_call(
        functools.partial(_ret_kernel, L=L, use_rope=use_rope, has_state=has_state),
        grid=(B, H),
        in_specs=in_specs,
        out_specs=[pl.BlockSpec((1, L, LANES), lambda b, h: (b, 0, h)),
                   pl.BlockSpec((1, 2, 1, RET_HEAD_DIM, RET_HEAD_DIM), lambda b, h: (b, 0, h, 0, 0))],
        out_shape=[jax.ShapeDtypeStruct((B, L, RET_WIDTH), BF16),
                   jax.ShapeDtypeStruct((B, 2, H, RET_HEAD_DIM, RET_HEAD_DIM), F32)],
        scratch_shapes=[pltpu.VMEM((L, LANES), F32), pltpu.VMEM((RET_CHUNK, RET_CHUNK), F32)],
        compiler_params=_params(("arbitrary", "arbitrary")),
        name="retention",
    )(*args)


def _dft_matrices():
    T = DFT_T
    N = 2 * T
    k = np.arange(T, dtype=np.float64)[:, None]
    m = np.arange(T, dtype=np.float64)[None, :]
    ang = 2.0 * np.pi * k * m / N
    fwd = np.concatenate([np.cos(ang), -np.sin(ang)], axis=0)
    fwd[T, :] = (-1.0) ** np.arange(T)
    ck = np.full((T,), 2.0)
    ck[0] = 1.0
    inv_re = (np.cos(ang) * ck[:, None]).T / N
    inv_im = (-2.0 * np.sin(ang)).T / N
    inv_im[:, 0] = ((-1.0) ** np.arange(T)) / N
    inv = np.concatenate([inv_re, inv_im], axis=1)
    return jnp.asarray(fwd, F32), jnp.asarray(inv, F32)


def _filt_kernel(zemb_ref, w1_ref, b1_ref, w2_ref, b2_ref, fr_ref, w3f_ref, w3b_ref, dec_ref,
                 bias_ref, fwd_ref, kre_ref, kim_ref, hid_ref, hn_ref, F_ref, *, L):
    T = DFT_T
    nb = L // T

    @pl.when((pl.program_id(0) == 0) & (pl.program_id(1) == 0))
    def _():
        z1 = jnp.dot(zemb_ref[...], w1_ref[...], precision=HI, preferred_element_type=F32) + b1_ref[...]
        h1 = jnp.sin(fr_ref[0:1, :] * z1)
        z2 = jnp.dot(h1, w2_ref[...], precision=HI, preferred_element_type=F32) + b2_ref[...]
        hid_ref[...] = jnp.sin(fr_ref[1:2, :] * z2)

    hid = hid_ref[...]
    for dr, w3_ref in enumerate((w3f_ref, w3b_ref)):
        h = jnp.dot(hid, w3_ref[...], precision=HI, preferred_element_type=F32) * dec_ref[...]
        h = h / (jnp.sum(jnp.abs(h), axis=0, keepdims=True) + EPS)
        hn_ref[dr] = h
        for blk in range(nb):
            F_ref[dr, blk] = jnp.dot(fwd_ref[...], h[blk * T:(blk + 1) * T], precision=HI,
                                     preferred_element_type=F32)

    row = lax.broadcasted_iota(jnp.int32, (T, LANES), 0)
    sgn = jnp.where((row & 1) == 0, 1.0, -1.0).astype(F32)
    row0 = row == 0
    bias = bias_ref[0]
    for d in range(-(nb - 1), nb):
        re = jnp.zeros((T, LANES), F32)
        im = jnp.zeros((T, LANES), F32)
        nyq = jnp.zeros((1, LANES), F32)
        if d >= 0:
            F = F_ref[0, d]
            re, im, nyq = re + F[:T], im + F[T:], nyq + F[T:T + 1]
        if d >= 1:
            F = F_ref[0, d - 1]
            p0 = hn_ref[0, (d - 1) * T:(d - 1) * T + 1, :]
            re, im, nyq = re + sgn * (F[:T] - p0), im + sgn * F[T:], nyq + (F[T:T + 1] - p0)
        e = -d
        if e >= 0:
            F = F_ref[1, e]
            re, im, nyq = re + F[:T], im - F[T:], nyq + F[T:T + 1]
        if e >= 1:
            F = F_ref[1, e - 1]
            p0 = hn_ref[1, (e - 1) * T:(e - 1) * T + 1, :]
            re, im, nyq = re + sgn * (F[:T] - p0), im - sgn * F[T:], nyq + (F[T:T + 1] - p0)
        if d == 0:
            re, nyq = re + bias, nyq + bias
        kre_ref[0, d + nb - 1] = re
        kim_ref[0, d + nb - 1] = jnp.where(row0, nyq, im)


def _hyena_filter_spectra(L, w1, b1, w2, b2, w3, freq, bias, fwd):
    T = DFT_T
    nb = L // T
    nlag = 2 * nb - 1
    P = LANES
    t = jnp.linspace(0.0, 1.0, L, dtype=F32)[:, None]
    n_bands = (HY_EMB - 1) // 2
    f = jnp.linspace(1e-4, n_bands - 1, n_bands, dtype=F32)
    ang = (2.0 * math.pi / L) * jnp.arange(L, dtype=F32)[:, None] * f[None, :]
    zemb = jnp.concatenate([t, jnp.cos(ang), -jnp.sin(ang)], axis=-1)
    zemb = jnp.pad(zemb, ((0, 0), (0, P - HY_EMB)))
    min_decay = math.log(HY_TARGET) / HY_LONG_DECAY_PCT
    max_decay = math.log(HY_TARGET) / HY_SHORT_DECAY_PCT
    deltas = jnp.abs(jnp.linspace(min_decay, max_decay, HY_WIDTH, dtype=F32))
    dec = jnp.exp(-t * deltas[None, :])
    pad = P - HY_FFN
    w1p = jnp.pad(w1, ((0, P - HY_EMB), (0, pad)))
    b1p = jnp.pad(b1, (0, pad)).reshape(1, P)
    w2p = jnp.pad(w2, ((0, pad), (0, pad)))
    b2p = jnp.pad(b2, (0, pad)).reshape(1, P)
    w3p = jnp.pad(w3, ((0, pad), (0, 0)))
    frp = jnp.pad(freq, ((0, 0), (0, pad)))
    ncg = HY_WIDTH // P
    const = lambda shape: pl.BlockSpec(shape, lambda o, c: tuple(0 for _ in shape))
    kshape = jax.ShapeDtypeStruct((HY_ORDER, nlag, T, HY_WIDTH), F32)
    kspec = pl.BlockSpec((1, nlag, T, P), lambda o, c: (o, 0, 0, c))
    return pl.pallas_call(
        functools.partial(_filt_kernel, L=L),
        grid=(HY_ORDER, ncg),
        in_specs=[const((L, P)), const((P, P)), const((1, P)), const((P, P)), const((1, P)), const((2, P)),
                  pl.BlockSpec((P, P), lambda o, c: (0, o * 2 * ncg + c)),
                  pl.BlockSpec((P, P), lambda o, c: (0, o * 2 * ncg + ncg + c)),
                  pl.BlockSpec((L, P), lambda o, c: (0, c)),
                  pl.BlockSpec((1, 1, P), lambda o, c: (o, 0, c)),
                  const((2 * T, T))],
        out_specs=[kspec, kspec],
        out_shape=[kshape, kshape],
        scratch_shapes=[pltpu.VMEM((L, P), F32), pltpu.VMEM((2, L, P), F32),
                        pltpu.VMEM((2, nb, 2 * T, P), F32)],
        compiler_params=_params(("arbitrary", "arbitrary")),
        name="hyena_filters",
    )(zemb, w1p, b1p, w2p, b2p, frp, w3p, w3p, dec, bias.reshape(HY_ORDER, 1, HY_WIDTH), fwd)


def _conv_block(x_ref, j, nb, w, b, left):
    T = DFT_T
    t0 = j * T
    C = x_ref.shape[-1]
    blk = x_ref[0, t0:t0 + T, :]
    prev = x_ref[0, t0 - SUBLANES:t0, :] if j > 0 else jnp.zeros((SUBLANES, C), F32)
    nxt = x_ref[0, t0 + T:t0 + T + SUBLANES, :] if j < nb - 1 else jnp.zeros((SUBLANES, C), F32)
    ext = jnp.concatenate([prev, blk, nxt], axis=0)
    n = T + 2 * SUBLANES
    acc = jnp.broadcast_to(b, (T, C))
    for tap in range(w.shape[0]):
        s = tap - left
        sh = blk if s == 0 else pltpu.roll(ext, (-s) % n, axis=0)[SUBLANES:SUBLANES + T]
        acc = acc + w[tap:tap + 1, :] * sh
    return acc


def _hy_kernel(v_ref, x1_ref, x2_ref, g_ref, wv_ref, w1_ref, w2_ref, bv_ref, b1_ref, b2_ref,
               kre_ref, kim_ref, fwd_ref, inv_ref, y_ref, x1c_ref, x2c_ref, z1_ref, U_ref, Y_ref, *, L):
    T = DFT_T
    nb = L // T
    left = (HY_SHORT - 1) // 2
    for j in range(nb):
        u = _conv_block(v_ref, j, nb, wv_ref[...], bv_ref[...], left)
        U_ref[j] = jnp.dot(fwd_ref[...], u, precision=HI, preferred_element_type=F32)
        x1c_ref[j * T:(j + 1) * T, :] = _conv_block(x1_ref, j, nb, w1_ref[...], b1_ref[...], left)
        x2c_ref[j * T:(j + 1) * T, :] = _conv_block(x2_ref, j, nb, w2_ref[...], b2_ref[...], left)

    row0 = lax.broadcasted_iota(jnp.int32, (SUBLANES, LANES), 0) == 0

    def long_conv_block(o, i):
        for rc in range(T // MAC_ROWS):
            r = rc * MAC_ROWS
            are = jnp.zeros((MAC_ROWS, LANES), F32)
            aim = jnp.zeros((MAC_ROWS, LANES), F32)
            for j in range(nb):
                lag = i - j + (nb - 1)
                kr = kre_ref[o, lag, r:r + MAC_ROWS, :]
                ki = kim_ref[o, lag, r:r + MAC_ROWS, :]
                ur = U_ref[j, r:r + MAC_ROWS, :]
                ui = U_ref[j, T + r:T + r + MAC_ROWS, :]
                are = are + (kr * ur - ki * ui)
                aim = aim + (kr * ui + ki * ur)
            Y_ref[r:r + MAC_ROWS, :] = are
            Y_ref[T + r:T + r + MAC_ROWS, :] = aim
        fre = jnp.zeros((SUBLANES, LANES), F32)
        fim = jnp.zeros((SUBLANES, LANES), F32)
        for j in range(nb):
            lag = i - j + (nb - 1)
            fre = fre + kre_ref[o, lag, 0:SUBLANES, :] * U_ref[j, 0:SUBLANES, :]
            fim = fim + kim_ref[o, lag, 0:SUBLANES, :] * U_ref[j, T:T + SUBLANES, :]
        Y_ref[0:SUBLANES, :] = jnp.where(row0, fre, Y_ref[0:SUBLANES, :])
        Y_ref[T:T + SUBLANES, :] = jnp.where(row0, fim, Y_ref[T:T + SUBLANES, :])
        return jnp.dot(inv_ref[...], Y_ref[...], precision=HI, preferred_element_type=F32)

    def order0(i, carry):
        r0 = pl.multiple_of(i * T, T)
        z1_ref[pl.ds(r0, T), :] = x1c_ref[pl.ds(r0, T), :] * long_conv_block(0, i)
        return carry

    lax.fori_loop(0, nb, order0, 0)
    for j in range(nb):
        U_ref[j] = jnp.dot(fwd_ref[...], z1_ref[j * T:(j + 1) * T, :], precision=HI,
                           preferred_element_type=F32)

    def order1(i, carry):
        r0 = pl.multiple_of(i * T, T)
        out = x2c_ref[pl.ds(r0, T), :] * long_conv_block(1, i)
        y_ref[0, pl.ds(r0, T), :] = (out * _silu(g_ref[0, pl.ds(r0, T), :])).astype(BF16)
        return carry

    lax.fori_loop(0, nb, order1, 0)


def _hyena(z, conv_w, conv_b, kre, kim, fwd, inv):
    B, L, _ = z.shape
    T = DFT_T
    nb = L // T
    nlag = 2 * nb - 1
    P = LANES
    ncg = HY_WIDTH // P
    col = lambda off: pl.BlockSpec((1, L, P), lambda c, b, off=off: (b, 0, off + c))
    wspec = lambda part: pl.BlockSpec((HY_SHORT, P), lambda c, b, part=part: (0, part * ncg + c))
    bspec = lambda part: pl.BlockSpec((1, P), lambda c, b, part=part: (0, part * ncg + c))
    kspec = pl.BlockSpec((HY_ORDER, nlag, T, P), lambda c, b: (0, 0, 0, c))
    cb = conv_b.reshape(1, 3 * HY_WIDTH)
    return pl.pallas_call(
        functools.partial(_hy_kernel, L=L),
        grid=(ncg, B),
        in_specs=[col(_HV0), col(_HX10), col(_HX20), col(_GH0),
                  wspec(0), wspec(1), wspec(2), bspec(0), bspec(1), bspec(2),
                  kspec, kspec,
                  pl.BlockSpec((2 * T, T), lambda c, b: (0, 0)),
                  pl.BlockSpec((T, 2 * T), lambda c, b: (0, 0))],
        out_specs=pl.BlockSpec((1, L, P), lambda c, b: (b, 0, c)),
        out_shape=jax.ShapeDtypeStruct((B, L, HY_WIDTH), BF16),
        scratch_shapes=[pltpu.VMEM((L, P), F32), pltpu.VMEM((L, P), F32), pltpu.VMEM((L, P), F32),
                        pltpu.VMEM((nb, 2 * T, P), F32), pltpu.VMEM((2 * T, P), F32)],
        compiler_params=_params(("arbitrary", "arbitrary")),
        name="hyena",
    )(z, z, z, z, conv_w, conv_w, conv_w, cb, cb, cb, kre, kim, fwd, inv)


LRU_TILE = 256
LRU_GROUPS = LRU_TILE // LANES


def _lru_kernel(*refs, L, has_state):
    it = iter(refs)
    x_ref, g_ref, cw_ref, cb_ref, wg_ref, gb_ref, lam_ref = (next(it) for _ in range(7))
    h0_ref = next(it) if has_state else None
    y_ref, sN_ref, A_ref, B_ref = next(it), next(it), next(it), next(it)

    T = DFT_T
    nb = L // T
    S = L // SUBLANES
    W = LRU_TILE
    G = LRU_GROUPS

    sp = [_softplus(-lam_ref[d]) for d in (0, 1)]
    for j in range(nb):
        u = _conv_block(x_ref, j, nb, cw_ref[...], cb_ref[...], LRU_CONV // 2)
        gates = jnp.dot(u, wg_ref[0], precision=HI, preferred_element_type=F32) + gb_ref[0]
        for d in (0, 1):
            r = _sigmoid(gates[:, (2 * d) * W:(2 * d + 1) * W])
            i = _sigmoid(gates[:, (2 * d + 1) * W:(2 * d + 2) * W])
            log_a = -LRU_C * r * sp[d]
            a = jnp.exp(log_a)
            b = jnp.sqrt(-jnp.tanh(log_a) * (a * a + 1.0)) * (i * u)
            for gi in range(G):
                A_ref[d, gi, j * T:(j + 1) * T, :] = a[:, gi * LANES:(gi + 1) * LANES]
                B_ref[d, gi, j * T:(j + 1) * T, :] = b[:, gi * LANES:(gi + 1) * LANES]

    def scan_body(t, carry):
        out = []
        for d in (0, 1):
            i = t if d == 0 else S - 1 - t
            for gi in range(G):
                h, acc = carry[d * G + gi]
                a = A_ref[d, gi, pl.ds(i, SUBLANES, stride=S), :]
                b = B_ref[d, gi, pl.ds(i, SUBLANES, stride=S), :]
                h = a * h + b
                acc = acc * a
                B_ref[d, gi, pl.ds(i, SUBLANES, stride=S), :] = h
                A_ref[d, gi, pl.ds(i, SUBLANES, stride=S), :] = acc
                out.append((h, acc))
        return tuple(out)

    init = tuple((jnp.zeros((SUBLANES, LANES), F32), jnp.ones((SUBLANES, LANES), F32))
                 for _ in range(2 * G))
    lax.fori_loop(0, S, scan_body, init)

    hin = [[None] * SUBLANES for _ in range(2 * G)]
    for d in (0, 1):
        last = S - 1 if d == 0 else 0
        for gi in range(G):
            hl = B_ref[d, gi, pl.ds(last, SUBLANES, stride=S), :]
            ac = A_ref[d, gi, pl.ds(last, SUBLANES, stride=S), :]
            if has_state:
                h = h0_ref[0, d:d + 1, gi * LANES:(gi + 1) * LANES]
            else:
                h = jnp.zeros((1, LANES), F32)
            order = range(SUBLANES) if d == 0 else range(SUBLANES - 1, -1, -1)
            for j in order:
                hin[d * G + gi][j] = h
                h = hl[j:j + 1, :] + ac[j:j + 1, :] * h
            sN_ref[0, d:d + 1, gi * LANES:(gi + 1) * LANES] = h

    for j in range(SUBLANES):
        cols = []
        for gi in range(G):
            tot = jnp.zeros((S, LANES), F32)
            for d in (0, 1):
                tot = tot + (B_ref[d, gi, j * S:(j + 1) * S, :]
                             + A_ref[d, gi, j * S:(j + 1) * S, :] * hin[d * G + gi][j])
            cols.append(tot)
        out = jnp.concatenate(cols, axis=1)
        y_ref[0, j * S:(j + 1) * S, :] = (out * _silu(g_ref[0, j * S:(j + 1) * S, :])).astype(BF16)


def _rglru(z, conv_w, conv_b, gate_w, gate_b, lam, h0):
    B, L, _ = z.shape
    has_state = h0 is not None
    W = LRU_TILE
    nh = LRU_WIDTH // W
    bpt = W // LRU_BLOCK_DIM
    gw = gate_w.reshape(2, 2, nh, bpt, LRU_BLOCK_DIM, LRU_BLOCK_DIM)
    eye = jnp.eye(bpt, dtype=F32)
    dense = jnp.einsum('dkhnij,nm->hnidkmj', gw, eye).reshape(nh, W, 4 * W)
    gb = gate_b.reshape(2, 2, nh, W).transpose(2, 0, 1, 3).reshape(nh, 1, 4 * W)
    off = lambda base: pl.BlockSpec((1, L, W), lambda b, h, base=base: (b, 0, base * LANES // W + h))
    in_specs = [off(_LX0), off(_GL0),
                pl.BlockSpec((LRU_CONV, W), lambda b, h: (0, h)),
                pl.BlockSpec((1, W), lambda b, h: (0, h)),
                pl.BlockSpec((1, W, 4 * W), lambda b, h: (h, 0, 0)),
                pl.BlockSpec((1, 1, 4 * W), lambda b, h: (h, 0, 0)),
                pl.BlockSpec((2, 1, W), lambda b, h: (0, 0, h))]
    args = [z, z, conv_w, conv_b.reshape(1, LRU_WIDTH), dense, gb, lam.reshape(2, 1, LRU_WIDTH)]
    if has_state:
        in_specs.append(pl.BlockSpec((1, 2, W), lambda b, h: (b, 0, h)))
        args.append(h0)
    return pl.pallas_call(
        functools.partial(_lru_kernel, L=L, has_state=has_state),
        grid=(B, nh),
        in_specs=in_specs,
        out_specs=[pl.BlockSpec((1, L, W), lambda b, h: (b, 0, h)),
                   pl.BlockSpec((1, 2, W), lambda b, h: (b, 0, h))],
        out_shape=[jax.ShapeDtypeStruct((B, L, LRU_WIDTH), BF16),
                   jax.ShapeDtypeStruct((B, 2, LRU_WIDTH), F32)],
        scratch_shapes=[pltpu.VMEM((2, LRU_GROUPS, L, LANES), F32),
                        pltpu.VMEM((2, LRU_GROUPS, L, LANES), F32)],
        compiler_params=_params(("arbitrary", "arbitrary")),
        name="rglru",
    )(*args)


def _rope_tables(L):
    rows = L // GRID_W
    row = jnp.repeat(jnp.arange(rows, dtype=F32), GRID_W)
    col = jnp.broadcast_to(jnp.arange(GRID_W, dtype=F32)[None, :], (rows, GRID_W)).reshape(-1)
    n_f = RET_HEAD_DIM // 4
    inv = ROPE_BASE ** (-jnp.arange(n_f, dtype=F32) / n_f)
    ang = jnp.concatenate([row[:, None] * inv[None], col[:, None] * inv[None]], axis=-1)
    cos, sin = jnp.cos(ang), jnp.sin(ang)
    return jnp.concatenate([cos, cos], axis=-1), jnp.concatenate([-sin, sin], axis=-1)


def kernel(x_prompt, x_sample, state_ret, state_lru, c, c_ctx, norm_g, ada_w, ada_b, w_in, ret_decay_logit, hy_conv_w, hy_conv_b, hy_ffn_w1, hy_ffn_b1, hy_ffn_w2, hy_ffn_b2, hy_ffn_w3, hy_freq, hy_bias, lru_conv_w, lru_conv_b, lru_gate_w, lru_gate_b, lru_lambda, w_out, final_g):
    Bp, Lp, _ = x_prompt.shape
    Bs, Ls, _ = x_sample.shape
    assert Lp % DFT_T == 0 and Ls % DFT_T == 0 and Bs + 1 <= SUBLANES

    cvec = jnp.zeros((SUBLANES, D_MODEL), F32).at[:Bs].set(c).at[Bs].set(c_ctx)
    mod = _modulation(cvec, ada_w, ada_b)
    rope = _rope_tables(Ls)
    fwd, inv = _dft_matrices()
    w_in_b = w_in.astype(BF16)
    w_out_b = w_out.astype(BF16)
    fg = final_g.reshape(1, D_MODEL)

    xc, xl = x_prompt, x_sample
    new_ret, new_lru = [], []
    for l in range(DEPTH):
        shift, scale, gate = (mod[l, :, i * D_MODEL:(i + 1) * D_MODEL] for i in range(3))
        g = norm_g[l].reshape(1, D_MODEL)
        final = l == DEPTH - 1
        filt = {}
        for L in sorted({Lp, Ls}):
            filt[L] = _hyena_filter_spectra(L, hy_ffn_w1[l], hy_ffn_b1[l], hy_ffn_w2[l], hy_ffn_b2[l],
                                            hy_ffn_w3[l], hy_freq[l], hy_bias[l], fwd)

        def layer(x, sel, rope_t, ret_s0, lru_s0):
            sh, sc, gt = (m[sel][:, None, :] for m in (shift, scale, gate))
            z = _in_proj(x, sh, sc, g, w_in_b[l])
            y_ret, s_ret = _retention(z, ret_decay_logit[l], rope_t, ret_s0)
            kre, kim = filt[x.shape[1]]
            y_hy = _hyena(z, hy_conv_w[l], hy_conv_b[l], kre, kim, fwd, inv)
            y_lru, s_lru = _rglru(z, lru_conv_w[l], lru_conv_b[l], lru_gate_w[l], lru_gate_b[l],
                                  lru_lambda[l], lru_s0)
            return _out_proj(y_ret, y_hy, y_lru, x, gt, w_out_b[l], fg, final), s_ret, s_lru

        xc, rs, ls = layer(xc, slice(Bs, Bs + 1), None, None, None)
        new_ret.append(rs)
        new_lru.append(ls)
        xl, _, _ = layer(xl, slice(0, Bs), rope, state_ret[:, l], state_lru[:, l])

    new_state_ret = jnp.stack(new_ret, axis=1).astype(x_prompt.dtype)
    new_state_lru = jnp.stack(new_lru, axis=1).astype(x_prompt.dtype)
    return (xc, xl, new_state_ret, new_state_lru)
```

```python
import functools
import math

import numpy as np
import jax
import jax.numpy as jnp
from jax import lax
from jax.experimental import pallas as pl
from jax.experimental.pallas import tpu as pltpu

F32 = jnp.float32
BF16 = jnp.bfloat16
HI = lax.Precision.HIGHEST

D_MODEL = 1024
DEPTH = 2
GRID_W = 64
EPS = 1e-6
RET_HEADS = 4
RET_HEAD_DIM = 128
RET_WIDTH = RET_HEADS * RET_HEAD_DIM
RET_CHUNK = 128
ROPE_BASE = 10000.0
HY_WIDTH = 512
HY_ORDER = 2
HY_SHORT = 3
HY_EMB = 33
HY_FFN = 64
HY_SHORT_DECAY_PCT = 0.3
HY_LONG_DECAY_PCT = 1.5
HY_TARGET = 1e-2
LRU_WIDTH = 512
LRU_BLOCKS = 8
LRU_BLOCK_DIM = LRU_WIDTH // LRU_BLOCKS
LRU_CONV = 4
LRU_C = 8.0
D_MIX = RET_WIDTH + HY_WIDTH + LRU_WIDTH
D_IN = 4 * RET_WIDTH + 4 * HY_WIDTH + 2 * LRU_WIDTH

LANES = 128
SUBLANES = 8
DFT_T = 256
ROW_TILE = 256
MAC_ROWS = 32
VMEM_LIMIT = 56 * 1024 * 1024

_Q0, _K0, _V0, _GR0 = 0, 4, 8, 12
_HV0, _HX10, _HX20, _GH0 = 16, 20, 24, 28
_LX0, _GL0 = 32, 36


def _sigmoid(x):
    return 0.5 * jnp.tanh(0.5 * x) + 0.5


def _silu(x):
    return x * _sigmoid(x)


def _softplus(x):
    return jnp.maximum(x, 0.0) + jnp.log1p(jnp.exp(-jnp.abs(x)))


def _params(sem):
    return pltpu.CompilerParams(dimension_semantics=sem, vmem_limit_bytes=VMEM_LIMIT)


def _mod_kernel(c_ref, w_ref, b_ref, o_ref):
    c = c_ref[...]
    o_ref[0] = jnp.dot(_silu(c), w_ref[0], precision=HI, preferred_element_type=F32) + b_ref[0]


def _modulation(cvec, ada_w, ada_b):
    tn = 768
    return pl.pallas_call(
        _mod_kernel,
        grid=(DEPTH, 3 * D_MODEL // tn),
        in_specs=[pl.BlockSpec((SUBLANES, D_MODEL), lambda l, j: (0, 0)),
                  pl.BlockSpec((1, D_MODEL, tn), lambda l, j: (l, 0, j)),
                  pl.BlockSpec((1, 1, tn), lambda l, j: (l, 0, j))],
        out_specs=pl.BlockSpec((1, SUBLANES, tn), lambda l, j: (l, 0, j)),
        out_shape=jax.ShapeDtypeStruct((DEPTH, SUBLANES, 3 * D_MODEL), F32),
        compiler_params=_params(("arbitrary", "arbitrary")),
        name="modulation",
    )(cvec, ada_w, ada_b.reshape(DEPTH, 1, 3 * D_MODEL))


def _in_kernel(x_ref, sh_ref, sc_ref, g_ref, w_ref, o_ref):
    x = x_ref[0]
    ms = jnp.mean(x * x, axis=-1, keepdims=True)
    y = x * lax.rsqrt(ms + EPS) * g_ref[...]
    h = (y * (1.0 + sc_ref[0]) + sh_ref[0]).astype(BF16)
    tn = 1024
    for n in range(D_IN // tn):
        o_ref[0, :, n * tn:(n + 1) * tn] = jnp.dot(h, w_ref[:, n * tn:(n + 1) * tn],
                                                   preferred_element_type=F32)


def _in_proj(x, shift, scale, g, w_bf16):
    B, L, _ = x.shape
    per_batch = shift.shape[0] > 1
    mod_map = (lambda b, i: (b, 0, 0)) if per_batch else (lambda b, i: (0, 0, 0))
    return pl.pallas_call(
        _in_kernel,
        grid=(B, L // ROW_TILE),
        in_specs=[pl.BlockSpec((1, ROW_TILE, D_MODEL), lambda b, i: (b, i, 0)),
                  pl.BlockSpec((1, 1, D_MODEL), mod_map),
                  pl.BlockSpec((1, 1, D_MODEL), mod_map),
                  pl.BlockSpec((1, D_MODEL), lambda b, i: (0, 0)),
                  pl.BlockSpec((D_MODEL, D_IN), lambda b, i: (0, 0), pipeline_mode=pl.Buffered(1))],
        out_specs=pl.BlockSpec((1, ROW_TILE, D_IN), lambda b, i: (b, i, 0)),
        out_shape=jax.ShapeDtypeStruct((B, L, D_IN), F32),
        compiler_params=_params(("arbitrary", "arbitrary")),
        name="in_proj",
    )(x, shift, scale, g, w_bf16)


def _out_kernel(yr_ref, yh_ref, yl_ref, x_ref, gate_ref, w_ref, fg_ref, o_ref, *, final):
    acc = jnp.dot(yr_ref[0], w_ref[0:RET_WIDTH], preferred_element_type=F32)
    acc = acc + jnp.dot(yh_ref[0], w_ref[RET_WIDTH:RET_WIDTH + HY_WIDTH], preferred_element_type=F32)
    acc = acc + jnp.dot(yl_ref[0], w_ref[RET_WIDTH + HY_WIDTH:D_MIX], preferred_element_type=F32)
    x = x_ref[0] + gate_ref[0] * acc
    if final:
        ms = jnp.mean(x * x, axis=-1, keepdims=True)
        x = x * lax.rsqrt(ms + EPS) * fg_ref[...]
    o_ref[0] = x


def _out_proj(y_ret, y_hy, y_lru, x, gate, w_bf16, final_g, final):
    B, L, _ = x.shape
    per_batch = gate.shape[0] > 1
    mod_map = (lambda b, i: (b, 0, 0)) if per_batch else (lambda b, i: (0, 0, 0))
    yspec = pl.BlockSpec((1, ROW_TILE, 512), lambda b, i: (b, i, 0))
    return pl.pallas_call(
        functools.partial(_out_kernel, final=final),
        grid=(B, L // ROW_TILE),
        in_specs=[yspec, yspec, yspec,
                  pl.BlockSpec((1, ROW_TILE, D_MODEL), lambda b, i: (b, i, 0)),
                  pl.BlockSpec((1, 1, D_MODEL), mod_map),
                  pl.BlockSpec((D_MIX, D_MODEL), lambda b, i: (0, 0), pipeline_mode=pl.Buffered(1)),
                  pl.BlockSpec((1, D_MODEL), lambda b, i: (0, 0))],
        out_specs=pl.BlockSpec((1, ROW_TILE, D_MODEL), lambda b, i: (b, i, 0)),
        out_shape=jax.ShapeDtypeStruct((B, L, D_MODEL), F32),
        compiler_params=_params(("arbitrary", "arbitrary")),
        name="out_proj",
    )(y_ret, y_hy, y_lru, x, gate, w_bf16, final_g)


def _ret_kernel(*refs, L, use_rope, has_state):
    it = iter(refs)
    q_ref, k_ref, v_ref, g_ref = next(it), next(it), next(it), next(it)
    cos_ref = sin_ref = s0_ref = None
    if use_rope:
        cos_ref, sin_ref = next(it), next(it)
    lg_ref = next(it)
    if has_state:
        s0_ref = next(it)
    y_ref, sN_ref, acc_ref, R_ref = next(it), next(it), next(it), next(it)

    C = RET_CHUNK
    n = L // C
    ii = lax.broadcasted_iota(jnp.int32, (C, C), 0).astype(F32)
    jj = lax.broadcasted_iota(jnp.int32, (C, C), 1).astype(F32)
    kscale = RET_HEAD_DIM ** -0.5

    for d in (0, 1):
        logit = jnp.broadcast_to(lg_ref[d, 0], (C, C))
        lg = -_softplus(-logit)
        diff = (ii - jj) if d == 0 else (jj - ii)
        dmat = jnp.where(diff >= 0, jnp.exp(lg * jnp.maximum(diff, 0.0)), 0.0)
        if d == 0:
            zeta = jnp.exp(lg * (C - 1.0 - ii))
            xi = jnp.exp(lg * (ii + 1.0))
        else:
            zeta = jnp.exp(lg * ii)
            xi = jnp.exp(lg * (C - ii))
        cdecay = jnp.exp(lg * C)
        if has_state:
            R_ref[...] = s0_ref[0, d, 0]
        else:
            R_ref[...] = jnp.zeros((C, C), F32)

        def body(t, carry, d=d, dmat=dmat, zeta=zeta, xi=xi, cdecay=cdecay):
            c = t if d == 0 else n - 1 - t
            r0 = pl.multiple_of(c * C, C)
            q = q_ref[0, pl.ds(r0, C), :]
            k = k_ref[0, pl.ds(r0, C), :] * kscale
            v = v_ref[0, pl.ds(r0, C), :]
            if use_rope:
                cs = cos_ref[pl.ds(r0, C), :]
                sn = sin_ref[pl.ds(r0, C), :]
                q = q * cs + pltpu.roll(q, 64, axis=1) * sn
                k = k * cs + pltpu.roll(k, 64, axis=1) * sn
            vb = v.astype(BF16)
            s = lax.dot_general(q.astype(BF16), k.astype(BF16), (((1,), (1,)), ((), ())),
                                preferred_element_type=F32) * dmat
            R = R_ref[...]
            o = jnp.dot(s.astype(BF16), vb, preferred_element_type=F32)
            o = o + jnp.dot((q * xi).astype(BF16), R.astype(BF16), preferred_element_type=F32)
            kv = lax.dot_general((k * zeta).astype(BF16), vb, (((0,), (0,)), ((), ())),
                                 preferred_element_type=F32)
            R_ref[...] = cdecay * R + kv
            if d == 0:
                acc_ref[pl.ds(r0, C), :] = o
            else:
                tot = acc_ref[pl.ds(r0, C), :] + o
                nrm = tot * lax.rsqrt(jnp.mean(tot * tot, axis=-1, keepdims=True) + EPS)
                y_ref[0, pl.ds(r0, C), :] = (nrm * _silu(g_ref[0, pl.ds(r0, C), :])).astype(BF16)
            return carry

        lax.fori_loop(0, n, body, 0)
        sN_ref[0, d, 0] = R_ref[...]


def _retention(z, decay_logit, rope, state0):
    B, L, _ = z.shape
    use_rope = rope is not None
    has_state = state0 is not None
    H = RET_HEADS
    col = lambda off: pl.BlockSpec((1, L, LANES), lambda b, h, off=off: (b, 0, off + h))
    in_specs = [col(_Q0), col(_K0), col(_V0), col(_GR0)]
    args = [z, z, z, z]
    if use_rope:
        in_specs += [pl.BlockSpec((L, LANES), lambda b, h: (0, 0))] * 2
        args += list(rope)
    in_specs.append(pl.BlockSpec((2, 1, 1, LANES), lambda b, h: (0, h, 0, 0)))
    args.append(jnp.broadcast_to(decay_logit[:, :, None, None], (2, H, 1, LANES)))
    if has_state:
        in_specs.append(pl.BlockSpec((1, 2, 1, RET_HEAD_DIM, RET_HEAD_DIM), lambda b, h: (b, 0, h, 0, 0)))
        args.append(state0)
    return pl.pallas_call(
        functools.partial(_ret_kernel, L=L, use_rope=use_rope, has_state=has_state),
        grid=(B, H),
        in_specs=in_specs,
        out_specs=[pl.BlockSpec((1, L, LANES), lambda b, h: (b, 0, h)),
                   pl.BlockSpec((1, 2, 1, RET_HEAD_DIM, RET_HEAD_DIM), lambda b, h: (b, 0, h, 0, 0))],
        out_shape=[jax.ShapeDtypeStruct((B, L, RET_WIDTH), BF16),
                   jax.ShapeDtypeStruct((B, 2, H, RET_HEAD_DIM, RET_HEAD_DIM), F32)],
        scratch_shapes=[pltpu.VMEM((L, LANES), F32), pltpu.VMEM((RET_CHUNK, RET_CHUNK), F32)],
        compiler_params=_params(("arbitrary", "arbitrary")),
        name="retention",
    )(*args)


def _dft_matrices():
    T = DFT_T
    N = 2 * T
    k = np.arange(T, dtype=np.float64)[:, None]
    m = np.arange(T, dtype=np.float64)[None, :]
    ang = 2.0 * np.pi * k * m / N
    fwd = np.concatenate([np.cos(ang), -np.sin(ang)], axis=0)
    fwd[T, :] = (-1.0) ** np.arange(T)
    ck = np.full((T,), 2.0)
    ck[0] = 1.0
    inv_re = (np.cos(ang) * ck[:, None]).T / N
    inv_im = (-2.0 * np.sin(ang)).T / N
    inv_im[:, 0] = ((-1.0) ** np.arange(T)) / N
    inv = np.concatenate([inv_re, inv_im], axis=1)
    out = []
    for m in (fwd, inv):
        m = jnp.asarray(m, F32)
        hi = m.astype(BF16)
        out += [hi, (m - hi.astype(F32)).astype(BF16)]
    return tuple(out)


def _split_bf16(x):
    hi = x.astype(BF16)
    return hi, (x - hi.astype(F32)).astype(BF16)


def _dot3(a_hi, a_lo, b):
    b_hi, b_lo = _split_bf16(b)
    return (jnp.dot(a_hi, b_hi, preferred_element_type=F32)
            + jnp.dot(a_lo, b_hi, preferred_element_type=F32)
            + jnp.dot(a_hi, b_lo, preferred_element_type=F32))


def _filt_kernel(zemb_ref, w1_ref, b1_ref, w2_ref, b2_ref, fr_ref, w3f_ref, w3b_ref, dec_ref,
                 bias_ref, fwdh_ref, fwdl_ref, kre_ref, kim_ref, hid_ref, hn_ref, F_ref, *, L):
    T = DFT_T
    nb = L // T

    @pl.when((pl.program_id(0) == 0) & (pl.program_id(1) == 0))
    def _():
        z1 = jnp.dot(zemb_ref[...], w1_ref[...], precision=HI, preferred_element_type=F32) + b1_ref[...]
        h1 = jnp.sin(fr_ref[0:1, :] * z1)
        z2 = jnp.dot(h1, w2_ref[...], precision=HI, preferred_element_type=F32) + b2_ref[...]
        hid_ref[...] = jnp.sin(fr_ref[1:2, :] * z2)

    hid = hid_ref[...]
    for dr, w3_ref in enumerate((w3f_ref, w3b_ref)):
        h = jnp.dot(hid, w3_ref[...], precision=HI, preferred_element_type=F32) * dec_ref[...]
        h = h / (jnp.sum(jnp.abs(h), axis=0, keepdims=True) + EPS)
        hn_ref[dr] = h
        for blk in range(nb):
            F_ref[dr, blk] = _dot3(fwdh_ref[...], fwdl_ref[...], h[blk * T:(blk + 1) * T])

    row = lax.broadcasted_iota(jnp.int32, (T, LANES), 0)
    sgn = jnp.where((row & 1) == 0, 1.0, -1.0).astype(F32)
    row0 = row == 0
    bias = bias_ref[0]
    for d in range(-(nb - 1), nb):
        re = jnp.zeros((T, LANES), F32)
        im = jnp.zeros((T, LANES), F32)
        nyq = jnp.zeros((1, LANES), F32)
        if d >= 0:
            F = F_ref[0, d]
            re, im, nyq = re + F[:T], im + F[T:], nyq + F[T:T + 1]
        if d >= 1:
            F = F_ref[0, d - 1]
            p0 = hn_ref[0, (d - 1) * T:(d - 1) * T + 1, :]
            re, im, nyq = re + sgn * (F[:T] - p0), im + sgn * F[T:], nyq + (F[T:T + 1] - p0)
        e = -d
        if e >= 0:
            F = F_ref[1, e]
            re, im, nyq = re + F[:T], im - F[T:], nyq + F[T:T + 1]
        if e >= 1:
            F = F_ref[1, e - 1]
            p0 = hn_ref[1, (e - 1) * T:(e - 1) * T + 1, :]
            re, im, nyq = re + sgn * (F[:T] - p0), im - sgn * F[T:], nyq + (F[T:T + 1] - p0)
        if d == 0:
            re, nyq = re + bias, nyq + bias
        kre_ref[0, d + nb - 1] = re
        kim_ref[0, d + nb - 1] = jnp.where(row0, nyq, im)


def _hyena_filter_spectra(L, w1, b1, w2, b2, w3, freq, bias, fwd):
    T = DFT_T
    nb = L // T
    nlag = 2 * nb - 1
    P = LANES
    t = jnp.linspace(0.0, 1.0, L, dtype=F32)[:, None]
    n_bands = (HY_EMB - 1) // 2
    f = jnp.linspace(1e-4, n_bands - 1, n_bands, dtype=F32)
    ang = (2.0 * math.pi / L) * jnp.arange(L, dtype=F32)[:, None] * f[None, :]
    zemb = jnp.concatenate([t, jnp.cos(ang), -jnp.sin(ang)], axis=-1)
    zemb = jnp.pad(zemb, ((0, 0), (0, P - HY_EMB)))
    min_decay = math.log(HY_TARGET) / HY_LONG_DECAY_PCT
    max_decay = math.log(HY_TARGET) / HY_SHORT_DECAY_PCT
    deltas = jnp.abs(jnp.linspace(min_decay, max_decay, HY_WIDTH, dtype=F32))
    dec = jnp.exp(-t * deltas[None, :])
    pad = P - HY_FFN
    w1p = jnp.pad(w1, ((0, P - HY_EMB), (0, pad)))
    b1p = jnp.pad(b1, (0, pad)).reshape(1, P)
    w2p = jnp.pad(w2, ((0, pad), (0, pad)))
    b2p = jnp.pad(b2, (0, pad)).reshape(1, P)
    w3p = jnp.pad(w3, ((0, pad), (0, 0)))
    frp = jnp.pad(freq, ((0, 0), (0, pad)))
    ncg = HY_WIDTH // P
    const = lambda shape: pl.BlockSpec(shape, lambda o, c: tuple(0 for _ in shape))
    kshape = jax.ShapeDtypeStruct((HY_ORDER, nlag, T, HY_WIDTH), F32)
    kspec = pl.BlockSpec((1, nlag, T, P), lambda o, c: (o, 0, 0, c))
    return pl.pallas_call(
        functools.partial(_filt_kernel, L=L),
        grid=(HY_ORDER, ncg),
        in_specs=[const((L, P)), const((P, P)), const((1, P)), const((P, P)), const((1, P)), const((2, P)),
                  pl.BlockSpec((P, P), lambda o, c: (0, o * 2 * ncg + c)),
                  pl.BlockSpec((P, P), lambda o, c: (0, o * 2 * ncg + ncg + c)),
                  pl.BlockSpec((L, P), lambda o, c: (0, c)),
                  pl.BlockSpec((1, 1, P), lambda o, c: (o, 0, c)),
                  const((2 * T, T)), const((2 * T, T))],
        out_specs=[kspec, kspec],
        out_shape=[kshape, kshape],
        scratch_shapes=[pltpu.VMEM((L, P), F32), pltpu.VMEM((2, L, P), F32),
                        pltpu.VMEM((2, nb, 2 * T, P), F32)],
        compiler_params=_params(("arbitrary", "arbitrary")),
        name="hyena_filters",
    )(zemb, w1p, b1p, w2p, b2p, frp, w3p, w3p, dec, bias.reshape(HY_ORDER, 1, HY_WIDTH), *fwd)


def _conv_block(x_ref, j, nb, w, b, left):
    T = DFT_T
    t0 = j * T
    C = x_ref.shape[-1]
    blk = x_ref[t0:t0 + T, :]
    prev = x_ref[t0 - SUBLANES:t0, :] if j > 0 else jnp.zeros((SUBLANES, C), F32)
    nxt = x_ref[t0 + T:t0 + T + SUBLANES, :] if j < nb - 1 else jnp.zeros((SUBLANES, C), F32)
    ext = jnp.concatenate([prev, blk, nxt], axis=0)
    n = T + 2 * SUBLANES
    acc = jnp.broadcast_to(b, (T, C))
    for tap in range(w.shape[0]):
        s = tap - left
        sh = blk if s == 0 else pltpu.roll(ext, (-s) % n, axis=0)[SUBLANES:SUBLANES + T]
        acc = acc + w[tap:tap + 1, :] * sh
    return acc


def _conv_rows(x_ref, t0, rows, w, b, left):
    L, C = x_ref.shape
    blk = x_ref[pl.ds(t0, rows), :]
    p0 = pl.multiple_of(jnp.maximum(t0 - SUBLANES, 0), SUBLANES)
    n0 = pl.multiple_of(jnp.minimum(t0 + rows, L - SUBLANES), SUBLANES)
    prev = jnp.where(t0 > 0, x_ref[pl.ds(p0, SUBLANES), :], 0.0)
    nxt = jnp.where(t0 + rows < L, x_ref[pl.ds(n0, SUBLANES), :], 0.0)
    ext = jnp.concatenate([prev, blk, nxt], axis=0)
    n = rows + 2 * SUBLANES
    acc = jnp.broadcast_to(b, (rows, C))
    for tap in range(w.shape[0]):
        s = tap - left
        sh = blk if s == 0 else pltpu.roll(ext, (-s) % n, axis=0)[SUBLANES:SUBLANES + rows]
        acc = acc + w[tap:tap + 1, :] * sh
    return acc


HY_PAIR = 2


def _hy_kernel(v_ref, x1_ref, x2_ref, g_ref, wv_ref, w1_ref, w2_ref, bv_ref, b1_ref, b2_ref,
               kre_ref, kim_ref, fwdh_ref, fwdl_ref, invh_ref, invl_ref, y_ref,
               x1c_ref, x2c_ref, z1_ref, U_ref, Y_ref, *, L):
    T = DFT_T
    P = LANES
    nb = L // T
    left = (HY_SHORT - 1) // 2
    for j in range(nb):
        u = [_conv_block(v_ref.at[p], j, nb, wv_ref[...], bv_ref[...], left) for p in range(HY_PAIR)]
        U_ref[j] = _dot3(fwdh_ref[...], fwdl_ref[...], jnp.concatenate(u, axis=1))
        for p in range(HY_PAIR):
            x1c_ref[p, j * T:(j + 1) * T, :] = _conv_block(x1_ref.at[p], j, nb, w1_ref[...], b1_ref[...], left)
            x2c_ref[p, j * T:(j + 1) * T, :] = _conv_block(x2_ref.at[p], j, nb, w2_ref[...], b2_ref[...], left)

    row0 = lax.broadcasted_iota(jnp.int32, (SUBLANES, P), 0) == 0

    def long_conv_block(o, i):
        for rc in range(T // MAC_ROWS):
            r = rc * MAC_ROWS
            are = [jnp.zeros((MAC_ROWS, P), F32) for _ in range(HY_PAIR)]
            aim = [jnp.zeros((MAC_ROWS, P), F32) for _ in range(HY_PAIR)]
            for j in range(nb):
                lag = i - j + (nb - 1)
                kr = kre_ref[o, lag, r:r + MAC_ROWS, :]
                ki = kim_ref[o, lag, r:r + MAC_ROWS, :]
                for p in range(HY_PAIR):
                    ur = U_ref[j, r:r + MAC_ROWS, p * P:(p + 1) * P]
                    ui = U_ref[j, T + r:T + r + MAC_ROWS, p * P:(p + 1) * P]
                    are[p] = are[p] + (kr * ur - ki * ui)
                    aim[p] = aim[p] + (kr * ui + ki * ur)
            for p in range(HY_PAIR):
                Y_ref[r:r + MAC_ROWS, p * P:(p + 1) * P] = are[p]
                Y_ref[T + r:T + r + MAC_ROWS, p * P:(p + 1) * P] = aim[p]
        for p in range(HY_PAIR):
            fre = jnp.zeros((SUBLANES, P), F32)
            fim = jnp.zeros((SUBLANES, P), F32)
            for j in range(nb):
                lag = i - j + (nb - 1)
                fre = fre + kre_ref[o, lag, 0:SUBLANES, :] * U_ref[j, 0:SUBLANES, p * P:(p + 1) * P]
                fim = fim + kim_ref[o, lag, 0:SUBLANES, :] * U_ref[j, T:T + SUBLANES, p * P:(p + 1) * P]
            Y_ref[0:SUBLANES, p * P:(p + 1) * P] = jnp.where(row0, fre, Y_ref[0:SUBLANES, p * P:(p + 1) * P])
            Y_ref[T:T + SUBLANES, p * P:(p + 1) * P] = jnp.where(row0, fim, Y_ref[T:T + SUBLANES, p * P:(p + 1) * P])
        return _dot3(invh_ref[...], invl_ref[...], Y_ref[...])

    def order0(i, carry):
        r0 = pl.multiple_of(i * T, T)
        y = long_conv_block(0, i)
        for p in range(HY_PAIR):
            z1_ref[p, pl.ds(r0, T), :] = x1c_ref[p, pl.ds(r0, T), :] * y[:, p * P:(p + 1) * P]
        return carry

    lax.fori_loop(0, nb, order0, 0)
    for j in range(nb):
        zc = jnp.concatenate([z1_ref[p, j * T:(j + 1) * T, :] for p in range(HY_PAIR)], axis=1)
        U_ref[j] = _dot3(fwdh_ref[...], fwdl_ref[...], zc)

    def order1(i, carry):
        r0 = pl.multiple_of(i * T, T)
        y = long_conv_block(1, i)
        for p in range(HY_PAIR):
            out = x2c_ref[p, pl.ds(r0, T), :] * y[:, p * P:(p + 1) * P]
            y_ref[p, pl.ds(r0, T), :] = (out * _silu(g_ref[p, pl.ds(r0, T), :])).astype(BF16)
        return carry

    lax.fori_loop(0, nb, order1, 0)


def _hyena(z, conv_w, conv_b, kre, kim, dft):
    B, L, _ = z.shape
    assert B % HY_PAIR == 0
    T = DFT_T
    nb = L // T
    nlag = 2 * nb - 1
    P = LANES
    ncg = HY_WIDTH // P
    col = lambda off: pl.BlockSpec((HY_PAIR, L, P), lambda c, b, off=off: (b, 0, off + c))
    wspec = lambda part: pl.BlockSpec((HY_SHORT, P), lambda c, b, part=part: (0, part * ncg + c))
    bspec = lambda part: pl.BlockSpec((1, P), lambda c, b, part=part: (0, part * ncg + c))
    kspec = pl.BlockSpec((HY_ORDER, nlag, T, P), lambda c, b: (0, 0, 0, c))
    fspec = pl.BlockSpec((2 * T, T), lambda c, b: (0, 0))
    ispec = pl.BlockSpec((T, 2 * T), lambda c, b: (0, 0))
    cb = conv_b.reshape(1, 3 * HY_WIDTH)
    return pl.pallas_call(
        functools.partial(_hy_kernel, L=L),
        grid=(ncg, B // HY_PAIR),
        in_specs=[col(_HV0), col(_HX10), col(_HX20), col(_GH0),
                  wspec(0), wspec(1), wspec(2), bspec(0), bspec(1), bspec(2),
                  kspec, kspec, fspec, fspec, ispec, ispec],
        out_specs=pl.BlockSpec((HY_PAIR, L, P), lambda c, b: (b, 0, c)),
        out_shape=jax.ShapeDtypeStruct((B, L, HY_WIDTH), BF16),
        scratch_shapes=[pltpu.VMEM((HY_PAIR, L, P), F32), pltpu.VMEM((HY_PAIR, L, P), F32),
                        pltpu.VMEM((HY_PAIR, L, P), F32),
                        pltpu.VMEM((nb, 2 * T, HY_PAIR * P), F32), pltpu.VMEM((2 * T, HY_PAIR * P), F32)],
        compiler_params=_params(("arbitrary", "arbitrary")),
        name="hyena",
    )(z, z, z, z, conv_w, conv_w, conv_w, cb, cb, cb, kre, kim, *dft)


LRU_TILE = 256
LRU_GROUPS = LRU_TILE // LANES
LRU_ROWS = 128


def _lru_kernel(*refs, L, has_state):
    it = iter(refs)
    x_ref, g_ref, cw_ref, cb_ref, wg_ref, gb_ref, lam_ref = (next(it) for _ in range(7))
    h0_ref = next(it) if has_state else None
    y_ref, sN_ref, A_ref, B_ref, H_ref, C_ref = (next(it) for _ in range(6))

    S = L // SUBLANES
    W = LRU_TILE
    G = LRU_GROUPS
    R = LRU_ROWS

    sp = [_softplus(-lam_ref[d]) for d in (0, 1)]

    def coeffs(c, carry):
        t0 = pl.multiple_of(c * R, R)
        u = _conv_rows(x_ref.at[0], t0, R, cw_ref[...], cb_ref[...], LRU_CONV // 2)
        gates = jnp.dot(u.astype(BF16), wg_ref[0], preferred_element_type=F32) + gb_ref[0]
        for d in (0, 1):
            r = _sigmoid(gates[:, (2 * d) * W:(2 * d + 1) * W])
            i = _sigmoid(gates[:, (2 * d + 1) * W:(2 * d + 2) * W])
            log_a = -LRU_C * r * sp[d]
            a = jnp.exp(log_a)
            b = jnp.sqrt(-jnp.tanh(log_a) * (a * a + 1.0)) * (i * u)
            for gi in range(G):
                A_ref[d, gi, pl.ds(t0, R), :] = a[:, gi * LANES:(gi + 1) * LANES]
                B_ref[d, gi, pl.ds(t0, R), :] = b[:, gi * LANES:(gi + 1) * LANES]
        return carry

    lax.fori_loop(0, L // R, coeffs, 0)

    def scan_body(t, carry):
        out = []
        for d in (0, 1):
            i = t if d == 0 else S - 1 - t
            r8 = pl.multiple_of(i * SUBLANES, SUBLANES)
            for gi in range(G):
                h, acc = carry[d * G + gi]
                a = A_ref[d, gi, pl.ds(i, SUBLANES, stride=S), :]
                b = B_ref[d, gi, pl.ds(i, SUBLANES, stride=S), :]
                h = a * h + b
                acc = acc * a
                H_ref[d, gi, pl.ds(r8, SUBLANES), :] = h
                C_ref[d, gi, pl.ds(r8, SUBLANES), :] = acc
                out.append((h, acc))
        return tuple(out)

    init = tuple((jnp.zeros((SUBLANES, LANES), F32), jnp.ones((SUBLANES, LANES), F32))
                 for _ in range(2 * G))
    lax.fori_loop(0, S, scan_body, init)

    hin = [[None] * SUBLANES for _ in range(2 * G)]
    for d in (0, 1):
        last = (S - 1 if d == 0 else 0) * SUBLANES
        for gi in range(G):
            hl = H_ref[d, gi, last:last + SUBLANES, :]
            ac = C_ref[d, gi, last:last + SUBLANES, :]
            if has_state:
                h = h0_ref[0, d:d + 1, gi * LANES:(gi + 1) * LANES]
            else:
                h = jnp.zeros((1, LANES), F32)
            order = range(SUBLANES) if d == 0 else range(SUBLANES - 1, -1, -1)
            for j in order:
                hin[d * G + gi][j] = h
                h = hl[j:j + 1, :] + ac[j:j + 1, :] * h
            sN_ref[0, d:d + 1, gi * LANES:(gi + 1) * LANES] = h

    for j in range(SUBLANES):
        cols = []
        for gi in range(G):
            tot = jnp.zeros((S, LANES), F32)
            for d in (0, 1):
                tot = tot + (H_ref[d, gi, pl.ds(j, S, stride=SUBLANES), :]
                             + C_ref[d, gi, pl.ds(j, S, stride=SUBLANES), :] * hin[d * G + gi][j])
            cols.append(tot)
        out = jnp.concatenate(cols, axis=1)
        y_ref[0, j * S:(j + 1) * S, :] = (out * _silu(g_ref[0, j * S:(j + 1) * S, :])).astype(BF16)


def _rglru(z, conv_w, conv_b, gate_w, gate_b, lam, h0):
    B, L, _ = z.shape
    has_state = h0 is not None
    W = LRU_TILE
    nh = LRU_WIDTH // W
    bpt = W // LRU_BLOCK_DIM
    gw = gate_w.reshape(2, 2, nh, bpt, LRU_BLOCK_DIM, LRU_BLOCK_DIM)
    eye = jnp.eye(bpt, dtype=F32)
    dense = jnp.einsum('dkhnij,nm->hnidkmj', gw, eye).reshape(nh, W, 4 * W).astype(BF16)
    gb = gate_b.reshape(2, 2, nh, W).transpose(2, 0, 1, 3).reshape(nh, 1, 4 * W)
    off = lambda base: pl.BlockSpec((1, L, W), lambda b, h, base=base: (b, 0, base * LANES // W + h))
    in_specs = [off(_LX0), off(_GL0),
                pl.BlockSpec((LRU_CONV, W), lambda b, h: (0, h)),
                pl.BlockSpec((1, W), lambda b, h: (0, h)),
                pl.BlockSpec((1, W, 4 * W), lambda b, h: (h, 0, 0)),
                pl.BlockSpec((1, 1, 4 * W), lambda b, h: (h, 0, 0)),
                pl.BlockSpec((2, 1, W), lambda b, h: (0, 0, h))]
    args = [z, z, conv_w, conv_b.reshape(1, LRU_WIDTH), dense, gb, lam.reshape(2, 1, LRU_WIDTH)]
    if has_state:
        in_specs.append(pl.BlockSpec((1, 2, W), lambda b, h: (b, 0, h)))
        args.append(h0)
    return pl.pallas_call(
        functools.partial(_lru_kernel, L=L, has_state=has_state),
        grid=(B, nh),
        in_specs=in_specs,
        out_specs=[pl.BlockSpec((1, L, W), lambda b, h: (b, 0, h)),
                   pl.BlockSpec((1, 2, W), lambda b, h: (b, 0, h))],
        out_shape=[jax.ShapeDtypeStruct((B, L, LRU_WIDTH), BF16),
                   jax.ShapeDtypeStruct((B, 2, LRU_WIDTH), F32)],
        scratch_shapes=[pltpu.VMEM((2, LRU_GROUPS, L, LANES), F32) for _ in range(4)],
        compiler_params=_params(("arbitrary", "arbitrary")),
        name="rglru",
    )(*args)


def _rope_tables(L):
    rows = L // GRID_W
    row = jnp.repeat(jnp.arange(rows, dtype=F32), GRID_W)
    col = jnp.broadcast_to(jnp.arange(GRID_W, dtype=F32)[None, :], (rows, GRID_W)).reshape(-1)
    n_f = RET_HEAD_DIM // 4
    inv = ROPE_BASE ** (-jnp.arange(n_f, dtype=F32) / n_f)
    ang = jnp.concatenate([row[:, None] * inv[None], col[:, None] * inv[None]], axis=-1)
    cos, sin = jnp.cos(ang), jnp.sin(ang)
    return jnp.concatenate([cos, cos], axis=-1), jnp.concatenate([-sin, sin], axis=-1)


def kernel(x_prompt, x_sample, state_ret, state_lru, c, c_ctx, norm_g, ada_w, ada_b, w_in, ret_decay_logit, hy_conv_w, hy_conv_b, hy_ffn_w1, hy_ffn_b1, hy_ffn_w2, hy_ffn_b2, hy_ffn_w3, hy_freq, hy_bias, lru_conv_w, lru_conv_b, lru_gate_w, lru_gate_b, lru_lambda, w_out, final_g):
    Bp, Lp, _ = x_prompt.shape
    Bs, Ls, _ = x_sample.shape
    assert Lp % DFT_T == 0 and Ls % DFT_T == 0 and Bs + 1 <= SUBLANES

    cvec = jnp.zeros((SUBLANES, D_MODEL), F32).at[:Bs].set(c).at[Bs].set(c_ctx)
    mod = _modulation(cvec, ada_w, ada_b)
    rope = _rope_tables(Ls)
    dft = _dft_matrices()
    w_in_b = w_in.astype(BF16)
    w_out_b = w_out.astype(BF16)
    fg = final_g.reshape(1, D_MODEL)

    xc, xl = x_prompt, x_sample
    new_ret, new_lru = [], []
    for l in range(DEPTH):
        shift, scale, gate = (mod[l, :, i * D_MODEL:(i + 1) * D_MODEL] for i in range(3))
        g = norm_g[l].reshape(1, D_MODEL)
        final = l == DEPTH - 1
        filt = {}
        for L in sorted({Lp, Ls}):
            filt[L] = _hyena_filter_spectra(L, hy_ffn_w1[l], hy_ffn_b1[l], hy_ffn_w2[l], hy_ffn_b2[l],
                                            hy_ffn_w3[l], hy_freq[l], hy_bias[l], dft[:2])

        def layer(x, sel, rope_t, ret_s0, lru_s0):
            sh, sc, gt = (m[sel][:, None, :] for m in (shift, scale, gate))
            z = _in_proj(x, sh, sc, g, w_in_b[l])
            y_ret, s_ret = _retention(z, ret_decay_logit[l], rope_t, ret_s0)
            kre, kim = filt[x.shape[1]]
            y_hy = _hyena(z, hy_conv_w[l], hy_conv_b[l], kre, kim, dft)
            y_lru, s_lru = _rglru(z, lru_conv_w[l], lru_conv_b[l], lru_gate_w[l], lru_gate_b[l],
                                  lru_lambda[l], lru_s0)
            return _out_proj(y_ret, y_hy, y_lru, x, gt, w_out_b[l], fg, final), s_ret, s_lru

        xc, rs, ls = layer(xc, slice(Bs, Bs + 1), None, None, None)
        new_ret.append(rs)
        new_lru.append(ls)
        xl, _, _ = layer(xl, slice(0, Bs), rope, state_ret[:, l], state_lru[:, l])

    new_state_ret = jnp.stack(new_ret, axis=1).astype(x_prompt.dtype)
    new_state_lru = jnp.stack(new_lru, axis=1).astype(x_prompt.dtype)
    return (xc, xl, new_state_ret, new_state_lru)
```

```python
import functools
import math

import numpy as np
import jax
import jax.numpy as jnp
from jax import lax
from jax.experimental import pallas as pl
from jax.experimental.pallas import tpu as pltpu

F32 = jnp.float32
BF16 = jnp.bfloat16
HI = lax.Precision.HIGHEST

D_MODEL = 1024
DEPTH = 2
GRID_W = 64
EPS = 1e-6
RET_HEADS = 4
RET_HEAD_DIM = 128
RET_WIDTH = RET_HEADS * RET_HEAD_DIM
RET_CHUNK = 128
ROPE_BASE = 10000.0
HY_WIDTH = 512
HY_ORDER = 2
HY_SHORT = 3
HY_EMB = 33
HY_FFN = 64
HY_SHORT_DECAY_PCT = 0.3
HY_LONG_DECAY_PCT = 1.5
HY_TARGET = 1e-2
LRU_WIDTH = 512
LRU_BLOCKS = 8
LRU_BLOCK_DIM = LRU_WIDTH // LRU_BLOCKS
LRU_CONV = 4
LRU_C = 8.0
D_MIX = RET_WIDTH + HY_WIDTH + LRU_WIDTH
D_IN = 4 * RET_WIDTH + 4 * HY_WIDTH + 2 * LRU_WIDTH

LANES = 128
SUBLANES = 8
DFT_T = 256
ROW_TILE = 256
MAC_ROWS = 32
VMEM_LIMIT = 56 * 1024 * 1024

_Q0, _K0, _V0, _GR0 = 0, 4, 8, 12
_HV0, _HX10, _HX20, _GH0 = 16, 20, 24, 28


def _sigmoid(x):
    return 0.5 * jnp.tanh(0.5 * x) + 0.5


def _silu(x):
    return x * _sigmoid(x)


def _softplus(x):
    return jnp.maximum(x, 0.0) + jnp.log1p(jnp.exp(-jnp.abs(x)))


def _params(sem):
    return pltpu.CompilerParams(dimension_semantics=sem, vmem_limit_bytes=VMEM_LIMIT)


def _mod_kernel(c_ref, w_ref, b_ref, o_ref):
    c = c_ref[...]
    o_ref[0] = jnp.dot(_silu(c), w_ref[0], precision=HI, preferred_element_type=F32) + b_ref[0]


def _modulation(cvec, ada_w, ada_b):
    tn = 768
    return pl.pallas_call(
        _mod_kernel,
        grid=(DEPTH, 3 * D_MODEL // tn),
        in_specs=[pl.BlockSpec((SUBLANES, D_MODEL), lambda l, j: (0, 0)),
                  pl.BlockSpec((1, D_MODEL, tn), lambda l, j: (l, 0, j)),
                  pl.BlockSpec((1, 1, tn), lambda l, j: (l, 0, j))],
        out_specs=pl.BlockSpec((1, SUBLANES, tn), lambda l, j: (l, 0, j)),
        out_shape=jax.ShapeDtypeStruct((DEPTH, SUBLANES, 3 * D_MODEL), F32),
        compiler_params=_params(("arbitrary", "arbitrary")),
        name="modulation",
    )(cvec, ada_w, ada_b.reshape(DEPTH, 1, 3 * D_MODEL))


D_MAIN = 4 * RET_WIDTH + 4 * HY_WIDTH
D_LRU = 2 * LRU_WIDTH


def _seg_case(L):
    S = L // SUBLANES
    assert S == ROW_TILE or L == ROW_TILE
    return S


def _in_kernel(x_ref, sh_ref, sc_ref, g_ref, w_ref, o_ref, ol_ref, *, S):
    x = x_ref[0]
    ms = jnp.mean(x * x, axis=-1, keepdims=True)
    y = x * lax.rsqrt(ms + EPS) * g_ref[...]
    h = (y * (1.0 + sc_ref[0]) + sh_ref[0]).astype(BF16)
    tn = 1024
    for n in range(D_MAIN // tn):
        o_ref[0, :, n * tn:(n + 1) * tn] = jnp.dot(h, w_ref[:, n * tn:(n + 1) * tn],
                                                   preferred_element_type=F32)
    zl = jnp.dot(h, w_ref[:, D_MAIN:D_IN], preferred_element_type=F32)
    if S == ROW_TILE:
        ol_ref[0] = zl
    else:
        for j in range(SUBLANES):
            ol_ref[0, :, j * D_LRU:(j + 1) * D_LRU] = zl[j * S:(j + 1) * S]


def _in_proj(x, shift, scale, g, w_bf16):
    B, L, _ = x.shape
    S = _seg_case(L)
    per_batch = shift.shape[0] > 1
    mod_map = (lambda b, i: (b, 0, 0)) if per_batch else (lambda b, i: (0, 0, 0))
    if S == ROW_TILE:
        lru_spec = pl.BlockSpec((1, ROW_TILE, D_LRU), lambda b, i: (b, 0, i))
    else:
        lru_spec = pl.BlockSpec((1, S, SUBLANES * D_LRU), lambda b, i: (b, 0, 0))
    return pl.pallas_call(
        functools.partial(_in_kernel, S=S),
        grid=(B, L // ROW_TILE),
        in_specs=[pl.BlockSpec((1, ROW_TILE, D_MODEL), lambda b, i: (b, i, 0)),
                  pl.BlockSpec((1, 1, D_MODEL), mod_map),
                  pl.BlockSpec((1, 1, D_MODEL), mod_map),
                  pl.BlockSpec((1, D_MODEL), lambda b, i: (0, 0)),
                  pl.BlockSpec((D_MODEL, D_IN), lambda b, i: (0, 0), pipeline_mode=pl.Buffered(1))],
        out_specs=[pl.BlockSpec((1, ROW_TILE, D_MAIN), lambda b, i: (b, i, 0)), lru_spec],
        out_shape=[jax.ShapeDtypeStruct((B, L, D_MAIN), F32),
                   jax.ShapeDtypeStruct((B, S, SUBLANES * D_LRU), F32)],
        compiler_params=_params(("arbitrary", "arbitrary")),
        name="in_proj",
    )(x, shift, scale, g, w_bf16)


def _out_kernel(yr_ref, yh_ref, yl_ref, x_ref, gate_ref, w_ref, fg_ref, o_ref, *, final, S):
    acc = jnp.dot(yr_ref[0], w_ref[0:RET_WIDTH], preferred_element_type=F32)
    acc = acc + jnp.dot(yh_ref[0], w_ref[RET_WIDTH:RET_WIDTH + HY_WIDTH], preferred_element_type=F32)
    yl = yl_ref[0]
    if S != ROW_TILE:
        yl = jnp.concatenate([yl[:, j * LRU_WIDTH:(j + 1) * LRU_WIDTH] for j in range(SUBLANES)], axis=0)
    acc = acc + jnp.dot(yl.astype(BF16), w_ref[RET_WIDTH + HY_WIDTH:D_MIX], preferred_element_type=F32)
    x = x_ref[0] + gate_ref[0] * acc
    if final:
        ms = jnp.mean(x * x, axis=-1, keepdims=True)
        x = x * lax.rsqrt(ms + EPS) * fg_ref[...]
    o_ref[0] = x


def _out_proj(y_ret, y_hy, y_lru, x, gate, w_bf16, final_g, final):
    B, L, _ = x.shape
    S = _seg_case(L)
    per_batch = gate.shape[0] > 1
    mod_map = (lambda b, i: (b, 0, 0)) if per_batch else (lambda b, i: (0, 0, 0))
    yspec = pl.BlockSpec((1, ROW_TILE, 512), lambda b, i: (b, i, 0))
    if S == ROW_TILE:
        lspec = pl.BlockSpec((1, ROW_TILE, LRU_WIDTH), lambda b, i: (b, 0, i))
    else:
        lspec = pl.BlockSpec((1, S, SUBLANES * LRU_WIDTH), lambda b, i: (b, 0, 0))
    return pl.pallas_call(
        functools.partial(_out_kernel, final=final, S=S),
        grid=(B, L // ROW_TILE),
        in_specs=[yspec, yspec, lspec,
                  pl.BlockSpec((1, ROW_TILE, D_MODEL), lambda b, i: (b, i, 0)),
                  pl.BlockSpec((1, 1, D_MODEL), mod_map),
                  pl.BlockSpec((D_MIX, D_MODEL), lambda b, i: (0, 0), pipeline_mode=pl.Buffered(1)),
                  pl.BlockSpec((1, D_MODEL), lambda b, i: (0, 0))],
        out_specs=pl.BlockSpec((1, ROW_TILE, D_MODEL), lambda b, i: (b, i, 0)),
        out_shape=jax.ShapeDtypeStruct((B, L, D_MODEL), F32),
        compiler_params=_params(("arbitrary", "arbitrary")),
        name="out_proj",
    )(y_ret, y_hy, y_lru, x, gate, w_bf16, final_g)


RET_HP = 2


def _ret_kernel(*refs, L, use_rope, has_state):
    it = iter(refs)
    q_ref, k_ref, v_ref, g_ref = next(it), next(it), next(it), next(it)
    cos_ref = sin_ref = s0_ref = None
    if use_rope:
        cos_ref, sin_ref = next(it), next(it)
    lg_ref = next(it)
    if has_state:
        s0_ref = next(it)
    y_ref, sN_ref, o_ref, R_ref, D_ref = (next(it) for _ in range(5))

    C = RET_CHUNK
    HP = RET_HP
    n = L // C
    ii = lax.broadcasted_iota(jnp.int32, (C, C), 0).astype(F32)
    jj = lax.broadcasted_iota(jnp.int32, (C, C), 1).astype(F32)
    kscale = RET_HEAD_DIM ** -0.5

    for d in (0, 1):
        for hh in range(HP):
            logit = jnp.broadcast_to(lg_ref[d, hh], (C, C))
            lg = -_softplus(-logit)
            diff = (ii - jj) if d == 0 else (jj - ii)
            D_ref[d, hh, 0] = jnp.where(diff >= 0, jnp.exp(lg * jnp.maximum(diff, 0.0)), 0.0)
            D_ref[d, hh, 1] = jnp.exp(lg * ((C - 1.0 - ii) if d == 0 else ii))
            D_ref[d, hh, 2] = jnp.exp(lg * ((ii + 1.0) if d == 0 else (C - ii)))
            D_ref[d, hh, 3] = jnp.exp(lg * C)
            if has_state:
                R_ref[d, hh] = s0_ref[0, d, hh]
            else:
                R_ref[d, hh] = jnp.zeros((C, C), F32)

    def body(t, carry):
        for d in (0, 1):
            c = t if d == 0 else n - 1 - t
            r0 = pl.multiple_of(c * C, C)
            for hh in range(HP):
                lanes = slice(hh * C, (hh + 1) * C)
                q = q_ref[0, pl.ds(r0, C), lanes]
                k = k_ref[0, pl.ds(r0, C), lanes] * kscale
                vb = v_ref[0, pl.ds(r0, C), lanes].astype(BF16)
                if use_rope:
                    cs = cos_ref[pl.ds(r0, C), :]
                    sn = sin_ref[pl.ds(r0, C), :]
                    q = q * cs + pltpu.roll(q, 64, axis=1) * sn
                    k = k * cs + pltpu.roll(k, 64, axis=1) * sn
                s = lax.dot_general(q.astype(BF16), k.astype(BF16), (((1,), (1,)), ((), ())),
                                    preferred_element_type=F32) * D_ref[d, hh, 0]
                R = R_ref[d, hh]
                o = jnp.dot(s.astype(BF16), vb, preferred_element_type=F32)
                o = o + jnp.dot((q * D_ref[d, hh, 2]).astype(BF16), R.astype(BF16),
                                preferred_element_type=F32)
                kv = lax.dot_general((k * D_ref[d, hh, 1]).astype(BF16), vb, (((0,), (0,)), ((), ())),
                                     preferred_element_type=F32)
                R_ref[d, hh] = D_ref[d, hh, 3] * R + kv
                o_ref[d, pl.ds(r0, C), lanes] = o
        return carry

    lax.fori_loop(0, n, body, 0, unroll=2)
    for d in (0, 1):
        for hh in range(HP):
            sN_ref[0, d, hh] = R_ref[d, hh]

    def finish(c, carry):
        r0 = pl.multiple_of(c * C, C)
        tot = o_ref[0, pl.ds(r0, C), :] + o_ref[1, pl.ds(r0, C), :]
        g = g_ref[0, pl.ds(r0, C), :]
        cols = []
        for hh in range(HP):
            th = tot[:, hh * C:(hh + 1) * C]
            cols.append(th * lax.rsqrt(jnp.mean(th * th, axis=-1, keepdims=True) + EPS))
        y_ref[0, pl.ds(r0, C), :] = (jnp.concatenate(cols, axis=1) * _silu(g)).astype(BF16)
        return carry

    lax.fori_loop(0, n, finish, 0)


def _retention(z, decay_logit, rope, state0):
    B, L, _ = z.shape
    use_rope = rope is not None
    has_state = state0 is not None
    H = RET_HEADS
    HP = RET_HP
    W = HP * RET_HEAD_DIM
    col = lambda off: pl.BlockSpec((1, L, W), lambda b, h, off=off: (b, 0, off * LANES // W + h))
    in_specs = [col(_Q0), col(_K0), col(_V0), col(_GR0)]
    args = [z, z, z, z]
    if use_rope:
        in_specs += [pl.BlockSpec((L, LANES), lambda b, h: (0, 0))] * 2
        args += list(rope)
    in_specs.append(pl.BlockSpec((2, HP, 1, LANES), lambda b, h: (0, h, 0, 0)))
    args.append(jnp.broadcast_to(decay_logit[:, :, None, None], (2, H, 1, LANES)))
    sspec = pl.BlockSpec((1, 2, HP, RET_HEAD_DIM, RET_HEAD_DIM), lambda b, h: (b, 0, h, 0, 0))
    if has_state:
        in_specs.append(sspec)
        args.append(state0)
    return pl.pallas_call(
        functools.partial(_ret_kernel, L=L, use_rope=use_rope, has_state=has_state),
        grid=(B, H // HP),
        in_specs=in_specs,
        out_specs=[pl.BlockSpec((1, L, W), lambda b, h: (b, 0, h)), sspec],
        out_shape=[jax.ShapeDtypeStruct((B, L, RET_WIDTH), BF16),
                   jax.ShapeDtypeStruct((B, 2, H, RET_HEAD_DIM, RET_HEAD_DIM), F32)],
        scratch_shapes=[pltpu.VMEM((2, L, W), F32),
                        pltpu.VMEM((2, HP, RET_CHUNK, RET_CHUNK), F32),
                        pltpu.VMEM((2, HP, 4, RET_CHUNK, RET_CHUNK), F32)],
        compiler_params=_params(("arbitrary", "arbitrary")),
        name="retention",
    )(*args)


def _dft_matrices():
    T = DFT_T
    N = 2 * T
    k = np.arange(T, dtype=np.float64)[:, None]
    m = np.arange(T, dtype=np.float64)[None, :]
    ang = 2.0 * np.pi * k * m / N
    fwd = np.concatenate([np.cos(ang), -np.sin(ang)], axis=0)
    fwd[T, :] = (-1.0) ** np.arange(T)
    ck = np.full((T,), 2.0)
    ck[0] = 1.0
    inv_re = (np.cos(ang) * ck[:, None]).T / N
    inv_im = (-2.0 * np.sin(ang)).T / N
    inv_im[:, 0] = ((-1.0) ** np.arange(T)) / N
    inv = np.concatenate([inv_re, inv_im], axis=1)
    out = []
    for m in (fwd, inv):
        m = jnp.asarray(m, F32)
        hi = m.astype(BF16)
        out += [hi, (m - hi.astype(F32)).astype(BF16)]
    return tuple(out)


def _split_bf16(x):
    hi = x.astype(BF16)
    return hi, (x - hi.astype(F32)).astype(BF16)


def _dot3(a_hi, a_lo, b):
    b_hi, b_lo = _split_bf16(b)
    return (jnp.dot(a_hi, b_hi, preferred_element_type=F32)
            + jnp.dot(a_lo, b_hi, preferred_element_type=F32)
            + jnp.dot(a_hi, b_lo, preferred_element_type=F32))


def _filt_kernel(zemb_ref, w1_ref, b1_ref, w2_ref, b2_ref, fr_ref, w3f_ref, w3b_ref, dec_ref,
                 bias_ref, fwdh_ref, fwdl_ref, kre_ref, kim_ref, hid_ref, hn_ref, F_ref, *, L):
    T = DFT_T
    nb = L // T

    @pl.when((pl.program_id(0) == 0) & (pl.program_id(1) == 0))
    def _():
        z1 = jnp.dot(zemb_ref[...], w1_ref[...], precision=HI, preferred_element_type=F32) + b1_ref[...]
        h1 = jnp.sin(fr_ref[0:1, :] * z1)
        z2 = jnp.dot(h1, w2_ref[...], precision=HI, preferred_element_type=F32) + b2_ref[...]
        hid_ref[...] = jnp.sin(fr_ref[1:2, :] * z2)

    hid = hid_ref[...]
    for dr, w3_ref in enumerate((w3f_ref, w3b_ref)):
        h = jnp.dot(hid, w3_ref[...], precision=HI, preferred_element_type=F32) * dec_ref[...]
        h = h / (jnp.sum(jnp.abs(h), axis=0, keepdims=True) + EPS)
        hn_ref[dr] = h
        for blk in range(nb):
            F_ref[dr, blk] = _dot3(fwdh_ref[...], fwdl_ref[...], h[blk * T:(blk + 1) * T])

    row = lax.broadcasted_iota(jnp.int32, (T, LANES), 0)
    sgn = jnp.where((row & 1) == 0, 1.0, -1.0).astype(F32)
    row0 = row == 0
    bias = bias_ref[0]
    for d in range(-(nb - 1), nb):
        re = jnp.zeros((T, LANES), F32)
        im = jnp.zeros((T, LANES), F32)
        nyq = jnp.zeros((1, LANES), F32)
        if d >= 0:
            F = F_ref[0, d]
            re, im, nyq = re + F[:T], im + F[T:], nyq + F[T:T + 1]
        if d >= 1:
            F = F_ref[0, d - 1]
            p0 = hn_ref[0, (d - 1) * T:(d - 1) * T + 1, :]
            re, im, nyq = re + sgn * (F[:T] - p0), im + sgn * F[T:], nyq + (F[T:T + 1] - p0)
        e = -d
        if e >= 0:
            F = F_ref[1, e]
            re, im, nyq = re + F[:T], im - F[T:], nyq + F[T:T + 1]
        if e >= 1:
            F = F_ref[1, e - 1]
            p0 = hn_ref[1, (e - 1) * T:(e - 1) * T + 1, :]
            re, im, nyq = re + sgn * (F[:T] - p0), im - sgn * F[T:], nyq + (F[T:T + 1] - p0)
        if d == 0:
            re, nyq = re + bias, nyq + bias
        kre_ref[0, d + nb - 1] = re
        kim_ref[0, d + nb - 1] = jnp.where(row0, nyq, im)


def _hyena_filter_spectra(L, w1, b1, w2, b2, w3, freq, bias, fwd):
    T = DFT_T
    nb = L // T
    nlag = 2 * nb - 1
    P = LANES
    t = jnp.linspace(0.0, 1.0, L, dtype=F32)[:, None]
    n_bands = (HY_EMB - 1) // 2
    f = jnp.linspace(1e-4, n_bands - 1, n_bands, dtype=F32)
    ang = (2.0 * math.pi / L) * jnp.arange(L, dtype=F32)[:, None] * f[None, :]
    zemb = jnp.concatenate([t, jnp.cos(ang), -jnp.sin(ang)], axis=-1)
    zemb = jnp.pad(zemb, ((0, 0), (0, P - HY_EMB)))
    min_decay = math.log(HY_TARGET) / HY_LONG_DECAY_PCT
    max_decay = math.log(HY_TARGET) / HY_SHORT_DECAY_PCT
    deltas = jnp.abs(jnp.linspace(min_decay, max_decay, HY_WIDTH, dtype=F32))
    dec = jnp.exp(-t * deltas[None, :])
    pad = P - HY_FFN
    w1p = jnp.pad(w1, ((0, P - HY_EMB), (0, pad)))
    b1p = jnp.pad(b1, (0, pad)).reshape(1, P)
    w2p = jnp.pad(w2, ((0, pad), (0, pad)))
    b2p = jnp.pad(b2, (0, pad)).reshape(1, P)
    w3p = jnp.pad(w3, ((0, pad), (0, 0)))
    frp = jnp.pad(freq, ((0, 0), (0, pad)))
    ncg = HY_WIDTH // P
    const = lambda shape: pl.BlockSpec(shape, lambda o, c: tuple(0 for _ in shape))
    kshape = jax.ShapeDtypeStruct((HY_ORDER, nlag, T, HY_WIDTH), F32)
    kspec = pl.BlockSpec((1, nlag, T, P), lambda o, c: (o, 0, 0, c))
    return pl.pallas_call(
        functools.partial(_filt_kernel, L=L),
        grid=(HY_ORDER, ncg),
        in_specs=[const((L, P)), const((P, P)), const((1, P)), const((P, P)), const((1, P)), const((2, P)),
                  pl.BlockSpec((P, P), lambda o, c: (0, o * 2 * ncg + c)),
                  pl.BlockSpec((P, P), lambda o, c: (0, o * 2 * ncg + ncg + c)),
                  pl.BlockSpec((L, P), lambda o, c: (0, c)),
                  pl.BlockSpec((1, 1, P), lambda o, c: (o, 0, c)),
                  const((2 * T, T)), const((2 * T, T))],
        out_specs=[kspec, kspec],
        out_shape=[kshape, kshape],
        scratch_shapes=[pltpu.VMEM((L, P), F32), pltpu.VMEM((2, L, P), F32),
                        pltpu.VMEM((2, nb, 2 * T, P), F32)],
        compiler_params=_params(("arbitrary", "arbitrary")),
        name="hyena_filters",
    )(zemb, w1p, b1p, w2p, b2p, frp, w3p, w3p, dec, bias.reshape(HY_ORDER, 1, HY_WIDTH), *fwd)


def _conv_block(x_ref, j, nb, w, b, left):
    T = DFT_T
    t0 = j * T
    C = x_ref.shape[-1]
    blk = x_ref[t0:t0 + T, :]
    prev = x_ref[t0 - SUBLANES:t0, :] if j > 0 else jnp.zeros((SUBLANES, C), F32)
    nxt = x_ref[t0 + T:t0 + T + SUBLANES, :] if j < nb - 1 else jnp.zeros((SUBLANES, C), F32)
    ext = jnp.concatenate([prev, blk, nxt], axis=0)
    n = T + 2 * SUBLANES
    acc = jnp.broadcast_to(b, (T, C))
    for tap in range(w.shape[0]):
        s = tap - left
        sh = blk if s == 0 else pltpu.roll(ext, (-s) % n, axis=0)[SUBLANES:SUBLANES + T]
        acc = acc + w[tap:tap + 1, :] * sh
    return acc


HY_PAIR = 2


def _hy_kernel(v_ref, x1_ref, x2_ref, g_ref, wv_ref, w1_ref, w2_ref, bv_ref, b1_ref, b2_ref,
               kre_ref, kim_ref, fwdh_ref, fwdl_ref, invh_ref, invl_ref, y_ref,
               x1c_ref, x2c_ref, z1_ref, U_ref, Y_ref, *, L):
    T = DFT_T
    P = LANES
    nb = L // T
    left = (HY_SHORT - 1) // 2
    for j in range(nb):
        u = [_conv_block(v_ref.at[p], j, nb, wv_ref[...], bv_ref[...], left) for p in range(HY_PAIR)]
        U_ref[j] = _dot3(fwdh_ref[...], fwdl_ref[...], jnp.concatenate(u, axis=1))
        for p in range(HY_PAIR):
            x1c_ref[p, j * T:(j + 1) * T, :] = _conv_block(x1_ref.at[p], j, nb, w1_ref[...], b1_ref[...], left)
            x2c_ref[p, j * T:(j + 1) * T, :] = _conv_block(x2_ref.at[p], j, nb, w2_ref[...], b2_ref[...], left)

    row0 = lax.broadcasted_iota(jnp.int32, (SUBLANES, P), 0) == 0

    def long_conv_block(o, i):
        for rc in range(T // MAC_ROWS):
            r = rc * MAC_ROWS
            are = [jnp.zeros((MAC_ROWS, P), F32) for _ in range(HY_PAIR)]
            aim = [jnp.zeros((MAC_ROWS, P), F32) for _ in range(HY_PAIR)]
            for j in range(nb):
                lag = i - j + (nb - 1)
                kr = kre_ref[o, lag, r:r + MAC_ROWS, :]
                ki = kim_ref[o, lag, r:r + MAC_ROWS, :]
                for p in range(HY_PAIR):
                    ur = U_ref[j, r:r + MAC_ROWS, p * P:(p + 1) * P]
                    ui = U_ref[j, T + r:T + r + MAC_ROWS, p * P:(p + 1) * P]
                    are[p] = are[p] + (kr * ur - ki * ui)
                    aim[p] = aim[p] + (kr * ui + ki * ur)
            for p in range(HY_PAIR):
                Y_ref[r:r + MAC_ROWS, p * P:(p + 1) * P] = are[p]
                Y_ref[T + r:T + r + MAC_ROWS, p * P:(p + 1) * P] = aim[p]
        for p in range(HY_PAIR):
            fre = jnp.zeros((SUBLANES, P), F32)
            fim = jnp.zeros((SUBLANES, P), F32)
            for j in range(nb):
                lag = i - j + (nb - 1)
                fre = fre + kre_ref[o, lag, 0:SUBLANES, :] * U_ref[j, 0:SUBLANES, p * P:(p + 1) * P]
                fim = fim + kim_ref[o, lag, 0:SUBLANES, :] * U_ref[j, T:T + SUBLANES, p * P:(p + 1) * P]
            Y_ref[0:SUBLANES, p * P:(p + 1) * P] = jnp.where(row0, fre, Y_ref[0:SUBLANES, p * P:(p + 1) * P])
            Y_ref[T:T + SUBLANES, p * P:(p + 1) * P] = jnp.where(row0, fim, Y_ref[T:T + SUBLANES, p * P:(p + 1) * P])
        return _dot3(invh_ref[...], invl_ref[...], Y_ref[...])

    def order0(i, carry):
        r0 = pl.multiple_of(i * T, T)
        y = long_conv_block(0, i)
        for p in range(HY_PAIR):
            z1_ref[p, pl.ds(r0, T), :] = x1c_ref[p, pl.ds(r0, T), :] * y[:, p * P:(p + 1) * P]
        return carry

    lax.fori_loop(0, nb, order0, 0)
    for j in range(nb):
        zc = jnp.concatenate([z1_ref[p, j * T:(j + 1) * T, :] for p in range(HY_PAIR)], axis=1)
        U_ref[j] = _dot3(fwdh_ref[...], fwdl_ref[...], zc)

    def order1(i, carry):
        r0 = pl.multiple_of(i * T, T)
        y = long_conv_block(1, i)
        for p in range(HY_PAIR):
            out = x2c_ref[p, pl.ds(r0, T), :] * y[:, p * P:(p + 1) * P]
            y_ref[p, pl.ds(r0, T), :] = (out * _silu(g_ref[p, pl.ds(r0, T), :])).astype(BF16)
        return carry

    lax.fori_loop(0, nb, order1, 0)


def _hyena(z, conv_w, conv_b, kre, kim, dft):
    B, L, _ = z.shape
    assert B % HY_PAIR == 0
    T = DFT_T
    nb = L // T
    nlag = 2 * nb - 1
    P = LANES
    ncg = HY_WIDTH // P
    col = lambda off: pl.BlockSpec((HY_PAIR, L, P), lambda c, b, off=off: (b, 0, off + c))
    wspec = lambda part: pl.BlockSpec((HY_SHORT, P), lambda c, b, part=part: (0, part * ncg + c))
    bspec = lambda part: pl.BlockSpec((1, P), lambda c, b, part=part: (0, part * ncg + c))
    kspec = pl.BlockSpec((HY_ORDER, nlag, T, P), lambda c, b: (0, 0, 0, c))
    fspec = pl.BlockSpec((2 * T, T), lambda c, b: (0, 0))
    ispec = pl.BlockSpec((T, 2 * T), lambda c, b: (0, 0))
    cb = conv_b.reshape(1, 3 * HY_WIDTH)
    return pl.pallas_call(
        functools.partial(_hy_kernel, L=L),
        grid=(ncg, B // HY_PAIR),
        in_specs=[col(_HV0), col(_HX10), col(_HX20), col(_GH0),
                  wspec(0), wspec(1), wspec(2), bspec(0), bspec(1), bspec(2),
                  kspec, kspec, fspec, fspec, ispec, ispec],
        out_specs=pl.BlockSpec((HY_PAIR, L, P), lambda c, b: (b, 0, c)),
        out_shape=jax.ShapeDtypeStruct((B, L, HY_WIDTH), BF16),
        scratch_shapes=[pltpu.VMEM((HY_PAIR, L, P), F32), pltpu.VMEM((HY_PAIR, L, P), F32),
                        pltpu.VMEM((HY_PAIR, L, P), F32),
                        pltpu.VMEM((nb, 2 * T, HY_PAIR * P), F32), pltpu.VMEM((2 * T, HY_PAIR * P), F32)],
        compiler_params=_params(("arbitrary", "arbitrary")),
        name="hyena",
    )(z, z, z, z, conv_w, conv_w, conv_w, cb, cb, cb, kre, kim, *dft)


LRU_TILE = 256
LRU_GROUPS = LRU_TILE // LANES
LRU_ROWS = 128


def _lru_kernel(*refs, L, has_state):
    it = iter(refs)
    x_ref, g_ref, cw_ref, cb_ref, wg_ref, gb_ref, lam_ref = (next(it) for _ in range(7))
    h0_ref = next(it) if has_state else None
    y_ref, sN_ref, xp_ref, A_ref, B_ref, H_ref, C_ref = (next(it) for _ in range(7))

    S = L // SUBLANES
    W = LRU_TILE
    G = LRU_GROUPS
    RT = LRU_ROWS // SUBLANES
    left = LRU_CONV // 2

    sub = lax.broadcasted_iota(jnp.int32, (SUBLANES, W), 0)
    for k in range(left):
        tail = pltpu.roll(x_ref[0, S - left + k], 1, axis=0)
        xp_ref[k] = jnp.where(sub == 0, 0.0, tail)
    for k in range(LRU_CONV - 1 - left):
        head = pltpu.roll(x_ref[0, k], SUBLANES - 1, axis=0)
        xp_ref[left + S + k] = jnp.where(sub == SUBLANES - 1, 0.0, head)

    def copy(c, carry):
        i0 = pl.multiple_of(c * RT, RT)
        xp_ref[pl.ds(left + i0, RT)] = x_ref[0, pl.ds(i0, RT)]
        return carry

    lax.fori_loop(0, S // RT, copy, 0)

    sp = [_softplus(-lam_ref[d]) for d in (0, 1)]

    def coeffs(c, carry):
        i0 = pl.multiple_of(c * RT, RT)
        u = jnp.broadcast_to(cb_ref[...], (RT, SUBLANES, W))
        for k in range(LRU_CONV):
            u = u + cw_ref[k:k + 1, :] * xp_ref[pl.ds(i0 + k, RT)]
        u = u.reshape(RT * SUBLANES, W)
        gates = jnp.dot(u.astype(BF16), wg_ref[0], preferred_element_type=F32) + gb_ref[0]
        for d in (0, 1):
            r = _sigmoid(gates[:, (2 * d) * W:(2 * d + 1) * W])
            i = _sigmoid(gates[:, (2 * d + 1) * W:(2 * d + 2) * W])
            log_a = -LRU_C * r * sp[d]
            a = jnp.exp(log_a)
            b = jnp.sqrt(-jnp.tanh(log_a) * (a * a + 1.0)) * (i * u)
            for gi in range(G):
                lanes = slice(gi * LANES, (gi + 1) * LANES)
                A_ref[d, gi, pl.ds(i0, RT)] = a[:, lanes].reshape(RT, SUBLANES, LANES)
                B_ref[d, gi, pl.ds(i0, RT)] = b[:, lanes].reshape(RT, SUBLANES, LANES)
        return carry

    lax.fori_loop(0, S // RT, coeffs, 0)

    def scan_body(t, carry):
        out = []
        for d in (0, 1):
            i = t if d == 0 else S - 1 - t
            for gi in range(G):
                h, acc = carry[d * G + gi]
                a = A_ref[d, gi, i]
                h = a * h + B_ref[d, gi, i]
                acc = acc * a
                H_ref[d, gi, i] = h
                C_ref[d, gi, i] = acc
                out.append((h, acc))
        return tuple(out)

    init = tuple((jnp.zeros((SUBLANES, LANES), F32), jnp.ones((SUBLANES, LANES), F32))
                 for _ in range(2 * G))
    lax.fori_loop(0, S, scan_body, init, unroll=4)

    hin = [None] * (2 * G)
    for d in (0, 1):
        last = S - 1 if d == 0 else 0
        for gi in range(G):
            hl = H_ref[d, gi, last]
            ac = C_ref[d, gi, last]
            if has_state:
                h = h0_ref[0, d:d + 1, gi * LANES:(gi + 1) * LANES]
            else:
                h = jnp.zeros((1, LANES), F32)
            rows = [None] * SUBLANES
            for j in (range(SUBLANES) if d == 0 else range(SUBLANES - 1, -1, -1)):
                rows[j] = h
                h = hl[j:j + 1, :] + ac[j:j + 1, :] * h
            hin[d * G + gi] = jnp.concatenate(rows, axis=0)
            sN_ref[0, d:d + 1, gi * LANES:(gi + 1) * LANES] = h

    def combine(c, carry):
        i0 = pl.multiple_of(c * RT, RT)
        cols = []
        for gi in range(G):
            tot = jnp.zeros((RT, SUBLANES, LANES), F32)
            for d in (0, 1):
                tot = tot + (H_ref[d, gi, pl.ds(i0, RT)] + C_ref[d, gi, pl.ds(i0, RT)] * hin[d * G + gi])
            cols.append(tot)
        y_ref[0, pl.ds(i0, RT)] = jnp.concatenate(cols, axis=2) * _silu(g_ref[0, pl.ds(i0, RT)])
        return carry

    lax.fori_loop(0, S // RT, combine, 0)


def _rglru(z, conv_w, conv_b, gate_w, gate_b, lam, h0):
    B, S, _ = z.shape
    L = S * SUBLANES
    has_state = h0 is not None
    W = LRU_TILE
    nh = LRU_WIDTH // W
    bpt = W // LRU_BLOCK_DIM
    gw = gate_w.reshape(2, 2, nh, bpt, LRU_BLOCK_DIM, LRU_BLOCK_DIM)
    eye = jnp.eye(bpt, dtype=F32)
    dense = jnp.einsum('dkhnij,nm->hnidkmj', gw, eye).reshape(nh, W, 4 * W).astype(BF16)
    gb = gate_b.reshape(2, 2, nh, W).transpose(2, 0, 1, 3).reshape(nh, 1, 4 * W)
    z4 = z.reshape(B, S, SUBLANES, D_LRU)
    off = lambda base: pl.BlockSpec((1, S, SUBLANES, W), lambda b, h, base=base: (b, 0, 0, base + h))
    in_specs = [off(0), off(LRU_WIDTH // W),
                pl.BlockSpec((LRU_CONV, W), lambda b, h: (0, h)),
                pl.BlockSpec((1, W), lambda b, h: (0, h)),
                pl.BlockSpec((1, W, 4 * W), lambda b, h: (h, 0, 0)),
                pl.BlockSpec((1, 1, 4 * W), lambda b, h: (h, 0, 0)),
                pl.BlockSpec((2, 1, W), lambda b, h: (0, 0, h))]
    args = [z4, z4, conv_w, conv_b.reshape(1, LRU_WIDTH), dense, gb, lam.reshape(2, 1, LRU_WIDTH)]
    if has_state:
        in_specs.append(pl.BlockSpec((1, 2, W), lambda b, h: (b, 0, h)))
        args.append(h0)
    y, s = pl.pallas_call(
        functools.partial(_lru_kernel, L=L, has_state=has_state),
        grid=(B, nh),
        in_specs=in_specs,
        out_specs=[pl.BlockSpec((1, S, SUBLANES, W), lambda b, h: (b, 0, 0, h)),
                   pl.BlockSpec((1, 2, W), lambda b, h: (b, 0, h))],
        out_shape=[jax.ShapeDtypeStruct((B, S, SUBLANES, LRU_WIDTH), F32),
                   jax.ShapeDtypeStruct((B, 2, LRU_WIDTH), F32)],
        scratch_shapes=[pltpu.VMEM((S + LRU_CONV - 1, SUBLANES, W), F32)]
        + [pltpu.VMEM((2, LRU_GROUPS, S, SUBLANES, LANES), F32) for _ in range(4)],
        compiler_params=_params(("arbitrary", "arbitrary")),
        name="rglru",
    )(*args)
    return y.reshape(B, S, SUBLANES * LRU_WIDTH), s


def _rope_tables(L):
    rows = L // GRID_W
    row = jnp.repeat(jnp.arange(rows, dtype=F32), GRID_W)
    col = jnp.broadcast_to(jnp.arange(GRID_W, dtype=F32)[None, :], (rows, GRID_W)).reshape(-1)
    n_f = RET_HEAD_DIM // 4
    inv = ROPE_BASE ** (-jnp.arange(n_f, dtype=F32) / n_f)
    ang = jnp.concatenate([row[:, None] * inv[None], col[:, None] * inv[None]], axis=-1)
    cos, sin = jnp.cos(ang), jnp.sin(ang)
    return jnp.concatenate([cos, cos], axis=-1), jnp.concatenate([-sin, sin], axis=-1)


def kernel(x_prompt, x_sample, state_ret, state_lru, c, c_ctx, norm_g, ada_w, ada_b, w_in, ret_decay_logit, hy_conv_w, hy_conv_b, hy_ffn_w1, hy_ffn_b1, hy_ffn_w2, hy_ffn_b2, hy_ffn_w3, hy_freq, hy_bias, lru_conv_w, lru_conv_b, lru_gate_w, lru_gate_b, lru_lambda, w_out, final_g):
    Bp, Lp, _ = x_prompt.shape
    Bs, Ls, _ = x_sample.shape
    assert Lp % DFT_T == 0 and Ls % DFT_T == 0 and Bs + 1 <= SUBLANES

    cvec = jnp.zeros((SUBLANES, D_MODEL), F32).at[:Bs].set(c).at[Bs].set(c_ctx)
    mod = _modulation(cvec, ada_w, ada_b)
    rope = _rope_tables(Ls)
    dft = _dft_matrices()
    w_in_b = w_in.astype(BF16)
    w_out_b = w_out.astype(BF16)
    fg = final_g.reshape(1, D_MODEL)

    xc, xl = x_prompt, x_sample
    new_ret, new_lru = [], []
    for l in range(DEPTH):
        shift, scale, gate = (mod[l, :, i * D_MODEL:(i + 1) * D_MODEL] for i in range(3))
        g = norm_g[l].reshape(1, D_MODEL)
        final = l == DEPTH - 1
        filt = {}
        for L in sorted({Lp, Ls}):
            filt[L] = _hyena_filter_spectra(L, hy_ffn_w1[l], hy_ffn_b1[l], hy_ffn_w2[l], hy_ffn_b2[l],
                                            hy_ffn_w3[l], hy_freq[l], hy_bias[l], dft[:2])

        def layer(x, sel, rope_t, ret_s0, lru_s0):
            sh, sc, gt = (m[sel][:, None, :] for m in (shift, scale, gate))
            z, z_lru = _in_proj(x, sh, sc, g, w_in_b[l])
            y_ret, s_ret = _retention(z, ret_decay_logit[l], rope_t, ret_s0)
            kre, kim = filt[x.shape[1]]
            y_hy = _hyena(z, hy_conv_w[l], hy_conv_b[l], kre, kim, dft)
            y_lru, s_lru = _rglru(z_lru, lru_conv_w[l], lru_conv_b[l], lru_gate_w[l], lru_gate_b[l],
                                  lru_lambda[l], lru_s0)
            return _out_proj(y_ret, y_hy, y_lru, x, gt, w_out_b[l], fg, final), s_ret, s_lru

        xc, rs, ls = layer(xc, slice(Bs, Bs + 1), None, None, None)
        new_ret.append(rs)
        new_lru.append(ls)
        xl, _, _ = layer(xl, slice(0, Bs), rope, state_ret[:, l], state_lru[:, l])

    new_state_ret = jnp.stack(new_ret, axis=1).astype(x_prompt.dtype)
    new_state_lru = jnp.stack(new_lru, axis=1).astype(x_prompt.dtype)
    return (xc, xl, new_state_ret, new_state_lru)
```

```python
import functools
import math

import numpy as np
import jax
import jax.numpy as jnp
from jax import lax
from jax.experimental import pallas as pl
from jax.experimental.pallas import tpu as pltpu

F32 = jnp.float32
BF16 = jnp.bfloat16
HI = lax.Precision.HIGHEST

D_MODEL = 1024
DEPTH = 2
GRID_W = 64
EPS = 1e-6
RET_HEADS = 4
RET_HEAD_DIM = 128
RET_WIDTH = RET_HEADS * RET_HEAD_DIM
RET_CHUNK = 128
ROPE_BASE = 10000.0
HY_WIDTH = 512
HY_ORDER = 2
HY_SHORT = 3
HY_EMB = 33
HY_FFN = 64
HY_SHORT_DECAY_PCT = 0.3
HY_LONG_DECAY_PCT = 1.5
HY_TARGET = 1e-2
LRU_WIDTH = 512
LRU_BLOCKS = 8
LRU_BLOCK_DIM = LRU_WIDTH // LRU_BLOCKS
LRU_CONV = 4
LRU_C = 8.0
D_MIX = RET_WIDTH + HY_WIDTH + LRU_WIDTH
D_IN = 4 * RET_WIDTH + 4 * HY_WIDTH + 2 * LRU_WIDTH

LANES = 128
SUBLANES = 8
DFT_T = 256
ROW_TILE = 256
MAC_GROUP = 64
VMEM_LIMIT = 56 * 1024 * 1024

_Q0, _K0, _V0, _GR0 = 0, 4, 8, 12
_HV0, _HX10, _HX20, _GH0 = 16, 20, 24, 28


def _sigmoid(x):
    return 0.5 * jnp.tanh(0.5 * x) + 0.5


def _silu(x):
    return x * _sigmoid(x)


def _softplus(x):
    return jnp.maximum(x, 0.0) + jnp.log1p(jnp.exp(-jnp.abs(x)))


def _params(sem):
    return pltpu.CompilerParams(dimension_semantics=sem, vmem_limit_bytes=VMEM_LIMIT)


def _mod_kernel(c_ref, w_ref, b_ref, o_ref):
    c = c_ref[...]
    o_ref[0] = jnp.dot(_silu(c), w_ref[0], precision=HI, preferred_element_type=F32) + b_ref[0]


def _modulation(cvec, ada_w, ada_b):
    tn = 768
    return pl.pallas_call(
        _mod_kernel,
        grid=(DEPTH, 3 * D_MODEL // tn),
        in_specs=[pl.BlockSpec((SUBLANES, D_MODEL), lambda l, j: (0, 0)),
                  pl.BlockSpec((1, D_MODEL, tn), lambda l, j: (l, 0, j)),
                  pl.BlockSpec((1, 1, tn), lambda l, j: (l, 0, j))],
        out_specs=pl.BlockSpec((1, SUBLANES, tn), lambda l, j: (l, 0, j)),
        out_shape=jax.ShapeDtypeStruct((DEPTH, SUBLANES, 3 * D_MODEL), F32),
        compiler_params=_params(("arbitrary", "arbitrary")),
        name="modulation",
    )(cvec, ada_w, ada_b.reshape(DEPTH, 1, 3 * D_MODEL))


D_MAIN = 4 * RET_WIDTH + 4 * HY_WIDTH
D_LRU = 2 * LRU_WIDTH


def _seg_case(L):
    S = L // SUBLANES
    assert S == ROW_TILE or L == ROW_TILE
    return S


def _in_kernel(x_ref, sh_ref, sc_ref, g_ref, w_ref, o_ref, ol_ref, *, S):
    x = x_ref[0]
    ms = jnp.mean(x * x, axis=-1, keepdims=True)
    y = x * lax.rsqrt(ms + EPS) * g_ref[...]
    h = (y * (1.0 + sc_ref[0]) + sh_ref[0]).astype(BF16)
    tn = 1024
    for n in range(D_MAIN // tn):
        o_ref[0, :, n * tn:(n + 1) * tn] = jnp.dot(h, w_ref[:, n * tn:(n + 1) * tn],
                                                   preferred_element_type=F32)
    zl = jnp.dot(h, w_ref[:, D_MAIN:D_IN], preferred_element_type=F32)
    if S == ROW_TILE:
        ol_ref[0] = zl
    else:
        for j in range(SUBLANES):
            ol_ref[0, :, j * D_LRU:(j + 1) * D_LRU] = zl[j * S:(j + 1) * S]


def _in_proj(x, shift, scale, g, w_bf16):
    B, L, _ = x.shape
    S = _seg_case(L)
    per_batch = shift.shape[0] > 1
    mod_map = (lambda b, i: (b, 0, 0)) if per_batch else (lambda b, i: (0, 0, 0))
    if S == ROW_TILE:
        lru_spec = pl.BlockSpec((1, ROW_TILE, D_LRU), lambda b, i: (b, 0, i))
    else:
        lru_spec = pl.BlockSpec((1, S, SUBLANES * D_LRU), lambda b, i: (b, 0, 0))
    return pl.pallas_call(
        functools.partial(_in_kernel, S=S),
        grid=(B, L // ROW_TILE),
        in_specs=[pl.BlockSpec((1, ROW_TILE, D_MODEL), lambda b, i: (b, i, 0)),
                  pl.BlockSpec((1, 1, D_MODEL), mod_map),
                  pl.BlockSpec((1, 1, D_MODEL), mod_map),
                  pl.BlockSpec((1, D_MODEL), lambda b, i: (0, 0)),
                  pl.BlockSpec((D_MODEL, D_IN), lambda b, i: (0, 0), pipeline_mode=pl.Buffered(1))],
        out_specs=[pl.BlockSpec((1, ROW_TILE, D_MAIN), lambda b, i: (b, i, 0)), lru_spec],
        out_shape=[jax.ShapeDtypeStruct((B, L, D_MAIN), F32),
                   jax.ShapeDtypeStruct((B, S, SUBLANES * D_LRU), F32)],
        compiler_params=_params(("arbitrary", "arbitrary")),
        name="in_proj",
    )(x, shift, scale, g, w_bf16)


OUT_ROWS = 512


def _out_kernel(yr_ref, yh_ref, yl_ref, x_ref, gate_ref, w_ref, fg_ref, o_ref, *, final, nseg):
    acc = jnp.dot(yr_ref[0], w_ref[0:RET_WIDTH], preferred_element_type=F32)
    acc = acc + jnp.dot(yh_ref[0], w_ref[RET_WIDTH:RET_WIDTH + HY_WIDTH], preferred_element_type=F32)
    yl = yl_ref[0]
    if nseg > 1:
        yl = jnp.concatenate([yl[:, j * LRU_WIDTH:(j + 1) * LRU_WIDTH] for j in range(nseg)], axis=0)
    acc = acc + jnp.dot(yl.astype(BF16), w_ref[RET_WIDTH + HY_WIDTH:D_MIX], preferred_element_type=F32)
    x = x_ref[0] + gate_ref[0] * acc
    if final:
        ms = jnp.mean(x * x, axis=-1, keepdims=True)
        x = x * lax.rsqrt(ms + EPS) * fg_ref[...]
    o_ref[0] = x


def _out_proj(y_ret, y_hy, y_lru, x, gate, w_bf16, final_g, final):
    B, L, _ = x.shape
    S = L // SUBLANES
    rows = min(OUT_ROWS, L)
    nseg = rows // S
    assert nseg * S == rows and L % rows == 0
    per_batch = gate.shape[0] > 1
    mod_map = (lambda b, i: (b, 0, 0)) if per_batch else (lambda b, i: (0, 0, 0))
    yspec = pl.BlockSpec((1, rows, 512), lambda b, i: (b, i, 0))
    lspec = pl.BlockSpec((1, S, nseg * LRU_WIDTH), lambda b, i: (b, 0, i))
    return pl.pallas_call(
        functools.partial(_out_kernel, final=final, nseg=nseg),
        grid=(B, L // rows),
        in_specs=[yspec, yspec, lspec,
                  pl.BlockSpec((1, rows, D_MODEL), lambda b, i: (b, i, 0)),
                  pl.BlockSpec((1, 1, D_MODEL), mod_map),
                  pl.BlockSpec((D_MIX, D_MODEL), lambda b, i: (0, 0), pipeline_mode=pl.Buffered(1)),
                  pl.BlockSpec((1, D_MODEL), lambda b, i: (0, 0))],
        out_specs=pl.BlockSpec((1, rows, D_MODEL), lambda b, i: (b, i, 0)),
        out_shape=jax.ShapeDtypeStruct((B, L, D_MODEL), F32),
        compiler_params=_params(("arbitrary", "arbitrary")),
        name="out_proj",
    )(y_ret, y_hy, y_lru, x, gate, w_bf16, final_g)


RET_HP = 2
RET_BLOCK = 256


def _ret_kernel(*refs, L, use_rope, has_state):
    it = iter(refs)
    q_ref, k_ref, v_ref, g_ref = next(it), next(it), next(it), next(it)
    cos_ref = sin_ref = s0_ref = None
    if use_rope:
        cos_ref, sin_ref = next(it), next(it)
    lg_ref = next(it)
    if has_state:
        s0_ref = next(it)
    y_ref, sN_ref, o_ref, R_ref, M_ref, Z_ref = (next(it) for _ in range(6))

    C = min(RET_BLOCK, L)
    HP = RET_HP
    DH = RET_HEAD_DIM
    n = L // C
    cross = has_state or n > 1
    ii = lax.broadcasted_iota(jnp.int32, (C, C), 0)
    jj = lax.broadcasted_iota(jnp.int32, (C, C), 1)
    ri = lax.broadcasted_iota(jnp.int32, (C, DH), 0).astype(F32)
    kscale = DH ** -0.5

    @pl.when(pl.program_id(1) == 0)
    def _():
        for d in (0, 1):
            for hh in range(HP):
                lg = -_softplus(-lg_ref[d, hh])
                diff = ((ii - jj) if d == 0 else (jj - ii)).astype(F32)
                M_ref[d, hh] = jnp.where(diff >= 0, jnp.exp(lg[:, :C] * jnp.maximum(diff, 0.0)), 0.0)
                lgh = lg[:, :DH]
                Z_ref[d, hh, 0] = jnp.exp(lgh * ((C - 1.0 - ri) if d == 0 else ri))
                Z_ref[d, hh, 1] = jnp.exp(lgh * ((ri + 1.0) if d == 0 else (C - ri)))
                Z_ref[d, hh, 2] = jnp.exp(jnp.broadcast_to(lgh * C, (C, DH)))

    for d in (0, 1):
        for hh in range(HP):
            if has_state:
                R_ref[d, hh] = s0_ref[0, d, hh]
            else:
                R_ref[d, hh] = jnp.zeros((DH, DH), F32)

    def body(t, carry):
        for d in (0, 1):
            c = t if d == 0 else n - 1 - t
            r0 = pl.multiple_of(c * C, C)
            for hh in range(HP):
                lanes = slice(hh * DH, (hh + 1) * DH)
                q = q_ref[0, pl.ds(r0, C), lanes]
                k = k_ref[0, pl.ds(r0, C), lanes] * kscale
                vb = v_ref[0, pl.ds(r0, C), lanes].astype(BF16)
                if use_rope:
                    cs = cos_ref[pl.ds(r0, C), :]
                    sn = sin_ref[pl.ds(r0, C), :]
                    q = q * cs + pltpu.roll(q, 64, axis=1) * sn
                    k = k * cs + pltpu.roll(k, 64, axis=1) * sn
                s = lax.dot_general(q.astype(BF16), k.astype(BF16), (((1,), (1,)), ((), ())),
                                    preferred_element_type=F32) * M_ref[d, hh]
                o = jnp.dot(s.astype(BF16), vb, preferred_element_type=F32)
                R = R_ref[d, hh]
                if cross:
                    o = o + jnp.dot((q * Z_ref[d, hh, 1]).astype(BF16), R.astype(BF16),
                                    preferred_element_type=F32)
                kv = lax.dot_general((k * Z_ref[d, hh, 0]).astype(BF16), vb, (((0,), (0,)), ((), ())),
                                     preferred_element_type=F32)
                R_ref[d, hh] = Z_ref[d, hh, 2, 0:DH, :] * R + kv
                o_ref[d, pl.ds(r0, C), lanes] = o
        return carry

    lax.fori_loop(0, n, body, 0, unroll=2 if n % 2 == 0 else 1)
    for d in (0, 1):
        for hh in range(HP):
            sN_ref[0, d, hh] = R_ref[d, hh]

    def finish(c, carry):
        r0 = pl.multiple_of(c * DH, DH)
        tot = o_ref[0, pl.ds(r0, DH), :] + o_ref[1, pl.ds(r0, DH), :]
        g = g_ref[0, pl.ds(r0, DH), :]
        cols = []
        for hh in range(HP):
            th = tot[:, hh * DH:(hh + 1) * DH]
            cols.append(th * lax.rsqrt(jnp.mean(th * th, axis=-1, keepdims=True) + EPS))
        y_ref[0, pl.ds(r0, DH), :] = (jnp.concatenate(cols, axis=1) * _silu(g)).astype(BF16)
        return carry

    lax.fori_loop(0, L // DH, finish, 0)


def _retention(z, decay_logit, rope, state0):
    B, L, _ = z.shape
    use_rope = rope is not None
    has_state = state0 is not None
    H = RET_HEADS
    HP = RET_HP
    W = HP * RET_HEAD_DIM
    col = lambda off: pl.BlockSpec((1, L, W), lambda h, b, off=off: (b, 0, off * LANES // W + h))
    in_specs = [col(_Q0), col(_K0), col(_V0), col(_GR0)]
    args = [z, z, z, z]
    if use_rope:
        in_specs += [pl.BlockSpec((L, LANES), lambda h, b: (0, 0))] * 2
        args += list(rope)
    C = min(RET_BLOCK, L)
    in_specs.append(pl.BlockSpec((2, HP, 1, RET_BLOCK), lambda h, b: (0, h, 0, 0)))
    args.append(jnp.broadcast_to(decay_logit[:, :, None, None], (2, H, 1, RET_BLOCK)))
    sspec = pl.BlockSpec((1, 2, HP, RET_HEAD_DIM, RET_HEAD_DIM), lambda h, b: (b, 0, h, 0, 0))
    if has_state:
        in_specs.append(sspec)
        args.append(state0)
    return pl.pallas_call(
        functools.partial(_ret_kernel, L=L, use_rope=use_rope, has_state=has_state),
        grid=(H // HP, B),
        in_specs=in_specs,
        out_specs=[pl.BlockSpec((1, L, W), lambda h, b: (b, 0, h)), sspec],
        out_shape=[jax.ShapeDtypeStruct((B, L, RET_WIDTH), BF16),
                   jax.ShapeDtypeStruct((B, 2, H, RET_HEAD_DIM, RET_HEAD_DIM), F32)],
        scratch_shapes=[pltpu.VMEM((2, L, W), F32),
                        pltpu.VMEM((2, HP, RET_HEAD_DIM, RET_HEAD_DIM), F32),
                        pltpu.VMEM((2, HP, C, C), F32),
                        pltpu.VMEM((2, HP, 3, C, RET_HEAD_DIM), F32)],
        compiler_params=_params(("arbitrary", "arbitrary")),
        name="retention",
    )(*args)


def _dft_matrices():
    T = DFT_T
    N = 2 * T
    k = np.arange(T, dtype=np.float64)[:, None]
    m = np.arange(T, dtype=np.float64)[None, :]
    ang = 2.0 * np.pi * k * m / N
    fwd = np.concatenate([np.cos(ang), -np.sin(ang)], axis=0)
    fwd[T, :] = (-1.0) ** np.arange(T)
    ck = np.full((T,), 2.0)
    ck[0] = 1.0
    inv_re = (np.cos(ang) * ck[:, None]).T / N
    inv_im = (-2.0 * np.sin(ang)).T / N
    inv_im[:, 0] = ((-1.0) ** np.arange(T)) / N
    inv = np.concatenate([inv_re, inv_im], axis=1)
    out = []
    for m in (fwd, inv):
        m = jnp.asarray(m, F32)
        hi = m.astype(BF16)
        out += [hi, (m - hi.astype(F32)).astype(BF16)]
    return tuple(out)


def _split_bf16(x):
    hi = x.astype(BF16)
    return hi, (x - hi.astype(F32)).astype(BF16)


def _dot3(a_hi, a_lo, b):
    b_hi, b_lo = _split_bf16(b)
    return (jnp.dot(a_hi, b_hi, preferred_element_type=F32)
            + jnp.dot(a_lo, b_hi, preferred_element_type=F32)
            + jnp.dot(a_hi, b_lo, preferred_element_type=F32))


def _filt_kernel(zemb_ref, w1_ref, b1_ref, w2_ref, b2_ref, fr_ref, w3f_ref, w3b_ref, dec_ref,
                 bias_ref, fwdh_ref, fwdl_ref, kre_ref, kim_ref, hid_ref, hn_ref, F_ref, *, L):
    T = DFT_T
    nb = L // T

    @pl.when((pl.program_id(0) == 0) & (pl.program_id(1) == 0))
    def _():
        z1 = jnp.dot(zemb_ref[...], w1_ref[...], precision=HI, preferred_element_type=F32) + b1_ref[...]
        h1 = jnp.sin(fr_ref[0:1, :] * z1)
        z2 = jnp.dot(h1, w2_ref[...], precision=HI, preferred_element_type=F32) + b2_ref[...]
        hid_ref[...] = jnp.sin(fr_ref[1:2, :] * z2)

    hid = hid_ref[...]
    for dr, w3_ref in enumerate((w3f_ref, w3b_ref)):
        h = jnp.dot(hid, w3_ref[...], precision=HI, preferred_element_type=F32) * dec_ref[...]
        h = h / (jnp.sum(jnp.abs(h), axis=0, keepdims=True) + EPS)
        hn_ref[dr] = h
        for blk in range(nb):
            F_ref[dr, blk] = _dot3(fwdh_ref[...], fwdl_ref[...], h[blk * T:(blk + 1) * T])

    row = lax.broadcasted_iota(jnp.int32, (T, LANES), 0)
    sgn = jnp.where((row & 1) == 0, 1.0, -1.0).astype(F32)
    row0 = row == 0
    bias = bias_ref[0]
    for d in range(-(nb - 1), nb):
        re = jnp.zeros((T, LANES), F32)
        im = jnp.zeros((T, LANES), F32)
        nyq = jnp.zeros((1, LANES), F32)
        if d >= 0:
            F = F_ref[0, d]
            re, im, nyq = re + F[:T], im + F[T:], nyq + F[T:T + 1]
        if d >= 1:
            F = F_ref[0, d - 1]
            p0 = hn_ref[0, (d - 1) * T:(d - 1) * T + 1, :]
            re, im, nyq = re + sgn * (F[:T] - p0), im + sgn * F[T:], nyq + (F[T:T + 1] - p0)
        e = -d
        if e >= 0:
            F = F_ref[1, e]
            re, im, nyq = re + F[:T], im - F[T:], nyq + F[T:T + 1]
        if e >= 1:
            F = F_ref[1, e - 1]
            p0 = hn_ref[1, (e - 1) * T:(e - 1) * T + 1, :]
            re, im, nyq = re + sgn * (F[:T] - p0), im - sgn * F[T:], nyq + (F[T:T + 1] - p0)
        if d == 0:
            re, nyq = re + bias, nyq + bias
        kre_ref[0, d + nb - 1] = re
        kim_ref[0, d + nb - 1] = jnp.where(row0, nyq, im)


def _hyena_filter_spectra(L, w1, b1, w2, b2, w3, freq, bias, fwd):
    T = DFT_T
    nb = L // T
    nlag = 2 * nb - 1
    P = LANES
    t = np.linspace(0.0, 1.0, L)[:, None]
    n_bands = (HY_EMB - 1) // 2
    f = np.linspace(1e-4, n_bands - 1, n_bands)
    ang = (2.0 * math.pi / L) * np.arange(L)[:, None] * f[None, :]
    zemb = np.concatenate([t, np.cos(ang), -np.sin(ang)], axis=-1)
    zemb = jnp.asarray(np.pad(zemb, ((0, 0), (0, P - HY_EMB))), F32)
    min_decay = math.log(HY_TARGET) / HY_LONG_DECAY_PCT
    max_decay = math.log(HY_TARGET) / HY_SHORT_DECAY_PCT
    deltas = np.abs(np.linspace(min_decay, max_decay, HY_WIDTH))
    dec = jnp.asarray(np.exp(-t * deltas[None, :]), F32)
    pad = P - HY_FFN
    w1p = jnp.pad(w1, ((0, P - HY_EMB), (0, pad)))
    b1p = jnp.pad(b1, (0, pad)).reshape(1, P)
    w2p = jnp.pad(w2, ((0, pad), (0, pad)))
    b2p = jnp.pad(b2, (0, pad)).reshape(1, P)
    w3p = jnp.pad(w3, ((0, pad), (0, 0)))
    frp = jnp.pad(freq, ((0, 0), (0, pad)))
    ncg = HY_WIDTH // P
    const = lambda shape: pl.BlockSpec(shape, lambda o, c: tuple(0 for _ in shape))
    kshape = jax.ShapeDtypeStruct((HY_ORDER, nlag, T, HY_WIDTH), F32)
    kspec = pl.BlockSpec((1, nlag, T, P), lambda o, c: (o, 0, 0, c))
    return pl.pallas_call(
        functools.partial(_filt_kernel, L=L),
        grid=(HY_ORDER, ncg),
        in_specs=[const((L, P)), const((P, P)), const((1, P)), const((P, P)), const((1, P)), const((2, P)),
                  pl.BlockSpec((P, P), lambda o, c: (0, o * 2 * ncg + c)),
                  pl.BlockSpec((P, P), lambda o, c: (0, o * 2 * ncg + ncg + c)),
                  pl.BlockSpec((L, P), lambda o, c: (0, c)),
                  pl.BlockSpec((1, 1, P), lambda o, c: (o, 0, c)),
                  const((2 * T, T)), const((2 * T, T))],
        out_specs=[kspec, kspec],
        out_shape=[kshape, kshape],
        scratch_shapes=[pltpu.VMEM((L, P), F32), pltpu.VMEM((2, L, P), F32),
                        pltpu.VMEM((2, nb, 2 * T, P), F32)],
        compiler_params=_params(("arbitrary", "arbitrary")),
        name="hyena_filters",
    )(zemb, w1p, b1p, w2p, b2p, frp, w3p, w3p, dec, bias.reshape(HY_ORDER, 1, HY_WIDTH), *fwd)


def _conv_block(x_ref, j, nb, w, b, left):
    T = DFT_T
    t0 = j * T
    C = x_ref.shape[-1]
    blk = x_ref[t0:t0 + T, :]
    prev = x_ref[t0 - SUBLANES:t0, :] if j > 0 else jnp.zeros((SUBLANES, C), F32)
    nxt = x_ref[t0 + T:t0 + T + SUBLANES, :] if j < nb - 1 else jnp.zeros((SUBLANES, C), F32)
    ext = jnp.concatenate([prev, blk, nxt], axis=0)
    n = T + 2 * SUBLANES
    acc = jnp.broadcast_to(b, (T, C))
    for tap in range(w.shape[0]):
        s = tap - left
        sh = blk if s == 0 else pltpu.roll(ext, (-s) % n, axis=0)[SUBLANES:SUBLANES + T]
        acc = acc + w[tap:tap + 1, :] * sh
    return acc


HY_LANES_ROWS = 2 * 2048


def _hy_kernel(v_ref, x1_ref, x2_ref, g_ref, wv_ref, w1_ref, w2_ref, bv_ref, b1_ref, b2_ref,
               kre_ref, kim_ref, fwdh_ref, fwdl_ref, invh_ref, invl_ref, y_ref,
               x1c_ref, x2c_ref, z1_ref, U_ref, Y_ref, *, L):
    T = DFT_T
    P = LANES
    nb = L // T
    NBT = v_ref.shape[0]
    MR = MAC_GROUP // NBT
    left = (HY_SHORT - 1) // 2
    for j in range(nb):
        u = [_conv_block(v_ref.at[p], j, nb, wv_ref[...], bv_ref[...], left) for p in range(NBT)]
        U_ref[j] = _dot3(fwdh_ref[...], fwdl_ref[...], jnp.concatenate(u, axis=1))
        for p in range(NBT):
            x1c_ref[p, j * T:(j + 1) * T, :] = _conv_block(x1_ref.at[p], j, nb, w1_ref[...], b1_ref[...], left)
            x2c_ref[p, j * T:(j + 1) * T, :] = _conv_block(x2_ref.at[p], j, nb, w2_ref[...], b2_ref[...], left)

    row0 = lax.broadcasted_iota(jnp.int32, (SUBLANES, P), 0) == 0

    def long_conv_block(o, i):
        for rc in range(T // MR):
            r = rc * MR
            are = [jnp.zeros((MR, P), F32) for _ in range(NBT)]
            aim = [jnp.zeros((MR, P), F32) for _ in range(NBT)]
            for j in range(nb):
                lag = i - j + (nb - 1)
                kr = kre_ref[o, lag, r:r + MR, :]
                ki = kim_ref[o, lag, r:r + MR, :]
                for p in range(NBT):
                    ur = U_ref[j, r:r + MR, p * P:(p + 1) * P]
                    ui = U_ref[j, T + r:T + r + MR, p * P:(p + 1) * P]
                    are[p] = are[p] + (kr * ur - ki * ui)
                    aim[p] = aim[p] + (kr * ui + ki * ur)
            for p in range(NBT):
                Y_ref[r:r + MR, p * P:(p + 1) * P] = are[p]
                Y_ref[T + r:T + r + MR, p * P:(p + 1) * P] = aim[p]
        for p in range(NBT):
            fre = jnp.zeros((SUBLANES, P), F32)
            fim = jnp.zeros((SUBLANES, P), F32)
            for j in range(nb):
                lag = i - j + (nb - 1)
                fre = fre + kre_ref[o, lag, 0:SUBLANES, :] * U_ref[j, 0:SUBLANES, p * P:(p + 1) * P]
                fim = fim + kim_ref[o, lag, 0:SUBLANES, :] * U_ref[j, T:T + SUBLANES, p * P:(p + 1) * P]
            Y_ref[0:SUBLANES, p * P:(p + 1) * P] = jnp.where(row0, fre, Y_ref[0:SUBLANES, p * P:(p + 1) * P])
            Y_ref[T:T + SUBLANES, p * P:(p + 1) * P] = jnp.where(row0, fim, Y_ref[T:T + SUBLANES, p * P:(p + 1) * P])
        return _dot3(invh_ref[...], invl_ref[...], Y_ref[...])

    def order0(i, carry):
        r0 = pl.multiple_of(i * T, T)
        y = long_conv_block(0, i)
        for p in range(NBT):
            z1_ref[p, pl.ds(r0, T), :] = x1c_ref[p, pl.ds(r0, T), :] * y[:, p * P:(p + 1) * P]
        return carry

    lax.fori_loop(0, nb, order0, 0)
    for j in range(nb):
        zc = jnp.concatenate([z1_ref[p, j * T:(j + 1) * T, :] for p in range(NBT)], axis=1)
        U_ref[j] = _dot3(fwdh_ref[...], fwdl_ref[...], zc)

    def order1(i, carry):
        r0 = pl.multiple_of(i * T, T)
        y = long_conv_block(1, i)
        for p in range(NBT):
            out = x2c_ref[p, pl.ds(r0, T), :] * y[:, p * P:(p + 1) * P]
            y_ref[p, pl.ds(r0, T), :] = (out * _silu(g_ref[p, pl.ds(r0, T), :])).astype(BF16)
        return carry

    lax.fori_loop(0, nb, order1, 0)


def _hyena(z, conv_w, conv_b, kre, kim, dft):
    B, L, _ = z.shape
    NBT = min(B, max(2, HY_LANES_ROWS // L), MAC_GROUP // SUBLANES)
    assert B % NBT == 0
    T = DFT_T
    nb = L // T
    nlag = 2 * nb - 1
    P = LANES
    ncg = HY_WIDTH // P
    col = lambda off: pl.BlockSpec((NBT, L, P), lambda c, b, off=off: (b, 0, off + c))
    wspec = lambda part: pl.BlockSpec((HY_SHORT, P), lambda c, b, part=part: (0, part * ncg + c))
    bspec = lambda part: pl.BlockSpec((1, P), lambda c, b, part=part: (0, part * ncg + c))
    kspec = pl.BlockSpec((HY_ORDER, nlag, T, P), lambda c, b: (0, 0, 0, c))
    fspec = pl.BlockSpec((2 * T, T), lambda c, b: (0, 0))
    ispec = pl.BlockSpec((T, 2 * T), lambda c, b: (0, 0))
    cb = conv_b.reshape(1, 3 * HY_WIDTH)
    return pl.pallas_call(
        functools.partial(_hy_kernel, L=L),
        grid=(ncg, B // NBT),
        in_specs=[col(_HV0), col(_HX10), col(_HX20), col(_GH0),
                  wspec(0), wspec(1), wspec(2), bspec(0), bspec(1), bspec(2),
                  kspec, kspec, fspec, fspec, ispec, ispec],
        out_specs=pl.BlockSpec((NBT, L, P), lambda c, b: (b, 0, c)),
        out_shape=jax.ShapeDtypeStruct((B, L, HY_WIDTH), BF16),
        scratch_shapes=[pltpu.VMEM((NBT, L, P), F32), pltpu.VMEM((NBT, L, P), F32),
                        pltpu.VMEM((NBT, L, P), F32),
                        pltpu.VMEM((nb, 2 * T, NBT * P), F32), pltpu.VMEM((2 * T, NBT * P), F32)],
        compiler_params=_params(("arbitrary", "arbitrary")),
        name="hyena",
    )(z, z, z, z, conv_w, conv_w, conv_w, cb, cb, cb, kre, kim, *dft)


LRU_TILE = 256
LRU_GROUPS = LRU_TILE // LANES
LRU_ROWS = 128


def _lru_kernel(*refs, L, has_state):
    it = iter(refs)
    x_ref, g_ref, cw_ref, cb_ref, wg_ref, gb_ref, lam_ref = (next(it) for _ in range(7))
    h0_ref = next(it) if has_state else None
    y_ref, sN_ref, xp_ref, A_ref, B_ref, H_ref, C_ref = (next(it) for _ in range(7))

    S = L // SUBLANES
    W = LRU_TILE
    G = LRU_GROUPS
    RT = LRU_ROWS // SUBLANES
    left = LRU_CONV // 2

    sub = lax.broadcasted_iota(jnp.int32, (SUBLANES, W), 0)
    for k in range(left):
        tail = pltpu.roll(x_ref[0, S - left + k], 1, axis=0)
        xp_ref[k] = jnp.where(sub == 0, 0.0, tail)
    for k in range(LRU_CONV - 1 - left):
        head = pltpu.roll(x_ref[0, k], SUBLANES - 1, axis=0)
        xp_ref[left + S + k] = jnp.where(sub == SUBLANES - 1, 0.0, head)

    def copy(c, carry):
        i0 = pl.multiple_of(c * RT, RT)
        xp_ref[pl.ds(left + i0, RT)] = x_ref[0, pl.ds(i0, RT)]
        return carry

    lax.fori_loop(0, S // RT, copy, 0)

    sp = [_softplus(-lam_ref[d]) for d in (0, 1)]

    def coeffs(c, carry):
        i0 = pl.multiple_of(c * RT, RT)
        u = jnp.broadcast_to(cb_ref[...], (RT, SUBLANES, W))
        for k in range(LRU_CONV):
            u = u + cw_ref[k:k + 1, :] * xp_ref[pl.ds(i0 + k, RT)]
        u = u.reshape(RT * SUBLANES, W)
        gates = jnp.dot(u.astype(BF16), wg_ref[0], preferred_element_type=F32) + gb_ref[0]
        for d in (0, 1):
            r = _sigmoid(gates[:, (2 * d) * W:(2 * d + 1) * W])
            i = _sigmoid(gates[:, (2 * d + 1) * W:(2 * d + 2) * W])
            log_a = -LRU_C * r * sp[d]
            a = jnp.exp(log_a)
            b = jnp.sqrt(-jnp.tanh(log_a) * (a * a + 1.0)) * (i * u)
            for gi in range(G):
                lanes = slice(gi * LANES, (gi + 1) * LANES)
                A_ref[d, gi, pl.ds(i0, RT)] = a[:, lanes].reshape(RT, SUBLANES, LANES)
                B_ref[d, gi, pl.ds(i0, RT)] = b[:, lanes].reshape(RT, SUBLANES, LANES)
        return carry

    lax.fori_loop(0, S // RT, coeffs, 0)

    def scan_body(t, carry):
        out = []
        for d in (0, 1):
            i = t if d == 0 else S - 1 - t
            for gi in range(G):
                h, acc = carry[d * G + gi]
                a = A_ref[d, gi, i]
                h = a * h + B_ref[d, gi, i]
                acc = acc * a
                H_ref[d, gi, i] = h
                C_ref[d, gi, i] = acc
                out.append((h, acc))
        return tuple(out)

    init = tuple((jnp.zeros((SUBLANES, LANES), F32), jnp.ones((SUBLANES, LANES), F32))
                 for _ in range(2 * G))
    lax.fori_loop(0, S, scan_body, init, unroll=4)

    hin = [None] * (2 * G)
    for d in (0, 1):
        last = S - 1 if d == 0 else 0
        for gi in range(G):
            hl = H_ref[d, gi, last]
            ac = C_ref[d, gi, last]
            if has_state:
                h = h0_ref[0, d:d + 1, gi * LANES:(gi + 1) * LANES]
            else:
                h = jnp.zeros((1, LANES), F32)
            rows = [None] * SUBLANES
            for j in (range(SUBLANES) if d == 0 else range(SUBLANES - 1, -1, -1)):
                rows[j] = h
                h = hl[j:j + 1, :] + ac[j:j + 1, :] * h
            hin[d * G + gi] = jnp.concatenate(rows, axis=0)
            sN_ref[0, d:d + 1, gi * LANES:(gi + 1) * LANES] = h

    def combine(c, carry):
        i0 = pl.multiple_of(c * RT, RT)
        cols = []
        for gi in range(G):
            tot = jnp.zeros((RT, SUBLANES, LANES), F32)
            for d in (0, 1):
                tot = tot + (H_ref[d, gi, pl.ds(i0, RT)] + C_ref[d, gi, pl.ds(i0, RT)] * hin[d * G + gi])
            cols.append(tot)
        y_ref[0, pl.ds(i0, RT)] = jnp.concatenate(cols, axis=2) * _silu(g_ref[0, pl.ds(i0, RT)])
        return carry

    lax.fori_loop(0, S // RT, combine, 0)


def _rglru(z, conv_w, conv_b, gate_w, gate_b, lam, h0):
    B, S, _ = z.shape
    L = S * SUBLANES
    has_state = h0 is not None
    W = LRU_TILE
    nh = LRU_WIDTH // W
    bpt = W // LRU_BLOCK_DIM
    gw = gate_w.reshape(2, 2, nh, bpt, LRU_BLOCK_DIM, LRU_BLOCK_DIM)
    eye = jnp.eye(bpt, dtype=F32)
    dense = jnp.einsum('dkhnij,nm->hnidkmj', gw, eye).reshape(nh, W, 4 * W).astype(BF16)
    gb = gate_b.reshape(2, 2, nh, W).transpose(2, 0, 1, 3).reshape(nh, 1, 4 * W)
    z4 = z.reshape(B, S, SUBLANES, D_LRU)
    off = lambda base: pl.BlockSpec((1, S, SUBLANES, W), lambda b, h, base=base: (b, 0, 0, base + h))
    in_specs = [off(0), off(LRU_WIDTH // W),
                pl.BlockSpec((LRU_CONV, W), lambda b, h: (0, h)),
                pl.BlockSpec((1, W), lambda b, h: (0, h)),
                pl.BlockSpec((1, W, 4 * W), lambda b, h: (h, 0, 0)),
                pl.BlockSpec((1, 1, 4 * W), lambda b, h: (h, 0, 0)),
                pl.BlockSpec((2, 1, W), lambda b, h: (0, 0, h))]
    args = [z4, z4, conv_w, conv_b.reshape(1, LRU_WIDTH), dense, gb, lam.reshape(2, 1, LRU_WIDTH)]
    if has_state:
        in_specs.append(pl.BlockSpec((1, 2, W), lambda b, h: (b, 0, h)))
        args.append(h0)
    y, s = pl.pallas_call(
        functools.partial(_lru_kernel, L=L, has_state=has_state),
        grid=(B, nh),
        in_specs=in_specs,
        out_specs=[pl.BlockSpec((1, S, SUBLANES, W), lambda b, h: (b, 0, 0, h)),
                   pl.BlockSpec((1, 2, W), lambda b, h: (b, 0, h))],
        out_shape=[jax.ShapeDtypeStruct((B, S, SUBLANES, LRU_WIDTH), F32),
                   jax.ShapeDtypeStruct((B, 2, LRU_WIDTH), F32)],
        scratch_shapes=[pltpu.VMEM((S + LRU_CONV - 1, SUBLANES, W), F32)]
        + [pltpu.VMEM((2, LRU_GROUPS, S, SUBLANES, LANES), F32) for _ in range(4)],
        compiler_params=_params(("arbitrary", "arbitrary")),
        name="rglru",
    )(*args)
    return y.reshape(B, S, SUBLANES * LRU_WIDTH), s


def _rope_tables(L):
    rows = L // GRID_W
    row = np.repeat(np.arange(rows, dtype=np.float64), GRID_W)
    col = np.tile(np.arange(GRID_W, dtype=np.float64), rows)
    n_f = RET_HEAD_DIM // 4
    inv = ROPE_BASE ** (-np.arange(n_f, dtype=np.float64) / n_f)
    ang = np.concatenate([row[:, None] * inv[None], col[:, None] * inv[None]], axis=-1)
    cos, sin = np.cos(ang), np.sin(ang)
    return (jnp.asarray(np.concatenate([cos, cos], axis=-1), F32),
            jnp.asarray(np.concatenate([-sin, sin], axis=-1), F32))


def kernel(x_prompt, x_sample, state_ret, state_lru, c, c_ctx, norm_g, ada_w, ada_b, w_in, ret_decay_logit, hy_conv_w, hy_conv_b, hy_ffn_w1, hy_ffn_b1, hy_ffn_w2, hy_ffn_b2, hy_ffn_w3, hy_freq, hy_bias, lru_conv_w, lru_conv_b, lru_gate_w, lru_gate_b, lru_lambda, w_out, final_g):
    Bp, Lp, _ = x_prompt.shape
    Bs, Ls, _ = x_sample.shape
    assert Lp % DFT_T == 0 and Ls % DFT_T == 0 and Bs + 1 <= SUBLANES

    cvec = jnp.zeros((SUBLANES, D_MODEL), F32).at[:Bs].set(c).at[Bs].set(c_ctx)
    mod = _modulation(cvec, ada_w, ada_b)
    rope = _rope_tables(Ls)
    dft = _dft_matrices()
    w_in_b = w_in.astype(BF16)
    w_out_b = w_out.astype(BF16)
    fg = final_g.reshape(1, D_MODEL)

    xc, xl = x_prompt, x_sample
    new_ret, new_lru = [], []
    for l in range(DEPTH):
        shift, scale, gate = (mod[l, :, i * D_MODEL:(i + 1) * D_MODEL] for i in range(3))
        g = norm_g[l].reshape(1, D_MODEL)
        final = l == DEPTH - 1
        filt = {}
        for L in sorted({Lp, Ls}):
            filt[L] = _hyena_filter_spectra(L, hy_ffn_w1[l], hy_ffn_b1[l], hy_ffn_w2[l], hy_ffn_b2[l],
                                            hy_ffn_w3[l], hy_freq[l], hy_bias[l], dft[:2])

        def layer(x, sel, rope_t, ret_s0, lru_s0):
            sh, sc, gt = (m[sel][:, None, :] for m in (shift, scale, gate))
            z, z_lru = _in_proj(x, sh, sc, g, w_in_b[l])
            y_ret, s_ret = _retention(z, ret_decay_logit[l], rope_t, ret_s0)
            kre, kim = filt[x.shape[1]]
            y_hy = _hyena(z, hy_conv_w[l], hy_conv_b[l], kre, kim, dft)
            y_lru, s_lru = _rglru(z_lru, lru_conv_w[l], lru_conv_b[l], lru_gate_w[l], lru_gate_b[l],
                                  lru_lambda[l], lru_s0)
            return _out_proj(y_ret, y_hy, y_lru, x, gt, w_out_b[l], fg, final), s_ret, s_lru

        xc, rs, ls = layer(xc, slice(Bs, Bs + 1), None, None, None)
        new_ret.append(rs)
        new_lru.append(ls)
        xl, _, _ = layer(xl, slice(0, Bs), rope, state_ret[:, l], state_lru[:, l])

    new_state_ret = jnp.stack(new_ret, axis=1).astype(x_prompt.dtype)
    new_state_lru = jnp.stack(new_lru, axis=1).astype(x_prompt.dtype)
    return (xc, xl, new_state_ret, new_state_lru)
```

```python
import functools
import math

import numpy as np
import jax
import jax.numpy as jnp
from jax import lax
from jax.experimental import pallas as pl
from jax.experimental.pallas import tpu as pltpu

F32 = jnp.float32
BF16 = jnp.bfloat16
HI = lax.Precision.HIGHEST

D_MODEL = 1024
DEPTH = 2
GRID_W = 64
EPS = 1e-6
RET_HEADS = 4
RET_HEAD_DIM = 128
RET_WIDTH = RET_HEADS * RET_HEAD_DIM
ROPE_BASE = 10000.0
HY_WIDTH = 512
HY_ORDER = 2
HY_SHORT = 3
HY_EMB = 33
HY_FFN = 64
HY_SHORT_DECAY_PCT = 0.3
HY_LONG_DECAY_PCT = 1.5
HY_TARGET = 1e-2
LRU_WIDTH = 512
LRU_BLOCKS = 8
LRU_BLOCK_DIM = LRU_WIDTH // LRU_BLOCKS
LRU_CONV = 4
LRU_C = 8.0
D_MIX = RET_WIDTH + HY_WIDTH + LRU_WIDTH
D_IN = 4 * RET_WIDTH + 4 * HY_WIDTH + 2 * LRU_WIDTH

LANES = 128
SUBLANES = 8
DFT_T = 256
ROW_TILE = 256
VMEM_LIMIT = 56 * 1024 * 1024

_Q0, _K0, _V0, _GR0 = 0, 4, 8, 12
_HV0, _HX10, _HX20, _GH0 = 16, 20, 24, 28


def _sigmoid(x):
    return 0.5 * jnp.tanh(0.5 * x) + 0.5


def _silu(x):
    return x * _sigmoid(x)


def _softplus(x):
    return jnp.maximum(x, 0.0) + jnp.log1p(jnp.exp(-jnp.abs(x)))


def _params(sem):
    return pltpu.CompilerParams(dimension_semantics=sem, vmem_limit_bytes=VMEM_LIMIT)


def _mod_kernel(c_ref, w_ref, b_ref, o_ref):
    c = c_ref[...]
    o_ref[0] = jnp.dot(_silu(c), w_ref[0], precision=HI, preferred_element_type=F32) + b_ref[0]


def _modulation(cvec, ada_w, ada_b):
    tn = 768
    return pl.pallas_call(
        _mod_kernel,
        grid=(DEPTH, 3 * D_MODEL // tn),
        in_specs=[pl.BlockSpec((SUBLANES, D_MODEL), lambda l, j: (0, 0)),
                  pl.BlockSpec((1, D_MODEL, tn), lambda l, j: (l, 0, j)),
                  pl.BlockSpec((1, 1, tn), lambda l, j: (l, 0, j))],
        out_specs=pl.BlockSpec((1, SUBLANES, tn), lambda l, j: (l, 0, j)),
        out_shape=jax.ShapeDtypeStruct((DEPTH, SUBLANES, 3 * D_MODEL), F32),
        compiler_params=_params(("arbitrary", "arbitrary")),
        name="modulation",
    )(cvec, ada_w, ada_b.reshape(DEPTH, 1, 3 * D_MODEL))


D_MAIN = 4 * RET_WIDTH + 4 * HY_WIDTH
D_LRU = 2 * LRU_WIDTH


def _seg_case(L):
    S = L // SUBLANES
    assert S == ROW_TILE or L == ROW_TILE
    return S


def _in_kernel(x_ref, sh_ref, sc_ref, g_ref, w_ref, o_ref, ol_ref, *, S):
    x = x_ref[0]
    ms = jnp.mean(x * x, axis=-1, keepdims=True)
    y = x * lax.rsqrt(ms + EPS) * g_ref[...]
    h = (y * (1.0 + sc_ref[0]) + sh_ref[0]).astype(BF16)
    tn = 1024
    for n in range(D_MAIN // tn):
        o_ref[0, :, n * tn:(n + 1) * tn] = jnp.dot(h, w_ref[:, n * tn:(n + 1) * tn],
                                                   preferred_element_type=F32)
    zl = jnp.dot(h, w_ref[:, D_MAIN:D_IN], preferred_element_type=F32)
    if S == ROW_TILE:
        ol_ref[0] = zl
    else:
        for j in range(SUBLANES):
            ol_ref[0, :, j * D_LRU:(j + 1) * D_LRU] = zl[j * S:(j + 1) * S]


def _in_proj(x, shift, scale, g, w_bf16):
    B, L, _ = x.shape
    S = _seg_case(L)
    per_batch = shift.shape[0] > 1
    mod_map = (lambda b, i: (b, 0, 0)) if per_batch else (lambda b, i: (0, 0, 0))
    if S == ROW_TILE:
        lru_spec = pl.BlockSpec((1, ROW_TILE, D_LRU), lambda b, i: (b, 0, i))
    else:
        lru_spec = pl.BlockSpec((1, S, SUBLANES * D_LRU), lambda b, i: (b, 0, 0))
    return pl.pallas_call(
        functools.partial(_in_kernel, S=S),
        grid=(B, L // ROW_TILE),
        in_specs=[pl.BlockSpec((1, ROW_TILE, D_MODEL), lambda b, i: (b, i, 0)),
                  pl.BlockSpec((1, 1, D_MODEL), mod_map),
                  pl.BlockSpec((1, 1, D_MODEL), mod_map),
                  pl.BlockSpec((1, D_MODEL), lambda b, i: (0, 0)),
                  pl.BlockSpec((D_MODEL, D_IN), lambda b, i: (0, 0), pipeline_mode=pl.Buffered(1))],
        out_specs=[pl.BlockSpec((1, ROW_TILE, D_MAIN), lambda b, i: (b, i, 0)), lru_spec],
        out_shape=[jax.ShapeDtypeStruct((B, L, D_MAIN), F32),
                   jax.ShapeDtypeStruct((B, S, SUBLANES * D_LRU), F32)],
        compiler_params=_params(("arbitrary", "arbitrary")),
        name="in_proj",
    )(x, shift, scale, g, w_bf16)


OUT_ROWS = 512


def _out_kernel(yr_ref, yh_ref, yl_ref, x_ref, gate_ref, w_ref, fg_ref, o_ref, *, final, nseg):
    acc = jnp.dot(yr_ref[0], w_ref[0:RET_WIDTH], preferred_element_type=F32)
    acc = acc + jnp.dot(yh_ref[0], w_ref[RET_WIDTH:RET_WIDTH + HY_WIDTH], preferred_element_type=F32)
    yl = yl_ref[0]
    if nseg > 1:
        yl = jnp.concatenate([yl[:, j * LRU_WIDTH:(j + 1) * LRU_WIDTH] for j in range(nseg)], axis=0)
    acc = acc + jnp.dot(yl.astype(BF16), w_ref[RET_WIDTH + HY_WIDTH:D_MIX], preferred_element_type=F32)
    x = x_ref[0] + gate_ref[0] * acc
    if final:
        ms = jnp.mean(x * x, axis=-1, keepdims=True)
        x = x * lax.rsqrt(ms + EPS) * fg_ref[...]
    o_ref[0] = x


def _out_proj(y_ret, y_hy, y_lru, x, gate, w_bf16, final_g, final):
    B, L, _ = x.shape
    S = L // SUBLANES
    rows = min(OUT_ROWS, L)
    nseg = rows // S
    assert nseg * S == rows and L % rows == 0
    per_batch = gate.shape[0] > 1
    mod_map = (lambda b, i: (b, 0, 0)) if per_batch else (lambda b, i: (0, 0, 0))
    yspec = pl.BlockSpec((1, rows, 512), lambda b, i: (b, i, 0))
    lspec = pl.BlockSpec((1, S, nseg * LRU_WIDTH), lambda b, i: (b, 0, i))
    return pl.pallas_call(
        functools.partial(_out_kernel, final=final, nseg=nseg),
        grid=(B, L // rows),
        in_specs=[yspec, yspec, lspec,
                  pl.BlockSpec((1, rows, D_MODEL), lambda b, i: (b, i, 0)),
                  pl.BlockSpec((1, 1, D_MODEL), mod_map),
                  pl.BlockSpec((D_MIX, D_MODEL), lambda b, i: (0, 0), pipeline_mode=pl.Buffered(1)),
                  pl.BlockSpec((1, D_MODEL), lambda b, i: (0, 0))],
        out_specs=pl.BlockSpec((1, rows, D_MODEL), lambda b, i: (b, i, 0)),
        out_shape=jax.ShapeDtypeStruct((B, L, D_MODEL), F32),
        compiler_params=_params(("arbitrary", "arbitrary")),
        name="out_proj",
    )(y_ret, y_hy, y_lru, x, gate, w_bf16, final_g)


RET_HEAD_ROWS = 2 * 2048
RET_BLOCK = 256


def _ret_kernel(*refs, L, use_rope, has_state):
    it = iter(refs)
    q_ref, k_ref, v_ref, g_ref = next(it), next(it), next(it), next(it)
    cos_ref = sin_ref = s0_ref = None
    if use_rope:
        cos_ref, sin_ref = next(it), next(it)
    lg_ref = next(it)
    if has_state:
        s0_ref = next(it)
    y_ref, sN_ref, o_ref, R_ref, M_ref, Z_ref = (next(it) for _ in range(6))

    C = min(RET_BLOCK, L)
    DH = RET_HEAD_DIM
    HP = q_ref.shape[-1] // DH
    n = L // C
    cross = has_state or n > 1
    ii = lax.broadcasted_iota(jnp.int32, (C, C), 0)
    jj = lax.broadcasted_iota(jnp.int32, (C, C), 1)
    ri = lax.broadcasted_iota(jnp.int32, (C, DH), 0).astype(F32)
    kscale = DH ** -0.5

    @pl.when(pl.program_id(1) == 0)
    def _():
        for d in (0, 1):
            for hh in range(HP):
                lg = -_softplus(-lg_ref[d, hh])
                diff = ((ii - jj) if d == 0 else (jj - ii)).astype(F32)
                M_ref[d, hh] = jnp.where(diff >= 0, jnp.exp(lg[:, :C] * jnp.maximum(diff, 0.0)), 0.0)
                lgh = lg[:, :DH]
                Z_ref[d, hh, 0] = jnp.exp(lgh * ((C - 1.0 - ri) if d == 0 else ri))
                Z_ref[d, hh, 1] = jnp.exp(lgh * ((ri + 1.0) if d == 0 else (C - ri)))
                Z_ref[d, hh, 2] = jnp.exp(jnp.broadcast_to(lgh * C, (C, DH)))

    for d in (0, 1):
        for hh in range(HP):
            if has_state:
                R_ref[d, hh] = s0_ref[0, d, hh]
            else:
                R_ref[d, hh] = jnp.zeros((DH, DH), F32)

    def body(t, carry):
        for d in (0, 1):
            c = t if d == 0 else n - 1 - t
            r0 = pl.multiple_of(c * C, C)
            for hh in range(HP):
                lanes = slice(hh * DH, (hh + 1) * DH)
                q = q_ref[0, pl.ds(r0, C), lanes]
                k = k_ref[0, pl.ds(r0, C), lanes] * kscale
                vb = v_ref[0, pl.ds(r0, C), lanes].astype(BF16)
                if use_rope:
                    cs = cos_ref[pl.ds(r0, C), :]
                    sn = sin_ref[pl.ds(r0, C), :]
                    q = q * cs + pltpu.roll(q, 64, axis=1) * sn
                    k = k * cs + pltpu.roll(k, 64, axis=1) * sn
                s = lax.dot_general(q.astype(BF16), k.astype(BF16), (((1,), (1,)), ((), ())),
                                    preferred_element_type=F32) * M_ref[d, hh]
                o = jnp.dot(s.astype(BF16), vb, preferred_element_type=F32)
                R = R_ref[d, hh]
                if cross:
                    o = o + jnp.dot((q * Z_ref[d, hh, 1]).astype(BF16), R.astype(BF16),
                                    preferred_element_type=F32)
                kv = lax.dot_general((k * Z_ref[d, hh, 0]).astype(BF16), vb, (((0,), (0,)), ((), ())),
                                     preferred_element_type=F32)
                R_ref[d, hh] = Z_ref[d, hh, 2, 0:DH, :] * R + kv
                o_ref[d, pl.ds(r0, C), lanes] = o
        return carry

    lax.fori_loop(0, n, body, 0, unroll=2 if n % 2 == 0 else 1)
    for d in (0, 1):
        for hh in range(HP):
            sN_ref[0, d, hh] = R_ref[d, hh]

    def finish(c, carry):
        r0 = pl.multiple_of(c * DH, DH)
        tot = o_ref[0, pl.ds(r0, DH), :] + o_ref[1, pl.ds(r0, DH), :]
        g = g_ref[0, pl.ds(r0, DH), :]
        cols = []
        for hh in range(HP):
            th = tot[:, hh * DH:(hh + 1) * DH]
            cols.append(th * lax.rsqrt(jnp.mean(th * th, axis=-1, keepdims=True) + EPS))
        y_ref[0, pl.ds(r0, DH), :] = (jnp.concatenate(cols, axis=1) * _silu(g)).astype(BF16)
        return carry

    lax.fori_loop(0, L // DH, finish, 0)


def _retention(z, decay_logit, rope, state0):
    B, L, _ = z.shape
    use_rope = rope is not None
    has_state = state0 is not None
    H = RET_HEADS
    HP = min(H, max(1, RET_HEAD_ROWS // L))
    W = HP * RET_HEAD_DIM
    col = lambda off: pl.BlockSpec((1, L, W), lambda h, b, off=off: (b, 0, off * LANES // W + h))
    in_specs = [col(_Q0), col(_K0), col(_V0), col(_GR0)]
    args = [z, z, z, z]
    if use_rope:
        in_specs += [pl.BlockSpec((L, LANES), lambda h, b: (0, 0))] * 2
        args += list(rope)
    C = min(RET_BLOCK, L)
    in_specs.append(pl.BlockSpec((2, HP, 1, RET_BLOCK), lambda h, b: (0, h, 0, 0)))
    args.append(jnp.broadcast_to(decay_logit[:, :, None, None], (2, H, 1, RET_BLOCK)))
    sspec = pl.BlockSpec((1, 2, HP, RET_HEAD_DIM, RET_HEAD_DIM), lambda h, b: (b, 0, h, 0, 0))
    if has_state:
        in_specs.append(sspec)
        args.append(state0)
    return pl.pallas_call(
        functools.partial(_ret_kernel, L=L, use_rope=use_rope, has_state=has_state),
        grid=(H // HP, B),
        in_specs=in_specs,
        out_specs=[pl.BlockSpec((1, L, W), lambda h, b: (b, 0, h)), sspec],
        out_shape=[jax.ShapeDtypeStruct((B, L, RET_WIDTH), BF16),
                   jax.ShapeDtypeStruct((B, 2, H, RET_HEAD_DIM, RET_HEAD_DIM), F32)],
        scratch_shapes=[pltpu.VMEM((2, L, W), F32),
                        pltpu.VMEM((2, HP, RET_HEAD_DIM, RET_HEAD_DIM), F32),
                        pltpu.VMEM((2, HP, C, C), F32),
                        pltpu.VMEM((2, HP, 3, C, RET_HEAD_DIM), F32)],
        compiler_params=_params(("arbitrary", "arbitrary")),
        name="retention",
    )(*args)


def _dft_matrices():
    T = DFT_T
    N = 2 * T
    k = np.arange(T, dtype=np.float64)[:, None]
    m = np.arange(T, dtype=np.float64)[None, :]
    ang = 2.0 * np.pi * k * m / N
    fwd = np.concatenate([np.cos(ang), -np.sin(ang)], axis=0)
    fwd[T, :] = (-1.0) ** np.arange(T)
    ck = np.full((T,), 2.0)
    ck[0] = 1.0
    inv_re = (np.cos(ang) * ck[:, None]).T / N
    inv_im = (-2.0 * np.sin(ang)).T / N
    inv_im[:, 0] = ((-1.0) ** np.arange(T)) / N
    inv = np.concatenate([inv_re, inv_im], axis=1)
    out = []
    for m in (fwd, inv):
        m = jnp.asarray(m, F32)
        hi = m.astype(BF16)
        out += [hi, (m - hi.astype(F32)).astype(BF16)]
    return tuple(out)


def _split_bf16(x):
    hi = x.astype(BF16)
    return hi, (x - hi.astype(F32)).astype(BF16)


def _dot3(a_hi, a_lo, b):
    b_hi, b_lo = _split_bf16(b)
    return (jnp.dot(a_hi, b_hi, preferred_element_type=F32)
            + jnp.dot(a_lo, b_hi, preferred_element_type=F32)
            + jnp.dot(a_hi, b_lo, preferred_element_type=F32))


def _dot2(a_hi, a_lo, b):
    b = b.astype(BF16)
    return jnp.dot(a_hi, b, preferred_element_type=F32) + jnp.dot(a_lo, b, preferred_element_type=F32)


def _filt_kernel(zemb_ref, w1_ref, b1_ref, w2_ref, b2_ref, fr_ref, w3f_ref, w3b_ref, dec_ref,
                 bias_ref, fwdh_ref, fwdl_ref, kre_ref, kim_ref, hid_ref, hn_ref, F_ref, *, L):
    T = DFT_T
    nb = L // T

    @pl.when((pl.program_id(0) == 0) & (pl.program_id(1) == 0))
    def _():
        z1 = jnp.dot(zemb_ref[...], w1_ref[...], precision=HI, preferred_element_type=F32) + b1_ref[...]
        h1 = jnp.sin(fr_ref[0:1, :] * z1)
        z2 = jnp.dot(h1, w2_ref[...], precision=HI, preferred_element_type=F32) + b2_ref[...]
        hid = jnp.sin(fr_ref[1:2, :] * z2)
        hid_ref[0], hid_ref[1] = _split_bf16(hid)

    w3 = jnp.concatenate([w3f_ref[...], w3b_ref[...]], axis=1)
    w_hi, w_lo = _split_bf16(w3)
    h = (jnp.dot(hid_ref[0], w_hi, preferred_element_type=F32)
         + jnp.dot(hid_ref[1], w_hi, preferred_element_type=F32)
         + jnp.dot(hid_ref[0], w_lo, preferred_element_type=F32))
    h = h * jnp.concatenate([dec_ref[...]] * 2, axis=1)
    h = h / (jnp.sum(jnp.abs(h), axis=0, keepdims=True) + EPS)
    for dr in (0, 1):
        hn_ref[dr] = h[:, dr * LANES:(dr + 1) * LANES]
    for blk in range(nb):
        F = _dot3(fwdh_ref[...], fwdl_ref[...], h[blk * T:(blk + 1) * T])
        for dr in (0, 1):
            F_ref[dr, blk] = F[:, dr * LANES:(dr + 1) * LANES]

    row = lax.broadcasted_iota(jnp.int32, (T, LANES), 0)
    sgn = jnp.where((row & 1) == 0, 1.0, -1.0).astype(F32)
    row0 = row == 0
    bias = bias_ref[0]
    for d in range(-(nb - 1), nb):
        re = jnp.zeros((T, LANES), F32)
        im = jnp.zeros((T, LANES), F32)
        nyq = jnp.zeros((1, LANES), F32)
        if d >= 0:
            F = F_ref[0, d]
            re, im, nyq = re + F[:T], im + F[T:], nyq + F[T:T + 1]
        if d >= 1:
            F = F_ref[0, d - 1]
            p0 = hn_ref[0, (d - 1) * T:(d - 1) * T + 1, :]
            re, im, nyq = re + sgn * (F[:T] - p0), im + sgn * F[T:], nyq + (F[T:T + 1] - p0)
        e = -d
        if e >= 0:
            F = F_ref[1, e]
            re, im, nyq = re + F[:T], im - F[T:], nyq + F[T:T + 1]
        if e >= 1:
            F = F_ref[1, e - 1]
            p0 = hn_ref[1, (e - 1) * T:(e - 1) * T + 1, :]
            re, im, nyq = re + sgn * (F[:T] - p0), im - sgn * F[T:], nyq + (F[T:T + 1] - p0)
        if d == 0:
            re, nyq = re + bias, nyq + bias
        kre_ref[0, d + nb - 1] = re
        kim_ref[0, d + nb - 1] = jnp.where(row0, nyq, im)


def _hyena_filter_spectra(L, w1, b1, w2, b2, w3, freq, bias, fwd):
    T = DFT_T
    nb = L // T
    nlag = 2 * nb - 1
    P = LANES
    t = np.linspace(0.0, 1.0, L)[:, None]
    n_bands = (HY_EMB - 1) // 2
    f = np.linspace(1e-4, n_bands - 1, n_bands)
    ang = (2.0 * math.pi / L) * np.arange(L)[:, None] * f[None, :]
    zemb = np.concatenate([t, np.cos(ang), -np.sin(ang)], axis=-1)
    zemb = jnp.asarray(np.pad(zemb, ((0, 0), (0, P - HY_EMB))), F32)
    min_decay = math.log(HY_TARGET) / HY_LONG_DECAY_PCT
    max_decay = math.log(HY_TARGET) / HY_SHORT_DECAY_PCT
    deltas = np.abs(np.linspace(min_decay, max_decay, HY_WIDTH))
    dec = jnp.asarray(np.exp(-t * deltas[None, :]), F32)
    pad = P - HY_FFN
    w1p = jnp.pad(w1, ((0, P - HY_EMB), (0, pad)))
    b1p = jnp.pad(b1, (0, pad)).reshape(1, P)
    w2p = jnp.pad(w2, ((0, pad), (0, pad)))
    b2p = jnp.pad(b2, (0, pad)).reshape(1, P)
    w3p = jnp.pad(w3, ((0, pad), (0, 0)))
    frp = jnp.pad(freq, ((0, 0), (0, pad)))
    ncg = HY_WIDTH // P
    const = lambda shape: pl.BlockSpec(shape, lambda o, c: tuple(0 for _ in shape))
    kshape = jax.ShapeDtypeStruct((HY_ORDER, nlag, T, HY_WIDTH), F32)
    kspec = pl.BlockSpec((1, nlag, T, P), lambda o, c: (o, 0, 0, c))
    return pl.pallas_call(
        functools.partial(_filt_kernel, L=L),
        grid=(HY_ORDER, ncg),
        in_specs=[const((L, P)), const((P, P)), const((1, P)), const((P, P)), const((1, P)), const((2, P)),
                  pl.BlockSpec((P, P), lambda o, c: (0, o * 2 * ncg + c)),
                  pl.BlockSpec((P, P), lambda o, c: (0, o * 2 * ncg + ncg + c)),
                  pl.BlockSpec((L, P), lambda o, c: (0, c)),
                  pl.BlockSpec((1, 1, P), lambda o, c: (o, 0, c)),
                  const((2 * T, T)), const((2 * T, T))],
        out_specs=[kspec, kspec],
        out_shape=[kshape, kshape],
        scratch_shapes=[pltpu.VMEM((2, L, P), BF16), pltpu.VMEM((2, L, P), F32),
                        pltpu.VMEM((2, nb, 2 * T, P), F32)],
        compiler_params=_params(("arbitrary", "arbitrary")),
        name="hyena_filters",
    )(zemb, w1p, b1p, w2p, b2p, frp, w3p, w3p, dec, bias.reshape(HY_ORDER, 1, HY_WIDTH), *fwd)


def _conv_block(x_ref, j, nb, w, b, left):
    T = DFT_T
    t0 = j * T
    C = x_ref.shape[-1]
    blk = x_ref[t0:t0 + T, :]
    prev = x_ref[t0 - SUBLANES:t0, :] if j > 0 else jnp.zeros((SUBLANES, C), F32)
    nxt = x_ref[t0 + T:t0 + T + SUBLANES, :] if j < nb - 1 else jnp.zeros((SUBLANES, C), F32)
    ext = jnp.concatenate([prev, blk, nxt], axis=0)
    n = T + 2 * SUBLANES
    acc = jnp.broadcast_to(b, (T, C))
    for tap in range(w.shape[0]):
        s = tap - left
        sh = blk if s == 0 else pltpu.roll(ext, (-s) % n, axis=0)[SUBLANES:SUBLANES + T]
        acc = acc + w[tap:tap + 1, :] * sh
    return acc


HY_LANES_ROWS = 2 * 2048
HY_MAX_BATCH = 8


def _toeplitz(k, u, mul):
    n = len(u)
    if n == 1:
        return [mul(k[0], u[0])]
    h = n // 2
    add = lambda a, b: tuple(x + y for x, y in zip(a, b))
    sub = lambda a, b: tuple(x - y for x, y in zip(a, b))
    t0, t1, t2 = k[h:h + n - 1], k[0:n - 1], k[n:2 * n - 1]
    p = _toeplitz(t0, [add(a, b) for a, b in zip(u[:h], u[h:])], mul)
    q = _toeplitz([sub(a, b) for a, b in zip(t1, t0)], u[h:], mul)
    r = _toeplitz([sub(a, b) for a, b in zip(t2, t0)], u[:h], mul)
    return [add(a, b) for a, b in zip(p, q)] + [add(a, b) for a, b in zip(p, r)]


def _hy_kernel(v_ref, x1_ref, x2_ref, g_ref, wv_ref, w1_ref, w2_ref, bv_ref, b1_ref, b2_ref,
               kre_ref, kim_ref, fwdh_ref, fwdl_ref, invh_ref, invl_ref, y_ref,
               x1c_ref, x2c_ref, z1_ref, U_ref, Y_ref, *, L):
    T = DFT_T
    P = LANES
    nb = L // T
    NBT = v_ref.shape[0]
    left = (HY_SHORT - 1) // 2
    for j in range(nb):
        u = [_conv_block(v_ref.at[p], j, nb, wv_ref[...], bv_ref[...], left) for p in range(NBT)]
        U_ref[j] = _dot2(fwdh_ref[...], fwdl_ref[...], jnp.concatenate(u, axis=1))
        for p in range(NBT):
            x1c_ref[p, j * T:(j + 1) * T, :] = _conv_block(x1_ref.at[p], j, nb, w1_ref[...], b1_ref[...], left)
            x2c_ref[p, j * T:(j + 1) * T, :] = _conv_block(x2_ref.at[p], j, nb, w2_ref[...], b2_ref[...], left)

    row0 = lax.broadcasted_iota(jnp.int32, (SUBLANES, NBT * P), 0) == 0
    tile = lambda x: jnp.concatenate([x] * NBT, axis=1)

    def cmul(k, u):
        kr, ki = tile(k[0]), tile(k[1])
        return (kr * u[0] - ki * u[1], kr * u[1] + ki * u[0])

    def rmul(k, u):
        return (tile(k[0]) * u[0], tile(k[1]) * u[1])

    def spectra_products(o):
        def load(r, rows):
            k = [(kre_ref[o, l, pl.ds(r, rows), :], kim_ref[o, l, pl.ds(r, rows), :]) for l in range(2 * nb - 1)]
            u = [(U_ref[j, pl.ds(r, rows), :], U_ref[j, pl.ds(T + r, rows), :]) for j in range(nb)]
            return k, u

        def chunk(rc, carry):
            r = pl.multiple_of(rc * SUBLANES, SUBLANES)
            y = _toeplitz(*load(r, SUBLANES), cmul)
            for i in range(nb):
                Y_ref[i, pl.ds(r, SUBLANES), :] = y[i][0]
                Y_ref[i, pl.ds(T + r, SUBLANES), :] = y[i][1]
            return carry

        lax.fori_loop(0, T // SUBLANES, chunk, 0)
        y = _toeplitz(*load(0, SUBLANES), rmul)
        for i in range(nb):
            Y_ref[i, 0:SUBLANES, :] = jnp.where(row0, y[i][0], Y_ref[i, 0:SUBLANES, :])
            Y_ref[i, T:T + SUBLANES, :] = jnp.where(row0, y[i][1], Y_ref[i, T:T + SUBLANES, :])

    def long_conv_block(i):
        return _dot2(invh_ref[...], invl_ref[...], Y_ref[i])

    spectra_products(0)

    def order0(i, carry):
        r0 = pl.multiple_of(i * T, T)
        y = long_conv_block(i)
        for p in range(NBT):
            z1_ref[p, pl.ds(r0, T), :] = x1c_ref[p, pl.ds(r0, T), :] * y[:, p * P:(p + 1) * P]
        return carry

    lax.fori_loop(0, nb, order0, 0, unroll=2 if nb % 2 == 0 else 1)
    for j in range(nb):
        zc = jnp.concatenate([z1_ref[p, j * T:(j + 1) * T, :] for p in range(NBT)], axis=1)
        U_ref[j] = _dot2(fwdh_ref[...], fwdl_ref[...], zc)
    spectra_products(1)

    def order1(i, carry):
        r0 = pl.multiple_of(i * T, T)
        y = long_conv_block(i)
        for p in range(NBT):
            out = x2c_ref[p, pl.ds(r0, T), :] * y[:, p * P:(p + 1) * P]
            y_ref[p, pl.ds(r0, T), :] = (out * _silu(g_ref[p, pl.ds(r0, T), :])).astype(BF16)
        return carry

    lax.fori_loop(0, nb, order1, 0, unroll=2 if nb % 2 == 0 else 1)


def _hyena(z, conv_w, conv_b, kre, kim, dft):
    B, L, _ = z.shape
    NBT = min(B, max(2, HY_LANES_ROWS // L), HY_MAX_BATCH)
    assert B % NBT == 0
    T = DFT_T
    nb = L // T
    nlag = 2 * nb - 1
    P = LANES
    ncg = HY_WIDTH // P
    col = lambda off: pl.BlockSpec((NBT, L, P), lambda c, b, off=off: (b, 0, off + c))
    wspec = lambda part: pl.BlockSpec((HY_SHORT, P), lambda c, b, part=part: (0, part * ncg + c))
    bspec = lambda part: pl.BlockSpec((1, P), lambda c, b, part=part: (0, part * ncg + c))
    kspec = pl.BlockSpec((HY_ORDER, nlag, T, P), lambda c, b: (0, 0, 0, c))
    fspec = pl.BlockSpec((2 * T, T), lambda c, b: (0, 0))
    ispec = pl.BlockSpec((T, 2 * T), lambda c, b: (0, 0))
    cb = conv_b.reshape(1, 3 * HY_WIDTH)
    return pl.pallas_call(
        functools.partial(_hy_kernel, L=L),
        grid=(ncg, B // NBT),
        in_specs=[col(_HV0), col(_HX10), col(_HX20), col(_GH0),
                  wspec(0), wspec(1), wspec(2), bspec(0), bspec(1), bspec(2),
                  kspec, kspec, fspec, fspec, ispec, ispec],
        out_specs=pl.BlockSpec((NBT, L, P), lambda c, b: (b, 0, c)),
        out_shape=jax.ShapeDtypeStruct((B, L, HY_WIDTH), BF16),
        scratch_shapes=[pltpu.VMEM((NBT, L, P), F32), pltpu.VMEM((NBT, L, P), F32),
                        pltpu.VMEM((NBT, L, P), F32),
                        pltpu.VMEM((nb, 2 * T, NBT * P), F32), pltpu.VMEM((nb, 2 * T, NBT * P), F32)],
        compiler_params=_params(("arbitrary", "arbitrary")),
        name="hyena",
    )(z, z, z, z, conv_w, conv_w, conv_w, cb, cb, cb, kre, kim, *dft)


LRU_TILE = 256
LRU_GROUPS = LRU_TILE // LANES
LRU_ROWS = 128


def _lru_kernel(*refs, L, has_state):
    it = iter(refs)
    x_ref, g_ref, cw_ref, cb_ref, wg_ref, gb_ref, lam_ref = (next(it) for _ in range(7))
    h0_ref = next(it) if has_state else None
    y_ref, sN_ref, xp_ref, A_ref, B_ref, H_ref, C_ref = (next(it) for _ in range(7))

    S = L // SUBLANES
    W = LRU_TILE
    G = LRU_GROUPS
    RT = LRU_ROWS // SUBLANES
    left = LRU_CONV // 2

    sub = lax.broadcasted_iota(jnp.int32, (SUBLANES, W), 0)
    for k in range(left):
        tail = pltpu.roll(x_ref[0, S - left + k], 1, axis=0)
        xp_ref[k] = jnp.where(sub == 0, 0.0, tail)
    for k in range(LRU_CONV - 1 - left):
        head = pltpu.roll(x_ref[0, k], SUBLANES - 1, axis=0)
        xp_ref[left + S + k] = jnp.where(sub == SUBLANES - 1, 0.0, head)

    def copy(c, carry):
        i0 = pl.multiple_of(c * RT, RT)
        xp_ref[pl.ds(left + i0, RT)] = x_ref[0, pl.ds(i0, RT)]
        return carry

    lax.fori_loop(0, S // RT, copy, 0)

    csp = [(0.5 * LRU_C) * _softplus(-lam_ref[d]) for d in (0, 1)]

    def coeffs(c, carry):
        i0 = pl.multiple_of(c * RT, RT)
        u = jnp.broadcast_to(cb_ref[...], (RT, SUBLANES, W))
        for k in range(LRU_CONV):
            u = u + cw_ref[k:k + 1, :] * xp_ref[pl.ds(i0 + k, RT)]
        u = u.reshape(RT * SUBLANES, W)
        half = jnp.dot(u.astype(BF16), wg_ref[0], preferred_element_type=F32) + gb_ref[0]
        hu = 0.5 * u
        for d in (0, 1):
            t_r = jnp.tanh(half[:, (2 * d) * W:(2 * d + 1) * W])
            t_i = jnp.tanh(half[:, (2 * d + 1) * W:(2 * d + 2) * W])
            nl = csp[d] * t_r + csp[d]
            a = jnp.exp(-nl)
            b = jnp.sqrt(jnp.tanh(nl) * (a * a + 1.0)) * (t_i * hu + hu)
            for gi in range(G):
                lanes = slice(gi * LANES, (gi + 1) * LANES)
                A_ref[d, gi, pl.ds(i0, RT)] = a[:, lanes].reshape(RT, SUBLANES, LANES)
                B_ref[d, gi, pl.ds(i0, RT)] = b[:, lanes].reshape(RT, SUBLANES, LANES)
        return carry

    lax.fori_loop(0, S // RT, coeffs, 0)

    def scan_body(t, carry):
        out = []
        for d in (0, 1):
            i = t if d == 0 else S - 1 - t
            for gi in range(G):
                h, acc = carry[d * G + gi]
                a = A_ref[d, gi, i]
                h = a * h + B_ref[d, gi, i]
                acc = acc * a
                H_ref[d, gi, i] = h
                C_ref[d, gi, i] = acc
                out.append((h, acc))
        return tuple(out)

    init = tuple((jnp.zeros((SUBLANES, LANES), F32), jnp.ones((SUBLANES, LANES), F32))
                 for _ in range(2 * G))
    lax.fori_loop(0, S, scan_body, init, unroll=4)

    hin = [None] * (2 * G)
    for d in (0, 1):
        last = S - 1 if d == 0 else 0
        for gi in range(G):
            hl = H_ref[d, gi, last]
            ac = C_ref[d, gi, last]
            if has_state:
                h = h0_ref[0, d:d + 1, gi * LANES:(gi + 1) * LANES]
            else:
                h = jnp.zeros((1, LANES), F32)
            rows = [None] * SUBLANES
            for j in (range(SUBLANES) if d == 0 else range(SUBLANES - 1, -1, -1)):
                rows[j] = h
                h = hl[j:j + 1, :] + ac[j:j + 1, :] * h
            hin[d * G + gi] = jnp.concatenate(rows, axis=0)
            sN_ref[0, d:d + 1, gi * LANES:(gi + 1) * LANES] = h

    def combine(c, carry):
        i0 = pl.multiple_of(c * RT, RT)
        cols = []
        for gi in range(G):
            tot = jnp.zeros((RT, SUBLANES, LANES), F32)
            for d in (0, 1):
                tot = tot + (H_ref[d, gi, pl.ds(i0, RT)] + C_ref[d, gi, pl.ds(i0, RT)] * hin[d * G + gi])
            cols.append(tot)
        y_ref[0, pl.ds(i0, RT)] = jnp.concatenate(cols, axis=2) * _silu(g_ref[0, pl.ds(i0, RT)])
        return carry

    lax.fori_loop(0, S // RT, combine, 0)


def _rglru(z, conv_w, conv_b, gate_w, gate_b, lam, h0):
    B, S, _ = z.shape
    L = S * SUBLANES
    has_state = h0 is not None
    W = LRU_TILE
    nh = LRU_WIDTH // W
    bpt = W // LRU_BLOCK_DIM
    gw = gate_w.reshape(2, 2, nh, bpt, LRU_BLOCK_DIM, LRU_BLOCK_DIM)
    eye = jnp.eye(bpt, dtype=F32)
    dense = (0.5 * jnp.einsum('dkhnij,nm->hnidkmj', gw, eye)).reshape(nh, W, 4 * W).astype(BF16)
    gb = 0.5 * gate_b.reshape(2, 2, nh, W).transpose(2, 0, 1, 3).reshape(nh, 1, 4 * W)
    z4 = z.reshape(B, S, SUBLANES, D_LRU)
    off = lambda base: pl.BlockSpec((1, S, SUBLANES, W), lambda b, h, base=base: (b, 0, 0, base + h))
    in_specs = [off(0), off(LRU_WIDTH // W),
                pl.BlockSpec((LRU_CONV, W), lambda b, h: (0, h)),
                pl.BlockSpec((1, W), lambda b, h: (0, h)),
                pl.BlockSpec((1, W, 4 * W), lambda b, h: (h, 0, 0)),
                pl.BlockSpec((1, 1, 4 * W), lambda b, h: (h, 0, 0)),
                pl.BlockSpec((2, 1, W), lambda b, h: (0, 0, h))]
    args = [z4, z4, conv_w, conv_b.reshape(1, LRU_WIDTH), dense, gb, lam.reshape(2, 1, LRU_WIDTH)]
    if has_state:
        in_specs.append(pl.BlockSpec((1, 2, W), lambda b, h: (b, 0, h)))
        args.append(h0)
    y, s = pl.pallas_call(
        functools.partial(_lru_kernel, L=L, has_state=has_state),
        grid=(B, nh),
        in_specs=in_specs,
        out_specs=[pl.BlockSpec((1, S, SUBLANES, W), lambda b, h: (b, 0, 0, h)),
                   pl.BlockSpec((1, 2, W), lambda b, h: (b, 0, h))],
        out_shape=[jax.ShapeDtypeStruct((B, S, SUBLANES, LRU_WIDTH), F32),
                   jax.ShapeDtypeStruct((B, 2, LRU_WIDTH), F32)],
        scratch_shapes=[pltpu.VMEM((S + LRU_CONV - 1, SUBLANES, W), F32)]
        + [pltpu.VMEM((2, LRU_GROUPS, S, SUBLANES, LANES), F32) for _ in range(4)],
        compiler_params=_params(("arbitrary", "arbitrary")),
        name="rglru",
    )(*args)
    return y.reshape(B, S, SUBLANES * LRU_WIDTH), s


def _rope_tables(L):
    rows = L // GRID_W
    row = np.repeat(np.arange(rows, dtype=np.float64), GRID_W)
    col = np.tile(np.arange(GRID_W, dtype=np.float64), rows)
    n_f = RET_HEAD_DIM // 4
    inv = ROPE_BASE ** (-np.arange(n_f, dtype=np.float64) / n_f)
    ang = np.concatenate([row[:, None] * inv[None], col[:, None] * inv[None]], axis=-1)
    cos, sin = np.cos(ang), np.sin(ang)
    return (jnp.asarray(np.concatenate([cos, cos], axis=-1), F32),
            jnp.asarray(np.concatenate([-sin, sin], axis=-1), F32))


def kernel(x_prompt, x_sample, state_ret, state_lru, c, c_ctx, norm_g, ada_w, ada_b, w_in, ret_decay_logit, hy_conv_w, hy_conv_b, hy_ffn_w1, hy_ffn_b1, hy_ffn_w2, hy_ffn_b2, hy_ffn_w3, hy_freq, hy_bias, lru_conv_w, lru_conv_b, lru_gate_w, lru_gate_b, lru_lambda, w_out, final_g):
    Bp, Lp, _ = x_prompt.shape
    Bs, Ls, _ = x_sample.shape
    assert Lp % DFT_T == 0 and Ls % DFT_T == 0 and Bs + 1 <= SUBLANES

    cvec = jnp.zeros((SUBLANES, D_MODEL), F32).at[:Bs].set(c).at[Bs].set(c_ctx)
    mod = _modulation(cvec, ada_w, ada_b)
    rope = _rope_tables(Ls)
    dft = _dft_matrices()
    w_in_b = w_in.astype(BF16)
    w_out_b = w_out.astype(BF16)
    fg = final_g.reshape(1, D_MODEL)

    xc, xl = x_prompt, x_sample
    new_ret, new_lru = [], []
    for l in range(DEPTH):
        shift, scale, gate = (mod[l, :, i * D_MODEL:(i + 1) * D_MODEL] for i in range(3))
        g = norm_g[l].reshape(1, D_MODEL)
        final = l == DEPTH - 1
        filt = {}
        for L in sorted({Lp, Ls}):
            filt[L] = _hyena_filter_spectra(L, hy_ffn_w1[l], hy_ffn_b1[l], hy_ffn_w2[l], hy_ffn_b2[l],
                                            hy_ffn_w3[l], hy_freq[l], hy_bias[l], dft[:2])

        def layer(x, sel, rope_t, ret_s0, lru_s0):
            sh, sc, gt = (m[sel][:, None, :] for m in (shift, scale, gate))
            z, z_lru = _in_proj(x, sh, sc, g, w_in_b[l])
            y_ret, s_ret = _retention(z, ret_decay_logit[l], rope_t, ret_s0)
            kre, kim = filt[x.shape[1]]
            y_hy = _hyena(z, hy_conv_w[l], hy_conv_b[l], kre, kim, dft)
            y_lru, s_lru = _rglru(z_lru, lru_conv_w[l], lru_conv_b[l], lru_gate_w[l], lru_gate_b[l],
                                  lru_lambda[l], lru_s0)
            return _out_proj(y_ret, y_hy, y_lru, x, gt, w_out_b[l], fg, final), s_ret, s_lru

        xc, rs, ls = layer(xc, slice(Bs, Bs + 1), None, None, None)
        new_ret.append(rs)
        new_lru.append(ls)
        xl, _, _ = layer(xl, slice(0, Bs), rope, state_ret[:, l], state_lru[:, l])

    new_state_ret = jnp.stack(new_ret, axis=1).astype(x_prompt.dtype)
    new_state_lru = jnp.stack(new_lru, axis=1).astype(x_prompt.dtype)
    return (xc, xl, new_state_ret, new_state_lru)
```

```python
import functools
import math

import numpy as np
import jax
import jax.numpy as jnp
from jax import lax
from jax.experimental import pallas as pl
from jax.experimental.pallas import tpu as pltpu

F32 = jnp.float32
BF16 = jnp.bfloat16
HI = lax.Precision.HIGHEST

D_MODEL = 1024
DEPTH = 2
GRID_W = 64
EPS = 1e-6
RET_HEADS = 4
RET_HEAD_DIM = 128
RET_WIDTH = RET_HEADS * RET_HEAD_DIM
ROPE_BASE = 10000.0
HY_WIDTH = 512
HY_ORDER = 2
HY_SHORT = 3
HY_EMB = 33
HY_FFN = 64
HY_SHORT_DECAY_PCT = 0.3
HY_LONG_DECAY_PCT = 1.5
HY_TARGET = 1e-2
LRU_WIDTH = 512
LRU_BLOCKS = 8
LRU_BLOCK_DIM = LRU_WIDTH // LRU_BLOCKS
LRU_CONV = 4
LRU_C = 8.0
D_MIX = RET_WIDTH + HY_WIDTH + LRU_WIDTH
D_IN = 4 * RET_WIDTH + 4 * HY_WIDTH + 2 * LRU_WIDTH

LANES = 128
SUBLANES = 8
DFT_T = 256
ROW_TILE = 256
VMEM_LIMIT = 56 * 1024 * 1024

_Q0, _K0, _V0, _GR0 = 0, 4, 8, 12
_HV0, _HX10, _HX20, _GH0 = 16, 20, 24, 28


def _sigmoid(x):
    return 0.5 * jnp.tanh(0.5 * x) + 0.5


def _silu(x):
    return x * _sigmoid(x)


def _softplus(x):
    return jnp.maximum(x, 0.0) + jnp.log1p(jnp.exp(-jnp.abs(x)))


def _params(sem):
    return pltpu.CompilerParams(dimension_semantics=sem, vmem_limit_bytes=VMEM_LIMIT)


def _mod_kernel(c_ref, w_ref, b_ref, o_ref):
    c = c_ref[...]
    o_ref[0] = jnp.dot(_silu(c), w_ref[0], precision=HI, preferred_element_type=F32) + b_ref[0]


def _modulation(cvec, ada_w, ada_b):
    tn = 768
    return pl.pallas_call(
        _mod_kernel,
        grid=(DEPTH, 3 * D_MODEL // tn),
        in_specs=[pl.BlockSpec((SUBLANES, D_MODEL), lambda l, j: (0, 0)),
                  pl.BlockSpec((1, D_MODEL, tn), lambda l, j: (l, 0, j)),
                  pl.BlockSpec((1, 1, tn), lambda l, j: (l, 0, j))],
        out_specs=pl.BlockSpec((1, SUBLANES, tn), lambda l, j: (l, 0, j)),
        out_shape=jax.ShapeDtypeStruct((DEPTH, SUBLANES, 3 * D_MODEL), F32),
        compiler_params=_params(("arbitrary", "arbitrary")),
        name="modulation",
    )(cvec, ada_w, ada_b.reshape(DEPTH, 1, 3 * D_MODEL))


D_MAIN = 4 * RET_WIDTH + 4 * HY_WIDTH
D_LRU = 2 * LRU_WIDTH


def _in_kernel(x_ref, sh_ref, sc_ref, g_ref, w_ref, o_ref, ol_ref):
    _, nseg, rows, _ = x_ref.shape
    x = x_ref[0].reshape(nseg * rows, D_MODEL)
    ms = jnp.mean(x * x, axis=-1, keepdims=True)
    y = x * lax.rsqrt(ms + EPS) * g_ref[...]
    h = (y * (1.0 + sc_ref[0]) + sh_ref[0]).astype(BF16)
    tn = 1024
    for n in range(D_MAIN // tn):
        z = jnp.dot(h, w_ref[:, n * tn:(n + 1) * tn], preferred_element_type=F32)
        o_ref[0, :, :, n * tn:(n + 1) * tn] = z.reshape(nseg, rows, tn)
    zl = jnp.dot(h, w_ref[:, D_MAIN:D_IN], preferred_element_type=F32)
    for j in range(nseg):
        ol_ref[0, :, j, :] = zl[j * rows:(j + 1) * rows]


def _in_proj(x, shift, scale, g, w_bf16):
    B, L, _ = x.shape
    S = L // SUBLANES
    rows = min(ROW_TILE // SUBLANES, S)
    per_batch = shift.shape[0] > 1
    mod_map = (lambda b, i: (b, 0, 0)) if per_batch else (lambda b, i: (0, 0, 0))
    seg = lambda C: pl.BlockSpec((1, SUBLANES, rows, C), lambda b, i: (b, 0, i, 0))
    z, z_lru = pl.pallas_call(
        _in_kernel,
        grid=(B, S // rows),
        in_specs=[seg(D_MODEL),
                  pl.BlockSpec((1, 1, D_MODEL), mod_map),
                  pl.BlockSpec((1, 1, D_MODEL), mod_map),
                  pl.BlockSpec((1, D_MODEL), lambda b, i: (0, 0)),
                  pl.BlockSpec((D_MODEL, D_IN), lambda b, i: (0, 0), pipeline_mode=pl.Buffered(1))],
        out_specs=[seg(D_MAIN), pl.BlockSpec((1, rows, SUBLANES, D_LRU), lambda b, i: (b, i, 0, 0))],
        out_shape=[jax.ShapeDtypeStruct((B, SUBLANES, S, D_MAIN), F32),
                   jax.ShapeDtypeStruct((B, S, SUBLANES, D_LRU), F32)],
        compiler_params=_params(("arbitrary", "arbitrary")),
        name="in_proj",
    )(x.reshape(B, SUBLANES, S, D_MODEL), shift, scale, g, w_bf16)
    return z.reshape(B, L, D_MAIN), z_lru


OUT_ROWS = 512


def _out_kernel(yr_ref, yh_ref, yl_ref, x_ref, gate_ref, w_ref, fg_ref, o_ref, *, final):
    _, nseg, rows, _ = x_ref.shape
    n = nseg * rows
    acc = jnp.dot(yr_ref[0].reshape(n, RET_WIDTH), w_ref[0:RET_WIDTH], preferred_element_type=F32)
    acc = acc + jnp.dot(yh_ref[0].reshape(n, HY_WIDTH), w_ref[RET_WIDTH:RET_WIDTH + HY_WIDTH],
                        preferred_element_type=F32)
    yl = jnp.concatenate([yl_ref[0, :, j, :] for j in range(nseg)], axis=0)
    acc = acc + jnp.dot(yl.astype(BF16), w_ref[RET_WIDTH + HY_WIDTH:D_MIX], preferred_element_type=F32)
    x = x_ref[0].reshape(n, D_MODEL) + gate_ref[0] * acc
    if final:
        ms = jnp.mean(x * x, axis=-1, keepdims=True)
        x = x * lax.rsqrt(ms + EPS) * fg_ref[...]
    o_ref[0] = x.reshape(nseg, rows, D_MODEL)


def _out_proj(y_ret, y_hy, y_lru, x, gate, w_bf16, final_g, final):
    B, L, _ = x.shape
    S = L // SUBLANES
    rows = min(OUT_ROWS // SUBLANES, S)
    per_batch = gate.shape[0] > 1
    mod_map = (lambda b, i: (b, 0, 0)) if per_batch else (lambda b, i: (0, 0, 0))
    seg = lambda C: pl.BlockSpec((1, SUBLANES, rows, C), lambda b, i: (b, 0, i, 0))
    view = lambda a: a.reshape(B, SUBLANES, S, a.shape[-1])
    out = pl.pallas_call(
        functools.partial(_out_kernel, final=final),
        grid=(B, S // rows),
        in_specs=[seg(RET_WIDTH), seg(HY_WIDTH),
                  pl.BlockSpec((1, rows, SUBLANES, LRU_WIDTH), lambda b, i: (b, i, 0, 0)),
                  seg(D_MODEL),
                  pl.BlockSpec((1, 1, D_MODEL), mod_map),
                  pl.BlockSpec((D_MIX, D_MODEL), lambda b, i: (0, 0), pipeline_mode=pl.Buffered(1)),
                  pl.BlockSpec((1, D_MODEL), lambda b, i: (0, 0))],
        out_specs=seg(D_MODEL),
        out_shape=jax.ShapeDtypeStruct((B, SUBLANES, S, D_MODEL), F32),
        compiler_params=_params(("arbitrary", "arbitrary")),
        name="out_proj",
    )(view(y_ret), view(y_hy), y_lru, view(x), gate, w_bf16, final_g)
    return out.reshape(B, L, D_MODEL)


RET_HEAD_ROWS = 2 * 2048
RET_BLOCK = 256


def _ret_kernel(*refs, L, use_rope, has_state, emit_state, chained, layer):
    it = iter(refs)
    q_ref, k_ref, v_ref, g_ref = next(it), next(it), next(it), next(it)
    cos_ref = sin_ref = s0_ref = sN_ref = None
    if use_rope:
        cos_ref, sin_ref = next(it), next(it)
    lg_ref = next(it)
    if has_state:
        s0_ref = next(it)
    if chained:
        next(it)
    y_ref = next(it)
    if emit_state:
        sN_ref = next(it)
    o_ref, R_ref, M_ref, Z_ref = (next(it) for _ in range(4))

    C = min(RET_BLOCK, L)
    DH = RET_HEAD_DIM
    HP = q_ref.shape[-1] // DH
    n = L // C
    cross = has_state or n > 1
    ii = lax.broadcasted_iota(jnp.int32, (C, C), 0)
    jj = lax.broadcasted_iota(jnp.int32, (C, C), 1)
    ri = lax.broadcasted_iota(jnp.int32, (C, DH), 0).astype(F32)
    kscale = DH ** -0.5

    @pl.when(pl.program_id(1) == 0)
    def _():
        for d in (0, 1):
            for hh in range(HP):
                lg = -_softplus(-lg_ref[d, hh])
                diff = ((ii - jj) if d == 0 else (jj - ii)).astype(F32)
                M_ref[d, hh] = jnp.where(diff >= 0, jnp.exp(lg[:, :C] * jnp.maximum(diff, 0.0)), 0.0)
                lgh = lg[:, :DH]
                Z_ref[d, hh, 0] = jnp.exp(lgh * ((C - 1.0 - ri) if d == 0 else ri))
                Z_ref[d, hh, 1] = jnp.exp(lgh * ((ri + 1.0) if d == 0 else (C - ri)))
                Z_ref[d, hh, 2] = jnp.exp(jnp.broadcast_to(lgh * C, (C, DH)))

    for d in (0, 1):
        for hh in range(HP):
            if has_state:
                R_ref[d, hh] = s0_ref[0, d, hh]
            else:
                R_ref[d, hh] = jnp.zeros((DH, DH), F32)

    def body(t, carry):
        for d in (0, 1):
            c = t if d == 0 else n - 1 - t
            r0 = pl.multiple_of(c * C, C)
            for hh in range(HP):
                lanes = slice(hh * DH, (hh + 1) * DH)
                q = q_ref[0, pl.ds(r0, C), lanes]
                k = k_ref[0, pl.ds(r0, C), lanes] * kscale
                vb = v_ref[0, pl.ds(r0, C), lanes].astype(BF16)
                if use_rope:
                    cs = cos_ref[pl.ds(r0, C), :]
                    sn = sin_ref[pl.ds(r0, C), :]
                    q = q * cs + pltpu.roll(q, 64, axis=1) * sn
                    k = k * cs + pltpu.roll(k, 64, axis=1) * sn
                s = lax.dot_general(q.astype(BF16), k.astype(BF16), (((1,), (1,)), ((), ())),
                                    preferred_element_type=F32) * M_ref[d, hh]
                o = jnp.dot(s.astype(BF16), vb, preferred_element_type=F32)
                R = R_ref[d, hh]
                if cross:
                    o = o + jnp.dot((q * Z_ref[d, hh, 1]).astype(BF16), R.astype(BF16),
                                    preferred_element_type=F32)
                kv = lax.dot_general((k * Z_ref[d, hh, 0]).astype(BF16), vb, (((0,), (0,)), ((), ())),
                                     preferred_element_type=F32)
                R_ref[d, hh] = Z_ref[d, hh, 2, 0:DH, :] * R + kv
                o_ref[d, pl.ds(r0, C), lanes] = o
        return carry

    lax.fori_loop(0, n, body, 0, unroll=2 if n % 2 == 0 else 1)
    if emit_state:
        for l in range(sN_ref.shape[1]):
            if chained or l == layer:
                for d in (0, 1):
                    for hh in range(HP):
                        sN_ref[0, l, d, hh] = R_ref[d, hh]
            else:
                sN_ref[0, l] = jnp.zeros(sN_ref.shape[2:], F32)

    def finish(c, carry):
        r0 = pl.multiple_of(c * DH, DH)
        tot = o_ref[0, pl.ds(r0, DH), :] + o_ref[1, pl.ds(r0, DH), :]
        g = g_ref[0, pl.ds(r0, DH), :]
        cols = []
        for hh in range(HP):
            th = tot[:, hh * DH:(hh + 1) * DH]
            cols.append(th * lax.rsqrt(jnp.mean(th * th, axis=-1, keepdims=True) + EPS))
        y_ref[0, pl.ds(r0, DH), :] = (jnp.concatenate(cols, axis=1) * _silu(g)).astype(BF16)
        return carry

    lax.fori_loop(0, L // DH, finish, 0)


def _retention(z, decay_logit, rope, state0, layer, states):
    B, L, _ = z.shape
    use_rope = rope is not None
    has_state = state0 is not None
    emit_state = states is not None
    chained = emit_state and not isinstance(states, str)
    H = RET_HEADS
    HP = min(H, max(1, RET_HEAD_ROWS // L))
    W = HP * RET_HEAD_DIM
    col = lambda off: pl.BlockSpec((1, L, W), lambda h, b, off=off: (b, 0, off * LANES // W + h))
    in_specs = [col(_Q0), col(_K0), col(_V0), col(_GR0)]
    args = [z, z, z, z]
    if use_rope:
        in_specs += [pl.BlockSpec((L, LANES), lambda h, b: (0, 0))] * 2
        args += list(rope)
    C = min(RET_BLOCK, L)
    in_specs.append(pl.BlockSpec((2, HP, 1, RET_BLOCK), lambda h, b: (0, h, 0, 0)))
    args.append(jnp.broadcast_to(decay_logit[:, :, None, None], (2, H, 1, RET_BLOCK)))
    sspec = pl.BlockSpec((1, 2, HP, RET_HEAD_DIM, RET_HEAD_DIM), lambda h, b: (b, 0, h, 0, 0))
    if has_state:
        in_specs.append(sspec)
        args.append(state0)
    out_specs = [pl.BlockSpec((1, L, W), lambda h, b: (b, 0, h))]
    out_shape = [jax.ShapeDtypeStruct((B, L, RET_WIDTH), BF16)]
    aliases = {}
    if emit_state:
        nl, l0 = (1, layer) if chained else (DEPTH, 0)
        out_specs.append(pl.BlockSpec((1, nl, 2, HP, RET_HEAD_DIM, RET_HEAD_DIM),
                                      lambda h, b: (b, l0, 0, h, 0, 0)))
        out_shape.append(jax.ShapeDtypeStruct((B, DEPTH, 2, H, RET_HEAD_DIM, RET_HEAD_DIM), F32))
        if chained:
            aliases = {len(args): 1}
            in_specs.append(pl.BlockSpec(memory_space=pl.ANY))
            args.append(states)
    return pl.pallas_call(
        functools.partial(_ret_kernel, L=L, use_rope=use_rope, has_state=has_state,
                          emit_state=emit_state, chained=chained, layer=layer),
        grid=(H // HP, B),
        in_specs=in_specs,
        out_specs=out_specs,
        out_shape=out_shape,
        input_output_aliases=aliases,
        scratch_shapes=[pltpu.VMEM((2, L, W), F32),
                        pltpu.VMEM((2, HP, RET_HEAD_DIM, RET_HEAD_DIM), F32),
                        pltpu.VMEM((2, HP, C, C), F32),
                        pltpu.VMEM((2, HP, 3, C, RET_HEAD_DIM), F32)],
        compiler_params=_params(("arbitrary", "arbitrary")),
        name="retention",
    )(*args)


def _dft_matrices():
    T = DFT_T
    N = 2 * T
    k = np.arange(T, dtype=np.float64)[:, None]
    m = np.arange(T, dtype=np.float64)[None, :]
    ang = 2.0 * np.pi * k * m / N
    fwd = np.concatenate([np.cos(ang), -np.sin(ang)], axis=0)
    fwd[T, :] = (-1.0) ** np.arange(T)
    ck = np.full((T,), 2.0)
    ck[0] = 1.0
    inv_re = (np.cos(ang) * ck[:, None]).T / N
    inv_im = (-2.0 * np.sin(ang)).T / N
    inv_im[:, 0] = ((-1.0) ** np.arange(T)) / N
    inv = np.concatenate([inv_re, inv_im], axis=1)
    out = []
    for m in (fwd, inv):
        m = jnp.asarray(m, F32)
        hi = m.astype(BF16)
        out += [hi, (m - hi.astype(F32)).astype(BF16)]
    return tuple(out)


def _split_bf16(x):
    hi = x.astype(BF16)
    return hi, (x - hi.astype(F32)).astype(BF16)


def _dot3(a_hi, a_lo, b):
    b_hi, b_lo = _split_bf16(b)
    return (jnp.dot(a_hi, b_hi, preferred_element_type=F32)
            + jnp.dot(a_lo, b_hi, preferred_element_type=F32)
            + jnp.dot(a_hi, b_lo, preferred_element_type=F32))


def _dot2(a_hi, a_lo, b):
    b = b.astype(BF16)
    return jnp.dot(a_hi, b, preferred_element_type=F32) + jnp.dot(a_lo, b, preferred_element_type=F32)


def _filt_kernel(zemb_ref, w1_ref, b1_ref, w2_ref, b2_ref, fr_ref, w3f_ref, w3b_ref, dec_ref,
                 bias_ref, fwdh_ref, fwdl_ref, kre_ref, kim_ref, hid_ref, hn_ref, F_ref, *, L):
    T = DFT_T
    nb = L // T

    @pl.when((pl.program_id(0) == 0) & (pl.program_id(1) == 0))
    def _():
        z1 = jnp.dot(zemb_ref[...], w1_ref[...], precision=HI, preferred_element_type=F32) + b1_ref[...]
        h1 = jnp.sin(fr_ref[0:1, :] * z1)
        z2 = jnp.dot(h1, w2_ref[...], precision=HI, preferred_element_type=F32) + b2_ref[...]
        hid = jnp.sin(fr_ref[1:2, :] * z2)
        hid_ref[0], hid_ref[1] = _split_bf16(hid)

    w3 = jnp.concatenate([w3f_ref[...], w3b_ref[...]], axis=1)
    w_hi, w_lo = _split_bf16(w3)
    h = (jnp.dot(hid_ref[0], w_hi, preferred_element_type=F32)
         + jnp.dot(hid_ref[1], w_hi, preferred_element_type=F32)
         + jnp.dot(hid_ref[0], w_lo, preferred_element_type=F32))
    h = h * jnp.concatenate([dec_ref[...]] * 2, axis=1)
    h = h / (jnp.sum(jnp.abs(h), axis=0, keepdims=True) + EPS)
    for dr in (0, 1):
        hn_ref[dr] = h[:, dr * LANES:(dr + 1) * LANES]
    for blk in range(nb):
        F = _dot3(fwdh_ref[...], fwdl_ref[...], h[blk * T:(blk + 1) * T])
        for dr in (0, 1):
            F_ref[dr, blk] = F[:, dr * LANES:(dr + 1) * LANES]

    row = lax.broadcasted_iota(jnp.int32, (T, LANES), 0)
    sgn = jnp.where((row & 1) == 0, 1.0, -1.0).astype(F32)
    row0 = row == 0
    bias = bias_ref[0]
    for d in range(-(nb - 1), nb):
        re = jnp.zeros((T, LANES), F32)
        im = jnp.zeros((T, LANES), F32)
        nyq = jnp.zeros((1, LANES), F32)
        if d >= 0:
            F = F_ref[0, d]
            re, im, nyq = re + F[:T], im + F[T:], nyq + F[T:T + 1]
        if d >= 1:
            F = F_ref[0, d - 1]
            p0 = hn_ref[0, (d - 1) * T:(d - 1) * T + 1, :]
            re, im, nyq = re + sgn * (F[:T] - p0), im + sgn * F[T:], nyq + (F[T:T + 1] - p0)
        e = -d
        if e >= 0:
            F = F_ref[1, e]
            re, im, nyq = re + F[:T], im - F[T:], nyq + F[T:T + 1]
        if e >= 1:
            F = F_ref[1, e - 1]
            p0 = hn_ref[1, (e - 1) * T:(e - 1) * T + 1, :]
            re, im, nyq = re + sgn * (F[:T] - p0), im - sgn * F[T:], nyq + (F[T:T + 1] - p0)
        if d == 0:
            re, nyq = re + bias, nyq + bias
        kre_ref[0, d + nb - 1] = re
        kim_ref[0, d + nb - 1] = jnp.where(row0, nyq, im)


def _hyena_filter_spectra(L, w1, b1, w2, b2, w3, freq, bias, fwd):
    T = DFT_T
    nb = L // T
    nlag = 2 * nb - 1
    P = LANES
    t = np.linspace(0.0, 1.0, L)[:, None]
    n_bands = (HY_EMB - 1) // 2
    f = np.linspace(1e-4, n_bands - 1, n_bands)
    ang = (2.0 * math.pi / L) * np.arange(L)[:, None] * f[None, :]
    zemb = np.concatenate([t, np.cos(ang), -np.sin(ang)], axis=-1)
    zemb = jnp.asarray(np.pad(zemb, ((0, 0), (0, P - HY_EMB))), F32)
    min_decay = math.log(HY_TARGET) / HY_LONG_DECAY_PCT
    max_decay = math.log(HY_TARGET) / HY_SHORT_DECAY_PCT
    deltas = np.abs(np.linspace(min_decay, max_decay, HY_WIDTH))
    dec = jnp.asarray(np.exp(-t * deltas[None, :]), F32)
    pad = P - HY_FFN
    w1p = jnp.pad(w1, ((0, P - HY_EMB), (0, pad)))
    b1p = jnp.pad(b1, (0, pad)).reshape(1, P)
    w2p = jnp.pad(w2, ((0, pad), (0, pad)))
    b2p = jnp.pad(b2, (0, pad)).reshape(1, P)
    w3p = jnp.pad(w3, ((0, pad), (0, 0)))
    frp = jnp.pad(freq, ((0, 0), (0, pad)))
    ncg = HY_WIDTH // P
    const = lambda shape: pl.BlockSpec(shape, lambda o, c: tuple(0 for _ in shape))
    kshape = jax.ShapeDtypeStruct((HY_ORDER, nlag, T, HY_WIDTH), F32)
    kspec = pl.BlockSpec((1, nlag, T, P), lambda o, c: (o, 0, 0, c))
    return pl.pallas_call(
        functools.partial(_filt_kernel, L=L),
        grid=(HY_ORDER, ncg),
        in_specs=[const((L, P)), const((P, P)), const((1, P)), const((P, P)), const((1, P)), const((2, P)),
                  pl.BlockSpec((P, P), lambda o, c: (0, o * 2 * ncg + c)),
                  pl.BlockSpec((P, P), lambda o, c: (0, o * 2 * ncg + ncg + c)),
                  pl.BlockSpec((L, P), lambda o, c: (0, c)),
                  pl.BlockSpec((1, 1, P), lambda o, c: (o, 0, c)),
                  const((2 * T, T)), const((2 * T, T))],
        out_specs=[kspec, kspec],
        out_shape=[kshape, kshape],
        scratch_shapes=[pltpu.VMEM((2, L, P), BF16), pltpu.VMEM((2, L, P), F32),
                        pltpu.VMEM((2, nb, 2 * T, P), F32)],
        compiler_params=_params(("arbitrary", "arbitrary")),
        name="hyena_filters",
    )(zemb, w1p, b1p, w2p, b2p, frp, w3p, w3p, dec, bias.reshape(HY_ORDER, 1, HY_WIDTH), *fwd)


def _conv_block(x_ref, j, nb, w, b, left):
    T = DFT_T
    t0 = j * T
    C = x_ref.shape[-1]
    blk = x_ref[t0:t0 + T, :]
    prev = x_ref[t0 - SUBLANES:t0, :] if j > 0 else jnp.zeros((SUBLANES, C), F32)
    nxt = x_ref[t0 + T:t0 + T + SUBLANES, :] if j < nb - 1 else jnp.zeros((SUBLANES, C), F32)
    ext = jnp.concatenate([prev, blk, nxt], axis=0)
    n = T + 2 * SUBLANES
    acc = jnp.broadcast_to(b, (T, C))
    for tap in range(w.shape[0]):
        s = tap - left
        sh = blk if s == 0 else pltpu.roll(ext, (-s) % n, axis=0)[SUBLANES:SUBLANES + T]
        acc = acc + w[tap:tap + 1, :] * sh
    return acc


HY_LANES_ROWS = 2 * 2048
HY_MAX_BATCH = 8


def _toeplitz(k, u, mul):
    n = len(u)
    if n == 1:
        return [mul(k[0], u[0])]
    h = n // 2
    add = lambda a, b: tuple(x + y for x, y in zip(a, b))
    sub = lambda a, b: tuple(x - y for x, y in zip(a, b))
    t0, t1, t2 = k[h:h + n - 1], k[0:n - 1], k[n:2 * n - 1]
    p = _toeplitz(t0, [add(a, b) for a, b in zip(u[:h], u[h:])], mul)
    q = _toeplitz([sub(a, b) for a, b in zip(t1, t0)], u[h:], mul)
    r = _toeplitz([sub(a, b) for a, b in zip(t2, t0)], u[:h], mul)
    return [add(a, b) for a, b in zip(p, q)] + [add(a, b) for a, b in zip(p, r)]


def _hy_kernel(v_ref, x1_ref, x2_ref, g_ref, wv_ref, w1_ref, w2_ref, bv_ref, b1_ref, b2_ref,
               kre_ref, kim_ref, fwdh_ref, fwdl_ref, invh_ref, invl_ref, y_ref,
               x1c_ref, x2c_ref, z1_ref, U_ref, Y_ref, *, L):
    T = DFT_T
    P = LANES
    nb = L // T
    NBT = v_ref.shape[0]
    left = (HY_SHORT - 1) // 2
    for j in range(nb):
        u = [_conv_block(v_ref.at[p], j, nb, wv_ref[...], bv_ref[...], left) for p in range(NBT)]
        U_ref[j] = _dot2(fwdh_ref[...], fwdl_ref[...], jnp.concatenate(u, axis=1))
        for p in range(NBT):
            x1c_ref[p, j * T:(j + 1) * T, :] = _conv_block(x1_ref.at[p], j, nb, w1_ref[...], b1_ref[...], left)
            x2c_ref[p, j * T:(j + 1) * T, :] = _conv_block(x2_ref.at[p], j, nb, w2_ref[...], b2_ref[...], left)

    row0 = lax.broadcasted_iota(jnp.int32, (SUBLANES, NBT * P), 0) == 0
    tile = lambda x: jnp.concatenate([x] * NBT, axis=1)

    def cmul(k, u):
        kr, ki = tile(k[0]), tile(k[1])
        return (kr * u[0] - ki * u[1], kr * u[1] + ki * u[0])

    def rmul(k, u):
        return (tile(k[0]) * u[0], tile(k[1]) * u[1])

    def spectra_products(o):
        def load(r, rows):
            k = [(kre_ref[o, l, pl.ds(r, rows), :], kim_ref[o, l, pl.ds(r, rows), :]) for l in range(2 * nb - 1)]
            u = [(U_ref[j, pl.ds(r, rows), :], U_ref[j, pl.ds(T + r, rows), :]) for j in range(nb)]
            return k, u

        def chunk(rc, carry):
            r = pl.multiple_of(rc * SUBLANES, SUBLANES)
            y = _toeplitz(*load(r, SUBLANES), cmul)
            for i in range(nb):
                Y_ref[i, pl.ds(r, SUBLANES), :] = y[i][0]
                Y_ref[i, pl.ds(T + r, SUBLANES), :] = y[i][1]
            return carry

        lax.fori_loop(0, T // SUBLANES, chunk, 0)
        y = _toeplitz(*load(0, SUBLANES), rmul)
        for i in range(nb):
            Y_ref[i, 0:SUBLANES, :] = jnp.where(row0, y[i][0], Y_ref[i, 0:SUBLANES, :])
            Y_ref[i, T:T + SUBLANES, :] = jnp.where(row0, y[i][1], Y_ref[i, T:T + SUBLANES, :])

    def long_conv_block(i):
        return _dot2(invh_ref[...], invl_ref[...], Y_ref[i])

    spectra_products(0)

    def order0(i, carry):
        r0 = pl.multiple_of(i * T, T)
        y = long_conv_block(i)
        for p in range(NBT):
            z1_ref[p, pl.ds(r0, T), :] = x1c_ref[p, pl.ds(r0, T), :] * y[:, p * P:(p + 1) * P]
        return carry

    lax.fori_loop(0, nb, order0, 0, unroll=2 if nb % 2 == 0 else 1)
    for j in range(nb):
        zc = jnp.concatenate([z1_ref[p, j * T:(j + 1) * T, :] for p in range(NBT)], axis=1)
        U_ref[j] = _dot2(fwdh_ref[...], fwdl_ref[...], zc)
    spectra_products(1)

    def order1(i, carry):
        r0 = pl.multiple_of(i * T, T)
        y = long_conv_block(i)
        for p in range(NBT):
            out = x2c_ref[p, pl.ds(r0, T), :] * y[:, p * P:(p + 1) * P]
            y_ref[p, pl.ds(r0, T), :] = (out * _silu(g_ref[p, pl.ds(r0, T), :])).astype(BF16)
        return carry

    lax.fori_loop(0, nb, order1, 0, unroll=2 if nb % 2 == 0 else 1)


def _hyena(z, conv_w, conv_b, kre, kim, dft):
    B, L, _ = z.shape
    NBT = min(B, max(2, HY_LANES_ROWS // L), HY_MAX_BATCH)
    assert B % NBT == 0
    T = DFT_T
    nb = L // T
    nlag = 2 * nb - 1
    P = LANES
    ncg = HY_WIDTH // P
    col = lambda off: pl.BlockSpec((NBT, L, P), lambda c, b, off=off: (b, 0, off + c))
    wspec = lambda part: pl.BlockSpec((HY_SHORT, P), lambda c, b, part=part: (0, part * ncg + c))
    bspec = lambda part: pl.BlockSpec((1, P), lambda c, b, part=part: (0, part * ncg + c))
    kspec = pl.BlockSpec((HY_ORDER, nlag, T, P), lambda c, b: (0, 0, 0, c))
    fspec = pl.BlockSpec((2 * T, T), lambda c, b: (0, 0))
    ispec = pl.BlockSpec((T, 2 * T), lambda c, b: (0, 0))
    cb = conv_b.reshape(1, 3 * HY_WIDTH)
    return pl.pallas_call(
        functools.partial(_hy_kernel, L=L),
        grid=(ncg, B // NBT),
        in_specs=[col(_HV0), col(_HX10), col(_HX20), col(_GH0),
                  wspec(0), wspec(1), wspec(2), bspec(0), bspec(1), bspec(2),
                  kspec, kspec, fspec, fspec, ispec, ispec],
        out_specs=pl.BlockSpec((NBT, L, P), lambda c, b: (b, 0, c)),
        out_shape=jax.ShapeDtypeStruct((B, L, HY_WIDTH), BF16),
        scratch_shapes=[pltpu.VMEM((NBT, L, P), F32), pltpu.VMEM((NBT, L, P), F32),
                        pltpu.VMEM((NBT, L, P), F32),
                        pltpu.VMEM((nb, 2 * T, NBT * P), F32), pltpu.VMEM((nb, 2 * T, NBT * P), F32)],
        compiler_params=_params(("arbitrary", "arbitrary")),
        name="hyena",
    )(z, z, z, z, conv_w, conv_w, conv_w, cb, cb, cb, kre, kim, *dft)


LRU_TILE = 256
LRU_GROUPS = LRU_TILE // LANES
LRU_ROWS = 128


def _lru_kernel(*refs, L, has_state):
    it = iter(refs)
    x_ref, g_ref, cw_ref, cb_ref, wg_ref, gb_ref, lam_ref = (next(it) for _ in range(7))
    h0_ref = next(it) if has_state else None
    y_ref, sN_ref, xp_ref, A_ref, B_ref, H_ref, C_ref = (next(it) for _ in range(7))

    S = L // SUBLANES
    W = LRU_TILE
    G = LRU_GROUPS
    RT = LRU_ROWS // SUBLANES
    left = LRU_CONV // 2

    sub = lax.broadcasted_iota(jnp.int32, (SUBLANES, W), 0)
    for k in range(left):
        tail = pltpu.roll(x_ref[0, S - left + k], 1, axis=0)
        xp_ref[k] = jnp.where(sub == 0, 0.0, tail)
    for k in range(LRU_CONV - 1 - left):
        head = pltpu.roll(x_ref[0, k], SUBLANES - 1, axis=0)
        xp_ref[left + S + k] = jnp.where(sub == SUBLANES - 1, 0.0, head)

    def copy(c, carry):
        i0 = pl.multiple_of(c * RT, RT)
        xp_ref[pl.ds(left + i0, RT)] = x_ref[0, pl.ds(i0, RT)]
        return carry

    lax.fori_loop(0, S // RT, copy, 0)

    csp = [(0.5 * LRU_C) * _softplus(-lam_ref[d]) for d in (0, 1)]

    def coeffs(c, carry):
        i0 = pl.multiple_of(c * RT, RT)
        u = jnp.broadcast_to(cb_ref[...], (RT, SUBLANES, W))
        for k in range(LRU_CONV):
            u = u + cw_ref[k:k + 1, :] * xp_ref[pl.ds(i0 + k, RT)]
        u = u.reshape(RT * SUBLANES, W)
        half = jnp.dot(u.astype(BF16), wg_ref[0], preferred_element_type=F32) + gb_ref[0]
        hu = 0.5 * u
        for d in (0, 1):
            t_r = jnp.tanh(half[:, (2 * d) * W:(2 * d + 1) * W])
            t_i = jnp.tanh(half[:, (2 * d + 1) * W:(2 * d + 2) * W])
            nl = csp[d] * t_r + csp[d]
            a = jnp.exp(-nl)
            b = jnp.sqrt(jnp.tanh(nl) * (a * a + 1.0)) * (t_i * hu + hu)
            for gi in range(G):
                lanes = slice(gi * LANES, (gi + 1) * LANES)
                A_ref[d, gi, pl.ds(i0, RT)] = a[:, lanes].reshape(RT, SUBLANES, LANES)
                B_ref[d, gi, pl.ds(i0, RT)] = b[:, lanes].reshape(RT, SUBLANES, LANES)
        return carry

    lax.fori_loop(0, S // RT, coeffs, 0)

    def scan_body(t, carry):
        out = []
        for d in (0, 1):
            i = t if d == 0 else S - 1 - t
            for gi in range(G):
                h, acc = carry[d * G + gi]
                a = A_ref[d, gi, i]
                h = a * h + B_ref[d, gi, i]
                acc = acc * a
                H_ref[d, gi, i] = h
                C_ref[d, gi, i] = acc
                out.append((h, acc))
        return tuple(out)

    init = tuple((jnp.zeros((SUBLANES, LANES), F32), jnp.ones((SUBLANES, LANES), F32))
                 for _ in range(2 * G))
    lax.fori_loop(0, S, scan_body, init, unroll=4)

    hin = [None] * (2 * G)
    for d in (0, 1):
        last = S - 1 if d == 0 else 0
        for gi in range(G):
            hl = H_ref[d, gi, last]
            ac = C_ref[d, gi, last]
            if has_state:
                h = h0_ref[0, d:d + 1, gi * LANES:(gi + 1) * LANES]
            else:
                h = jnp.zeros((1, LANES), F32)
            rows = [None] * SUBLANES
            for j in (range(SUBLANES) if d == 0 else range(SUBLANES - 1, -1, -1)):
                rows[j] = h
                h = hl[j:j + 1, :] + ac[j:j + 1, :] * h
            hin[d * G + gi] = jnp.concatenate(rows, axis=0)
            sN_ref[0, d:d + 1, gi * LANES:(gi + 1) * LANES] = h

    def combine(c, carry):
        i0 = pl.multiple_of(c * RT, RT)
        cols = []
        for gi in range(G):
            tot = jnp.zeros((RT, SUBLANES, LANES), F32)
            for d in (0, 1):
                tot = tot + (H_ref[d, gi, pl.ds(i0, RT)] + C_ref[d, gi, pl.ds(i0, RT)] * hin[d * G + gi])
            cols.append(tot)
        y_ref[0, pl.ds(i0, RT)] = jnp.concatenate(cols, axis=2) * _silu(g_ref[0, pl.ds(i0, RT)])
        return carry

    lax.fori_loop(0, S // RT, combine, 0)


def _rglru(z4, conv_w, conv_b, gate_w, gate_b, lam, h0):
    B, S, _, _ = z4.shape
    L = S * SUBLANES
    has_state = h0 is not None
    W = LRU_TILE
    nh = LRU_WIDTH // W
    bpt = W // LRU_BLOCK_DIM
    gw = gate_w.reshape(2, 2, nh, bpt, LRU_BLOCK_DIM, LRU_BLOCK_DIM)
    eye = jnp.eye(bpt, dtype=F32)
    dense = (0.5 * jnp.einsum('dkhnij,nm->hnidkmj', gw, eye)).reshape(nh, W, 4 * W).astype(BF16)
    gb = 0.5 * gate_b.reshape(2, 2, nh, W).transpose(2, 0, 1, 3).reshape(nh, 1, 4 * W)
    off = lambda base: pl.BlockSpec((1, S, SUBLANES, W), lambda b, h, base=base: (b, 0, 0, base + h))
    in_specs = [off(0), off(LRU_WIDTH // W),
                pl.BlockSpec((LRU_CONV, W), lambda b, h: (0, h)),
                pl.BlockSpec((1, W), lambda b, h: (0, h)),
                pl.BlockSpec((1, W, 4 * W), lambda b, h: (h, 0, 0)),
                pl.BlockSpec((1, 1, 4 * W), lambda b, h: (h, 0, 0)),
                pl.BlockSpec((2, 1, W), lambda b, h: (0, 0, h))]
    args = [z4, z4, conv_w, conv_b.reshape(1, LRU_WIDTH), dense, gb, lam.reshape(2, 1, LRU_WIDTH)]
    if has_state:
        in_specs.append(pl.BlockSpec((1, 2, W), lambda b, h: (b, 0, h)))
        args.append(h0)
    y, s = pl.pallas_call(
        functools.partial(_lru_kernel, L=L, has_state=has_state),
        grid=(B, nh),
        in_specs=in_specs,
        out_specs=[pl.BlockSpec((1, S, SUBLANES, W), lambda b, h: (b, 0, 0, h)),
                   pl.BlockSpec((1, 2, W), lambda b, h: (b, 0, h))],
        out_shape=[jax.ShapeDtypeStruct((B, S, SUBLANES, LRU_WIDTH), F32),
                   jax.ShapeDtypeStruct((B, 2, LRU_WIDTH), F32)],
        scratch_shapes=[pltpu.VMEM((S + LRU_CONV - 1, SUBLANES, W), F32)]
        + [pltpu.VMEM((2, LRU_GROUPS, S, SUBLANES, LANES), F32) for _ in range(4)],
        compiler_params=_params(("arbitrary", "arbitrary")),
        name="rglru",
    )(*args)
    return y, s


def _rope_tables(L):
    rows = L // GRID_W
    row = np.repeat(np.arange(rows, dtype=np.float64), GRID_W)
    col = np.tile(np.arange(GRID_W, dtype=np.float64), rows)
    n_f = RET_HEAD_DIM // 4
    inv = ROPE_BASE ** (-np.arange(n_f, dtype=np.float64) / n_f)
    ang = np.concatenate([row[:, None] * inv[None], col[:, None] * inv[None]], axis=-1)
    cos, sin = np.cos(ang), np.sin(ang)
    return (jnp.asarray(np.concatenate([cos, cos], axis=-1), F32),
            jnp.asarray(np.concatenate([-sin, sin], axis=-1), F32))


def kernel(x_prompt, x_sample, state_ret, state_lru, c, c_ctx, norm_g, ada_w, ada_b, w_in, ret_decay_logit, hy_conv_w, hy_conv_b, hy_ffn_w1, hy_ffn_b1, hy_ffn_w2, hy_ffn_b2, hy_ffn_w3, hy_freq, hy_bias, lru_conv_w, lru_conv_b, lru_gate_w, lru_gate_b, lru_lambda, w_out, final_g):
    Bp, Lp, _ = x_prompt.shape
    Bs, Ls, _ = x_sample.shape
    assert Lp % DFT_T == 0 and Ls % DFT_T == 0 and Bs + 1 <= SUBLANES

    cvec = jnp.zeros((SUBLANES, D_MODEL), F32).at[:Bs].set(c).at[Bs].set(c_ctx)
    mod = _modulation(cvec, ada_w, ada_b)
    rope = _rope_tables(Ls)
    dft = _dft_matrices()
    w_in_b = w_in.astype(BF16)
    w_out_b = w_out.astype(BF16)
    fg = final_g.reshape(1, D_MODEL)

    xc, xl = x_prompt, x_sample
    new_state_ret, new_lru = None, []
    for l in range(DEPTH):
        shift, scale, gate = (mod[l, :, i * D_MODEL:(i + 1) * D_MODEL] for i in range(3))
        g = norm_g[l].reshape(1, D_MODEL)
        final = l == DEPTH - 1
        filt = {}
        for L in sorted({Lp, Ls}):
            filt[L] = _hyena_filter_spectra(L, hy_ffn_w1[l], hy_ffn_b1[l], hy_ffn_w2[l], hy_ffn_b2[l],
                                            hy_ffn_w3[l], hy_freq[l], hy_bias[l], dft[:2])

        def layer(x, sel, rope_t, ret_s0, lru_s0, ret_states):
            sh, sc, gt = (m[sel][:, None, :] for m in (shift, scale, gate))
            z, z_lru = _in_proj(x, sh, sc, g, w_in_b[l])
            y_ret, *s_ret = _retention(z, ret_decay_logit[l], rope_t, ret_s0, l, ret_states)
            kre, kim = filt[x.shape[1]]
            y_hy = _hyena(z, hy_conv_w[l], hy_conv_b[l], kre, kim, dft)
            y_lru, s_lru = _rglru(z_lru, lru_conv_w[l], lru_conv_b[l], lru_gate_w[l], lru_gate_b[l],
                                  lru_lambda[l], lru_s0)
            return _out_proj(y_ret, y_hy, y_lru, x, gt, w_out_b[l], fg, final), s_ret, s_lru

        xc, (new_state_ret,), ls = layer(xc, slice(Bs, Bs + 1), None, None, None,
                                         "new" if l == 0 else new_state_ret)
        new_lru.append(ls)
        xl, _, _ = layer(xl, slice(0, Bs), rope, state_ret[:, l], state_lru[:, l], None)

    new_state_lru = jnp.stack(new_lru, axis=1).astype(x_prompt.dtype)
    return (xc, xl, new_state_ret.astype(x_prompt.dtype), new_state_lru)
```

```python
import functools
import math

import numpy as np
import jax
import jax.numpy as jnp
from jax import lax
from jax.experimental import pallas as pl
from jax.experimental.pallas import tpu as pltpu

F32 = jnp.float32
BF16 = jnp.bfloat16
HI = lax.Precision.HIGHEST

D_MODEL = 1024
DEPTH = 2
GRID_W = 64
EPS = 1e-6
RET_HEADS = 4
RET_HEAD_DIM = 128
RET_WIDTH = RET_HEADS * RET_HEAD_DIM
ROPE_BASE = 10000.0
HY_WIDTH = 512
HY_ORDER = 2
HY_SHORT = 3
HY_EMB = 33
HY_FFN = 64
HY_SHORT_DECAY_PCT = 0.3
HY_LONG_DECAY_PCT = 1.5
HY_TARGET = 1e-2
LRU_WIDTH = 512
LRU_BLOCKS = 8
LRU_BLOCK_DIM = LRU_WIDTH // LRU_BLOCKS
LRU_CONV = 4
LRU_C = 8.0
D_MIX = RET_WIDTH + HY_WIDTH + LRU_WIDTH
D_IN = 4 * RET_WIDTH + 4 * HY_WIDTH + 2 * LRU_WIDTH

LANES = 128
SUBLANES = 8
DFT_MAX = 512
ROW_TILE = 256
VMEM_LIMIT = 56 * 1024 * 1024

_Q0, _K0, _V0, _GR0 = 0, 4, 8, 12
_HV0, _HX10, _HX20, _GH0 = 16, 20, 24, 28


def _sigmoid(x):
    return 0.5 * jnp.tanh(0.5 * x) + 0.5


def _silu(x):
    return x * _sigmoid(x)


def _softplus(x):
    return jnp.maximum(x, 0.0) + jnp.log1p(jnp.exp(-jnp.abs(x)))


def _params(sem):
    return pltpu.CompilerParams(dimension_semantics=sem, vmem_limit_bytes=VMEM_LIMIT)


def _mod_kernel(c_ref, w_ref, b_ref, o_ref):
    c = c_ref[...]
    o_ref[0] = jnp.dot(_silu(c), w_ref[0], precision=HI, preferred_element_type=F32) + b_ref[0]


def _modulation(cvec, ada_w, ada_b):
    tn = 768
    return pl.pallas_call(
        _mod_kernel,
        grid=(DEPTH, 3 * D_MODEL // tn),
        in_specs=[pl.BlockSpec((SUBLANES, D_MODEL), lambda l, j: (0, 0)),
                  pl.BlockSpec((1, D_MODEL, tn), lambda l, j: (l, 0, j)),
                  pl.BlockSpec((1, 1, tn), lambda l, j: (l, 0, j))],
        out_specs=pl.BlockSpec((1, SUBLANES, tn), lambda l, j: (l, 0, j)),
        out_shape=jax.ShapeDtypeStruct((DEPTH, SUBLANES, 3 * D_MODEL), F32),
        compiler_params=_params(("arbitrary", "arbitrary")),
        name="modulation",
    )(cvec, ada_w, ada_b.reshape(DEPTH, 1, 3 * D_MODEL))


D_MAIN = 4 * RET_WIDTH + 4 * HY_WIDTH
D_LRU = 2 * LRU_WIDTH


def _in_kernel(x_ref, sh_ref, sc_ref, g_ref, w_ref, o_ref, ol_ref):
    _, nseg, rows, _ = x_ref.shape
    x = x_ref[0].reshape(nseg * rows, D_MODEL)
    ms = jnp.mean(x * x, axis=-1, keepdims=True)
    y = x * lax.rsqrt(ms + EPS) * g_ref[...]
    h = (y * (1.0 + sc_ref[0]) + sh_ref[0]).astype(BF16)
    tn = 1024
    for n in range(D_MAIN // tn):
        z = jnp.dot(h, w_ref[:, n * tn:(n + 1) * tn], preferred_element_type=F32)
        o_ref[0, :, :, n * tn:(n + 1) * tn] = z.reshape(nseg, rows, tn)
    zl = jnp.dot(h, w_ref[:, D_MAIN:D_IN], preferred_element_type=F32)
    for j in range(nseg):
        ol_ref[0, :, j, :] = zl[j * rows:(j + 1) * rows]


def _in_proj(x, shift, scale, g, w_bf16):
    B, L, _ = x.shape
    S = L // SUBLANES
    rows = min(ROW_TILE // SUBLANES, S)
    per_batch = shift.shape[0] > 1
    mod_map = (lambda b, i: (b, 0, 0)) if per_batch else (lambda b, i: (0, 0, 0))
    seg = lambda C: pl.BlockSpec((1, SUBLANES, rows, C), lambda b, i: (b, 0, i, 0))
    z, z_lru = pl.pallas_call(
        _in_kernel,
        grid=(B, S // rows),
        in_specs=[seg(D_MODEL),
                  pl.BlockSpec((1, 1, D_MODEL), mod_map),
                  pl.BlockSpec((1, 1, D_MODEL), mod_map),
                  pl.BlockSpec((1, D_MODEL), lambda b, i: (0, 0)),
                  pl.BlockSpec((D_MODEL, D_IN), lambda b, i: (0, 0), pipeline_mode=pl.Buffered(1))],
        out_specs=[seg(D_MAIN), pl.BlockSpec((1, rows, SUBLANES, D_LRU), lambda b, i: (b, i, 0, 0))],
        out_shape=[jax.ShapeDtypeStruct((B, SUBLANES, S, D_MAIN), F32),
                   jax.ShapeDtypeStruct((B, S, SUBLANES, D_LRU), F32)],
        compiler_params=_params(("arbitrary", "arbitrary")),
        name="in_proj",
    )(x.reshape(B, SUBLANES, S, D_MODEL), shift, scale, g, w_bf16)
    return z.reshape(B, L, D_MAIN), z_lru


OUT_ROWS = 512


def _out_kernel(yr_ref, yh_ref, yl_ref, x_ref, gate_ref, w_ref, fg_ref, o_ref, *, final):
    _, nseg, rows, _ = x_ref.shape
    n = nseg * rows
    acc = jnp.dot(yr_ref[0].reshape(n, RET_WIDTH), w_ref[0:RET_WIDTH], preferred_element_type=F32)
    acc = acc + jnp.dot(yh_ref[0].reshape(n, HY_WIDTH), w_ref[RET_WIDTH:RET_WIDTH + HY_WIDTH],
                        preferred_element_type=F32)
    yl = jnp.concatenate([yl_ref[0, :, j, :] for j in range(nseg)], axis=0)
    acc = acc + jnp.dot(yl.astype(BF16), w_ref[RET_WIDTH + HY_WIDTH:D_MIX], preferred_element_type=F32)
    x = x_ref[0].reshape(n, D_MODEL) + gate_ref[0] * acc
    if final:
        ms = jnp.mean(x * x, axis=-1, keepdims=True)
        x = x * lax.rsqrt(ms + EPS) * fg_ref[...]
    o_ref[0] = x.reshape(nseg, rows, D_MODEL)


def _out_proj(y_ret, y_hy, y_lru, x, gate, w_bf16, final_g, final):
    B, L, _ = x.shape
    S = L // SUBLANES
    rows = min(OUT_ROWS // SUBLANES, S)
    per_batch = gate.shape[0] > 1
    mod_map = (lambda b, i: (b, 0, 0)) if per_batch else (lambda b, i: (0, 0, 0))
    seg = lambda C: pl.BlockSpec((1, SUBLANES, rows, C), lambda b, i: (b, 0, i, 0))
    view = lambda a: a.reshape(B, SUBLANES, S, a.shape[-1])
    out = pl.pallas_call(
        functools.partial(_out_kernel, final=final),
        grid=(B, S // rows),
        in_specs=[seg(RET_WIDTH), seg(HY_WIDTH),
                  pl.BlockSpec((1, rows, SUBLANES, LRU_WIDTH), lambda b, i: (b, i, 0, 0)),
                  seg(D_MODEL),
                  pl.BlockSpec((1, 1, D_MODEL), mod_map),
                  pl.BlockSpec((D_MIX, D_MODEL), lambda b, i: (0, 0), pipeline_mode=pl.Buffered(1)),
                  pl.BlockSpec((1, D_MODEL), lambda b, i: (0, 0))],
        out_specs=seg(D_MODEL),
        out_shape=jax.ShapeDtypeStruct((B, SUBLANES, S, D_MODEL), F32),
        compiler_params=_params(("arbitrary", "arbitrary")),
        name="out_proj",
    )(view(y_ret), view(y_hy), y_lru, view(x), gate, w_bf16, final_g)
    return out.reshape(B, L, D_MODEL)


RET_HEAD_ROWS = 2 * 2048
RET_BLOCK = 256


def _ret_kernel(*refs, L, use_rope, has_state, emit_state, chained, layer):
    it = iter(refs)
    q_ref, k_ref, v_ref, g_ref = next(it), next(it), next(it), next(it)
    cos_ref = sin_ref = s0_ref = sN_ref = None
    if use_rope:
        cos_ref, sin_ref = next(it), next(it)
    lg_ref = next(it)
    if has_state:
        s0_ref = next(it)
    if chained:
        next(it)
    y_ref = next(it)
    if emit_state:
        sN_ref = next(it)
    o_ref, R_ref, M_ref, Z_ref = (next(it) for _ in range(4))

    C = min(RET_BLOCK, L)
    DH = RET_HEAD_DIM
    HP = q_ref.shape[-1] // DH
    n = L // C
    cross = has_state or n > 1
    ii = lax.broadcasted_iota(jnp.int32, (C, C), 0)
    jj = lax.broadcasted_iota(jnp.int32, (C, C), 1)
    ri = lax.broadcasted_iota(jnp.int32, (C, DH), 0).astype(F32)
    kscale = DH ** -0.5

    @pl.when(pl.program_id(1) == 0)
    def _():
        for d in (0, 1):
            for hh in range(HP):
                lg = -_softplus(-lg_ref[d, hh])
                diff = ((ii - jj) if d == 0 else (jj - ii)).astype(F32)
                M_ref[d, hh] = jnp.where(diff >= 0, jnp.exp(lg[:, :C] * jnp.maximum(diff, 0.0)), 0.0)
                lgh = lg[:, :DH]
                Z_ref[d, hh, 0] = jnp.exp(lgh * ((C - 1.0 - ri) if d == 0 else ri))
                Z_ref[d, hh, 1] = jnp.exp(lgh * ((ri + 1.0) if d == 0 else (C - ri)))
                Z_ref[d, hh, 2] = jnp.exp(jnp.broadcast_to(lgh * C, (C, DH)))

    for d in (0, 1):
        for hh in range(HP):
            if has_state:
                R_ref[d, hh] = s0_ref[0, d, hh]
            else:
                R_ref[d, hh] = jnp.zeros((DH, DH), F32)

    def body(t, carry):
        for d in (0, 1):
            c = t if d == 0 else n - 1 - t
            r0 = pl.multiple_of(c * C, C)
            for hh in range(HP):
                lanes = slice(hh * DH, (hh + 1) * DH)
                q = q_ref[0, pl.ds(r0, C), lanes]
                k = k_ref[0, pl.ds(r0, C), lanes] * kscale
                vb = v_ref[0, pl.ds(r0, C), lanes].astype(BF16)
                if use_rope:
                    cs = cos_ref[pl.ds(r0, C), :]
                    sn = sin_ref[pl.ds(r0, C), :]
                    q = q * cs + pltpu.roll(q, 64, axis=1) * sn
                    k = k * cs + pltpu.roll(k, 64, axis=1) * sn
                s = lax.dot_general(q.astype(BF16), k.astype(BF16), (((1,), (1,)), ((), ())),
                                    preferred_element_type=F32) * M_ref[d, hh]
                o = jnp.dot(s.astype(BF16), vb, preferred_element_type=F32)
                R = R_ref[d, hh]
                if cross:
                    o = o + jnp.dot((q * Z_ref[d, hh, 1]).astype(BF16), R.astype(BF16),
                                    preferred_element_type=F32)
                kv = lax.dot_general((k * Z_ref[d, hh, 0]).astype(BF16), vb, (((0,), (0,)), ((), ())),
                                     preferred_element_type=F32)
                R_ref[d, hh] = Z_ref[d, hh, 2, 0:DH, :] * R + kv
                o_ref[d, pl.ds(r0, C), lanes] = o
        return carry

    lax.fori_loop(0, n, body, 0, unroll=2 if n % 2 == 0 else 1)
    if emit_state:
        for l in range(sN_ref.shape[1]):
            if chained or l == layer:
                for d in (0, 1):
                    for hh in range(HP):
                        sN_ref[0, l, d, hh] = R_ref[d, hh]
            else:
                sN_ref[0, l] = jnp.zeros(sN_ref.shape[2:], F32)

    def finish(c, carry):
        r0 = pl.multiple_of(c * DH, DH)
        tot = o_ref[0, pl.ds(r0, DH), :] + o_ref[1, pl.ds(r0, DH), :]
        g = g_ref[0, pl.ds(r0, DH), :]
        cols = []
        for hh in range(HP):
            th = tot[:, hh * DH:(hh + 1) * DH]
            cols.append(th * lax.rsqrt(jnp.mean(th * th, axis=-1, keepdims=True) + EPS))
        y_ref[0, pl.ds(r0, DH), :] = (jnp.concatenate(cols, axis=1) * _silu(g)).astype(BF16)
        return carry

    lax.fori_loop(0, L // DH, finish, 0)


def _retention(z, decay_logit, rope, state0, layer, states):
    B, L, _ = z.shape
    use_rope = rope is not None
    has_state = state0 is not None
    emit_state = states is not None
    chained = emit_state and not isinstance(states, str)
    H = RET_HEADS
    HP = min(H, max(1, RET_HEAD_ROWS // L))
    W = HP * RET_HEAD_DIM
    col = lambda off: pl.BlockSpec((1, L, W), lambda h, b, off=off: (b, 0, off * LANES // W + h))
    in_specs = [col(_Q0), col(_K0), col(_V0), col(_GR0)]
    args = [z, z, z, z]
    if use_rope:
        in_specs += [pl.BlockSpec((L, LANES), lambda h, b: (0, 0))] * 2
        args += list(rope)
    C = min(RET_BLOCK, L)
    in_specs.append(pl.BlockSpec((2, HP, 1, RET_BLOCK), lambda h, b: (0, h, 0, 0)))
    args.append(jnp.broadcast_to(decay_logit[:, :, None, None], (2, H, 1, RET_BLOCK)))
    sspec = pl.BlockSpec((1, 2, HP, RET_HEAD_DIM, RET_HEAD_DIM), lambda h, b: (b, 0, h, 0, 0))
    if has_state:
        in_specs.append(sspec)
        args.append(state0)
    out_specs = [pl.BlockSpec((1, L, W), lambda h, b: (b, 0, h))]
    out_shape = [jax.ShapeDtypeStruct((B, L, RET_WIDTH), BF16)]
    aliases = {}
    if emit_state:
        nl, l0 = (1, layer) if chained else (DEPTH, 0)
        out_specs.append(pl.BlockSpec((1, nl, 2, HP, RET_HEAD_DIM, RET_HEAD_DIM),
                                      lambda h, b: (b, l0, 0, h, 0, 0)))
        out_shape.append(jax.ShapeDtypeStruct((B, DEPTH, 2, H, RET_HEAD_DIM, RET_HEAD_DIM), F32))
        if chained:
            aliases = {len(args): 1}
            in_specs.append(pl.BlockSpec(memory_space=pl.ANY))
            args.append(states)
    return pl.pallas_call(
        functools.partial(_ret_kernel, L=L, use_rope=use_rope, has_state=has_state,
                          emit_state=emit_state, chained=chained, layer=layer),
        grid=(H // HP, B),
        in_specs=in_specs,
        out_specs=out_specs,
        out_shape=out_shape,
        input_output_aliases=aliases,
        scratch_shapes=[pltpu.VMEM((2, L, W), F32),
                        pltpu.VMEM((2, HP, RET_HEAD_DIM, RET_HEAD_DIM), F32),
                        pltpu.VMEM((2, HP, C, C), F32),
                        pltpu.VMEM((2, HP, 3, C, RET_HEAD_DIM), F32)],
        compiler_params=_params(("arbitrary", "arbitrary")),
        name="retention",
    )(*args)


def _dft_matrices(T):
    N = 2 * T
    k = np.arange(T, dtype=np.float64)[:, None]
    m = np.arange(T, dtype=np.float64)[None, :]
    ang = 2.0 * np.pi * k * m / N
    fwd = np.concatenate([np.cos(ang), -np.sin(ang)], axis=0)
    fwd[T, :] = (-1.0) ** np.arange(T)
    ck = np.full((T,), 2.0)
    ck[0] = 1.0
    inv_re = (np.cos(ang) * ck[:, None]).T / N
    inv_im = (-2.0 * np.sin(ang)).T / N
    inv_im[:, 0] = ((-1.0) ** np.arange(T)) / N
    inv = np.concatenate([inv_re, inv_im], axis=1)
    out = []
    for m in (fwd, inv):
        m = jnp.asarray(m, F32)
        hi = m.astype(BF16)
        out += [hi, (m - hi.astype(F32)).astype(BF16)]
    return tuple(out)


def _split_bf16(x):
    hi = x.astype(BF16)
    return hi, (x - hi.astype(F32)).astype(BF16)


def _dot3(a_hi, a_lo, b):
    b_hi, b_lo = _split_bf16(b)
    return (jnp.dot(a_hi, b_hi, preferred_element_type=F32)
            + jnp.dot(a_lo, b_hi, preferred_element_type=F32)
            + jnp.dot(a_hi, b_lo, preferred_element_type=F32))


def _dot1(a, b):
    return jnp.dot(a, b.astype(BF16), preferred_element_type=F32)


def _filt_kernel(zemb_ref, w1_ref, b1_ref, w2_ref, b2_ref, fr_ref, w3f_ref, w3b_ref, dec_ref,
                 bias_ref, fwdh_ref, fwdl_ref, kre_ref, kim_ref, hid_ref, hn_ref, F_ref, *, L):
    T = fwdh_ref.shape[1]
    nb = L // T

    @pl.when((pl.program_id(0) == 0) & (pl.program_id(1) == 0))
    def _():
        z1 = jnp.dot(zemb_ref[...], w1_ref[...], precision=HI, preferred_element_type=F32) + b1_ref[...]
        h1 = jnp.sin(fr_ref[0:1, :] * z1)
        z2 = jnp.dot(h1, w2_ref[...], precision=HI, preferred_element_type=F32) + b2_ref[...]
        hid = jnp.sin(fr_ref[1:2, :] * z2)
        hid_ref[0], hid_ref[1] = _split_bf16(hid)

    w3 = jnp.concatenate([w3f_ref[...], w3b_ref[...]], axis=1)
    w_hi, w_lo = _split_bf16(w3)
    h = (jnp.dot(hid_ref[0], w_hi, preferred_element_type=F32)
         + jnp.dot(hid_ref[1], w_hi, preferred_element_type=F32)
         + jnp.dot(hid_ref[0], w_lo, preferred_element_type=F32))
    h = h * jnp.concatenate([dec_ref[...]] * 2, axis=1)
    h = h / (jnp.sum(jnp.abs(h), axis=0, keepdims=True) + EPS)
    for dr in (0, 1):
        hn_ref[dr] = h[:, dr * LANES:(dr + 1) * LANES]
    for blk in range(nb):
        F = _dot3(fwdh_ref[...], fwdl_ref[...], h[blk * T:(blk + 1) * T])
        for dr in (0, 1):
            F_ref[dr, blk] = F[:, dr * LANES:(dr + 1) * LANES]

    row = lax.broadcasted_iota(jnp.int32, (T, LANES), 0)
    sgn = jnp.where((row & 1) == 0, 1.0, -1.0).astype(F32)
    row0 = row == 0
    bias = bias_ref[0]
    for d in range(-(nb - 1), nb):
        re = jnp.zeros((T, LANES), F32)
        im = jnp.zeros((T, LANES), F32)
        nyq = jnp.zeros((1, LANES), F32)
        if d >= 0:
            F = F_ref[0, d]
            re, im, nyq = re + F[:T], im + F[T:], nyq + F[T:T + 1]
        if d >= 1:
            F = F_ref[0, d - 1]
            p0 = hn_ref[0, (d - 1) * T:(d - 1) * T + 1, :]
            re, im, nyq = re + sgn * (F[:T] - p0), im + sgn * F[T:], nyq + (F[T:T + 1] - p0)
        e = -d
        if e >= 0:
            F = F_ref[1, e]
            re, im, nyq = re + F[:T], im - F[T:], nyq + F[T:T + 1]
        if e >= 1:
            F = F_ref[1, e - 1]
            p0 = hn_ref[1, (e - 1) * T:(e - 1) * T + 1, :]
            re, im, nyq = re + sgn * (F[:T] - p0), im - sgn * F[T:], nyq + (F[T:T + 1] - p0)
        if d == 0:
            re, nyq = re + bias, nyq + bias
        kre_ref[0, d + nb - 1] = re
        kim_ref[0, d + nb - 1] = jnp.where(row0, nyq, im)


def _hyena_filter_spectra(L, w1, b1, w2, b2, w3, freq, bias, fwd):
    T = fwd[0].shape[1]
    nb = L // T
    nlag = 2 * nb - 1
    P = LANES
    t = np.linspace(0.0, 1.0, L)[:, None]
    n_bands = (HY_EMB - 1) // 2
    f = np.linspace(1e-4, n_bands - 1, n_bands)
    ang = (2.0 * math.pi / L) * np.arange(L)[:, None] * f[None, :]
    zemb = np.concatenate([t, np.cos(ang), -np.sin(ang)], axis=-1)
    zemb = jnp.asarray(np.pad(zemb, ((0, 0), (0, P - HY_EMB))), F32)
    min_decay = math.log(HY_TARGET) / HY_LONG_DECAY_PCT
    max_decay = math.log(HY_TARGET) / HY_SHORT_DECAY_PCT
    deltas = np.abs(np.linspace(min_decay, max_decay, HY_WIDTH))
    dec = jnp.asarray(np.exp(-t * deltas[None, :]), F32)
    pad = P - HY_FFN
    w1p = jnp.pad(w1, ((0, P - HY_EMB), (0, pad)))
    b1p = jnp.pad(b1, (0, pad)).reshape(1, P)
    w2p = jnp.pad(w2, ((0, pad), (0, pad)))
    b2p = jnp.pad(b2, (0, pad)).reshape(1, P)
    w3p = jnp.pad(w3, ((0, pad), (0, 0)))
    frp = jnp.pad(freq, ((0, 0), (0, pad)))
    ncg = HY_WIDTH // P
    const = lambda shape: pl.BlockSpec(shape, lambda o, c: tuple(0 for _ in shape))
    kshape = jax.ShapeDtypeStruct((HY_ORDER, nlag, T, HY_WIDTH), F32)
    kspec = pl.BlockSpec((1, nlag, T, P), lambda o, c: (o, 0, 0, c))
    return pl.pallas_call(
        functools.partial(_filt_kernel, L=L),
        grid=(HY_ORDER, ncg),
        in_specs=[const((L, P)), const((P, P)), const((1, P)), const((P, P)), const((1, P)), const((2, P)),
                  pl.BlockSpec((P, P), lambda o, c: (0, o * 2 * ncg + c)),
                  pl.BlockSpec((P, P), lambda o, c: (0, o * 2 * ncg + ncg + c)),
                  pl.BlockSpec((L, P), lambda o, c: (0, c)),
                  pl.BlockSpec((1, 1, P), lambda o, c: (o, 0, c)),
                  const((2 * T, T)), const((2 * T, T))],
        out_specs=[kspec, kspec],
        out_shape=[kshape, kshape],
        scratch_shapes=[pltpu.VMEM((2, L, P), BF16), pltpu.VMEM((2, L, P), F32),
                        pltpu.VMEM((2, nb, 2 * T, P), F32)],
        compiler_params=_params(("arbitrary", "arbitrary")),
        name="hyena_filters",
    )(zemb, w1p, b1p, w2p, b2p, frp, w3p, w3p, dec, bias.reshape(HY_ORDER, 1, HY_WIDTH), *fwd)


def _conv_block(x_ref, j, T, nb, w, b, left):
    t0 = j * T
    C = x_ref.shape[-1]
    blk = x_ref[t0:t0 + T, :]
    prev = x_ref[t0 - SUBLANES:t0, :] if j > 0 else jnp.zeros((SUBLANES, C), F32)
    nxt = x_ref[t0 + T:t0 + T + SUBLANES, :] if j < nb - 1 else jnp.zeros((SUBLANES, C), F32)
    ext = jnp.concatenate([prev, blk, nxt], axis=0)
    n = T + 2 * SUBLANES
    acc = jnp.broadcast_to(b, (T, C))
    for tap in range(w.shape[0]):
        s = tap - left
        sh = blk if s == 0 else pltpu.roll(ext, (-s) % n, axis=0)[SUBLANES:SUBLANES + T]
        acc = acc + w[tap:tap + 1, :] * sh
    return acc


HY_LANES_ROWS = 2 * 2048
HY_MAX_BATCH = 8


def _toeplitz(k, u, mul):
    n = len(u)
    if n == 1:
        return [mul(k[0], u[0])]
    h = n // 2
    add = lambda a, b: tuple(x + y for x, y in zip(a, b))
    sub = lambda a, b: tuple(x - y for x, y in zip(a, b))
    t0, t1, t2 = k[h:h + n - 1], k[0:n - 1], k[n:2 * n - 1]
    p = _toeplitz(t0, [add(a, b) for a, b in zip(u[:h], u[h:])], mul)
    q = _toeplitz([sub(a, b) for a, b in zip(t1, t0)], u[h:], mul)
    r = _toeplitz([sub(a, b) for a, b in zip(t2, t0)], u[:h], mul)
    return [add(a, b) for a, b in zip(p, q)] + [add(a, b) for a, b in zip(p, r)]


def _hy_kernel(v_ref, x1_ref, x2_ref, g_ref, wv_ref, w1_ref, w2_ref, bv_ref, b1_ref, b2_ref,
               kre_ref, kim_ref, fwd_ref, inv_ref, y_ref,
               x1c_ref, x2c_ref, z1_ref, U_ref, Y_ref, *, L):
    T = fwd_ref.shape[1]
    P = LANES
    nb = L // T
    NBT = v_ref.shape[0]
    left = (HY_SHORT - 1) // 2
    for j in range(nb):
        u = [_conv_block(v_ref.at[p], j, T, nb, wv_ref[...], bv_ref[...], left) for p in range(NBT)]
        U_ref[j] = _dot1(fwd_ref[...], jnp.concatenate(u, axis=1))
        for p in range(NBT):
            x1c_ref[p, j * T:(j + 1) * T, :] = _conv_block(x1_ref.at[p], j, T, nb, w1_ref[...], b1_ref[...], left)
            x2c_ref[p, j * T:(j + 1) * T, :] = _conv_block(x2_ref.at[p], j, T, nb, w2_ref[...], b2_ref[...], left)

    row0 = lax.broadcasted_iota(jnp.int32, (SUBLANES, NBT * P), 0) == 0
    tile = lambda x: jnp.concatenate([x] * NBT, axis=1)

    def cmul(k, u):
        kr, ki = tile(k[0]), tile(k[1])
        return (kr * u[0] - ki * u[1], kr * u[1] + ki * u[0])

    def rmul(k, u):
        return (tile(k[0]) * u[0], tile(k[1]) * u[1])

    def spectra_products(o):
        def load(r, rows):
            k = [(kre_ref[o, l, pl.ds(r, rows), :], kim_ref[o, l, pl.ds(r, rows), :]) for l in range(2 * nb - 1)]
            u = [(U_ref[j, pl.ds(r, rows), :], U_ref[j, pl.ds(T + r, rows), :]) for j in range(nb)]
            return k, u

        def chunk(rc, carry):
            r = pl.multiple_of(rc * SUBLANES, SUBLANES)
            y = _toeplitz(*load(r, SUBLANES), cmul)
            for i in range(nb):
                Y_ref[i, pl.ds(r, SUBLANES), :] = y[i][0]
                Y_ref[i, pl.ds(T + r, SUBLANES), :] = y[i][1]
            return carry

        lax.fori_loop(0, T // SUBLANES, chunk, 0)
        y = _toeplitz(*load(0, SUBLANES), rmul)
        for i in range(nb):
            Y_ref[i, 0:SUBLANES, :] = jnp.where(row0, y[i][0], Y_ref[i, 0:SUBLANES, :])
            Y_ref[i, T:T + SUBLANES, :] = jnp.where(row0, y[i][1], Y_ref[i, T:T + SUBLANES, :])

    def long_conv_block(i):
        return _dot1(inv_ref[...], Y_ref[i])

    spectra_products(0)

    def order0(i, carry):
        r0 = pl.multiple_of(i * T, T)
        y = long_conv_block(i)
        for p in range(NBT):
            z1_ref[p, pl.ds(r0, T), :] = x1c_ref[p, pl.ds(r0, T), :] * y[:, p * P:(p + 1) * P]
        return carry

    lax.fori_loop(0, nb, order0, 0, unroll=2 if nb % 2 == 0 else 1)
    for j in range(nb):
        zc = jnp.concatenate([z1_ref[p, j * T:(j + 1) * T, :] for p in range(NBT)], axis=1)
        U_ref[j] = _dot1(fwd_ref[...], zc)
    spectra_products(1)

    def order1(i, carry):
        r0 = pl.multiple_of(i * T, T)
        y = long_conv_block(i)
        for p in range(NBT):
            out = x2c_ref[p, pl.ds(r0, T), :] * y[:, p * P:(p + 1) * P]
            y_ref[p, pl.ds(r0, T), :] = (out * _silu(g_ref[p, pl.ds(r0, T), :])).astype(BF16)
        return carry

    lax.fori_loop(0, nb, order1, 0, unroll=2 if nb % 2 == 0 else 1)


def _hyena(z, conv_w, conv_b, kre, kim, fwd, inv):
    B, L, _ = z.shape
    NBT = min(B, max(2, HY_LANES_ROWS // L), HY_MAX_BATCH)
    assert B % NBT == 0
    T = fwd.shape[1]
    nb = L // T
    nlag = 2 * nb - 1
    P = LANES
    ncg = HY_WIDTH // P
    col = lambda off: pl.BlockSpec((NBT, L, P), lambda c, b, off=off: (b, 0, off + c))
    wspec = lambda part: pl.BlockSpec((HY_SHORT, P), lambda c, b, part=part: (0, part * ncg + c))
    bspec = lambda part: pl.BlockSpec((1, P), lambda c, b, part=part: (0, part * ncg + c))
    kspec = pl.BlockSpec((HY_ORDER, nlag, T, P), lambda c, b: (0, 0, 0, c))
    fspec = pl.BlockSpec((2 * T, T), lambda c, b: (0, 0))
    ispec = pl.BlockSpec((T, 2 * T), lambda c, b: (0, 0))
    cb = conv_b.reshape(1, 3 * HY_WIDTH)
    return pl.pallas_call(
        functools.partial(_hy_kernel, L=L),
        grid=(ncg, B // NBT),
        in_specs=[col(_HV0), col(_HX10), col(_HX20), col(_GH0),
                  wspec(0), wspec(1), wspec(2), bspec(0), bspec(1), bspec(2),
                  kspec, kspec, fspec, ispec],
        out_specs=pl.BlockSpec((NBT, L, P), lambda c, b: (b, 0, c)),
        out_shape=jax.ShapeDtypeStruct((B, L, HY_WIDTH), BF16),
        scratch_shapes=[pltpu.VMEM((NBT, L, P), F32), pltpu.VMEM((NBT, L, P), F32),
                        pltpu.VMEM((NBT, L, P), F32),
                        pltpu.VMEM((nb, 2 * T, NBT * P), F32), pltpu.VMEM((nb, 2 * T, NBT * P), F32)],
        compiler_params=_params(("arbitrary", "arbitrary")),
        name="hyena",
    )(z, z, z, z, conv_w, conv_w, conv_w, cb, cb, cb, kre, kim, fwd, inv)


LRU_TILE = 256
LRU_GROUPS = LRU_TILE // LANES
LRU_ROWS = 128


def _lru_kernel(*refs, L, has_state):
    it = iter(refs)
    x_ref, g_ref, cw_ref, cb_ref, wg_ref, gb_ref, lam_ref = (next(it) for _ in range(7))
    h0_ref = next(it) if has_state else None
    y_ref, sN_ref, xp_ref, A_ref, B_ref, H_ref, C_ref = (next(it) for _ in range(7))

    S = L // SUBLANES
    W = LRU_TILE
    G = LRU_GROUPS
    RT = LRU_ROWS // SUBLANES
    left = LRU_CONV // 2

    sub = lax.broadcasted_iota(jnp.int32, (SUBLANES, W), 0)
    for k in range(left):
        tail = pltpu.roll(x_ref[0, S - left + k], 1, axis=0)
        xp_ref[k] = jnp.where(sub == 0, 0.0, tail)
    for k in range(LRU_CONV - 1 - left):
        head = pltpu.roll(x_ref[0, k], SUBLANES - 1, axis=0)
        xp_ref[left + S + k] = jnp.where(sub == SUBLANES - 1, 0.0, head)

    def copy(c, carry):
        i0 = pl.multiple_of(c * RT, RT)
        xp_ref[pl.ds(left + i0, RT)] = x_ref[0, pl.ds(i0, RT)]
        return carry

    lax.fori_loop(0, S // RT, copy, 0)

    csp = [(0.5 * LRU_C) * _softplus(-lam_ref[d]) for d in (0, 1)]

    def coeffs(c, carry):
        i0 = pl.multiple_of(c * RT, RT)
        u = jnp.broadcast_to(cb_ref[...], (RT, SUBLANES, W))
        for k in range(LRU_CONV):
            u = u + cw_ref[k:k + 1, :] * xp_ref[pl.ds(i0 + k, RT)]
        u = u.reshape(RT * SUBLANES, W)
        half = jnp.dot(u.astype(BF16), wg_ref[0], preferred_element_type=F32) + gb_ref[0]
        hu = 0.5 * u
        for d in (0, 1):
            t_r = jnp.tanh(half[:, (2 * d) * W:(2 * d + 1) * W])
            t_i = jnp.tanh(half[:, (2 * d + 1) * W:(2 * d + 2) * W])
            nl = csp[d] * t_r + csp[d]
            a = jnp.exp(-nl)
            b = jnp.sqrt(jnp.tanh(nl) * (a * a + 1.0)) * (t_i * hu + hu)
            for gi in range(G):
                lanes = slice(gi * LANES, (gi + 1) * LANES)
                A_ref[d, gi, pl.ds(i0, RT)] = a[:, lanes].reshape(RT, SUBLANES, LANES)
                B_ref[d, gi, pl.ds(i0, RT)] = b[:, lanes].reshape(RT, SUBLANES, LANES)
        return carry

    lax.fori_loop(0, S // RT, coeffs, 0)

    def scan_body(t, carry):
        out = []
        for d in (0, 1):
            i = t if d == 0 else S - 1 - t
            for gi in range(G):
                h, acc = carry[d * G + gi]
                a = A_ref[d, gi, i]
                h = a * h + B_ref[d, gi, i]
                acc = acc * a
                H_ref[d, gi, i] = h
                C_ref[d, gi, i] = acc
                out.append((h, acc))
        return tuple(out)

    init = tuple((jnp.zeros((SUBLANES, LANES), F32), jnp.ones((SUBLANES, LANES), F32))
                 for _ in range(2 * G))
    lax.fori_loop(0, S, scan_body, init, unroll=4)

    hin = [None] * (2 * G)
    for d in (0, 1):
        last = S - 1 if d == 0 else 0
        for gi in range(G):
            hl = H_ref[d, gi, last]
            ac = C_ref[d, gi, last]
            if has_state:
                h = h0_ref[0, d:d + 1, gi * LANES:(gi + 1) * LANES]
            else:
                h = jnp.zeros((1, LANES), F32)
            rows = [None] * SUBLANES
            for j in (range(SUBLANES) if d == 0 else range(SUBLANES - 1, -1, -1)):
                rows[j] = h
                h = hl[j:j + 1, :] + ac[j:j + 1, :] * h
            hin[d * G + gi] = jnp.concatenate(rows, axis=0)
            sN_ref[0, d:d + 1, gi * LANES:(gi + 1) * LANES] = h

    def combine(c, carry):
        i0 = pl.multiple_of(c * RT, RT)
        cols = []
        for gi in range(G):
            tot = jnp.zeros((RT, SUBLANES, LANES), F32)
            for d in (0, 1):
                tot = tot + (H_ref[d, gi, pl.ds(i0, RT)] + C_ref[d, gi, pl.ds(i0, RT)] * hin[d * G + gi])
            cols.append(tot)
        y_ref[0, pl.ds(i0, RT)] = jnp.concatenate(cols, axis=2) * _silu(g_ref[0, pl.ds(i0, RT)])
        return carry

    lax.fori_loop(0, S // RT, combine, 0)


def _rglru(z4, conv_w, conv_b, gate_w, gate_b, lam, h0):
    B, S, _, _ = z4.shape
    L = S * SUBLANES
    has_state = h0 is not None
    W = LRU_TILE
    nh = LRU_WIDTH // W
    bpt = W // LRU_BLOCK_DIM
    gw = gate_w.reshape(2, 2, nh, bpt, LRU_BLOCK_DIM, LRU_BLOCK_DIM)
    eye = jnp.eye(bpt, dtype=F32)
    dense = (0.5 * jnp.einsum('dkhnij,nm->hnidkmj', gw, eye)).reshape(nh, W, 4 * W).astype(BF16)
    gb = 0.5 * gate_b.reshape(2, 2, nh, W).transpose(2, 0, 1, 3).reshape(nh, 1, 4 * W)
    off = lambda base: pl.BlockSpec((1, S, SUBLANES, W), lambda b, h, base=base: (b, 0, 0, base + h))
    in_specs = [off(0), off(LRU_WIDTH // W),
                pl.BlockSpec((LRU_CONV, W), lambda b, h: (0, h)),
                pl.BlockSpec((1, W), lambda b, h: (0, h)),
                pl.BlockSpec((1, W, 4 * W), lambda b, h: (h, 0, 0)),
                pl.BlockSpec((1, 1, 4 * W), lambda b, h: (h, 0, 0)),
                pl.BlockSpec((2, 1, W), lambda b, h: (0, 0, h))]
    args = [z4, z4, conv_w, conv_b.reshape(1, LRU_WIDTH), dense, gb, lam.reshape(2, 1, LRU_WIDTH)]
    if has_state:
        in_specs.append(pl.BlockSpec((1, 2, W), lambda b, h: (b, 0, h)))
        args.append(h0)
    y, s = pl.pallas_call(
        functools.partial(_lru_kernel, L=L, has_state=has_state),
        grid=(B, nh),
        in_specs=in_specs,
        out_specs=[pl.BlockSpec((1, S, SUBLANES, W), lambda b, h: (b, 0, 0, h)),
                   pl.BlockSpec((1, 2, W), lambda b, h: (b, 0, h))],
        out_shape=[jax.ShapeDtypeStruct((B, S, SUBLANES, LRU_WIDTH), F32),
                   jax.ShapeDtypeStruct((B, 2, LRU_WIDTH), F32)],
        scratch_shapes=[pltpu.VMEM((S + LRU_CONV - 1, SUBLANES, W), F32)]
        + [pltpu.VMEM((2, LRU_GROUPS, S, SUBLANES, LANES), F32) for _ in range(4)],
        compiler_params=_params(("arbitrary", "arbitrary")),
        name="rglru",
    )(*args)
    return y, s


def _rope_tables(L):
    rows = L // GRID_W
    row = np.repeat(np.arange(rows, dtype=np.float64), GRID_W)
    col = np.tile(np.arange(GRID_W, dtype=np.float64), rows)
    n_f = RET_HEAD_DIM // 4
    inv = ROPE_BASE ** (-np.arange(n_f, dtype=np.float64) / n_f)
    ang = np.concatenate([row[:, None] * inv[None], col[:, None] * inv[None]], axis=-1)
    cos, sin = np.cos(ang), np.sin(ang)
    return (jnp.asarray(np.concatenate([cos, cos], axis=-1), F32),
            jnp.asarray(np.concatenate([-sin, sin], axis=-1), F32))


def kernel(x_prompt, x_sample, state_ret, state_lru, c, c_ctx, norm_g, ada_w, ada_b, w_in, ret_decay_logit, hy_conv_w, hy_conv_b, hy_ffn_w1, hy_ffn_b1, hy_ffn_w2, hy_ffn_b2, hy_ffn_w3, hy_freq, hy_bias, lru_conv_w, lru_conv_b, lru_gate_w, lru_gate_b, lru_lambda, w_out, final_g):
    Bp, Lp, _ = x_prompt.shape
    Bs, Ls, _ = x_sample.shape
    assert Bs + 1 <= SUBLANES

    cvec = jnp.zeros((SUBLANES, D_MODEL), F32).at[:Bs].set(c).at[Bs].set(c_ctx)
    mod = _modulation(cvec, ada_w, ada_b)
    rope = _rope_tables(Ls)
    dft = {L: _dft_matrices(min(DFT_MAX, L)) for L in {Lp, Ls}}
    w_in_b = w_in.astype(BF16)
    w_out_b = w_out.astype(BF16)
    fg = final_g.reshape(1, D_MODEL)

    xc, xl = x_prompt, x_sample
    new_state_ret, new_lru = None, []
    for l in range(DEPTH):
        shift, scale, gate = (mod[l, :, i * D_MODEL:(i + 1) * D_MODEL] for i in range(3))
        g = norm_g[l].reshape(1, D_MODEL)
        final = l == DEPTH - 1
        filt = {}
        for L in sorted({Lp, Ls}):
            filt[L] = _hyena_filter_spectra(L, hy_ffn_w1[l], hy_ffn_b1[l], hy_ffn_w2[l], hy_ffn_b2[l],
                                            hy_ffn_w3[l], hy_freq[l], hy_bias[l], dft[L][:2])

        def layer(x, sel, rope_t, ret_s0, lru_s0, ret_states):
            sh, sc, gt = (m[sel][:, None, :] for m in (shift, scale, gate))
            z, z_lru = _in_proj(x, sh, sc, g, w_in_b[l])
            y_ret, *s_ret = _retention(z, ret_decay_logit[l], rope_t, ret_s0, l, ret_states)
            kre, kim = filt[x.shape[1]]
            fwd_hi, _, inv_hi, _ = dft[x.shape[1]]
            y_hy = _hyena(z, hy_conv_w[l], hy_conv_b[l], kre, kim, fwd_hi, inv_hi)
            y_lru, s_lru = _rglru(z_lru, lru_conv_w[l], lru_conv_b[l], lru_gate_w[l], lru_gate_b[l],
                                  lru_lambda[l], lru_s0)
            return _out_proj(y_ret, y_hy, y_lru, x, gt, w_out_b[l], fg, final), s_ret, s_lru

        xc, (new_state_ret,), ls = layer(xc, slice(Bs, Bs + 1), None, None, None,
                                         "new" if l == 0 else new_state_ret)
        new_lru.append(ls)
        xl, _, _ = layer(xl, slice(0, Bs), rope, state_ret[:, l], state_lru[:, l], None)

    new_state_lru = jnp.stack(new_lru, axis=1).astype(x_prompt.dtype)
    return (xc, xl, new_state_ret.astype(x_prompt.dtype), new_state_lru)
```

```python
import functools
import math

import numpy as np
import jax
import jax.numpy as jnp
from jax import lax
from jax.experimental import pallas as pl
from jax.experimental.pallas import tpu as pltpu

F32 = jnp.float32
BF16 = jnp.bfloat16
HI = lax.Precision.HIGHEST

D_MODEL = 1024
DEPTH = 2
GRID_W = 64
EPS = 1e-6
RET_HEADS = 4
RET_HEAD_DIM = 128
RET_WIDTH = RET_HEADS * RET_HEAD_DIM
ROPE_BASE = 10000.0
HY_WIDTH = 512
HY_ORDER = 2
HY_SHORT = 3
HY_EMB = 33
HY_FFN = 64
HY_SHORT_DECAY_PCT = 0.3
HY_LONG_DECAY_PCT = 1.5
HY_TARGET = 1e-2
LRU_WIDTH = 512
LRU_BLOCKS = 8
LRU_BLOCK_DIM = LRU_WIDTH // LRU_BLOCKS
LRU_CONV = 4
LRU_C = 8.0
D_MIX = RET_WIDTH + HY_WIDTH + LRU_WIDTH
D_IN = 4 * RET_WIDTH + 4 * HY_WIDTH + 2 * LRU_WIDTH

LANES = 128
SUBLANES = 8
DFT_MAX = 512
ROW_TILE = 256
VMEM_LIMIT = 56 * 1024 * 1024

_Q0, _K0, _V0, _GR0 = 0, 4, 8, 12
_HV0, _HX10, _HX20, _GH0 = 16, 20, 24, 28


def _sigmoid(x):
    return 0.5 * jnp.tanh(0.5 * x) + 0.5


def _silu(x):
    return x * _sigmoid(x)


def _softplus(x):
    return jnp.maximum(x, 0.0) + jnp.log1p(jnp.exp(-jnp.abs(x)))


def _params(sem):
    return pltpu.CompilerParams(dimension_semantics=sem, vmem_limit_bytes=VMEM_LIMIT)


def _mod_kernel(c_ref, w_ref, b_ref, o_ref):
    s_hi, s_lo = _split_bf16(_silu(c_ref[...]))
    w_hi, w_lo = _split_bf16(w_ref[0])
    o_ref[0] = (jnp.dot(s_hi, w_hi, preferred_element_type=F32)
                + jnp.dot(s_lo, w_hi, preferred_element_type=F32)
                + jnp.dot(s_hi, w_lo, preferred_element_type=F32)) + b_ref[0]


def _modulation(cvec, ada_w, ada_b):
    tn = 768
    return pl.pallas_call(
        _mod_kernel,
        grid=(DEPTH, 3 * D_MODEL // tn),
        in_specs=[pl.BlockSpec((SUBLANES, D_MODEL), lambda l, j: (0, 0)),
                  pl.BlockSpec((1, D_MODEL, tn), lambda l, j: (l, 0, j)),
                  pl.BlockSpec((1, 1, tn), lambda l, j: (l, 0, j))],
        out_specs=pl.BlockSpec((1, SUBLANES, tn), lambda l, j: (l, 0, j)),
        out_shape=jax.ShapeDtypeStruct((DEPTH, SUBLANES, 3 * D_MODEL), F32),
        compiler_params=_params(("arbitrary", "arbitrary")),
        name="modulation",
    )(cvec, ada_w, ada_b.reshape(DEPTH, 1, 3 * D_MODEL))


D_MAIN = 4 * RET_WIDTH + 4 * HY_WIDTH
D_LRU = 2 * LRU_WIDTH


def _in_kernel(x_ref, sh_ref, sc_ref, g_ref, w_ref, o_ref, ol_ref):
    _, nseg, rows, _ = x_ref.shape
    x = x_ref[0].reshape(nseg * rows, D_MODEL)
    ms = jnp.mean(x * x, axis=-1, keepdims=True)
    y = x * lax.rsqrt(ms + EPS) * g_ref[...]
    h = (y * (1.0 + sc_ref[0]) + sh_ref[0]).astype(BF16)
    tn = 1024
    for n in range(D_MAIN // tn):
        z = jnp.dot(h, w_ref[:, n * tn:(n + 1) * tn], preferred_element_type=F32)
        o_ref[0, :, :, n * tn:(n + 1) * tn] = z.reshape(nseg, rows, tn)
    zl = jnp.dot(h, w_ref[:, D_MAIN:D_IN], preferred_element_type=F32)
    for j in range(nseg):
        ol_ref[0, :, j, :] = zl[j * rows:(j + 1) * rows]


def _in_proj(x, shift, scale, g, w_bf16):
    B, L, _ = x.shape
    S = L // SUBLANES
    rows = min(ROW_TILE // SUBLANES, S)
    per_batch = shift.shape[0] > 1
    mod_map = (lambda b, i: (b, 0, 0)) if per_batch else (lambda b, i: (0, 0, 0))
    seg = lambda C: pl.BlockSpec((1, SUBLANES, rows, C), lambda b, i: (b, 0, i, 0))
    z, z_lru = pl.pallas_call(
        _in_kernel,
        grid=(B, S // rows),
        in_specs=[seg(D_MODEL),
                  pl.BlockSpec((1, 1, D_MODEL), mod_map),
                  pl.BlockSpec((1, 1, D_MODEL), mod_map),
                  pl.BlockSpec((1, D_MODEL), lambda b, i: (0, 0)),
                  pl.BlockSpec((D_MODEL, D_IN), lambda b, i: (0, 0), pipeline_mode=pl.Buffered(1))],
        out_specs=[seg(D_MAIN), pl.BlockSpec((1, rows, SUBLANES, D_LRU), lambda b, i: (b, i, 0, 0))],
        out_shape=[jax.ShapeDtypeStruct((B, SUBLANES, S, D_MAIN), F32),
                   jax.ShapeDtypeStruct((B, S, SUBLANES, D_LRU), F32)],
        compiler_params=_params(("arbitrary", "arbitrary")),
        name="in_proj",
    )(x.reshape(B, SUBLANES, S, D_MODEL), shift, scale, g, w_bf16)
    return z.reshape(B, L, D_MAIN), z_lru


OUT_ROWS = 512


def _out_kernel(yr_ref, yh_ref, yl_ref, x_ref, gate_ref, w_ref, fg_ref, o_ref, *, final):
    _, nseg, rows, _ = x_ref.shape
    n = nseg * rows
    acc = jnp.dot(yr_ref[0].reshape(n, RET_WIDTH), w_ref[0:RET_WIDTH], preferred_element_type=F32)
    acc = acc + jnp.dot(yh_ref[0].reshape(n, HY_WIDTH), w_ref[RET_WIDTH:RET_WIDTH + HY_WIDTH],
                        preferred_element_type=F32)
    yl = jnp.concatenate([yl_ref[0, :, j, :] for j in range(nseg)], axis=0)
    acc = acc + jnp.dot(yl.astype(BF16), w_ref[RET_WIDTH + HY_WIDTH:D_MIX], preferred_element_type=F32)
    x = x_ref[0].reshape(n, D_MODEL) + gate_ref[0] * acc
    if final:
        ms = jnp.mean(x * x, axis=-1, keepdims=True)
        x = x * lax.rsqrt(ms + EPS) * fg_ref[...]
    o_ref[0] = x.reshape(nseg, rows, D_MODEL)


def _out_proj(y_ret, y_hy, y_lru, x, gate, w_bf16, final_g, final):
    B, L, _ = x.shape
    S = L // SUBLANES
    rows = min(OUT_ROWS // SUBLANES, S)
    per_batch = gate.shape[0] > 1
    mod_map = (lambda b, i: (b, 0, 0)) if per_batch else (lambda b, i: (0, 0, 0))
    seg = lambda C: pl.BlockSpec((1, SUBLANES, rows, C), lambda b, i: (b, 0, i, 0))
    view = lambda a: a.reshape(B, SUBLANES, S, a.shape[-1])
    out = pl.pallas_call(
        functools.partial(_out_kernel, final=final),
        grid=(B, S // rows),
        in_specs=[seg(RET_WIDTH), seg(HY_WIDTH),
                  pl.BlockSpec((1, rows, SUBLANES, LRU_WIDTH), lambda b, i: (b, i, 0, 0)),
                  seg(D_MODEL),
                  pl.BlockSpec((1, 1, D_MODEL), mod_map),
                  pl.BlockSpec((D_MIX, D_MODEL), lambda b, i: (0, 0), pipeline_mode=pl.Buffered(1)),
                  pl.BlockSpec((1, D_MODEL), lambda b, i: (0, 0))],
        out_specs=seg(D_MODEL),
        out_shape=jax.ShapeDtypeStruct((B, SUBLANES, S, D_MODEL), F32),
        compiler_params=_params(("arbitrary", "arbitrary")),
        name="out_proj",
    )(view(y_ret), view(y_hy), y_lru, view(x), gate, w_bf16, final_g)
    return out.reshape(B, L, D_MODEL)


RET_HEAD_ROWS = 2 * 2048
RET_BLOCK = 256


def _ret_kernel(*refs, L, use_rope, has_state, emit_state, chained, layer):
    it = iter(refs)
    q_ref, k_ref, v_ref, g_ref = next(it), next(it), next(it), next(it)
    cos_ref = sin_ref = s0_ref = sN_ref = None
    if use_rope:
        cos_ref, sin_ref = next(it), next(it)
    lg_ref = next(it)
    if has_state:
        s0_ref = next(it)
    if chained:
        next(it)
    y_ref = next(it)
    if emit_state:
        sN_ref = next(it)
    o_ref, R_ref, M_ref, Z_ref = (next(it) for _ in range(4))

    C = min(RET_BLOCK, L)
    DH = RET_HEAD_DIM
    HP = q_ref.shape[-1] // DH
    n = L // C
    cross = has_state or n > 1
    ii = lax.broadcasted_iota(jnp.int32, (C, C), 0)
    jj = lax.broadcasted_iota(jnp.int32, (C, C), 1)
    ri = lax.broadcasted_iota(jnp.int32, (C, DH), 0).astype(F32)
    kscale = DH ** -0.5

    @pl.when(pl.program_id(1) == 0)
    def _():
        for d in (0, 1):
            for hh in range(HP):
                lg = -_softplus(-lg_ref[d, hh])
                diff = ((ii - jj) if d == 0 else (jj - ii)).astype(F32)
                M_ref[d, hh] = jnp.where(diff >= 0, jnp.exp(lg[:, :C] * jnp.maximum(diff, 0.0)), 0.0)
                lgh = lg[:, :DH]
                Z_ref[d, hh, 0] = jnp.exp(lgh * ((C - 1.0 - ri) if d == 0 else ri))
                Z_ref[d, hh, 1] = jnp.exp(lgh * ((ri + 1.0) if d == 0 else (C - ri)))
                Z_ref[d, hh, 2] = jnp.exp(jnp.broadcast_to(lgh * C, (C, DH)))

    for d in (0, 1):
        for hh in range(HP):
            if has_state:
                R_ref[d, hh] = s0_ref[0, d, hh]
            else:
                R_ref[d, hh] = jnp.zeros((DH, DH), F32)

    def body(t, carry):
        for d in (0, 1):
            c = t if d == 0 else n - 1 - t
            r0 = pl.multiple_of(c * C, C)
            for hh in range(HP):
                lanes = slice(hh * DH, (hh + 1) * DH)
                q = q_ref[0, pl.ds(r0, C), lanes]
                k = k_ref[0, pl.ds(r0, C), lanes] * kscale
                vb = v_ref[0, pl.ds(r0, C), lanes].astype(BF16)
                if use_rope:
                    cs = cos_ref[pl.ds(r0, C), :]
                    sn = sin_ref[pl.ds(r0, C), :]
                    q = q * cs + pltpu.roll(q, 64, axis=1) * sn
                    k = k * cs + pltpu.roll(k, 64, axis=1) * sn
                s = lax.dot_general(q.astype(BF16), k.astype(BF16), (((1,), (1,)), ((), ())),
                                    preferred_element_type=F32) * M_ref[d, hh]
                o = jnp.dot(s.astype(BF16), vb, preferred_element_type=F32)
                R = R_ref[d, hh]
                if cross:
                    o = o + jnp.dot((q * Z_ref[d, hh, 1]).astype(BF16), R.astype(BF16),
                                    preferred_element_type=F32)
                kv = lax.dot_general((k * Z_ref[d, hh, 0]).astype(BF16), vb, (((0,), (0,)), ((), ())),
                                     preferred_element_type=F32)
                R_ref[d, hh] = Z_ref[d, hh, 2, 0:DH, :] * R + kv
                o_ref[d, pl.ds(r0, C), lanes] = o
        return carry

    lax.fori_loop(0, n, body, 0, unroll=2 if n % 2 == 0 else 1)
    if emit_state:
        for l in range(sN_ref.shape[1]):
            if chained or l == layer:
                for d in (0, 1):
                    for hh in range(HP):
                        sN_ref[0, l, d, hh] = R_ref[d, hh]
            else:
                sN_ref[0, l] = jnp.zeros(sN_ref.shape[2:], F32)

    def finish(c, carry):
        r0 = pl.multiple_of(c * DH, DH)
        tot = o_ref[0, pl.ds(r0, DH), :] + o_ref[1, pl.ds(r0, DH), :]
        g = g_ref[0, pl.ds(r0, DH), :]
        cols = []
        for hh in range(HP):
            th = tot[:, hh * DH:(hh + 1) * DH]
            cols.append(th * lax.rsqrt(jnp.mean(th * th, axis=-1, keepdims=True) + EPS))
        y_ref[0, pl.ds(r0, DH), :] = (jnp.concatenate(cols, axis=1) * _silu(g)).astype(BF16)
        return carry

    lax.fori_loop(0, L // DH, finish, 0)


def _retention(z, decay_logit, rope, state0, layer, states):
    B, L, _ = z.shape
    use_rope = rope is not None
    has_state = state0 is not None
    emit_state = states is not None
    chained = emit_state and not isinstance(states, str)
    H = RET_HEADS
    HP = min(H, max(1, RET_HEAD_ROWS // L))
    W = HP * RET_HEAD_DIM
    col = lambda off: pl.BlockSpec((1, L, W), lambda h, b, off=off: (b, 0, off * LANES // W + h))
    in_specs = [col(_Q0), col(_K0), col(_V0), col(_GR0)]
    args = [z, z, z, z]
    if use_rope:
        in_specs += [pl.BlockSpec((L, LANES), lambda h, b: (0, 0))] * 2
        args += list(rope)
    C = min(RET_BLOCK, L)
    in_specs.append(pl.BlockSpec((2, HP, 1, RET_BLOCK), lambda h, b: (0, h, 0, 0)))
    args.append(jnp.broadcast_to(decay_logit[:, :, None, None], (2, H, 1, RET_BLOCK)))
    sspec = pl.BlockSpec((1, 2, HP, RET_HEAD_DIM, RET_HEAD_DIM), lambda h, b: (b, 0, h, 0, 0))
    if has_state:
        in_specs.append(sspec)
        args.append(state0)
    out_specs = [pl.BlockSpec((1, L, W), lambda h, b: (b, 0, h))]
    out_shape = [jax.ShapeDtypeStruct((B, L, RET_WIDTH), BF16)]
    aliases = {}
    if emit_state:
        nl, l0 = (1, layer) if chained else (DEPTH, 0)
        out_specs.append(pl.BlockSpec((1, nl, 2, HP, RET_HEAD_DIM, RET_HEAD_DIM),
                                      lambda h, b: (b, l0, 0, h, 0, 0)))
        out_shape.append(jax.ShapeDtypeStruct((B, DEPTH, 2, H, RET_HEAD_DIM, RET_HEAD_DIM), F32))
        if chained:
            aliases = {len(args): 1}
            in_specs.append(pl.BlockSpec(memory_space=pl.ANY))
            args.append(states)
    return pl.pallas_call(
        functools.partial(_ret_kernel, L=L, use_rope=use_rope, has_state=has_state,
                          emit_state=emit_state, chained=chained, layer=layer),
        grid=(H // HP, B),
        in_specs=in_specs,
        out_specs=out_specs,
        out_shape=out_shape,
        input_output_aliases=aliases,
        scratch_shapes=[pltpu.VMEM((2, L, W), F32),
                        pltpu.VMEM((2, HP, RET_HEAD_DIM, RET_HEAD_DIM), F32),
                        pltpu.VMEM((2, HP, C, C), F32),
                        pltpu.VMEM((2, HP, 3, C, RET_HEAD_DIM), F32)],
        compiler_params=_params(("arbitrary", "arbitrary")),
        name="retention",
    )(*args)


def _dft_matrices(T):
    N = 2 * T
    k = np.arange(T, dtype=np.float64)[:, None]
    m = np.arange(T, dtype=np.float64)[None, :]
    ang = 2.0 * np.pi * k * m / N
    fwd = np.concatenate([np.cos(ang), -np.sin(ang)], axis=0)
    fwd[T, :] = (-1.0) ** np.arange(T)
    ck = np.full((T,), 2.0)
    ck[0] = 1.0
    inv_re = (np.cos(ang) * ck[:, None]).T / N
    inv_im = (-2.0 * np.sin(ang)).T / N
    inv_im[:, 0] = ((-1.0) ** np.arange(T)) / N
    inv = np.concatenate([inv_re, inv_im], axis=1)
    out = []
    for m in (fwd, inv):
        m = jnp.asarray(m, F32)
        hi = m.astype(BF16)
        out += [hi, (m - hi.astype(F32)).astype(BF16)]
    return tuple(out)


def _split_bf16(x):
    hi = x.astype(BF16)
    return hi, (x - hi.astype(F32)).astype(BF16)


def _dot3(a_hi, a_lo, b):
    b_hi, b_lo = _split_bf16(b)
    return (jnp.dot(a_hi, b_hi, preferred_element_type=F32)
            + jnp.dot(a_lo, b_hi, preferred_element_type=F32)
            + jnp.dot(a_hi, b_lo, preferred_element_type=F32))


def _dot1(a, b):
    return jnp.dot(a, b.astype(BF16), preferred_element_type=F32)


def _filt_kernel(zemb_ref, w1_ref, b1_ref, w2_ref, b2_ref, fr_ref, w3f_ref, w3b_ref, dec_ref,
                 bias_ref, fwdh_ref, fwdl_ref, kre_ref, kim_ref, hid_ref, hn_ref, F_ref, *, L):
    T = fwdh_ref.shape[1]
    nb = L // T

    @pl.when((pl.program_id(0) == 0) & (pl.program_id(1) == 0))
    def _():
        z1 = jnp.dot(zemb_ref[...], w1_ref[...], precision=HI, preferred_element_type=F32) + b1_ref[...]
        h1 = jnp.sin(fr_ref[0:1, :] * z1)
        z2 = jnp.dot(h1, w2_ref[...], precision=HI, preferred_element_type=F32) + b2_ref[...]
        hid = jnp.sin(fr_ref[1:2, :] * z2)
        hid_ref[0], hid_ref[1] = _split_bf16(hid)

    w3 = jnp.concatenate([w3f_ref[...], w3b_ref[...]], axis=1)
    w_hi, w_lo = _split_bf16(w3)
    h = (jnp.dot(hid_ref[0], w_hi, preferred_element_type=F32)
         + jnp.dot(hid_ref[1], w_hi, preferred_element_type=F32)
         + jnp.dot(hid_ref[0], w_lo, preferred_element_type=F32))
    h = h * jnp.concatenate([dec_ref[...]] * 2, axis=1)
    h = h / (jnp.sum(jnp.abs(h), axis=0, keepdims=True) + EPS)
    for dr in (0, 1):
        hn_ref[dr] = h[:, dr * LANES:(dr + 1) * LANES]
    for blk in range(nb):
        F = _dot3(fwdh_ref[...], fwdl_ref[...], h[blk * T:(blk + 1) * T])
        for dr in (0, 1):
            F_ref[dr, blk] = F[:, dr * LANES:(dr + 1) * LANES]

    row = lax.broadcasted_iota(jnp.int32, (T, LANES), 0)
    sgn = jnp.where((row & 1) == 0, 1.0, -1.0).astype(F32)
    row0 = row == 0
    bias = bias_ref[0]
    for d in range(-(nb - 1), nb):
        re = jnp.zeros((T, LANES), F32)
        im = jnp.zeros((T, LANES), F32)
        nyq = jnp.zeros((1, LANES), F32)
        if d >= 0:
            F = F_ref[0, d]
            re, im, nyq = re + F[:T], im + F[T:], nyq + F[T:T + 1]
        if d >= 1:
            F = F_ref[0, d - 1]
            p0 = hn_ref[0, (d - 1) * T:(d - 1) * T + 1, :]
            re, im, nyq = re + sgn * (F[:T] - p0), im + sgn * F[T:], nyq + (F[T:T + 1] - p0)
        e = -d
        if e >= 0:
            F = F_ref[1, e]
            re, im, nyq = re + F[:T], im - F[T:], nyq + F[T:T + 1]
        if e >= 1:
            F = F_ref[1, e - 1]
            p0 = hn_ref[1, (e - 1) * T:(e - 1) * T + 1, :]
            re, im, nyq = re + sgn * (F[:T] - p0), im - sgn * F[T:], nyq + (F[T:T + 1] - p0)
        if d == 0:
            re, nyq = re + bias, nyq + bias
        kre_ref[0, d + nb - 1] = re
        kim_ref[0, d + nb - 1] = jnp.where(row0, nyq, im)


def _hyena_filter_spectra(L, w1, b1, w2, b2, w3, freq, bias, fwd):
    T = fwd[0].shape[1]
    nb = L // T
    nlag = 2 * nb - 1
    P = LANES
    t = np.linspace(0.0, 1.0, L)[:, None]
    n_bands = (HY_EMB - 1) // 2
    f = np.linspace(1e-4, n_bands - 1, n_bands)
    ang = (2.0 * math.pi / L) * np.arange(L)[:, None] * f[None, :]
    zemb = np.concatenate([t, np.cos(ang), -np.sin(ang)], axis=-1)
    zemb = jnp.asarray(np.pad(zemb, ((0, 0), (0, P - HY_EMB))), F32)
    min_decay = math.log(HY_TARGET) / HY_LONG_DECAY_PCT
    max_decay = math.log(HY_TARGET) / HY_SHORT_DECAY_PCT
    deltas = np.abs(np.linspace(min_decay, max_decay, HY_WIDTH))
    dec = jnp.asarray(np.exp(-t * deltas[None, :]), F32)
    pad = P - HY_FFN
    w1p = jnp.pad(w1, ((0, P - HY_EMB), (0, pad)))
    b1p = jnp.pad(b1, (0, pad)).reshape(1, P)
    w2p = jnp.pad(w2, ((0, pad), (0, pad)))
    b2p = jnp.pad(b2, (0, pad)).reshape(1, P)
    w3p = jnp.pad(w3, ((0, pad), (0, 0)))
    frp = jnp.pad(freq, ((0, 0), (0, pad)))
    ncg = HY_WIDTH // P
    const = lambda shape: pl.BlockSpec(shape, lambda o, c: tuple(0 for _ in shape))
    kshape = jax.ShapeDtypeStruct((HY_ORDER, nlag, T, HY_WIDTH), F32)
    kspec = pl.BlockSpec((1, nlag, T, P), lambda o, c: (o, 0, 0, c))
    return pl.pallas_call(
        functools.partial(_filt_kernel, L=L),
        grid=(HY_ORDER, ncg),
        in_specs=[const((L, P)), const((P, P)), const((1, P)), const((P, P)), const((1, P)), const((2, P)),
                  pl.BlockSpec((P, P), lambda o, c: (0, o * 2 * ncg + c)),
                  pl.BlockSpec((P, P), lambda o, c: (0, o * 2 * ncg + ncg + c)),
                  pl.BlockSpec((L, P), lambda o, c: (0, c)),
                  pl.BlockSpec((1, 1, P), lambda o, c: (o, 0, c)),
                  const((2 * T, T)), const((2 * T, T))],
        out_specs=[kspec, kspec],
        out_shape=[kshape, kshape],
        scratch_shapes=[pltpu.VMEM((2, L, P), BF16), pltpu.VMEM((2, L, P), F32),
                        pltpu.VMEM((2, nb, 2 * T, P), F32)],
        compiler_params=_params(("arbitrary", "arbitrary")),
        name="hyena_filters",
    )(zemb, w1p, b1p, w2p, b2p, frp, w3p, w3p, dec, bias.reshape(HY_ORDER, 1, HY_WIDTH), *fwd)


CONV_ROWS = 128


def _conv_block(x_ref, t0, w, b, left):
    L, C = x_ref.shape
    T = CONV_ROWS
    blk = x_ref[t0:t0 + T, :]
    prev = x_ref[t0 - SUBLANES:t0, :] if t0 > 0 else jnp.zeros((SUBLANES, C), F32)
    nxt = x_ref[t0 + T:t0 + T + SUBLANES, :] if t0 + T < L else jnp.zeros((SUBLANES, C), F32)
    ext = jnp.concatenate([prev, blk, nxt], axis=0)
    n = T + 2 * SUBLANES
    acc = jnp.broadcast_to(b, (T, C))
    for tap in range(w.shape[0]):
        s = tap - left
        sh = blk if s == 0 else pltpu.roll(ext, (-s) % n, axis=0)[SUBLANES:SUBLANES + T]
        acc = acc + w[tap:tap + 1, :] * sh
    return acc


HY_LANES_ROWS = 2 * 2048
HY_MAX_BATCH = 8


def _toeplitz(k, u, mul):
    n = len(u)
    if n == 1:
        return [mul(k[0], u[0])]
    h = n // 2
    add = lambda a, b: tuple(x + y for x, y in zip(a, b))
    sub = lambda a, b: tuple(x - y for x, y in zip(a, b))
    t0, t1, t2 = k[h:h + n - 1], k[0:n - 1], k[n:2 * n - 1]
    p = _toeplitz(t0, [add(a, b) for a, b in zip(u[:h], u[h:])], mul)
    q = _toeplitz([sub(a, b) for a, b in zip(t1, t0)], u[h:], mul)
    r = _toeplitz([sub(a, b) for a, b in zip(t2, t0)], u[:h], mul)
    return [add(a, b) for a, b in zip(p, q)] + [add(a, b) for a, b in zip(p, r)]


def _hy_kernel(v_ref, x1_ref, x2_ref, g_ref, wv_ref, w1_ref, w2_ref, bv_ref, b1_ref, b2_ref,
               kre_ref, kim_ref, fwd_ref, inv_ref, y_ref,
               x1c_ref, x2c_ref, u_ref, U_ref, Y_ref, *, L):
    T = fwd_ref.shape[1]
    P = LANES
    nb = L // T
    NBT = v_ref.shape[0]
    left = (HY_SHORT - 1) // 2
    for t0 in range(0, L, CONV_ROWS):
        rows = slice(t0, t0 + CONV_ROWS)
        for p in range(NBT):
            u_ref[rows, p * P:(p + 1) * P] = _conv_block(v_ref.at[p], t0, wv_ref[...], bv_ref[...], left).astype(BF16)
            x1c_ref[p, rows, :] = _conv_block(x1_ref.at[p], t0, w1_ref[...], b1_ref[...], left)
            x2c_ref[p, rows, :] = _conv_block(x2_ref.at[p], t0, w2_ref[...], b2_ref[...], left)

    def block_spectra():
        for j in range(nb):
            U_ref[j] = jnp.dot(fwd_ref[...], u_ref[j * T:(j + 1) * T, :], preferred_element_type=F32)

    block_spectra()

    row0 = lax.broadcasted_iota(jnp.int32, (SUBLANES, NBT * P), 0) == 0
    tile = lambda x: jnp.concatenate([x] * NBT, axis=1)

    def cmul(k, u):
        kr, ki = tile(k[0]), tile(k[1])
        return (kr * u[0] - ki * u[1], kr * u[1] + ki * u[0])

    def rmul(k, u):
        return (tile(k[0]) * u[0], tile(k[1]) * u[1])

    def spectra_products(o):
        def load(r, rows):
            k = [(kre_ref[o, l, pl.ds(r, rows), :], kim_ref[o, l, pl.ds(r, rows), :]) for l in range(2 * nb - 1)]
            u = [(U_ref[j, pl.ds(r, rows), :], U_ref[j, pl.ds(T + r, rows), :]) for j in range(nb)]
            return k, u

        def chunk(rc, carry):
            r = pl.multiple_of(rc * SUBLANES, SUBLANES)
            y = _toeplitz(*load(r, SUBLANES), cmul)
            for i in range(nb):
                Y_ref[i, pl.ds(r, SUBLANES), :] = y[i][0]
                Y_ref[i, pl.ds(T + r, SUBLANES), :] = y[i][1]
            return carry

        lax.fori_loop(0, T // SUBLANES, chunk, 0)
        y = _toeplitz(*load(0, SUBLANES), rmul)
        for i in range(nb):
            Y_ref[i, 0:SUBLANES, :] = jnp.where(row0, y[i][0], Y_ref[i, 0:SUBLANES, :])
            Y_ref[i, T:T + SUBLANES, :] = jnp.where(row0, y[i][1], Y_ref[i, T:T + SUBLANES, :])

    def long_conv_block(i):
        return _dot1(inv_ref[...], Y_ref[i])

    spectra_products(0)

    def order0(i, carry):
        r0 = pl.multiple_of(i * T, T)
        y = long_conv_block(i)
        for p in range(NBT):
            z1 = x1c_ref[p, pl.ds(r0, T), :] * y[:, p * P:(p + 1) * P]
            u_ref[pl.ds(r0, T), p * P:(p + 1) * P] = z1.astype(BF16)
        return carry

    lax.fori_loop(0, nb, order0, 0, unroll=2 if nb % 2 == 0 else 1)
    block_spectra()
    spectra_products(1)

    def order1(i, carry):
        r0 = pl.multiple_of(i * T, T)
        y = long_conv_block(i)
        for p in range(NBT):
            out = x2c_ref[p, pl.ds(r0, T), :] * y[:, p * P:(p + 1) * P]
            y_ref[p, pl.ds(r0, T), :] = (out * _silu(g_ref[p, pl.ds(r0, T), :])).astype(BF16)
        return carry

    lax.fori_loop(0, nb, order1, 0, unroll=2 if nb % 2 == 0 else 1)


def _hyena(z, conv_w, conv_b, kre, kim, fwd, inv):
    B, L, _ = z.shape
    NBT = min(B, max(2, HY_LANES_ROWS // L), HY_MAX_BATCH)
    assert B % NBT == 0
    T = fwd.shape[1]
    nb = L // T
    nlag = 2 * nb - 1
    P = LANES
    ncg = HY_WIDTH // P
    col = lambda off: pl.BlockSpec((NBT, L, P), lambda c, b, off=off: (b, 0, off + c))
    wspec = lambda part: pl.BlockSpec((HY_SHORT, P), lambda c, b, part=part: (0, part * ncg + c))
    bspec = lambda part: pl.BlockSpec((1, P), lambda c, b, part=part: (0, part * ncg + c))
    kspec = pl.BlockSpec((HY_ORDER, nlag, T, P), lambda c, b: (0, 0, 0, c))
    fspec = pl.BlockSpec((2 * T, T), lambda c, b: (0, 0))
    ispec = pl.BlockSpec((T, 2 * T), lambda c, b: (0, 0))
    cb = conv_b.reshape(1, 3 * HY_WIDTH)
    return pl.pallas_call(
        functools.partial(_hy_kernel, L=L),
        grid=(ncg, B // NBT),
        in_specs=[col(_HV0), col(_HX10), col(_HX20), col(_GH0),
                  wspec(0), wspec(1), wspec(2), bspec(0), bspec(1), bspec(2),
                  kspec, kspec, fspec, ispec],
        out_specs=pl.BlockSpec((NBT, L, P), lambda c, b: (b, 0, c)),
        out_shape=jax.ShapeDtypeStruct((B, L, HY_WIDTH), BF16),
        scratch_shapes=[pltpu.VMEM((NBT, L, P), F32), pltpu.VMEM((NBT, L, P), F32),
                        pltpu.VMEM((L, NBT * P), BF16),
                        pltpu.VMEM((nb, 2 * T, NBT * P), F32), pltpu.VMEM((nb, 2 * T, NBT * P), F32)],
        compiler_params=_params(("arbitrary", "arbitrary")),
        name="hyena",
    )(z, z, z, z, conv_w, conv_w, conv_w, cb, cb, cb, kre, kim, fwd, inv)


LRU_TILE = 256
LRU_GROUPS = LRU_TILE // LANES
LRU_ROWS = 128


def _lru_kernel(*refs, L, has_state):
    it = iter(refs)
    x_ref, g_ref, cw_ref, cb_ref, wg_ref, gb_ref, lam_ref = (next(it) for _ in range(7))
    h0_ref = next(it) if has_state else None
    y_ref, sN_ref, xp_ref, A_ref, B_ref, H_ref, C_ref = (next(it) for _ in range(7))

    S = L // SUBLANES
    W = LRU_TILE
    G = LRU_GROUPS
    R = LRU_ROWS
    left = LRU_CONV // 2
    pad = left * SUBLANES

    sub = lax.broadcasted_iota(jnp.int32, (SUBLANES, W), 0)
    for k in range(left):
        tail = pltpu.roll(x_ref[0, L - pad + k * SUBLANES:L - pad + (k + 1) * SUBLANES, :], 1, axis=0)
        xp_ref[k * SUBLANES:(k + 1) * SUBLANES, :] = jnp.where(sub == 0, 0.0, tail)
    for k in range(LRU_CONV - 1 - left):
        head = pltpu.roll(x_ref[0, k * SUBLANES:(k + 1) * SUBLANES, :], SUBLANES - 1, axis=0)
        xp_ref[pad + L + k * SUBLANES:pad + L + (k + 1) * SUBLANES, :] = jnp.where(sub == SUBLANES - 1, 0.0, head)

    def copy(c, carry):
        r0 = pl.multiple_of(c * R, R)
        xp_ref[pl.ds(pad + r0, R), :] = x_ref[0, pl.ds(r0, R), :]
        return carry

    lax.fori_loop(0, L // R, copy, 0)

    csp = [(0.5 * LRU_C) * _softplus(-lam_ref[d]) for d in (0, 1)]

    def coeffs(c, carry):
        r0 = pl.multiple_of(c * R, R)
        u = jnp.broadcast_to(cb_ref[...], (R, W))
        for k in range(LRU_CONV):
            u = u + cw_ref[k:k + 1, :] * xp_ref[pl.ds(r0 + k * SUBLANES, R), :]
        half = jnp.dot(u.astype(BF16), wg_ref[0], preferred_element_type=F32) + gb_ref[0]
        hu = 0.5 * u
        for d in (0, 1):
            t_r = jnp.tanh(half[:, (2 * d) * W:(2 * d + 1) * W])
            t_i = jnp.tanh(half[:, (2 * d + 1) * W:(2 * d + 2) * W])
            nl = csp[d] * t_r + csp[d]
            a = jnp.exp(-nl)
            s2 = jnp.tanh(nl) * (a * a + 1.0)
            b = jnp.where(s2 > 0.0, s2 * lax.rsqrt(s2), 0.0) * (t_i * hu + hu)
            for gi in range(G):
                lanes = slice(gi * LANES, (gi + 1) * LANES)
                A_ref[d, gi, pl.ds(r0, R), :] = a[:, lanes]
                B_ref[d, gi, pl.ds(r0, R), :] = b[:, lanes]
        return carry

    lax.fori_loop(0, L // R, coeffs, 0)

    def scan_body(t, carry):
        out = []
        for d in (0, 1):
            r8 = pl.multiple_of((t if d == 0 else S - 1 - t) * SUBLANES, SUBLANES)
            for gi in range(G):
                h, acc = carry[d * G + gi]
                a = A_ref[d, gi, pl.ds(r8, SUBLANES), :]
                h = a * h + B_ref[d, gi, pl.ds(r8, SUBLANES), :]
                acc = acc * a
                H_ref[d, gi, pl.ds(r8, SUBLANES), :] = h
                C_ref[d, gi, pl.ds(r8, SUBLANES), :] = acc
                out.append((h, acc))
        return tuple(out)

    init = tuple((jnp.zeros((SUBLANES, LANES), F32), jnp.ones((SUBLANES, LANES), F32))
                 for _ in range(2 * G))
    lax.fori_loop(0, S, scan_body, init, unroll=4)

    hin = [None] * (2 * G)
    for d in (0, 1):
        last = (S - 1 if d == 0 else 0) * SUBLANES
        for gi in range(G):
            hl = H_ref[d, gi, last:last + SUBLANES, :]
            ac = C_ref[d, gi, last:last + SUBLANES, :]
            if has_state:
                h = h0_ref[0, d:d + 1, gi * LANES:(gi + 1) * LANES]
            else:
                h = jnp.zeros((1, LANES), F32)
            rows = [None] * SUBLANES
            for j in (range(SUBLANES) if d == 0 else range(SUBLANES - 1, -1, -1)):
                rows[j] = h
                h = hl[j:j + 1, :] + ac[j:j + 1, :] * h
            hin[d * G + gi] = jnp.concatenate([jnp.concatenate(rows, axis=0)] * (R // SUBLANES), axis=0)
            sN_ref[0, d:d + 1, gi * LANES:(gi + 1) * LANES] = h

    def combine(c, carry):
        r0 = pl.multiple_of(c * R, R)
        cols = []
        for gi in range(G):
            tot = jnp.zeros((R, LANES), F32)
            for d in (0, 1):
                tot = tot + (H_ref[d, gi, pl.ds(r0, R), :] + C_ref[d, gi, pl.ds(r0, R), :] * hin[d * G + gi])
            cols.append(tot)
        y_ref[0, pl.ds(r0, R), :] = jnp.concatenate(cols, axis=1) * _silu(g_ref[0, pl.ds(r0, R), :])
        return carry

    lax.fori_loop(0, L // R, combine, 0)


def _rglru(z4, conv_w, conv_b, gate_w, gate_b, lam, h0):
    B, S, _, _ = z4.shape
    L = S * SUBLANES
    has_state = h0 is not None
    W = LRU_TILE
    nh = LRU_WIDTH // W
    bpt = W // LRU_BLOCK_DIM
    gw = gate_w.reshape(2, 2, nh, bpt, LRU_BLOCK_DIM, LRU_BLOCK_DIM)
    eye = jnp.eye(bpt, dtype=F32)
    dense = (0.5 * jnp.einsum('dkhnij,nm->hnidkmj', gw, eye)).reshape(nh, W, 4 * W).astype(BF16)
    gb = 0.5 * gate_b.reshape(2, 2, nh, W).transpose(2, 0, 1, 3).reshape(nh, 1, 4 * W)
    zr = z4.reshape(B, L, D_LRU)
    off = lambda base: pl.BlockSpec((1, L, W), lambda b, h, base=base: (b, 0, base + h))
    in_specs = [off(0), off(LRU_WIDTH // W),
                pl.BlockSpec((LRU_CONV, W), lambda b, h: (0, h)),
                pl.BlockSpec((1, W), lambda b, h: (0, h)),
                pl.BlockSpec((1, W, 4 * W), lambda b, h: (h, 0, 0)),
                pl.BlockSpec((1, 1, 4 * W), lambda b, h: (h, 0, 0)),
                pl.BlockSpec((2, 1, W), lambda b, h: (0, 0, h))]
    args = [zr, zr, conv_w, conv_b.reshape(1, LRU_WIDTH), dense, gb, lam.reshape(2, 1, LRU_WIDTH)]
    if has_state:
        in_specs.append(pl.BlockSpec((1, 2, W), lambda b, h: (b, 0, h)))
        args.append(h0)
    y, s = pl.pallas_call(
        functools.partial(_lru_kernel, L=L, has_state=has_state),
        grid=(B, nh),
        in_specs=in_specs,
        out_specs=[pl.BlockSpec((1, L, W), lambda b, h: (b, 0, h)),
                   pl.BlockSpec((1, 2, W), lambda b, h: (b, 0, h))],
        out_shape=[jax.ShapeDtypeStruct((B, L, LRU_WIDTH), F32),
                   jax.ShapeDtypeStruct((B, 2, LRU_WIDTH), F32)],
        scratch_shapes=[pltpu.VMEM((L + (LRU_CONV - 1) * SUBLANES, W), F32)]
        + [pltpu.VMEM((2, LRU_GROUPS, L, LANES), F32) for _ in range(4)],
        compiler_params=_params(("arbitrary", "arbitrary")),
        name="rglru",
    )(*args)
    return y.reshape(B, S, SUBLANES, LRU_WIDTH), s


def _rope_tables(L):
    rows = L // GRID_W
    row = np.repeat(np.arange(rows, dtype=np.float64), GRID_W)
    col = np.tile(np.arange(GRID_W, dtype=np.float64), rows)
    n_f = RET_HEAD_DIM // 4
    inv = ROPE_BASE ** (-np.arange(n_f, dtype=np.float64) / n_f)
    ang = np.concatenate([row[:, None] * inv[None], col[:, None] * inv[None]], axis=-1)
    cos, sin = np.cos(ang), np.sin(ang)
    return (jnp.asarray(np.concatenate([cos, cos], axis=-1), F32),
            jnp.asarray(np.concatenate([-sin, sin], axis=-1), F32))


def kernel(x_prompt, x_sample, state_ret, state_lru, c, c_ctx, norm_g, ada_w, ada_b, w_in, ret_decay_logit, hy_conv_w, hy_conv_b, hy_ffn_w1, hy_ffn_b1, hy_ffn_w2, hy_ffn_b2, hy_ffn_w3, hy_freq, hy_bias, lru_conv_w, lru_conv_b, lru_gate_w, lru_gate_b, lru_lambda, w_out, final_g):
    Bp, Lp, _ = x_prompt.shape
    Bs, Ls, _ = x_sample.shape
    assert Bs + 1 <= SUBLANES

    cvec = jnp.zeros((SUBLANES, D_MODEL), F32).at[:Bs].set(c).at[Bs].set(c_ctx)
    mod = _modulation(cvec, ada_w, ada_b)
    rope = _rope_tables(Ls)
    dft = {L: _dft_matrices(min(DFT_MAX, L)) for L in {Lp, Ls}}
    w_in_b = w_in.astype(BF16)
    w_out_b = w_out.astype(BF16)
    fg = final_g.reshape(1, D_MODEL)

    xc, xl = x_prompt, x_sample
    new_state_ret, new_lru = None, []
    for l in range(DEPTH):
        shift, scale, gate = (mod[l, :, i * D_MODEL:(i + 1) * D_MODEL] for i in range(3))
        g = norm_g[l].reshape(1, D_MODEL)
        final = l == DEPTH - 1
        filt = {}
        for L in sorted({Lp, Ls}):
            filt[L] = _hyena_filter_spectra(L, hy_ffn_w1[l], hy_ffn_b1[l], hy_ffn_w2[l], hy_ffn_b2[l],
                                            hy_ffn_w3[l], hy_freq[l], hy_bias[l], dft[L][:2])

        def layer(x, sel, rope_t, ret_s0, lru_s0, ret_states):
            sh, sc, gt = (m[sel][:, None, :] for m in (shift, scale, gate))
            z, z_lru = _in_proj(x, sh, sc, g, w_in_b[l])
            y_ret, *s_ret = _retention(z, ret_decay_logit[l], rope_t, ret_s0, l, ret_states)
            kre, kim = filt[x.shape[1]]
            fwd_hi, _, inv_hi, _ = dft[x.shape[1]]
            y_hy = _hyena(z, hy_conv_w[l], hy_conv_b[l], kre, kim, fwd_hi, inv_hi)
            y_lru, s_lru = _rglru(z_lru, lru_conv_w[l], lru_conv_b[l], lru_gate_w[l], lru_gate_b[l],
                                  lru_lambda[l], lru_s0)
            return _out_proj(y_ret, y_hy, y_lru, x, gt, w_out_b[l], fg, final), s_ret, s_lru

        xc, (new_state_ret,), ls = layer(xc, slice(Bs, Bs + 1), None, None, None,
                                         "new" if l == 0 else new_state_ret)
        new_lru.append(ls)
        xl, _, _ = layer(xl, slice(0, Bs), rope, state_ret[:, l], state_lru[:, l], None)

    new_state_lru = jnp.stack(new_lru, axis=1).astype(x_prompt.dtype)
    return (xc, xl, new_state_ret.astype(x_prompt.dtype), new_state_lru)
```

```python
import functools
import math

import numpy as np
import jax
import jax.numpy as jnp
from jax import lax
from jax.experimental import pallas as pl
from jax.experimental.pallas import tpu as pltpu

F32 = jnp.float32
BF16 = jnp.bfloat16
HI = lax.Precision.HIGHEST

D_MODEL = 1024
DEPTH = 2
GRID_W = 64
EPS = 1e-6
RET_HEADS = 4
RET_HEAD_DIM = 128
RET_WIDTH = RET_HEADS * RET_HEAD_DIM
ROPE_BASE = 10000.0
HY_WIDTH = 512
HY_ORDER = 2
HY_SHORT = 3
HY_EMB = 33
HY_FFN = 64
HY_SHORT_DECAY_PCT = 0.3
HY_LONG_DECAY_PCT = 1.5
HY_TARGET = 1e-2
LRU_WIDTH = 512
LRU_BLOCKS = 8
LRU_BLOCK_DIM = LRU_WIDTH // LRU_BLOCKS
LRU_CONV = 4
LRU_C = 8.0
D_MIX = RET_WIDTH + HY_WIDTH + LRU_WIDTH
D_IN = 4 * RET_WIDTH + 4 * HY_WIDTH + 2 * LRU_WIDTH

LANES = 128
SUBLANES = 8
DFT_MAX = 512
ROW_TILE = 256
VMEM_LIMIT = 56 * 1024 * 1024

_Q0, _K0, _V0, _GR0 = 0, 4, 8, 12
_HV0, _HX10, _HX20, _GH0 = 16, 20, 24, 28


def _sigmoid(x):
    return 0.5 * jnp.tanh(0.5 * x) + 0.5


def _silu(x):
    return x * _sigmoid(x)


def _softplus(x):
    return jnp.maximum(x, 0.0) + jnp.log1p(jnp.exp(-jnp.abs(x)))


def _params(sem):
    return pltpu.CompilerParams(dimension_semantics=sem, vmem_limit_bytes=VMEM_LIMIT)


def _mod_kernel(c_ref, w_ref, b_ref, o_ref):
    s_hi, s_lo = _split_bf16(_silu(c_ref[...]))
    w_hi, w_lo = _split_bf16(w_ref[0])
    o_ref[0] = (jnp.dot(s_hi, w_hi, preferred_element_type=F32)
                + jnp.dot(s_lo, w_hi, preferred_element_type=F32)
                + jnp.dot(s_hi, w_lo, preferred_element_type=F32)) + b_ref[0]


def _modulation(cvec, ada_w, ada_b):
    tn = 768
    return pl.pallas_call(
        _mod_kernel,
        grid=(DEPTH, 3 * D_MODEL // tn),
        in_specs=[pl.BlockSpec((SUBLANES, D_MODEL), lambda l, j: (0, 0)),
                  pl.BlockSpec((1, D_MODEL, tn), lambda l, j: (l, 0, j)),
                  pl.BlockSpec((1, 1, tn), lambda l, j: (l, 0, j))],
        out_specs=pl.BlockSpec((1, SUBLANES, tn), lambda l, j: (l, 0, j)),
        out_shape=jax.ShapeDtypeStruct((DEPTH, SUBLANES, 3 * D_MODEL), F32),
        compiler_params=_params(("arbitrary", "arbitrary")),
        name="modulation",
    )(cvec, ada_w, ada_b.reshape(DEPTH, 1, 3 * D_MODEL))


D_MAIN = 4 * RET_WIDTH + 4 * HY_WIDTH
D_LRU = 2 * LRU_WIDTH


def _in_kernel(x_ref, sh_ref, sc_ref, g_ref, w_ref, o_ref, ol_ref):
    _, nseg, rows, _ = x_ref.shape
    x = x_ref[0].reshape(nseg * rows, D_MODEL)
    ms = jnp.mean(x * x, axis=-1, keepdims=True)
    y = x * lax.rsqrt(ms + EPS) * g_ref[...]
    h = (y * (1.0 + sc_ref[0]) + sh_ref[0]).astype(BF16)
    tn = 1024
    for n in range(D_MAIN // tn):
        z = jnp.dot(h, w_ref[:, n * tn:(n + 1) * tn], preferred_element_type=F32)
        o_ref[0, :, :, n * tn:(n + 1) * tn] = z.reshape(nseg, rows, tn)
    zl = jnp.dot(h, w_ref[:, D_MAIN:D_IN], preferred_element_type=F32)
    for j in range(nseg):
        ol_ref[0, :, j, :] = zl[j * rows:(j + 1) * rows]


def _in_proj(x, shift, scale, g, w_bf16, layer):
    B, L, _ = x.shape
    S = L // SUBLANES
    rows = min(ROW_TILE // SUBLANES, S)
    per_batch = shift.shape[0] > 1
    mod_map = (lambda b, i: (b, 0, 0)) if per_batch else (lambda b, i: (0, 0, 0))
    seg = lambda C: pl.BlockSpec((1, SUBLANES, rows, C), lambda b, i: (b, 0, i, 0))
    z, z_lru = pl.pallas_call(
        _in_kernel,
        grid=(B, S // rows),
        in_specs=[seg(D_MODEL),
                  pl.BlockSpec((1, 1, D_MODEL), mod_map),
                  pl.BlockSpec((1, 1, D_MODEL), mod_map),
                  pl.BlockSpec((1, D_MODEL), lambda b, i: (0, 0)),
                  pl.BlockSpec((None, D_MODEL, D_IN), lambda b, i: (layer, 0, 0),
                               pipeline_mode=pl.Buffered(1))],
        out_specs=[seg(D_MAIN), pl.BlockSpec((1, rows, SUBLANES, D_LRU), lambda b, i: (b, i, 0, 0))],
        out_shape=[jax.ShapeDtypeStruct((B, SUBLANES, S, D_MAIN), F32),
                   jax.ShapeDtypeStruct((B, S, SUBLANES, D_LRU), F32)],
        compiler_params=_params(("arbitrary", "arbitrary")),
        name="in_proj",
    )(x.reshape(B, SUBLANES, S, D_MODEL), shift, scale, g, w_bf16)
    return z.reshape(B, L, D_MAIN), z_lru


OUT_ROWS = 512


def _out_kernel(yr_ref, yh_ref, yl_ref, x_ref, gate_ref, w_ref, fg_ref, o_ref, *, final):
    _, nseg, rows, _ = x_ref.shape
    n = nseg * rows
    acc = jnp.dot(yr_ref[0].reshape(n, RET_WIDTH), w_ref[0:RET_WIDTH], preferred_element_type=F32)
    acc = acc + jnp.dot(yh_ref[0].reshape(n, HY_WIDTH), w_ref[RET_WIDTH:RET_WIDTH + HY_WIDTH],
                        preferred_element_type=F32)
    yl = jnp.concatenate([yl_ref[0, :, j, :] for j in range(nseg)], axis=0)
    acc = acc + jnp.dot(yl.astype(BF16), w_ref[RET_WIDTH + HY_WIDTH:D_MIX], preferred_element_type=F32)
    x = x_ref[0].reshape(n, D_MODEL) + gate_ref[0] * acc
    if final:
        ms = jnp.mean(x * x, axis=-1, keepdims=True)
        x = x * lax.rsqrt(ms + EPS) * fg_ref[...]
    o_ref[0] = x.reshape(nseg, rows, D_MODEL)


def _out_proj(y_ret, y_hy, y_lru, x, gate, w_bf16, layer, final_g, final):
    B, L, _ = x.shape
    S = L // SUBLANES
    rows = min(OUT_ROWS // SUBLANES, S)
    per_batch = gate.shape[0] > 1
    mod_map = (lambda b, i: (b, 0, 0)) if per_batch else (lambda b, i: (0, 0, 0))
    seg = lambda C: pl.BlockSpec((1, SUBLANES, rows, C), lambda b, i: (b, 0, i, 0))
    view = lambda a: a.reshape(B, SUBLANES, S, a.shape[-1])
    out = pl.pallas_call(
        functools.partial(_out_kernel, final=final),
        grid=(B, S // rows),
        in_specs=[seg(RET_WIDTH), seg(HY_WIDTH),
                  pl.BlockSpec((1, rows, SUBLANES, LRU_WIDTH), lambda b, i: (b, i, 0, 0)),
                  seg(D_MODEL),
                  pl.BlockSpec((1, 1, D_MODEL), mod_map),
                  pl.BlockSpec((None, D_MIX, D_MODEL), lambda b, i: (layer, 0, 0),
                               pipeline_mode=pl.Buffered(1)),
                  pl.BlockSpec((1, D_MODEL), lambda b, i: (0, 0))],
        out_specs=seg(D_MODEL),
        out_shape=jax.ShapeDtypeStruct((B, SUBLANES, S, D_MODEL), F32),
        compiler_params=_params(("arbitrary", "arbitrary")),
        name="out_proj",
    )(view(y_ret), view(y_hy), y_lru, view(x), gate, w_bf16, final_g)
    return out.reshape(B, L, D_MODEL)


RET_HEAD_ROWS = 2 * 2048
RET_BLOCK = 256


def _ret_kernel(*refs, L, use_rope, has_state, emit_state, chained, layer):
    it = iter(refs)
    q_ref, k_ref, v_ref, g_ref = next(it), next(it), next(it), next(it)
    cos_ref = sin_ref = s0_ref = sN_ref = None
    if use_rope:
        cos_ref, sin_ref = next(it), next(it)
    lg_ref = next(it)
    if has_state:
        s0_ref = next(it)
    if chained:
        next(it)
    y_ref = next(it)
    if emit_state:
        sN_ref = next(it)
    o_ref, R_ref, M_ref, Z_ref = (next(it) for _ in range(4))

    C = min(RET_BLOCK, L)
    DH = RET_HEAD_DIM
    HP = q_ref.shape[-1] // DH
    n = L // C
    cross = has_state or n > 1
    ii = lax.broadcasted_iota(jnp.int32, (C, C), 0)
    jj = lax.broadcasted_iota(jnp.int32, (C, C), 1)
    ri = lax.broadcasted_iota(jnp.int32, (C, DH), 0).astype(F32)
    kscale = DH ** -0.5

    @pl.when(pl.program_id(1) == 0)
    def _():
        for d in (0, 1):
            for hh in range(HP):
                lg = -_softplus(-lg_ref[d, hh])
                diff = ((ii - jj) if d == 0 else (jj - ii)).astype(F32)
                M_ref[d, hh] = jnp.where(diff >= 0, jnp.exp(lg[:, :C] * jnp.maximum(diff, 0.0)), 0.0)
                lgh = lg[:, :DH]
                Z_ref[d, hh, 0] = jnp.exp(lgh * ((C - 1.0 - ri) if d == 0 else ri))
                Z_ref[d, hh, 1] = jnp.exp(lgh * ((ri + 1.0) if d == 0 else (C - ri)))
                Z_ref[d, hh, 2] = jnp.exp(jnp.broadcast_to(lgh * C, (C, DH)))

    for d in (0, 1):
        for hh in range(HP):
            if has_state:
                R_ref[d, hh] = s0_ref[0, d, hh]
            else:
                R_ref[d, hh] = jnp.zeros((DH, DH), F32)

    def body(t, carry):
        for d in (0, 1):
            c = t if d == 0 else n - 1 - t
            r0 = pl.multiple_of(c * C, C)
            for hh in range(HP):
                lanes = slice(hh * DH, (hh + 1) * DH)
                q = q_ref[0, pl.ds(r0, C), lanes]
                k = k_ref[0, pl.ds(r0, C), lanes] * kscale
                vb = v_ref[0, pl.ds(r0, C), lanes].astype(BF16)
                if use_rope:
                    cs = cos_ref[pl.ds(r0, C), :]
                    sn = sin_ref[pl.ds(r0, C), :]
                    q = q * cs + pltpu.roll(q, 64, axis=1) * sn
                    k = k * cs + pltpu.roll(k, 64, axis=1) * sn
                s = lax.dot_general(q.astype(BF16), k.astype(BF16), (((1,), (1,)), ((), ())),
                                    preferred_element_type=F32) * M_ref[d, hh]
                o = jnp.dot(s.astype(BF16), vb, preferred_element_type=F32)
                R = R_ref[d, hh]
                if cross:
                    o = o + jnp.dot((q * Z_ref[d, hh, 1]).astype(BF16), R.astype(BF16),
                                    preferred_element_type=F32)
                kv = lax.dot_general((k * Z_ref[d, hh, 0]).astype(BF16), vb, (((0,), (0,)), ((), ())),
                                     preferred_element_type=F32)
                R_ref[d, hh] = Z_ref[d, hh, 2, 0:DH, :] * R + kv
                o_ref[d, pl.ds(r0, C), lanes] = o
        return carry

    lax.fori_loop(0, n, body, 0, unroll=2 if n % 2 == 0 else 1)
    if emit_state:
        for l in range(sN_ref.shape[1]):
            if chained or l == layer:
                for d in (0, 1):
                    for hh in range(HP):
                        sN_ref[0, l, d, hh] = R_ref[d, hh]
            else:
                sN_ref[0, l] = jnp.zeros(sN_ref.shape[2:], F32)

    def finish(c, carry):
        r0 = pl.multiple_of(c * DH, DH)
        tot = o_ref[0, pl.ds(r0, DH), :] + o_ref[1, pl.ds(r0, DH), :]
        g = g_ref[0, pl.ds(r0, DH), :]
        cols = []
        for hh in range(HP):
            th = tot[:, hh * DH:(hh + 1) * DH]
            cols.append(th * lax.rsqrt(jnp.mean(th * th, axis=-1, keepdims=True) + EPS))
        y_ref[0, pl.ds(r0, DH), :] = (jnp.concatenate(cols, axis=1) * _silu(g)).astype(BF16)
        return carry

    lax.fori_loop(0, L // DH, finish, 0)


def _retention(z, decay_logit, rope, state0, layer, states):
    B, L, _ = z.shape
    use_rope = rope is not None
    has_state = state0 is not None
    emit_state = states is not None
    chained = emit_state and not isinstance(states, str)
    H = RET_HEADS
    HP = min(H, max(1, RET_HEAD_ROWS // L))
    W = HP * RET_HEAD_DIM
    col = lambda off: pl.BlockSpec((1, L, W), lambda h, b, off=off: (b, 0, off * LANES // W + h))
    in_specs = [col(_Q0), col(_K0), col(_V0), col(_GR0)]
    args = [z, z, z, z]
    if use_rope:
        in_specs += [pl.BlockSpec((L, LANES), lambda h, b: (0, 0))] * 2
        args += list(rope)
    C = min(RET_BLOCK, L)
    in_specs.append(pl.BlockSpec((2, HP, 1, RET_BLOCK), lambda h, b: (0, h, 0, 0)))
    args.append(jnp.broadcast_to(decay_logit[:, :, None, None], (2, H, 1, RET_BLOCK)))
    if has_state:
        in_specs.append(pl.BlockSpec((1, None, 2, HP, RET_HEAD_DIM, RET_HEAD_DIM),
                                     lambda h, b: (b, layer, 0, h, 0, 0)))
        args.append(state0)
    out_specs = [pl.BlockSpec((1, L, W), lambda h, b: (b, 0, h))]
    out_shape = [jax.ShapeDtypeStruct((B, L, RET_WIDTH), BF16)]
    aliases = {}
    if emit_state:
        nl, l0 = (1, layer) if chained else (DEPTH, 0)
        out_specs.append(pl.BlockSpec((1, nl, 2, HP, RET_HEAD_DIM, RET_HEAD_DIM),
                                      lambda h, b: (b, l0, 0, h, 0, 0)))
        out_shape.append(jax.ShapeDtypeStruct((B, DEPTH, 2, H, RET_HEAD_DIM, RET_HEAD_DIM), F32))
        if chained:
            aliases = {len(args): 1}
            in_specs.append(pl.BlockSpec(memory_space=pl.ANY))
            args.append(states)
    return pl.pallas_call(
        functools.partial(_ret_kernel, L=L, use_rope=use_rope, has_state=has_state,
                          emit_state=emit_state, chained=chained, layer=layer),
        grid=(H // HP, B),
        in_specs=in_specs,
        out_specs=out_specs,
        out_shape=out_shape,
        input_output_aliases=aliases,
        scratch_shapes=[pltpu.VMEM((2, L, W), F32),
                        pltpu.VMEM((2, HP, RET_HEAD_DIM, RET_HEAD_DIM), F32),
                        pltpu.VMEM((2, HP, C, C), F32),
                        pltpu.VMEM((2, HP, 3, C, RET_HEAD_DIM), F32)],
        compiler_params=_params(("arbitrary", "arbitrary")),
        name="retention",
    )(*args)


def _dft_matrices(T):
    N = 2 * T
    k = np.arange(T, dtype=np.float64)[:, None]
    m = np.arange(T, dtype=np.float64)[None, :]
    ang = 2.0 * np.pi * k * m / N
    fwd = np.concatenate([np.cos(ang), -np.sin(ang)], axis=0)
    fwd[T, :] = (-1.0) ** np.arange(T)
    ck = np.full((T,), 2.0)
    ck[0] = 1.0
    inv_re = (np.cos(ang) * ck[:, None]).T / N
    inv_im = (-2.0 * np.sin(ang)).T / N
    inv_im[:, 0] = ((-1.0) ** np.arange(T)) / N
    inv = np.concatenate([inv_re, inv_im], axis=1)
    return jnp.asarray(fwd, F32).astype(BF16), jnp.asarray(inv, F32).astype(BF16)


def _split_bf16(x):
    hi = x.astype(BF16)
    return hi, (x - hi.astype(F32)).astype(BF16)


def _dot1(a, b):
    return jnp.dot(a, b.astype(BF16), preferred_element_type=F32)


def _filt_kernel(zemb_ref, w1_ref, b1_ref, w2_ref, b2_ref, fr_ref, w3f_ref, w3b_ref, dec_ref,
                 bias_ref, fwd_ref, kre_ref, kim_ref, hid_ref, hn_ref, F_ref, *, L):
    T = fwd_ref.shape[1]
    nb = L // T

    @pl.when((pl.program_id(0) == 0) & (pl.program_id(1) == 0))
    def _():
        z1 = jnp.dot(zemb_ref[...], w1_ref[...], precision=HI, preferred_element_type=F32) + b1_ref[...]
        h1 = jnp.sin(fr_ref[0:1, :] * z1)
        z2 = jnp.dot(h1, w2_ref[...], precision=HI, preferred_element_type=F32) + b2_ref[...]
        hid = jnp.sin(fr_ref[1:2, :] * z2)
        hid_ref[0], hid_ref[1] = _split_bf16(hid)

    w3 = jnp.concatenate([w3f_ref[...], w3b_ref[...]], axis=1)
    w_hi, w_lo = _split_bf16(w3)
    h = (jnp.dot(hid_ref[0], w_hi, preferred_element_type=F32)
         + jnp.dot(hid_ref[1], w_hi, preferred_element_type=F32)
         + jnp.dot(hid_ref[0], w_lo, preferred_element_type=F32))
    h = h * jnp.concatenate([dec_ref[...]] * 2, axis=1)
    h = h / (jnp.sum(jnp.abs(h), axis=0, keepdims=True) + EPS)
    for dr in (0, 1):
        hn_ref[dr] = h[:, dr * LANES:(dr + 1) * LANES]
    for blk in range(nb):
        F = _dot1(fwd_ref[...], h[blk * T:(blk + 1) * T])
        for dr in (0, 1):
            F_ref[dr, blk] = F[:, dr * LANES:(dr + 1) * LANES]

    row = lax.broadcasted_iota(jnp.int32, (T, LANES), 0)
    sgn = jnp.where((row & 1) == 0, 1.0, -1.0).astype(F32)
    row0 = row == 0
    bias = bias_ref[0]
    for d in range(-(nb - 1), nb):
        re = jnp.zeros((T, LANES), F32)
        im = jnp.zeros((T, LANES), F32)
        nyq = jnp.zeros((1, LANES), F32)
        if d >= 0:
            F = F_ref[0, d]
            re, im, nyq = re + F[:T], im + F[T:], nyq + F[T:T + 1]
        if d >= 1:
            F = F_ref[0, d - 1]
            p0 = hn_ref[0, (d - 1) * T:(d - 1) * T + 1, :]
            re, im, nyq = re + sgn * (F[:T] - p0), im + sgn * F[T:], nyq + (F[T:T + 1] - p0)
        e = -d
        if e >= 0:
            F = F_ref[1, e]
            re, im, nyq = re + F[:T], im - F[T:], nyq + F[T:T + 1]
        if e >= 1:
            F = F_ref[1, e - 1]
            p0 = hn_ref[1, (e - 1) * T:(e - 1) * T + 1, :]
            re, im, nyq = re + sgn * (F[:T] - p0), im - sgn * F[T:], nyq + (F[T:T + 1] - p0)
        if d == 0:
            re, nyq = re + bias, nyq + bias
        kre_ref[0, d + nb - 1] = re
        kim_ref[0, d + nb - 1] = jnp.where(row0, nyq, im)


def _hyena_filter_spectra(L, w1, b1, w2, b2, w3, freq, bias, fwd):
    T = fwd.shape[1]
    nb = L // T
    nlag = 2 * nb - 1
    P = LANES
    t = np.linspace(0.0, 1.0, L)[:, None]
    n_bands = (HY_EMB - 1) // 2
    f = np.linspace(1e-4, n_bands - 1, n_bands)
    ang = (2.0 * math.pi / L) * np.arange(L)[:, None] * f[None, :]
    zemb = np.concatenate([t, np.cos(ang), -np.sin(ang)], axis=-1)
    zemb = jnp.asarray(np.pad(zemb, ((0, 0), (0, P - HY_EMB))), F32)
    min_decay = math.log(HY_TARGET) / HY_LONG_DECAY_PCT
    max_decay = math.log(HY_TARGET) / HY_SHORT_DECAY_PCT
    deltas = np.abs(np.linspace(min_decay, max_decay, HY_WIDTH))
    dec = jnp.asarray(np.exp(-t * deltas[None, :]), F32)
    pad = P - HY_FFN
    w1p = jnp.pad(w1, ((0, P - HY_EMB), (0, pad)))
    b1p = jnp.pad(b1, (0, pad)).reshape(1, P)
    w2p = jnp.pad(w2, ((0, pad), (0, pad)))
    b2p = jnp.pad(b2, (0, pad)).reshape(1, P)
    w3p = jnp.pad(w3, ((0, pad), (0, 0)))
    frp = jnp.pad(freq, ((0, 0), (0, pad)))
    ncg = HY_WIDTH // P
    const = lambda shape: pl.BlockSpec(shape, lambda o, c: tuple(0 for _ in shape))
    kshape = jax.ShapeDtypeStruct((HY_ORDER, nlag, T, HY_WIDTH), F32)
    kspec = pl.BlockSpec((1, nlag, T, P), lambda o, c: (o, 0, 0, c))
    return pl.pallas_call(
        functools.partial(_filt_kernel, L=L),
        grid=(HY_ORDER, ncg),
        in_specs=[const((L, P)), const((P, P)), const((1, P)), const((P, P)), const((1, P)), const((2, P)),
                  pl.BlockSpec((P, P), lambda o, c: (0, o * 2 * ncg + c)),
                  pl.BlockSpec((P, P), lambda o, c: (0, o * 2 * ncg + ncg + c)),
                  pl.BlockSpec((L, P), lambda o, c: (0, c)),
                  pl.BlockSpec((1, 1, P), lambda o, c: (o, 0, c)),
                  const((2 * T, T))],
        out_specs=[kspec, kspec],
        out_shape=[kshape, kshape],
        scratch_shapes=[pltpu.VMEM((2, L, P), BF16), pltpu.VMEM((2, L, P), F32),
                        pltpu.VMEM((2, nb, 2 * T, P), F32)],
        compiler_params=_params(("arbitrary", "arbitrary")),
        name="hyena_filters",
    )(zemb, w1p, b1p, w2p, b2p, frp, w3p, w3p, dec, bias.reshape(HY_ORDER, 1, HY_WIDTH), fwd)


CONV_ROWS = 128


def _conv_block(x_ref, t0, w, b, left):
    L, C = x_ref.shape
    T = CONV_ROWS
    blk = x_ref[t0:t0 + T, :]
    prev = x_ref[t0 - SUBLANES:t0, :] if t0 > 0 else jnp.zeros((SUBLANES, C), F32)
    nxt = x_ref[t0 + T:t0 + T + SUBLANES, :] if t0 + T < L else jnp.zeros((SUBLANES, C), F32)
    ext = jnp.concatenate([prev, blk, nxt], axis=0)
    n = T + 2 * SUBLANES
    acc = jnp.broadcast_to(b, (T, C))
    for tap in range(w.shape[0]):
        s = tap - left
        sh = blk if s == 0 else pltpu.roll(ext, (-s) % n, axis=0)[SUBLANES:SUBLANES + T]
        acc = acc + w[tap:tap + 1, :] * sh
    return acc


HY_LANES_ROWS = 2 * 2048
HY_MAX_BATCH = 8


def _toeplitz(k, u, mul):
    n = len(u)
    if n == 1:
        return [mul(k[0], u[0])]
    h = n // 2
    add = lambda a, b: tuple(x + y for x, y in zip(a, b))
    sub = lambda a, b: tuple(x - y for x, y in zip(a, b))
    t0, t1, t2 = k[h:h + n - 1], k[0:n - 1], k[n:2 * n - 1]
    p = _toeplitz(t0, [add(a, b) for a, b in zip(u[:h], u[h:])], mul)
    q = _toeplitz([sub(a, b) for a, b in zip(t1, t0)], u[h:], mul)
    r = _toeplitz([sub(a, b) for a, b in zip(t2, t0)], u[:h], mul)
    return [add(a, b) for a, b in zip(p, q)] + [add(a, b) for a, b in zip(p, r)]


def _hy_kernel(v_ref, x1_ref, x2_ref, g_ref, wv_ref, w1_ref, w2_ref, bv_ref, b1_ref, b2_ref,
               kre_ref, kim_ref, fwd_ref, inv_ref, y_ref,
               x1c_ref, x2c_ref, u_ref, U_ref, Y_ref, *, L):
    T = fwd_ref.shape[1]
    P = LANES
    nb = L // T
    NBT = v_ref.shape[0]
    left = (HY_SHORT - 1) // 2
    for t0 in range(0, L, CONV_ROWS):
        rows = slice(t0, t0 + CONV_ROWS)
        for p in range(NBT):
            u_ref[rows, p * P:(p + 1) * P] = _conv_block(v_ref.at[p], t0, wv_ref[...], bv_ref[...], left).astype(BF16)
            x1c_ref[p, rows, :] = _conv_block(x1_ref.at[p], t0, w1_ref[...], b1_ref[...], left)
            x2c_ref[p, rows, :] = _conv_block(x2_ref.at[p], t0, w2_ref[...], b2_ref[...], left)

    def block_spectra():
        for j in range(nb):
            U_ref[j] = jnp.dot(fwd_ref[...], u_ref[j * T:(j + 1) * T, :], preferred_element_type=F32)

    block_spectra()

    row0 = lax.broadcasted_iota(jnp.int32, (SUBLANES, NBT * P), 0) == 0
    tile = lambda x: jnp.concatenate([x] * NBT, axis=1)

    def cmul(k, u):
        kr, ki = tile(k[0]), tile(k[1])
        return (kr * u[0] - ki * u[1], kr * u[1] + ki * u[0])

    def rmul(k, u):
        return (tile(k[0]) * u[0], tile(k[1]) * u[1])

    def spectra_products(o):
        def load(r, rows):
            k = [(kre_ref[o, l, pl.ds(r, rows), :], kim_ref[o, l, pl.ds(r, rows), :]) for l in range(2 * nb - 1)]
            u = [(U_ref[j, pl.ds(r, rows), :], U_ref[j, pl.ds(T + r, rows), :]) for j in range(nb)]
            return k, u

        def chunk(rc, carry):
            r = pl.multiple_of(rc * SUBLANES, SUBLANES)
            y = _toeplitz(*load(r, SUBLANES), cmul)
            for i in range(nb):
                Y_ref[i, pl.ds(r, SUBLANES), :] = y[i][0]
                Y_ref[i, pl.ds(T + r, SUBLANES), :] = y[i][1]
            return carry

        lax.fori_loop(0, T // SUBLANES, chunk, 0)
        y = _toeplitz(*load(0, SUBLANES), rmul)
        for i in range(nb):
            Y_ref[i, 0:SUBLANES, :] = jnp.where(row0, y[i][0], Y_ref[i, 0:SUBLANES, :])
            Y_ref[i, T:T + SUBLANES, :] = jnp.where(row0, y[i][1], Y_ref[i, T:T + SUBLANES, :])

    def long_conv_block(i):
        return _dot1(inv_ref[...], Y_ref[i])

    spectra_products(0)

    def order0(i, carry):
        r0 = pl.multiple_of(i * T, T)
        y = long_conv_block(i)
        for p in range(NBT):
            z1 = x1c_ref[p, pl.ds(r0, T), :] * y[:, p * P:(p + 1) * P]
            u_ref[pl.ds(r0, T), p * P:(p + 1) * P] = z1.astype(BF16)
        return carry

    lax.fori_loop(0, nb, order0, 0, unroll=2 if nb % 2 == 0 else 1)
    block_spectra()
    spectra_products(1)

    def order1(i, carry):
        r0 = pl.multiple_of(i * T, T)
        y = long_conv_block(i)
        for p in range(NBT):
            out = x2c_ref[p, pl.ds(r0, T), :] * y[:, p * P:(p + 1) * P]
            y_ref[p, pl.ds(r0, T), :] = (out * _silu(g_ref[p, pl.ds(r0, T), :])).astype(BF16)
        return carry

    lax.fori_loop(0, nb, order1, 0, unroll=2 if nb % 2 == 0 else 1)


def _hyena(z, conv_w, conv_b, kre, kim, fwd, inv):
    B, L, _ = z.shape
    NBT = min(B, max(2, HY_LANES_ROWS // L), HY_MAX_BATCH)
    assert B % NBT == 0
    T = fwd.shape[1]
    nb = L // T
    nlag = 2 * nb - 1
    P = LANES
    ncg = HY_WIDTH // P
    col = lambda off: pl.BlockSpec((NBT, L, P), lambda c, b, off=off: (b, 0, off + c))
    wspec = lambda part: pl.BlockSpec((HY_SHORT, P), lambda c, b, part=part: (0, part * ncg + c))
    bspec = lambda part: pl.BlockSpec((1, P), lambda c, b, part=part: (0, part * ncg + c))
    kspec = pl.BlockSpec((HY_ORDER, nlag, T, P), lambda c, b: (0, 0, 0, c))
    fspec = pl.BlockSpec((2 * T, T), lambda c, b: (0, 0))
    ispec = pl.BlockSpec((T, 2 * T), lambda c, b: (0, 0))
    cb = conv_b.reshape(1, 3 * HY_WIDTH)
    return pl.pallas_call(
        functools.partial(_hy_kernel, L=L),
        grid=(ncg, B // NBT),
        in_specs=[col(_HV0), col(_HX10), col(_HX20), col(_GH0),
                  wspec(0), wspec(1), wspec(2), bspec(0), bspec(1), bspec(2),
                  kspec, kspec, fspec, ispec],
        out_specs=pl.BlockSpec((NBT, L, P), lambda c, b: (b, 0, c)),
        out_shape=jax.ShapeDtypeStruct((B, L, HY_WIDTH), BF16),
        scratch_shapes=[pltpu.VMEM((NBT, L, P), F32), pltpu.VMEM((NBT, L, P), F32),
                        pltpu.VMEM((L, NBT * P), BF16),
                        pltpu.VMEM((nb, 2 * T, NBT * P), F32), pltpu.VMEM((nb, 2 * T, NBT * P), F32)],
        compiler_params=_params(("arbitrary", "arbitrary")),
        name="hyena",
    )(z, z, z, z, conv_w, conv_w, conv_w, cb, cb, cb, kre, kim, fwd, inv)


LRU_TILE = 256
LRU_GROUPS = LRU_TILE // LANES
LRU_ROWS = 128


def _lru_kernel(*refs, L, has_state):
    it = iter(refs)
    x_ref, g_ref, cw_ref, cb_ref, wg_ref, gb_ref, lam_ref = (next(it) for _ in range(7))
    h0_ref = next(it) if has_state else None
    y_ref, sN_ref, xp_ref, A_ref, B_ref, H_ref, C_ref = (next(it) for _ in range(7))

    S = L // SUBLANES
    W = LRU_TILE
    G = LRU_GROUPS
    R = LRU_ROWS
    left = LRU_CONV // 2
    pad = left * SUBLANES

    sub = lax.broadcasted_iota(jnp.int32, (SUBLANES, W), 0)
    for k in range(left):
        tail = pltpu.roll(x_ref[0, L - pad + k * SUBLANES:L - pad + (k + 1) * SUBLANES, :], 1, axis=0)
        xp_ref[k * SUBLANES:(k + 1) * SUBLANES, :] = jnp.where(sub == 0, 0.0, tail)
    for k in range(LRU_CONV - 1 - left):
        head = pltpu.roll(x_ref[0, k * SUBLANES:(k + 1) * SUBLANES, :], SUBLANES - 1, axis=0)
        xp_ref[pad + L + k * SUBLANES:pad + L + (k + 1) * SUBLANES, :] = jnp.where(sub == SUBLANES - 1, 0.0, head)

    def copy(c, carry):
        r0 = pl.multiple_of(c * R, R)
        xp_ref[pl.ds(pad + r0, R), :] = x_ref[0, pl.ds(r0, R), :]
        return carry

    lax.fori_loop(0, L // R, copy, 0)

    csp = [(0.5 * LRU_C) * _softplus(-lam_ref[d]) for d in (0, 1)]

    def coeffs(c, carry):
        r0 = pl.multiple_of(c * R, R)
        u = jnp.broadcast_to(cb_ref[...], (R, W))
        for k in range(LRU_CONV):
            u = u + cw_ref[k:k + 1, :] * xp_ref[pl.ds(r0 + k * SUBLANES, R), :]
        half = jnp.dot(u.astype(BF16), wg_ref[0], preferred_element_type=F32) + gb_ref[0]
        hu = 0.5 * u
        for d in (0, 1):
            t_r = jnp.tanh(half[:, (2 * d) * W:(2 * d + 1) * W])
            t_i = jnp.tanh(half[:, (2 * d + 1) * W:(2 * d + 2) * W])
            nl = csp[d] * t_r + csp[d]
            a = jnp.exp(-nl)
            s2 = jnp.tanh(nl) * (a * a + 1.0)
            b = jnp.where(s2 > 0.0, s2 * lax.rsqrt(s2), 0.0) * (t_i * hu + hu)
            for gi in range(G):
                lanes = slice(gi * LANES, (gi + 1) * LANES)
                A_ref[d, gi, pl.ds(r0, R), :] = a[:, lanes]
                B_ref[d, gi, pl.ds(r0, R), :] = b[:, lanes]
        return carry

    lax.fori_loop(0, L // R, coeffs, 0)

    def scan_body(t, carry):
        out = []
        for d in (0, 1):
            r8 = pl.multiple_of((t if d == 0 else S - 1 - t) * SUBLANES, SUBLANES)
            for gi in range(G):
                h, acc = carry[d * G + gi]
                a = A_ref[d, gi, pl.ds(r8, SUBLANES), :]
                h = a * h + B_ref[d, gi, pl.ds(r8, SUBLANES), :]
                acc = acc * a
                H_ref[d, gi, pl.ds(r8, SUBLANES), :] = h
                C_ref[d, gi, pl.ds(r8, SUBLANES), :] = acc
                out.append((h, acc))
        return tuple(out)

    init = tuple((jnp.zeros((SUBLANES, LANES), F32), jnp.ones((SUBLANES, LANES), F32))
                 for _ in range(2 * G))
    lax.fori_loop(0, S, scan_body, init, unroll=4)

    hin = [None] * (2 * G)
    for d in (0, 1):
        last = (S - 1 if d == 0 else 0) * SUBLANES
        for gi in range(G):
            hl = H_ref[d, gi, last:last + SUBLANES, :]
            ac = C_ref[d, gi, last:last + SUBLANES, :]
            if has_state:
                h = h0_ref[0, d:d + 1, gi * LANES:(gi + 1) * LANES]
            else:
                h = jnp.zeros((1, LANES), F32)
            rows = [None] * SUBLANES
            for j in (range(SUBLANES) if d == 0 else range(SUBLANES - 1, -1, -1)):
                rows[j] = h
                h = hl[j:j + 1, :] + ac[j:j + 1, :] * h
            hin[d * G + gi] = jnp.concatenate([jnp.concatenate(rows, axis=0)] * (R // SUBLANES), axis=0)
            sN_ref[0, d:d + 1, gi * LANES:(gi + 1) * LANES] = h

    def combine(c, carry):
        r0 = pl.multiple_of(c * R, R)
        cols = []
        for gi in range(G):
            tot = jnp.zeros((R, LANES), F32)
            for d in (0, 1):
                tot = tot + (H_ref[d, gi, pl.ds(r0, R), :] + C_ref[d, gi, pl.ds(r0, R), :] * hin[d * G + gi])
            cols.append(tot)
        y_ref[0, pl.ds(r0, R), :] = jnp.concatenate(cols, axis=1) * _silu(g_ref[0, pl.ds(r0, R), :])
        return carry

    lax.fori_loop(0, L // R, combine, 0)


def _rglru(z4, conv_w, conv_b, gate_w, gate_b, lam, h0):
    B, S, _, _ = z4.shape
    L = S * SUBLANES
    has_state = h0 is not None
    W = LRU_TILE
    nh = LRU_WIDTH // W
    bpt = W // LRU_BLOCK_DIM
    gw = gate_w.reshape(2, 2, nh, bpt, LRU_BLOCK_DIM, LRU_BLOCK_DIM)
    eye = jnp.eye(bpt, dtype=F32)
    dense = (0.5 * jnp.einsum('dkhnij,nm->hnidkmj', gw, eye)).reshape(nh, W, 4 * W).astype(BF16)
    gb = 0.5 * gate_b.reshape(2, 2, nh, W).transpose(2, 0, 1, 3).reshape(nh, 1, 4 * W)
    zr = z4.reshape(B, L, D_LRU)
    off = lambda base: pl.BlockSpec((1, L, W), lambda b, h, base=base: (b, 0, base + h))
    in_specs = [off(0), off(LRU_WIDTH // W),
                pl.BlockSpec((LRU_CONV, W), lambda b, h: (0, h)),
                pl.BlockSpec((1, W), lambda b, h: (0, h)),
                pl.BlockSpec((1, W, 4 * W), lambda b, h: (h, 0, 0)),
                pl.BlockSpec((1, 1, 4 * W), lambda b, h: (h, 0, 0)),
                pl.BlockSpec((2, 1, W), lambda b, h: (0, 0, h))]
    args = [zr, zr, conv_w, conv_b.reshape(1, LRU_WIDTH), dense, gb, lam.reshape(2, 1, LRU_WIDTH)]
    if has_state:
        in_specs.append(pl.BlockSpec((1, 2, W), lambda b, h: (b, 0, h)))
        args.append(h0)
    y, s = pl.pallas_call(
        functools.partial(_lru_kernel, L=L, has_state=has_state),
        grid=(B, nh),
        in_specs=in_specs,
        out_specs=[pl.BlockSpec((1, L, W), lambda b, h: (b, 0, h)),
                   pl.BlockSpec((1, 2, W), lambda b, h: (b, 0, h))],
        out_shape=[jax.ShapeDtypeStruct((B, L, LRU_WIDTH), F32),
                   jax.ShapeDtypeStruct((B, 2, LRU_WIDTH), F32)],
        scratch_shapes=[pltpu.VMEM((L + (LRU_CONV - 1) * SUBLANES, W), F32)]
        + [pltpu.VMEM((2, LRU_GROUPS, L, LANES), F32) for _ in range(4)],
        compiler_params=_params(("arbitrary", "arbitrary")),
        name="rglru",
    )(*args)
    return y.reshape(B, S, SUBLANES, LRU_WIDTH), s


def _rope_tables(L):
    rows = L // GRID_W
    row = np.repeat(np.arange(rows, dtype=np.float64), GRID_W)
    col = np.tile(np.arange(GRID_W, dtype=np.float64), rows)
    n_f = RET_HEAD_DIM // 4
    inv = ROPE_BASE ** (-np.arange(n_f, dtype=np.float64) / n_f)
    ang = np.concatenate([row[:, None] * inv[None], col[:, None] * inv[None]], axis=-1)
    cos, sin = np.cos(ang), np.sin(ang)
    return (jnp.asarray(np.concatenate([cos, cos], axis=-1), F32),
            jnp.asarray(np.concatenate([-sin, sin], axis=-1), F32))


def kernel(x_prompt, x_sample, state_ret, state_lru, c, c_ctx, norm_g, ada_w, ada_b, w_in, ret_decay_logit, hy_conv_w, hy_conv_b, hy_ffn_w1, hy_ffn_b1, hy_ffn_w2, hy_ffn_b2, hy_ffn_w3, hy_freq, hy_bias, lru_conv_w, lru_conv_b, lru_gate_w, lru_gate_b, lru_lambda, w_out, final_g):
    Bp, Lp, _ = x_prompt.shape
    Bs, Ls, _ = x_sample.shape
    assert Bs + 1 <= SUBLANES

    cvec = jnp.zeros((SUBLANES, D_MODEL), F32).at[:Bs].set(c).at[Bs].set(c_ctx)
    mod = _modulation(cvec, ada_w, ada_b)
    rope = _rope_tables(Ls)
    dft = {L: _dft_matrices(min(DFT_MAX, L)) for L in {Lp, Ls}}
    w_in_b = w_in.astype(BF16)
    w_out_b = w_out.astype(BF16)
    fg = final_g.reshape(1, D_MODEL)

    xc, xl = x_prompt, x_sample
    new_state_ret, new_lru = None, []
    for l in range(DEPTH):
        shift, scale, gate = (mod[l, :, i * D_MODEL:(i + 1) * D_MODEL] for i in range(3))
        g = norm_g[l].reshape(1, D_MODEL)
        final = l == DEPTH - 1
        filt = {}
        for L in sorted({Lp, Ls}):
            filt[L] = _hyena_filter_spectra(L, hy_ffn_w1[l], hy_ffn_b1[l], hy_ffn_w2[l], hy_ffn_b2[l],
                                            hy_ffn_w3[l], hy_freq[l], hy_bias[l], dft[L][0])

        def layer(x, sel, rope_t, ret_s0, lru_s0, ret_states):
            sh, sc, gt = (m[sel][:, None, :] for m in (shift, scale, gate))
            z, z_lru = _in_proj(x, sh, sc, g, w_in_b, l)
            y_ret, *s_ret = _retention(z, ret_decay_logit[l], rope_t, ret_s0, l, ret_states)
            kre, kim = filt[x.shape[1]]
            y_hy = _hyena(z, hy_conv_w[l], hy_conv_b[l], kre, kim, *dft[x.shape[1]])
            y_lru, s_lru = _rglru(z_lru, lru_conv_w[l], lru_conv_b[l], lru_gate_w[l], lru_gate_b[l],
                                  lru_lambda[l], lru_s0)
            return _out_proj(y_ret, y_hy, y_lru, x, gt, w_out_b, l, fg, final), s_ret, s_lru

        xc, (new_state_ret,), ls = layer(xc, slice(Bs, Bs + 1), None, None, None,
                                         "new" if l == 0 else new_state_ret)
        new_lru.append(ls)
        xl, _, _ = layer(xl, slice(0, Bs), rope, state_ret, state_lru[:, l], None)

    new_state_lru = jnp.stack(new_lru, axis=1).astype(x_prompt.dtype)
    return (xc, xl, new_state_ret.astype(x_prompt.dtype), new_state_lru)
```

```python
import functools
import math

import numpy as np
import jax
import jax.numpy as jnp
from jax import lax
from jax.experimental import pallas as pl
from jax.experimental.pallas import tpu as pltpu

F32 = jnp.float32
BF16 = jnp.bfloat16
HI = lax.Precision.HIGHEST

D_MODEL = 1024
DEPTH = 2
GRID_W = 64
EPS = 1e-6
RET_HEADS = 4
RET_HEAD_DIM = 128
RET_WIDTH = RET_HEADS * RET_HEAD_DIM
ROPE_BASE = 10000.0
HY_WIDTH = 512
HY_ORDER = 2
HY_SHORT = 3
HY_EMB = 33
HY_FFN = 64
HY_SHORT_DECAY_PCT = 0.3
HY_LONG_DECAY_PCT = 1.5
HY_TARGET = 1e-2
LRU_WIDTH = 512
LRU_BLOCKS = 8
LRU_BLOCK_DIM = LRU_WIDTH // LRU_BLOCKS
LRU_CONV = 4
LRU_C = 8.0
D_MIX = RET_WIDTH + HY_WIDTH + LRU_WIDTH
D_IN = 4 * RET_WIDTH + 4 * HY_WIDTH + 2 * LRU_WIDTH

LANES = 128
SUBLANES = 8
DFT_MAX = 512
ROW_TILE = 256
VMEM_LIMIT = 56 * 1024 * 1024

_Q0, _K0, _V0, _GR0 = 0, 4, 8, 12
_HV0, _HX10, _HX20, _GH0 = 16, 20, 24, 28


def _sigmoid(x):
    return 0.5 * jnp.tanh(0.5 * x) + 0.5


def _silu(x):
    return x * _sigmoid(x)


def _softplus(x):
    return jnp.maximum(x, 0.0) + jnp.log1p(jnp.exp(-jnp.abs(x)))


def _params(sem):
    return pltpu.CompilerParams(dimension_semantics=sem, vmem_limit_bytes=VMEM_LIMIT)


def _mod_kernel(c_ref, w_ref, b_ref, o_ref):
    s_hi, s_lo = _split_bf16(_silu(c_ref[...]))
    w_hi, w_lo = _split_bf16(w_ref[0])
    o_ref[0] = (jnp.dot(s_hi, w_hi, preferred_element_type=F32)
                + jnp.dot(s_lo, w_hi, preferred_element_type=F32)
                + jnp.dot(s_hi, w_lo, preferred_element_type=F32)) + b_ref[0]


def _modulation(cvec, ada_w, ada_b):
    tn = 768
    return pl.pallas_call(
        _mod_kernel,
        grid=(DEPTH, 3 * D_MODEL // tn),
        in_specs=[pl.BlockSpec((SUBLANES, D_MODEL), lambda l, j: (0, 0)),
                  pl.BlockSpec((1, D_MODEL, tn), lambda l, j: (l, 0, j)),
                  pl.BlockSpec((1, 1, tn), lambda l, j: (l, 0, j))],
        out_specs=pl.BlockSpec((1, SUBLANES, tn), lambda l, j: (l, 0, j)),
        out_shape=jax.ShapeDtypeStruct((DEPTH, SUBLANES, 3 * D_MODEL), F32),
        compiler_params=_params(("arbitrary", "arbitrary")),
        name="modulation",
    )(cvec, ada_w, ada_b.reshape(DEPTH, 1, 3 * D_MODEL))


D_MAIN = 4 * RET_WIDTH + 4 * HY_WIDTH
D_LRU = 2 * LRU_WIDTH


def _in_kernel(x_ref, sh_ref, sc_ref, g_ref, w_ref, o_ref, ol_ref):
    _, nseg, rows, _ = x_ref.shape
    x = x_ref[0].reshape(nseg * rows, D_MODEL)
    ms = jnp.mean(x * x, axis=-1, keepdims=True)
    y = x * lax.rsqrt(ms + EPS) * g_ref[...]
    h = (y * (1.0 + sc_ref[0]) + sh_ref[0]).astype(BF16)
    tn = 1024
    for n in range(D_MAIN // tn):
        z = jnp.dot(h, w_ref[:, n * tn:(n + 1) * tn], preferred_element_type=F32)
        o_ref[0, :, :, n * tn:(n + 1) * tn] = z.reshape(nseg, rows, tn)
    zl = jnp.dot(h, w_ref[:, D_MAIN:D_IN], preferred_element_type=F32)
    for j in range(nseg):
        ol_ref[0, :, j, :] = zl[j * rows:(j + 1) * rows]


def _in_proj(x, shift, scale, g, w_bf16, layer):
    B, L, _ = x.shape
    S = L // SUBLANES
    rows = min(ROW_TILE // SUBLANES, S)
    per_batch = shift.shape[0] > 1
    mod_map = (lambda b, i: (b, 0, 0)) if per_batch else (lambda b, i: (0, 0, 0))
    seg = lambda C: pl.BlockSpec((1, SUBLANES, rows, C), lambda b, i: (b, 0, i, 0))
    z, z_lru = pl.pallas_call(
        _in_kernel,
        grid=(B, S // rows),
        in_specs=[seg(D_MODEL),
                  pl.BlockSpec((1, 1, D_MODEL), mod_map),
                  pl.BlockSpec((1, 1, D_MODEL), mod_map),
                  pl.BlockSpec((1, D_MODEL), lambda b, i: (0, 0)),
                  pl.BlockSpec((None, D_MODEL, D_IN), lambda b, i: (layer, 0, 0),
                               pipeline_mode=pl.Buffered(1))],
        out_specs=[seg(D_MAIN), pl.BlockSpec((1, rows, SUBLANES, D_LRU), lambda b, i: (b, i, 0, 0))],
        out_shape=[jax.ShapeDtypeStruct((B, SUBLANES, S, D_MAIN), F32),
                   jax.ShapeDtypeStruct((B, S, SUBLANES, D_LRU), F32)],
        compiler_params=_params(("arbitrary", "arbitrary")),
        name="in_proj",
    )(x.reshape(B, SUBLANES, S, D_MODEL), shift, scale, g, w_bf16)
    return z.reshape(B, L, D_MAIN), z_lru


OUT_ROWS = 512


def _out_kernel(yr_ref, yh_ref, yl_ref, x_ref, gate_ref, w_ref, fg_ref, o_ref, *, final):
    _, nseg, rows, _ = x_ref.shape
    n = nseg * rows
    acc = jnp.dot(yr_ref[0].reshape(n, RET_WIDTH), w_ref[0:RET_WIDTH], preferred_element_type=F32)
    acc = acc + jnp.dot(yh_ref[0].reshape(n, HY_WIDTH), w_ref[RET_WIDTH:RET_WIDTH + HY_WIDTH],
                        preferred_element_type=F32)
    yl = jnp.concatenate([yl_ref[0, :, j, :] for j in range(nseg)], axis=0)
    acc = acc + jnp.dot(yl.astype(BF16), w_ref[RET_WIDTH + HY_WIDTH:D_MIX], preferred_element_type=F32)
    x = x_ref[0].reshape(n, D_MODEL) + gate_ref[0] * acc
    if final:
        ms = jnp.mean(x * x, axis=-1, keepdims=True)
        x = x * lax.rsqrt(ms + EPS) * fg_ref[...]
    o_ref[0] = x.reshape(nseg, rows, D_MODEL)


def _out_proj(y_ret, y_hy, y_lru, x, gate, w_bf16, layer, final_g, final):
    B, L, _ = x.shape
    S = L // SUBLANES
    rows = min(OUT_ROWS // SUBLANES, S)
    per_batch = gate.shape[0] > 1
    mod_map = (lambda b, i: (b, 0, 0)) if per_batch else (lambda b, i: (0, 0, 0))
    seg = lambda C: pl.BlockSpec((1, SUBLANES, rows, C), lambda b, i: (b, 0, i, 0))
    view = lambda a: a.reshape(B, SUBLANES, S, a.shape[-1])
    out = pl.pallas_call(
        functools.partial(_out_kernel, final=final),
        grid=(B, S // rows),
        in_specs=[seg(RET_WIDTH), seg(HY_WIDTH),
                  pl.BlockSpec((1, rows, SUBLANES, LRU_WIDTH), lambda b, i: (b, i, 0, 0)),
                  seg(D_MODEL),
                  pl.BlockSpec((1, 1, D_MODEL), mod_map),
                  pl.BlockSpec((None, D_MIX, D_MODEL), lambda b, i: (layer, 0, 0),
                               pipeline_mode=pl.Buffered(1)),
                  pl.BlockSpec((1, D_MODEL), lambda b, i: (0, 0))],
        out_specs=seg(D_MODEL),
        out_shape=jax.ShapeDtypeStruct((B, SUBLANES, S, D_MODEL), F32),
        compiler_params=_params(("arbitrary", "arbitrary")),
        name="out_proj",
    )(view(y_ret), view(y_hy), y_lru, view(x), gate, w_bf16, final_g)
    return out.reshape(B, L, D_MODEL)


RET_HEAD_ROWS = 2 * 2048
RET_BLOCK = 256


def _ret_kernel(*refs, L, use_rope, has_state, emit_state, chained, layer):
    it = iter(refs)
    q_ref, k_ref, v_ref, g_ref = next(it), next(it), next(it), next(it)
    cos_ref = sin_ref = s0_ref = sN_ref = None
    if use_rope:
        cos_ref, sin_ref = next(it), next(it)
    lg_ref = next(it)
    if has_state:
        s0_ref = next(it)
    if chained:
        next(it)
    y_ref = next(it)
    if emit_state:
        sN_ref = next(it)
    R_ref, M_ref, Z_ref, D_ref, KV_ref, kb_ref = (next(it) for _ in range(6))

    C = min(RET_BLOCK, L)
    DH = RET_HEAD_DIM
    HP = q_ref.shape[-1] // DH
    n = L // C
    cross = has_state or n > 1
    kscale = DH ** -0.5

    @pl.when(pl.program_id(1) == 0)
    def _():
        ii = lax.broadcasted_iota(jnp.int32, (C, C), 0)
        jj = lax.broadcasted_iota(jnp.int32, (C, C), 1)
        diff = (ii - jj).astype(F32)
        ri = lax.broadcasted_iota(jnp.int32, (C, DH), 0).astype(F32)
        for hh in range(HP):
            lgf = -_softplus(-lg_ref[0, hh])
            lgb = -_softplus(-lg_ref[1, hh])
            M_ref[hh] = (jnp.where(diff >= 0, jnp.exp(lgf[:, :C] * jnp.maximum(diff, 0.0)), 0.0)
                         + jnp.where(diff <= 0, jnp.exp(lgb[:, :C] * jnp.maximum(-diff, 0.0)), 0.0))
            lf, lb = lgf[:, :DH], lgb[:, :DH]
            Z_ref[hh, 0] = jnp.concatenate([jnp.exp(lf * (C - 1.0 - ri)), jnp.exp(lb * ri)], axis=1)
            Z_ref[hh, 1] = jnp.concatenate([jnp.exp(lf * (ri + 1.0)), jnp.exp(lb * (C - ri))], axis=1)
            D_ref[hh] = jnp.exp(jnp.concatenate([jnp.broadcast_to(lf * C, (DH, DH)),
                                                 jnp.broadcast_to(lb * C, (DH, DH))], axis=0))

    def rotary(x, r0):
        if not use_rope:
            return x
        return x * cos_ref[pl.ds(r0, C), :] + pltpu.roll(x, 64, axis=1) * sin_ref[pl.ds(r0, C), :]

    def chunk_kv(c, carry):
        r0 = pl.multiple_of(c * C, C)
        for hh in range(HP):
            lanes = slice(hh * DH, (hh + 1) * DH)
            k = rotary(k_ref[0, pl.ds(r0, C), lanes] * kscale, r0)
            kb_ref[hh, pl.ds(r0, C), :] = k.astype(BF16)
            kz = (jnp.concatenate([k, k], axis=1) * Z_ref[hh, 0]).astype(BF16)
            vb = v_ref[0, pl.ds(r0, C), lanes].astype(BF16)
            KV_ref[hh, c] = lax.dot_general(kz, vb, (((0,), (0,)), ((), ())), preferred_element_type=F32)
        return carry

    lax.fori_loop(0, n, chunk_kv, 0, unroll=2 if n % 2 == 0 else 1)

    for d in (0, 1):
        for hh in range(HP):
            R_ref[d, hh] = s0_ref[0, d, hh] if has_state else jnp.zeros((DH, DH), F32)

    def states(t, carry):
        for d in (0, 1):
            c = t if d == 0 else n - 1 - t
            rows = slice(d * DH, (d + 1) * DH)
            for hh in range(HP):
                R = R_ref[d, hh]
                R_ref[d, hh] = D_ref[hh, rows, :] * R + KV_ref[hh, c, rows, :]
                KV_ref[hh, c, rows, :] = R
        return carry

    lax.fori_loop(0, n, states, 0)
    if emit_state:
        for l in range(sN_ref.shape[1]):
            if chained or l == layer:
                for d in (0, 1):
                    for hh in range(HP):
                        sN_ref[0, l, d, hh] = R_ref[d, hh]
            else:
                sN_ref[0, l] = jnp.zeros(sN_ref.shape[2:], F32)

    def chunk_out(c, carry):
        r0 = pl.multiple_of(c * C, C)
        cols = []
        for hh in range(HP):
            lanes = slice(hh * DH, (hh + 1) * DH)
            q = rotary(q_ref[0, pl.ds(r0, C), lanes], r0)
            vb = v_ref[0, pl.ds(r0, C), lanes].astype(BF16)
            s = lax.dot_general(q.astype(BF16), kb_ref[hh, pl.ds(r0, C), :], (((1,), (1,)), ((), ())),
                                preferred_element_type=F32) * M_ref[hh]
            o = jnp.dot(s.astype(BF16), vb, preferred_element_type=F32)
            if cross:
                qx = (jnp.concatenate([q, q], axis=1) * Z_ref[hh, 1]).astype(BF16)
                o = o + jnp.dot(qx, KV_ref[hh, c].astype(BF16), preferred_element_type=F32)
            cols.append(o * lax.rsqrt(jnp.mean(o * o, axis=-1, keepdims=True) + EPS))
        g = g_ref[0, pl.ds(r0, C), :]
        y_ref[0, pl.ds(r0, C), :] = (jnp.concatenate(cols, axis=1) * _silu(g)).astype(BF16)
        return carry

    lax.fori_loop(0, n, chunk_out, 0, unroll=2 if n % 2 == 0 else 1)


def _retention(z, decay_logit, rope, state0, layer, states):
    B, L, _ = z.shape
    use_rope = rope is not None
    has_state = state0 is not None
    emit_state = states is not None
    chained = emit_state and not isinstance(states, str)
    H = RET_HEADS
    HP = min(H, max(1, RET_HEAD_ROWS // L))
    W = HP * RET_HEAD_DIM
    col = lambda off: pl.BlockSpec((1, L, W), lambda h, b, off=off: (b, 0, off * LANES // W + h))
    in_specs = [col(_Q0), col(_K0), col(_V0), col(_GR0)]
    args = [z, z, z, z]
    if use_rope:
        in_specs += [pl.BlockSpec((L, LANES), lambda h, b: (0, 0))] * 2
        args += list(rope)
    C = min(RET_BLOCK, L)
    in_specs.append(pl.BlockSpec((2, HP, 1, RET_BLOCK), lambda h, b: (0, h, 0, 0)))
    args.append(jnp.broadcast_to(decay_logit[:, :, None, None], (2, H, 1, RET_BLOCK)))
    if has_state:
        in_specs.append(pl.BlockSpec((1, None, 2, HP, RET_HEAD_DIM, RET_HEAD_DIM),
                                     lambda h, b: (b, layer, 0, h, 0, 0)))
        args.append(state0)
    out_specs = [pl.BlockSpec((1, L, W), lambda h, b: (b, 0, h))]
    out_shape = [jax.ShapeDtypeStruct((B, L, RET_WIDTH), BF16)]
    aliases = {}
    if emit_state:
        nl, l0 = (1, layer) if chained else (DEPTH, 0)
        out_specs.append(pl.BlockSpec((1, nl, 2, HP, RET_HEAD_DIM, RET_HEAD_DIM),
                                      lambda h, b: (b, l0, 0, h, 0, 0)))
        out_shape.append(jax.ShapeDtypeStruct((B, DEPTH, 2, H, RET_HEAD_DIM, RET_HEAD_DIM), F32))
        if chained:
            aliases = {len(args): 1}
            in_specs.append(pl.BlockSpec(memory_space=pl.ANY))
            args.append(states)
    return pl.pallas_call(
        functools.partial(_ret_kernel, L=L, use_rope=use_rope, has_state=has_state,
                          emit_state=emit_state, chained=chained, layer=layer),
        grid=(H // HP, B),
        in_specs=in_specs,
        out_specs=out_specs,
        out_shape=out_shape,
        input_output_aliases=aliases,
        scratch_shapes=[pltpu.VMEM((2, HP, RET_HEAD_DIM, RET_HEAD_DIM), F32),
                        pltpu.VMEM((HP, C, C), F32),
                        pltpu.VMEM((HP, 2, C, 2 * RET_HEAD_DIM), F32),
                        pltpu.VMEM((HP, 2 * RET_HEAD_DIM, RET_HEAD_DIM), F32),
                        pltpu.VMEM((HP, L // C, 2 * RET_HEAD_DIM, RET_HEAD_DIM), F32),
                        pltpu.VMEM((HP, L, RET_HEAD_DIM), BF16)],
        compiler_params=_params(("arbitrary", "arbitrary")),
        name="retention",
    )(*args)


def _dft_matrices(T):
    N = 2 * T
    k = np.arange(T, dtype=np.float64)[:, None]
    m = np.arange(T, dtype=np.float64)[None, :]
    ang = 2.0 * np.pi * k * m / N
    fwd = np.concatenate([np.cos(ang), -np.sin(ang)], axis=0)
    fwd[T, :] = (-1.0) ** np.arange(T)
    ck = np.full((T,), 2.0)
    ck[0] = 1.0
    inv_re = (np.cos(ang) * ck[:, None]).T / N
    inv_im = (-2.0 * np.sin(ang)).T / N
    inv_im[:, 0] = ((-1.0) ** np.arange(T)) / N
    inv = np.concatenate([inv_re, inv_im], axis=1)
    return jnp.asarray(fwd, F32).astype(BF16), jnp.asarray(inv, F32).astype(BF16)


def _split_bf16(x):
    hi = x.astype(BF16)
    return hi, (x - hi.astype(F32)).astype(BF16)


def _dot1(a, b):
    return jnp.dot(a, b.astype(BF16), preferred_element_type=F32)


def _filt_kernel(zemb_ref, w1_ref, b1_ref, w2_ref, b2_ref, fr_ref, w3f_ref, w3b_ref, dec_ref,
                 bias_ref, fwd_ref, kre_ref, kim_ref, hid_ref, hn_ref, F_ref, *, L):
    T = fwd_ref.shape[1]
    nb = L // T

    @pl.when((pl.program_id(0) == 0) & (pl.program_id(1) == 0))
    def _():
        z1 = jnp.dot(zemb_ref[...], w1_ref[...], precision=HI, preferred_element_type=F32) + b1_ref[...]
        h1 = jnp.sin(fr_ref[0:1, :] * z1)
        z2 = jnp.dot(h1, w2_ref[...], precision=HI, preferred_element_type=F32) + b2_ref[...]
        hid = jnp.sin(fr_ref[1:2, :] * z2)
        hid_ref[0], hid_ref[1] = _split_bf16(hid)

    w3 = jnp.concatenate([w3f_ref[...], w3b_ref[...]], axis=1)
    w_hi, w_lo = _split_bf16(w3)
    h = (jnp.dot(hid_ref[0], w_hi, preferred_element_type=F32)
         + jnp.dot(hid_ref[1], w_hi, preferred_element_type=F32)
         + jnp.dot(hid_ref[0], w_lo, preferred_element_type=F32))
    h = h * jnp.concatenate([dec_ref[...]] * 2, axis=1)
    h = h / (jnp.sum(jnp.abs(h), axis=0, keepdims=True) + EPS)
    for dr in (0, 1):
        hn_ref[dr] = h[:, dr * LANES:(dr + 1) * LANES]
    for blk in range(nb):
        F = _dot1(fwd_ref[...], h[blk * T:(blk + 1) * T])
        for dr in (0, 1):
            F_ref[dr, blk] = F[:, dr * LANES:(dr + 1) * LANES]

    row = lax.broadcasted_iota(jnp.int32, (T, LANES), 0)
    sgn = jnp.where((row & 1) == 0, 1.0, -1.0).astype(F32)
    row0 = row == 0
    bias = bias_ref[0]
    for d in range(-(nb - 1), nb):
        re = jnp.zeros((T, LANES), F32)
        im = jnp.zeros((T, LANES), F32)
        nyq = jnp.zeros((1, LANES), F32)
        if d >= 0:
            F = F_ref[0, d]
            re, im, nyq = re + F[:T], im + F[T:], nyq + F[T:T + 1]
        if d >= 1:
            F = F_ref[0, d - 1]
            p0 = hn_ref[0, (d - 1) * T:(d - 1) * T + 1, :]
            re, im, nyq = re + sgn * (F[:T] - p0), im + sgn * F[T:], nyq + (F[T:T + 1] - p0)
        e = -d
        if e >= 0:
            F = F_ref[1, e]
            re, im, nyq = re + F[:T], im - F[T:], nyq + F[T:T + 1]
        if e >= 1:
            F = F_ref[1, e - 1]
            p0 = hn_ref[1, (e - 1) * T:(e - 1) * T + 1, :]
            re, im, nyq = re + sgn * (F[:T] - p0), im - sgn * F[T:], nyq + (F[T:T + 1] - p0)
        if d == 0:
            re, nyq = re + bias, nyq + bias
        kre_ref[0, d + nb - 1] = re
        kim_ref[0, d + nb - 1] = jnp.where(row0, nyq, im)


def _hyena_filter_spectra(L, w1, b1, w2, b2, w3, freq, bias, fwd):
    T = fwd.shape[1]
    nb = L // T
    nlag = 2 * nb - 1
    P = LANES
    t = np.linspace(0.0, 1.0, L)[:, None]
    n_bands = (HY_EMB - 1) // 2
    f = np.linspace(1e-4, n_bands - 1, n_bands)
    ang = (2.0 * math.pi / L) * np.arange(L)[:, None] * f[None, :]
    zemb = np.concatenate([t, np.cos(ang), -np.sin(ang)], axis=-1)
    zemb = jnp.asarray(np.pad(zemb, ((0, 0), (0, P - HY_EMB))), F32)
    min_decay = math.log(HY_TARGET) / HY_LONG_DECAY_PCT
    max_decay = math.log(HY_TARGET) / HY_SHORT_DECAY_PCT
    deltas = np.abs(np.linspace(min_decay, max_decay, HY_WIDTH))
    dec = jnp.asarray(np.exp(-t * deltas[None, :]), F32)
    pad = P - HY_FFN
    w1p = jnp.pad(w1, ((0, P - HY_EMB), (0, pad)))
    b1p = jnp.pad(b1, (0, pad)).reshape(1, P)
    w2p = jnp.pad(w2, ((0, pad), (0, pad)))
    b2p = jnp.pad(b2, (0, pad)).reshape(1, P)
    w3p = jnp.pad(w3, ((0, pad), (0, 0)))
    frp = jnp.pad(freq, ((0, 0), (0, pad)))
    ncg = HY_WIDTH // P
    const = lambda shape: pl.BlockSpec(shape, lambda o, c: tuple(0 for _ in shape))
    kshape = jax.ShapeDtypeStruct((HY_ORDER, nlag, T, HY_WIDTH), F32)
    kspec = pl.BlockSpec((1, nlag, T, P), lambda o, c: (o, 0, 0, c))
    return pl.pallas_call(
        functools.partial(_filt_kernel, L=L),
        grid=(HY_ORDER, ncg),
        in_specs=[const((L, P)), const((P, P)), const((1, P)), const((P, P)), const((1, P)), const((2, P)),
                  pl.BlockSpec((P, P), lambda o, c: (0, o * 2 * ncg + c)),
                  pl.BlockSpec((P, P), lambda o, c: (0, o * 2 * ncg + ncg + c)),
                  pl.BlockSpec((L, P), lambda o, c: (0, c)),
                  pl.BlockSpec((1, 1, P), lambda o, c: (o, 0, c)),
                  const((2 * T, T))],
        out_specs=[kspec, kspec],
        out_shape=[kshape, kshape],
        scratch_shapes=[pltpu.VMEM((2, L, P), BF16), pltpu.VMEM((2, L, P), F32),
                        pltpu.VMEM((2, nb, 2 * T, P), F32)],
        compiler_params=_params(("arbitrary", "arbitrary")),
        name="hyena_filters",
    )(zemb, w1p, b1p, w2p, b2p, frp, w3p, w3p, dec, bias.reshape(HY_ORDER, 1, HY_WIDTH), fwd)


CONV_ROWS = 128


def _conv_block(x_ref, t0, w, b, left):
    L, C = x_ref.shape
    T = CONV_ROWS
    blk = x_ref[t0:t0 + T, :]
    prev = x_ref[t0 - SUBLANES:t0, :] if t0 > 0 else jnp.zeros((SUBLANES, C), F32)
    nxt = x_ref[t0 + T:t0 + T + SUBLANES, :] if t0 + T < L else jnp.zeros((SUBLANES, C), F32)
    ext = jnp.concatenate([prev, blk, nxt], axis=0)
    n = T + 2 * SUBLANES
    acc = jnp.broadcast_to(b, (T, C))
    for tap in range(w.shape[0]):
        s = tap - left
        sh = blk if s == 0 else pltpu.roll(ext, (-s) % n, axis=0)[SUBLANES:SUBLANES + T]
        acc = acc + w[tap:tap + 1, :] * sh
    return acc


HY_LANES_ROWS = 2 * 2048
HY_MAX_BATCH = 8


def _toeplitz(k, u, mul):
    n = len(u)
    if n == 1:
        return [mul(k[0], u[0])]
    h = n // 2
    add = lambda a, b: tuple(x + y for x, y in zip(a, b))
    sub = lambda a, b: tuple(x - y for x, y in zip(a, b))
    t0, t1, t2 = k[h:h + n - 1], k[0:n - 1], k[n:2 * n - 1]
    p = _toeplitz(t0, [add(a, b) for a, b in zip(u[:h], u[h:])], mul)
    q = _toeplitz([sub(a, b) for a, b in zip(t1, t0)], u[h:], mul)
    r = _toeplitz([sub(a, b) for a, b in zip(t2, t0)], u[:h], mul)
    return [add(a, b) for a, b in zip(p, q)] + [add(a, b) for a, b in zip(p, r)]


def _hy_kernel(v_ref, x1_ref, x2_ref, g_ref, wv_ref, w1_ref, w2_ref, bv_ref, b1_ref, b2_ref,
               kre_ref, kim_ref, fwd_ref, inv_ref, y_ref,
               x1c_ref, x2c_ref, u_ref, U_ref, Y_ref, *, L):
    T = fwd_ref.shape[1]
    P = LANES
    nb = L // T
    NBT = v_ref.shape[0]
    left = (HY_SHORT - 1) // 2
    for t0 in range(0, L, CONV_ROWS):
        rows = slice(t0, t0 + CONV_ROWS)
        for p in range(NBT):
            u_ref[rows, p * P:(p + 1) * P] = _conv_block(v_ref.at[p], t0, wv_ref[...], bv_ref[...], left).astype(BF16)
            x1c_ref[p, rows, :] = _conv_block(x1_ref.at[p], t0, w1_ref[...], b1_ref[...], left)
            x2c_ref[p, rows, :] = _conv_block(x2_ref.at[p], t0, w2_ref[...], b2_ref[...], left)

    def block_spectra():
        for j in range(nb):
            U_ref[j] = jnp.dot(fwd_ref[...], u_ref[j * T:(j + 1) * T, :], preferred_element_type=F32)

    block_spectra()

    row0 = lax.broadcasted_iota(jnp.int32, (SUBLANES, NBT * P), 0) == 0
    tile = lambda x: jnp.concatenate([x] * NBT, axis=1)

    def cmul(k, u):
        kr, ki = tile(k[0]), tile(k[1])
        return (kr * u[0] - ki * u[1], kr * u[1] + ki * u[0])

    def rmul(k, u):
        return (tile(k[0]) * u[0], tile(k[1]) * u[1])

    def spectra_products(o):
        def load(r, rows):
            k = [(kre_ref[o, l, pl.ds(r, rows), :], kim_ref[o, l, pl.ds(r, rows), :]) for l in range(2 * nb - 1)]
            u = [(U_ref[j, pl.ds(r, rows), :], U_ref[j, pl.ds(T + r, rows), :]) for j in range(nb)]
            return k, u

        def chunk(rc, carry):
            r = pl.multiple_of(rc * SUBLANES, SUBLANES)
            y = _toeplitz(*load(r, SUBLANES), cmul)
            for i in range(nb):
                Y_ref[i, pl.ds(r, SUBLANES), :] = y[i][0]
                Y_ref[i, pl.ds(T + r, SUBLANES), :] = y[i][1]
            return carry

        lax.fori_loop(0, T // SUBLANES, chunk, 0)
        y = _toeplitz(*load(0, SUBLANES), rmul)
        for i in range(nb):
            Y_ref[i, 0:SUBLANES, :] = jnp.where(row0, y[i][0], Y_ref[i, 0:SUBLANES, :])
            Y_ref[i, T:T + SUBLANES, :] = jnp.where(row0, y[i][1], Y_ref[i, T:T + SUBLANES, :])

    def long_conv_block(i):
        return _dot1(inv_ref[...], Y_ref[i])

    spectra_products(0)

    def order0(i, carry):
        r0 = pl.multiple_of(i * T, T)
        y = long_conv_block(i)
        for p in range(NBT):
            z1 = x1c_ref[p, pl.ds(r0, T), :] * y[:, p * P:(p + 1) * P]
            u_ref[pl.ds(r0, T), p * P:(p + 1) * P] = z1.astype(BF16)
        return carry

    lax.fori_loop(0, nb, order0, 0, unroll=2 if nb % 2 == 0 else 1)
    block_spectra()
    spectra_products(1)

    def order1(i, carry):
        r0 = pl.multiple_of(i * T, T)
        y = long_conv_block(i)
        for p in range(NBT):
            out = x2c_ref[p, pl.ds(r0, T), :] * y[:, p * P:(p + 1) * P]
            y_ref[p, pl.ds(r0, T), :] = (out * _silu(g_ref[p, pl.ds(r0, T), :])).astype(BF16)
        return carry

    lax.fori_loop(0, nb, order1, 0, unroll=2 if nb % 2 == 0 else 1)


def _hyena(z, conv_w, conv_b, kre, kim, fwd, inv):
    B, L, _ = z.shape
    NBT = min(B, max(2, HY_LANES_ROWS // L), HY_MAX_BATCH)
    assert B % NBT == 0
    T = fwd.shape[1]
    nb = L // T
    nlag = 2 * nb - 1
    P = LANES
    ncg = HY_WIDTH // P
    col = lambda off: pl.BlockSpec((NBT, L, P), lambda c, b, off=off: (b, 0, off + c))
    wspec = lambda part: pl.BlockSpec((HY_SHORT, P), lambda c, b, part=part: (0, part * ncg + c))
    bspec = lambda part: pl.BlockSpec((1, P), lambda c, b, part=part: (0, part * ncg + c))
    kspec = pl.BlockSpec((HY_ORDER, nlag, T, P), lambda c, b: (0, 0, 0, c))
    fspec = pl.BlockSpec((2 * T, T), lambda c, b: (0, 0))
    ispec = pl.BlockSpec((T, 2 * T), lambda c, b: (0, 0))
    cb = conv_b.reshape(1, 3 * HY_WIDTH)
    return pl.pallas_call(
        functools.partial(_hy_kernel, L=L),
        grid=(ncg, B // NBT),
        in_specs=[col(_HV0), col(_HX10), col(_HX20), col(_GH0),
                  wspec(0), wspec(1), wspec(2), bspec(0), bspec(1), bspec(2),
                  kspec, kspec, fspec, ispec],
        out_specs=pl.BlockSpec((NBT, L, P), lambda c, b: (b, 0, c)),
        out_shape=jax.ShapeDtypeStruct((B, L, HY_WIDTH), BF16),
        scratch_shapes=[pltpu.VMEM((NBT, L, P), F32), pltpu.VMEM((NBT, L, P), F32),
                        pltpu.VMEM((L, NBT * P), BF16),
                        pltpu.VMEM((nb, 2 * T, NBT * P), F32), pltpu.VMEM((nb, 2 * T, NBT * P), F32)],
        compiler_params=_params(("arbitrary", "arbitrary")),
        name="hyena",
    )(z, z, z, z, conv_w, conv_w, conv_w, cb, cb, cb, kre, kim, fwd, inv)


LRU_TILE = 256
LRU_GROUPS = LRU_TILE // LANES
LRU_ROWS = 128


def _lru_kernel(*refs, L, has_state):
    it = iter(refs)
    x_ref, g_ref, cw_ref, cb_ref, wg_ref, gb_ref, lam_ref = (next(it) for _ in range(7))
    h0_ref = next(it) if has_state else None
    y_ref, sN_ref, xp_ref, A_ref, B_ref, H_ref, C_ref = (next(it) for _ in range(7))

    S = L // SUBLANES
    W = LRU_TILE
    G = LRU_GROUPS
    R = LRU_ROWS
    left = LRU_CONV // 2
    pad = left * SUBLANES

    sub = lax.broadcasted_iota(jnp.int32, (SUBLANES, W), 0)
    for k in range(left):
        tail = pltpu.roll(x_ref[0, L - pad + k * SUBLANES:L - pad + (k + 1) * SUBLANES, :], 1, axis=0)
        xp_ref[k * SUBLANES:(k + 1) * SUBLANES, :] = jnp.where(sub == 0, 0.0, tail)
    for k in range(LRU_CONV - 1 - left):
        head = pltpu.roll(x_ref[0, k * SUBLANES:(k + 1) * SUBLANES, :], SUBLANES - 1, axis=0)
        xp_ref[pad + L + k * SUBLANES:pad + L + (k + 1) * SUBLANES, :] = jnp.where(sub == SUBLANES - 1, 0.0, head)

    def copy(c, carry):
        r0 = pl.multiple_of(c * R, R)
        xp_ref[pl.ds(pad + r0, R), :] = x_ref[0, pl.ds(r0, R), :]
        return carry

    lax.fori_loop(0, L // R, copy, 0)

    csp = [(0.5 * LRU_C) * _softplus(-lam_ref[d]) for d in (0, 1)]

    def coeffs(c, carry):
        r0 = pl.multiple_of(c * R, R)
        u = jnp.broadcast_to(cb_ref[...], (R, W))
        for k in range(LRU_CONV):
            u = u + cw_ref[k:k + 1, :] * xp_ref[pl.ds(r0 + k * SUBLANES, R), :]
        half = jnp.dot(u.astype(BF16), wg_ref[0], preferred_element_type=F32) + gb_ref[0]
        hu = 0.5 * u
        for d in (0, 1):
            t_r = jnp.tanh(half[:, (2 * d) * W:(2 * d + 1) * W])
            t_i = jnp.tanh(half[:, (2 * d + 1) * W:(2 * d + 2) * W])
            nl = csp[d] * t_r + csp[d]
            a = jnp.exp(-nl)
            s2 = jnp.tanh(nl) * (a * a + 1.0)
            b = jnp.where(s2 > 0.0, s2 * lax.rsqrt(s2), 0.0) * (t_i * hu + hu)
            for gi in range(G):
                lanes = slice(gi * LANES, (gi + 1) * LANES)
                A_ref[d, gi, pl.ds(r0, R), :] = a[:, lanes]
                B_ref[d, gi, pl.ds(r0, R), :] = b[:, lanes]
        return carry

    lax.fori_loop(0, L // R, coeffs, 0)

    def scan_body(t, carry):
        out = []
        for d in (0, 1):
            r8 = pl.multiple_of((t if d == 0 else S - 1 - t) * SUBLANES, SUBLANES)
            for gi in range(G):
                h, acc = carry[d * G + gi]
                a = A_ref[d, gi, pl.ds(r8, SUBLANES), :]
                h = a * h + B_ref[d, gi, pl.ds(r8, SUBLANES), :]
                acc = acc * a
                H_ref[d, gi, pl.ds(r8, SUBLANES), :] = h
                C_ref[d, gi, pl.ds(r8, SUBLANES), :] = acc
                out.append((h, acc))
        return tuple(out)

    init = tuple((jnp.zeros((SUBLANES, LANES), F32), jnp.ones((SUBLANES, LANES), F32))
                 for _ in range(2 * G))
    lax.fori_loop(0, S, scan_body, init, unroll=4)

    hin = [None] * (2 * G)
    for d in (0, 1):
        last = (S - 1 if d == 0 else 0) * SUBLANES
        for gi in range(G):
            hl = H_ref[d, gi, last:last + SUBLANES, :]
            ac = C_ref[d, gi, last:last + SUBLANES, :]
            if has_state:
                h = h0_ref[0, d:d + 1, gi * LANES:(gi + 1) * LANES]
            else:
                h = jnp.zeros((1, LANES), F32)
            rows = [None] * SUBLANES
            for j in (range(SUBLANES) if d == 0 else range(SUBLANES - 1, -1, -1)):
                rows[j] = h
                h = hl[j:j + 1, :] + ac[j:j + 1, :] * h
            hin[d * G + gi] = jnp.concatenate([jnp.concatenate(rows, axis=0)] * (R // SUBLANES), axis=0)
            sN_ref[0, d:d + 1, gi * LANES:(gi + 1) * LANES] = h

    def combine(c, carry):
        r0 = pl.multiple_of(c * R, R)
        cols = []
        for gi in range(G):
            tot = jnp.zeros((R, LANES), F32)
            for d in (0, 1):
                tot = tot + (H_ref[d, gi, pl.ds(r0, R), :] + C_ref[d, gi, pl.ds(r0, R), :] * hin[d * G + gi])
            cols.append(tot)
        y_ref[0, pl.ds(r0, R), :] = jnp.concatenate(cols, axis=1) * _silu(g_ref[0, pl.ds(r0, R), :])
        return carry

    lax.fori_loop(0, L // R, combine, 0)


def _rglru(z4, conv_w, conv_b, gate_w, gate_b, lam, h0):
    B, S, _, _ = z4.shape
    L = S * SUBLANES
    has_state = h0 is not None
    W = LRU_TILE
    nh = LRU_WIDTH // W
    bpt = W // LRU_BLOCK_DIM
    gw = gate_w.reshape(2, 2, nh, bpt, LRU_BLOCK_DIM, LRU_BLOCK_DIM)
    eye = jnp.eye(bpt, dtype=F32)
    dense = (0.5 * jnp.einsum('dkhnij,nm->hnidkmj', gw, eye)).reshape(nh, W, 4 * W).astype(BF16)
    gb = 0.5 * gate_b.reshape(2, 2, nh, W).transpose(2, 0, 1, 3).reshape(nh, 1, 4 * W)
    zr = z4.reshape(B, L, D_LRU)
    off = lambda base: pl.BlockSpec((1, L, W), lambda b, h, base=base: (b, 0, base + h))
    in_specs = [off(0), off(LRU_WIDTH // W),
                pl.BlockSpec((LRU_CONV, W), lambda b, h: (0, h)),
                pl.BlockSpec((1, W), lambda b, h: (0, h)),
                pl.BlockSpec((1, W, 4 * W), lambda b, h: (h, 0, 0)),
                pl.BlockSpec((1, 1, 4 * W), lambda b, h: (h, 0, 0)),
                pl.BlockSpec((2, 1, W), lambda b, h: (0, 0, h))]
    args = [zr, zr, conv_w, conv_b.reshape(1, LRU_WIDTH), dense, gb, lam.reshape(2, 1, LRU_WIDTH)]
    if has_state:
        in_specs.append(pl.BlockSpec((1, 2, W), lambda b, h: (b, 0, h)))
        args.append(h0)
    y, s = pl.pallas_call(
        functools.partial(_lru_kernel, L=L, has_state=has_state),
        grid=(B, nh),
        in_specs=in_specs,
        out_specs=[pl.BlockSpec((1, L, W), lambda b, h: (b, 0, h)),
                   pl.BlockSpec((1, 2, W), lambda b, h: (b, 0, h))],
        out_shape=[jax.ShapeDtypeStruct((B, L, LRU_WIDTH), F32),
                   jax.ShapeDtypeStruct((B, 2, LRU_WIDTH), F32)],
        scratch_shapes=[pltpu.VMEM((L + (LRU_CONV - 1) * SUBLANES, W), F32)]
        + [pltpu.VMEM((2, LRU_GROUPS, L, LANES), F32) for _ in range(4)],
        compiler_params=_params(("arbitrary", "arbitrary")),
        name="rglru",
    )(*args)
    return y.reshape(B, S, SUBLANES, LRU_WIDTH), s


def _rope_tables(L):
    rows = L // GRID_W
    row = np.repeat(np.arange(rows, dtype=np.float64), GRID_W)
    col = np.tile(np.arange(GRID_W, dtype=np.float64), rows)
    n_f = RET_HEAD_DIM // 4
    inv = ROPE_BASE ** (-np.arange(n_f, dtype=np.float64) / n_f)
    ang = np.concatenate([row[:, None] * inv[None], col[:, None] * inv[None]], axis=-1)
    cos, sin = np.cos(ang), np.sin(ang)
    return (jnp.asarray(np.concatenate([cos, cos], axis=-1), F32),
            jnp.asarray(np.concatenate([-sin, sin], axis=-1), F32))


def kernel(x_prompt, x_sample, state_ret, state_lru, c, c_ctx, norm_g, ada_w, ada_b, w_in, ret_decay_logit, hy_conv_w, hy_conv_b, hy_ffn_w1, hy_ffn_b1, hy_ffn_w2, hy_ffn_b2, hy_ffn_w3, hy_freq, hy_bias, lru_conv_w, lru_conv_b, lru_gate_w, lru_gate_b, lru_lambda, w_out, final_g):
    Bp, Lp, _ = x_prompt.shape
    Bs, Ls, _ = x_sample.shape
    assert Bs + 1 <= SUBLANES

    cvec = jnp.zeros((SUBLANES, D_MODEL), F32).at[:Bs].set(c).at[Bs].set(c_ctx)
    mod = _modulation(cvec, ada_w, ada_b)
    rope = _rope_tables(Ls)
    dft = {L: _dft_matrices(min(DFT_MAX, L)) for L in {Lp, Ls}}
    w_in_b = w_in.astype(BF16)
    w_out_b = w_out.astype(BF16)
    fg = final_g.reshape(1, D_MODEL)

    xc, xl = x_prompt, x_sample
    new_state_ret, new_lru = None, []
    for l in range(DEPTH):
        shift, scale, gate = (mod[l, :, i * D_MODEL:(i + 1) * D_MODEL] for i in range(3))
        g = norm_g[l].reshape(1, D_MODEL)
        final = l == DEPTH - 1
        filt = {}
        for L in sorted({Lp, Ls}):
            filt[L] = _hyena_filter_spectra(L, hy_ffn_w1[l], hy_ffn_b1[l], hy_ffn_w2[l], hy_ffn_b2[l],
                                            hy_ffn_w3[l], hy_freq[l], hy_bias[l], dft[L][0])

        def layer(x, sel, rope_t, ret_s0, lru_s0, ret_states):
            sh, sc, gt = (m[sel][:, None, :] for m in (shift, scale, gate))
            z, z_lru = _in_proj(x, sh, sc, g, w_in_b, l)
            y_ret, *s_ret = _retention(z, ret_decay_logit[l], rope_t, ret_s0, l, ret_states)
            kre, kim = filt[x.shape[1]]
            y_hy = _hyena(z, hy_conv_w[l], hy_conv_b[l], kre, kim, *dft[x.shape[1]])
            y_lru, s_lru = _rglru(z_lru, lru_conv_w[l], lru_conv_b[l], lru_gate_w[l], lru_gate_b[l],
                                  lru_lambda[l], lru_s0)
            return _out_proj(y_ret, y_hy, y_lru, x, gt, w_out_b, l, fg, final), s_ret, s_lru

        xc, (new_state_ret,), ls = layer(xc, slice(Bs, Bs + 1), None, None, None,
                                         "new" if l == 0 else new_state_ret)
        new_lru.append(ls)
        xl, _, _ = layer(xl, slice(0, Bs), rope, state_ret, state_lru[:, l], None)

    new_state_lru = jnp.stack(new_lru, axis=1).astype(x_prompt.dtype)
    return (xc, xl, new_state_ret.astype(x_prompt.dtype), new_state_lru)
```

```python
import functools
import math

import numpy as np
import jax
import jax.numpy as jnp
from jax import lax
from jax.experimental import pallas as pl
from jax.experimental.pallas import tpu as pltpu

F32 = jnp.float32
BF16 = jnp.bfloat16
HI = lax.Precision.HIGHEST

D_MODEL = 1024
DEPTH = 2
GRID_W = 64
EPS = 1e-6
RET_HEADS = 4
RET_HEAD_DIM = 128
RET_WIDTH = RET_HEADS * RET_HEAD_DIM
ROPE_BASE = 10000.0
HY_WIDTH = 512
HY_ORDER = 2
HY_SHORT = 3
HY_EMB = 33
HY_FFN = 64
HY_SHORT_DECAY_PCT = 0.3
HY_LONG_DECAY_PCT = 1.5
HY_TARGET = 1e-2
LRU_WIDTH = 512
LRU_BLOCKS = 8
LRU_BLOCK_DIM = LRU_WIDTH // LRU_BLOCKS
LRU_CONV = 4
LRU_C = 8.0
D_MIX = RET_WIDTH + HY_WIDTH + LRU_WIDTH
D_IN = 4 * RET_WIDTH + 4 * HY_WIDTH + 2 * LRU_WIDTH

LANES = 128
SUBLANES = 8
DFT_MAX = 512
ROW_TILE = 256
VMEM_LIMIT = 56 * 1024 * 1024

_Q0, _K0, _V0, _GR0 = 0, 4, 8, 12
_HV0, _HX10, _HX20, _GH0 = 16, 20, 24, 28


def _sigmoid(x):
    return 0.5 * jnp.tanh(0.5 * x) + 0.5


def _silu(x):
    return x * _sigmoid(x)


def _softplus(x):
    return jnp.maximum(x, 0.0) + jnp.log1p(jnp.exp(-jnp.abs(x)))


def _params(sem):
    return pltpu.CompilerParams(dimension_semantics=sem, vmem_limit_bytes=VMEM_LIMIT)


def _mod_kernel(c_ref, w_ref, b_ref, o_ref):
    s_hi, s_lo = _split_bf16(_silu(c_ref[...]))
    w_hi, w_lo = _split_bf16(w_ref[0])
    o_ref[0] = (jnp.dot(s_hi, w_hi, preferred_element_type=F32)
                + jnp.dot(s_lo, w_hi, preferred_element_type=F32)
                + jnp.dot(s_hi, w_lo, preferred_element_type=F32)) + b_ref[0]


def _modulation(cvec, ada_w, ada_b):
    tn = 768
    return pl.pallas_call(
        _mod_kernel,
        grid=(DEPTH, 3 * D_MODEL // tn),
        in_specs=[pl.BlockSpec((SUBLANES, D_MODEL), lambda l, j: (0, 0)),
                  pl.BlockSpec((1, D_MODEL, tn), lambda l, j: (l, 0, j)),
                  pl.BlockSpec((1, 1, tn), lambda l, j: (l, 0, j))],
        out_specs=pl.BlockSpec((1, SUBLANES, tn), lambda l, j: (l, 0, j)),
        out_shape=jax.ShapeDtypeStruct((DEPTH, SUBLANES, 3 * D_MODEL), F32),
        compiler_params=_params(("arbitrary", "arbitrary")),
        name="modulation",
    )(cvec, ada_w, ada_b.reshape(DEPTH, 1, 3 * D_MODEL))


D_MAIN = 4 * RET_WIDTH + 4 * HY_WIDTH
D_LRU = 2 * LRU_WIDTH


def _in_kernel(x_ref, sh_ref, sc_ref, g_ref, w_ref, o_ref, ol_ref):
    _, nseg, rows, _ = x_ref.shape
    x = x_ref[0].reshape(nseg * rows, D_MODEL)
    ms = jnp.mean(x * x, axis=-1, keepdims=True)
    y = x * lax.rsqrt(ms + EPS) * g_ref[...]
    h = (y * (1.0 + sc_ref[0]) + sh_ref[0]).astype(BF16)
    tn = 1024
    for n in range(D_MAIN // tn):
        z = jnp.dot(h, w_ref[:, n * tn:(n + 1) * tn], preferred_element_type=F32)
        o_ref[0, :, :, n * tn:(n + 1) * tn] = z.reshape(nseg, rows, tn)
    zl = jnp.dot(h, w_ref[:, D_MAIN:D_IN], preferred_element_type=F32)
    for j in range(nseg):
        ol_ref[0, :, j, :] = zl[j * rows:(j + 1) * rows]


def _in_proj(x, shift, scale, g, w_bf16, layer):
    B, L, _ = x.shape
    S = L // SUBLANES
    rows = min(ROW_TILE // SUBLANES, S)
    per_batch = shift.shape[0] > 1
    mod_map = (lambda b, i: (b, 0, 0)) if per_batch else (lambda b, i: (0, 0, 0))
    seg = lambda C: pl.BlockSpec((1, SUBLANES, rows, C), lambda b, i: (b, 0, i, 0))
    z, z_lru = pl.pallas_call(
        _in_kernel,
        grid=(B, S // rows),
        in_specs=[seg(D_MODEL),
                  pl.BlockSpec((1, 1, D_MODEL), mod_map),
                  pl.BlockSpec((1, 1, D_MODEL), mod_map),
                  pl.BlockSpec((1, D_MODEL), lambda b, i: (0, 0)),
                  pl.BlockSpec((None, D_MODEL, D_IN), lambda b, i: (layer, 0, 0),
                               pipeline_mode=pl.Buffered(1))],
        out_specs=[seg(D_MAIN), pl.BlockSpec((1, rows, SUBLANES, D_LRU), lambda b, i: (b, i, 0, 0))],
        out_shape=[jax.ShapeDtypeStruct((B, SUBLANES, S, D_MAIN), F32),
                   jax.ShapeDtypeStruct((B, S, SUBLANES, D_LRU), F32)],
        compiler_params=_params(("arbitrary", "arbitrary")),
        name="in_proj",
    )(x.reshape(B, SUBLANES, S, D_MODEL), shift, scale, g, w_bf16)
    return z.reshape(B, L, D_MAIN), z_lru


OUT_ROWS = 512


def _out_kernel(yr_ref, yh_ref, yl_ref, x_ref, gate_ref, w_ref, fg_ref, o_ref, *, final):
    _, nseg, rows, _ = x_ref.shape
    n = nseg * rows
    acc = jnp.dot(yr_ref[0].reshape(n, RET_WIDTH), w_ref[0:RET_WIDTH], preferred_element_type=F32)
    acc = acc + jnp.dot(yh_ref[0].reshape(n, HY_WIDTH), w_ref[RET_WIDTH:RET_WIDTH + HY_WIDTH],
                        preferred_element_type=F32)
    yl = jnp.concatenate([yl_ref[0, :, j, :] for j in range(nseg)], axis=0)
    acc = acc + jnp.dot(yl.astype(BF16), w_ref[RET_WIDTH + HY_WIDTH:D_MIX], preferred_element_type=F32)
    x = x_ref[0].reshape(n, D_MODEL) + gate_ref[0] * acc
    if final:
        ms = jnp.mean(x * x, axis=-1, keepdims=True)
        x = x * lax.rsqrt(ms + EPS) * fg_ref[...]
    o_ref[0] = x.reshape(nseg, rows, D_MODEL)


def _out_proj(y_ret, y_hy, y_lru, x, gate, w_bf16, layer, final_g, final):
    B, L, _ = x.shape
    S = L // SUBLANES
    rows = min(OUT_ROWS // SUBLANES, S)
    per_batch = gate.shape[0] > 1
    mod_map = (lambda b, i: (b, 0, 0)) if per_batch else (lambda b, i: (0, 0, 0))
    seg = lambda C: pl.BlockSpec((1, SUBLANES, rows, C), lambda b, i: (b, 0, i, 0))
    view = lambda a: a.reshape(B, SUBLANES, S, a.shape[-1])
    out = pl.pallas_call(
        functools.partial(_out_kernel, final=final),
        grid=(B, S // rows),
        in_specs=[seg(RET_WIDTH), seg(HY_WIDTH),
                  pl.BlockSpec((1, rows, SUBLANES, LRU_WIDTH), lambda b, i: (b, i, 0, 0)),
                  seg(D_MODEL),
                  pl.BlockSpec((1, 1, D_MODEL), mod_map),
                  pl.BlockSpec((None, D_MIX, D_MODEL), lambda b, i: (layer, 0, 0),
                               pipeline_mode=pl.Buffered(1)),
                  pl.BlockSpec((1, D_MODEL), lambda b, i: (0, 0))],
        out_specs=seg(D_MODEL),
        out_shape=jax.ShapeDtypeStruct((B, SUBLANES, S, D_MODEL), F32),
        compiler_params=_params(("arbitrary", "arbitrary")),
        name="out_proj",
    )(view(y_ret), view(y_hy), y_lru, view(x), gate, w_bf16, final_g)
    return out.reshape(B, L, D_MODEL)


RET_HEAD_ROWS = 2 * 2048
RET_BLOCK = 256


def _ret_kernel(*refs, L, use_rope, has_state, emit_state, chained, layer):
    it = iter(refs)
    q_ref, k_ref, v_ref, g_ref = next(it), next(it), next(it), next(it)
    cos_ref = sin_ref = s0_ref = sN_ref = None
    if use_rope:
        cos_ref, sin_ref = next(it), next(it)
    lg_ref = next(it)
    if has_state:
        s0_ref = next(it)
    if chained:
        next(it)
    y_ref = next(it)
    if emit_state:
        sN_ref = next(it)
    R_ref, M_ref, Z_ref, D_ref, KV_ref, kb_ref = (next(it) for _ in range(6))

    C = min(RET_BLOCK, L)
    DH = RET_HEAD_DIM
    HP = q_ref.shape[-1] // DH
    n = L // C
    cross = has_state or n > 1
    kscale = DH ** -0.5

    @pl.when(pl.program_id(1) == 0)
    def _():
        ii = lax.broadcasted_iota(jnp.int32, (C, C), 0)
        jj = lax.broadcasted_iota(jnp.int32, (C, C), 1)
        diff = (ii - jj).astype(F32)
        ri = lax.broadcasted_iota(jnp.int32, (C, DH), 0).astype(F32)
        for hh in range(HP):
            lgf = -_softplus(-lg_ref[0, hh])
            lgb = -_softplus(-lg_ref[1, hh])
            M_ref[hh] = (jnp.where(diff >= 0, jnp.exp(lgf[:, :C] * jnp.maximum(diff, 0.0)), 0.0)
                         + jnp.where(diff <= 0, jnp.exp(lgb[:, :C] * jnp.maximum(-diff, 0.0)), 0.0))
            lf, lb = lgf[:, :DH], lgb[:, :DH]
            Z_ref[hh, 0] = jnp.concatenate([jnp.exp(lf * (C - 1.0 - ri)), jnp.exp(lb * ri)], axis=1)
            Z_ref[hh, 1] = jnp.concatenate([jnp.exp(lf * (ri + 1.0)), jnp.exp(lb * (C - ri))], axis=1)
            D_ref[hh] = jnp.exp(jnp.concatenate([jnp.broadcast_to(lf * C, (DH, DH)),
                                                 jnp.broadcast_to(lb * C, (DH, DH))], axis=0))

    def rotary(x, r0):
        if not use_rope:
            return x
        return x * cos_ref[pl.ds(r0, C), :] + pltpu.roll(x, 64, axis=1) * sin_ref[pl.ds(r0, C), :]

    def chunk_kv(c, carry):
        r0 = pl.multiple_of(c * C, C)
        for hh in range(HP):
            lanes = slice(hh * DH, (hh + 1) * DH)
            k = rotary(k_ref[0, pl.ds(r0, C), lanes] * kscale, r0)
            kb_ref[hh, pl.ds(r0, C), :] = k.astype(BF16)
            kz = (jnp.concatenate([k, k], axis=1) * Z_ref[hh, 0]).astype(BF16)
            vb = v_ref[0, pl.ds(r0, C), lanes].astype(BF16)
            KV_ref[hh, c] = lax.dot_general(kz, vb, (((0,), (0,)), ((), ())), preferred_element_type=F32)
        return carry

    lax.fori_loop(0, n, chunk_kv, 0, unroll=2 if n % 2 == 0 else 1)

    for d in (0, 1):
        for hh in range(HP):
            R_ref[d, hh] = s0_ref[0, d, hh] if has_state else jnp.zeros((DH, DH), F32)

    def states(t, carry):
        for d in (0, 1):
            c = t if d == 0 else n - 1 - t
            rows = slice(d * DH, (d + 1) * DH)
            for hh in range(HP):
                R = R_ref[d, hh]
                R_ref[d, hh] = D_ref[hh, rows, :] * R + KV_ref[hh, c, rows, :]
                KV_ref[hh, c, rows, :] = R
        return carry

    lax.fori_loop(0, n, states, 0)
    if emit_state:
        for l in range(sN_ref.shape[1]):
            if chained or l == layer:
                for d in (0, 1):
                    for hh in range(HP):
                        sN_ref[0, l, d, hh] = R_ref[d, hh]
            else:
                sN_ref[0, l] = jnp.zeros(sN_ref.shape[2:], F32)

    def chunk_out(c, carry):
        r0 = pl.multiple_of(c * C, C)
        cols = []
        for hh in range(HP):
            lanes = slice(hh * DH, (hh + 1) * DH)
            q = rotary(q_ref[0, pl.ds(r0, C), lanes], r0)
            vb = v_ref[0, pl.ds(r0, C), lanes].astype(BF16)
            s = lax.dot_general(q.astype(BF16), kb_ref[hh, pl.ds(r0, C), :], (((1,), (1,)), ((), ())),
                                preferred_element_type=F32) * M_ref[hh]
            o = jnp.dot(s.astype(BF16), vb, preferred_element_type=F32)
            if cross:
                qx = (jnp.concatenate([q, q], axis=1) * Z_ref[hh, 1]).astype(BF16)
                o = o + jnp.dot(qx, KV_ref[hh, c].astype(BF16), preferred_element_type=F32)
            cols.append(o * lax.rsqrt(jnp.mean(o * o, axis=-1, keepdims=True) + EPS))
        g = g_ref[0, pl.ds(r0, C), :]
        y_ref[0, pl.ds(r0, C), :] = (jnp.concatenate(cols, axis=1) * _silu(g)).astype(BF16)
        return carry

    lax.fori_loop(0, n, chunk_out, 0, unroll=2 if n % 2 == 0 else 1)


def _retention(z, decay_logit, rope, state0, layer, states):
    B, L, _ = z.shape
    use_rope = rope is not None
    has_state = state0 is not None
    emit_state = states is not None
    chained = emit_state and not isinstance(states, str)
    H = RET_HEADS
    HP = min(H, max(1, RET_HEAD_ROWS // L))
    W = HP * RET_HEAD_DIM
    col = lambda off: pl.BlockSpec((1, L, W), lambda h, b, off=off: (b, 0, off * LANES // W + h))
    in_specs = [col(_Q0), col(_K0), col(_V0), col(_GR0)]
    args = [z, z, z, z]
    if use_rope:
        in_specs += [pl.BlockSpec((L, LANES), lambda h, b: (0, 0))] * 2
        args += list(rope)
    C = min(RET_BLOCK, L)
    in_specs.append(pl.BlockSpec((2, HP, 1, RET_BLOCK), lambda h, b: (0, h, 0, 0)))
    args.append(jnp.broadcast_to(decay_logit[:, :, None, None], (2, H, 1, RET_BLOCK)))
    if has_state:
        in_specs.append(pl.BlockSpec((1, None, 2, HP, RET_HEAD_DIM, RET_HEAD_DIM),
                                     lambda h, b: (b, layer, 0, h, 0, 0)))
        args.append(state0)
    out_specs = [pl.BlockSpec((1, L, W), lambda h, b: (b, 0, h))]
    out_shape = [jax.ShapeDtypeStruct((B, L, RET_WIDTH), BF16)]
    aliases = {}
    if emit_state:
        nl, l0 = (1, layer) if chained else (DEPTH, 0)
        out_specs.append(pl.BlockSpec((1, nl, 2, HP, RET_HEAD_DIM, RET_HEAD_DIM),
                                      lambda h, b: (b, l0, 0, h, 0, 0)))
        out_shape.append(jax.ShapeDtypeStruct((B, DEPTH, 2, H, RET_HEAD_DIM, RET_HEAD_DIM), F32))
        if chained:
            aliases = {len(args): 1}
            in_specs.append(pl.BlockSpec(memory_space=pl.ANY))
            args.append(states)
    return pl.pallas_call(
        functools.partial(_ret_kernel, L=L, use_rope=use_rope, has_state=has_state,
                          emit_state=emit_state, chained=chained, layer=layer),
        grid=(H // HP, B),
        in_specs=in_specs,
        out_specs=out_specs,
        out_shape=out_shape,
        input_output_aliases=aliases,
        scratch_shapes=[pltpu.VMEM((2, HP, RET_HEAD_DIM, RET_HEAD_DIM), F32),
                        pltpu.VMEM((HP, C, C), F32),
                        pltpu.VMEM((HP, 2, C, 2 * RET_HEAD_DIM), F32),
                        pltpu.VMEM((HP, 2 * RET_HEAD_DIM, RET_HEAD_DIM), F32),
                        pltpu.VMEM((HP, L // C, 2 * RET_HEAD_DIM, RET_HEAD_DIM), F32),
                        pltpu.VMEM((HP, L, RET_HEAD_DIM), BF16)],
        compiler_params=_params(("arbitrary", "arbitrary")),
        name="retention",
    )(*args)


def _dft_matrices(T):
    N = 2 * T
    k = np.arange(T, dtype=np.float64)[:, None]
    m = np.arange(T, dtype=np.float64)[None, :]
    ang = 2.0 * np.pi * k * m / N
    fwd = np.concatenate([np.cos(ang), -np.sin(ang)], axis=0)
    fwd[T, :] = (-1.0) ** np.arange(T)
    ck = np.full((T,), 2.0)
    ck[0] = 1.0
    inv_re = (np.cos(ang) * ck[:, None]).T / N
    inv_im = (-2.0 * np.sin(ang)).T / N
    inv_im[:, 0] = ((-1.0) ** np.arange(T)) / N
    inv = np.concatenate([inv_re, inv_im], axis=1)
    return jnp.asarray(fwd, F32).astype(BF16), jnp.asarray(inv, F32).astype(BF16)


def _split_bf16(x):
    hi = x.astype(BF16)
    return hi, (x - hi.astype(F32)).astype(BF16)


def _dot1(a, b):
    return jnp.dot(a, b.astype(BF16), preferred_element_type=F32)


def _filt_kernel(zemb_ref, w1_ref, b1_ref, w2_ref, b2_ref, fr_ref, w3f_ref, w3b_ref, dec_ref,
                 bias_ref, fwd_ref, kre_ref, kim_ref, hid_ref, hn_ref, F_ref, *, L):
    T = fwd_ref.shape[1]
    nb = L // T

    @pl.when((pl.program_id(0) == 0) & (pl.program_id(1) == 0))
    def _():
        z1 = jnp.dot(zemb_ref[...], w1_ref[...], precision=HI, preferred_element_type=F32) + b1_ref[...]
        h1 = jnp.sin(fr_ref[0:1, :] * z1)
        z2 = jnp.dot(h1, w2_ref[...], precision=HI, preferred_element_type=F32) + b2_ref[...]
        hid = jnp.sin(fr_ref[1:2, :] * z2)
        hid_ref[0], hid_ref[1] = _split_bf16(hid)

    w3 = jnp.concatenate([w3f_ref[...], w3b_ref[...]], axis=1)
    w_hi, w_lo = _split_bf16(w3)
    h = (jnp.dot(hid_ref[0], w_hi, preferred_element_type=F32)
         + jnp.dot(hid_ref[1], w_hi, preferred_element_type=F32)
         + jnp.dot(hid_ref[0], w_lo, preferred_element_type=F32))
    h = h * jnp.concatenate([dec_ref[...]] * 2, axis=1)
    h = h / (jnp.sum(jnp.abs(h), axis=0, keepdims=True) + EPS)
    for dr in (0, 1):
        hn_ref[dr] = h[:, dr * LANES:(dr + 1) * LANES]
    for blk in range(nb):
        F = _dot1(fwd_ref[...], h[blk * T:(blk + 1) * T])
        for dr in (0, 1):
            F_ref[dr, blk] = F[:, dr * LANES:(dr + 1) * LANES]

    row = lax.broadcasted_iota(jnp.int32, (T, LANES), 0)
    sgn = jnp.where((row & 1) == 0, 1.0, -1.0).astype(F32)
    row0 = row == 0
    bias = bias_ref[0]
    for d in range(-(nb - 1), nb):
        re = jnp.zeros((T, LANES), F32)
        im = jnp.zeros((T, LANES), F32)
        nyq = jnp.zeros((1, LANES), F32)
        if d >= 0:
            F = F_ref[0, d]
            re, im, nyq = re + F[:T], im + F[T:], nyq + F[T:T + 1]
        if d >= 1:
            F = F_ref[0, d - 1]
            p0 = hn_ref[0, (d - 1) * T:(d - 1) * T + 1, :]
            re, im, nyq = re + sgn * (F[:T] - p0), im + sgn * F[T:], nyq + (F[T:T + 1] - p0)
        e = -d
        if e >= 0:
            F = F_ref[1, e]
            re, im, nyq = re + F[:T], im - F[T:], nyq + F[T:T + 1]
        if e >= 1:
            F = F_ref[1, e - 1]
            p0 = hn_ref[1, (e - 1) * T:(e - 1) * T + 1, :]
            re, im, nyq = re + sgn * (F[:T] - p0), im - sgn * F[T:], nyq + (F[T:T + 1] - p0)
        if d == 0:
            re, nyq = re + bias, nyq + bias
        kre_ref[0, d + nb - 1] = re
        kim_ref[0, d + nb - 1] = jnp.where(row0, nyq, im)


def _hyena_filter_spectra(L, w1, b1, w2, b2, w3, freq, bias, fwd):
    T = fwd.shape[1]
    nb = L // T
    nlag = 2 * nb - 1
    P = LANES
    t = np.linspace(0.0, 1.0, L)[:, None]
    n_bands = (HY_EMB - 1) // 2
    f = np.linspace(1e-4, n_bands - 1, n_bands)
    ang = (2.0 * math.pi / L) * np.arange(L)[:, None] * f[None, :]
    zemb = np.concatenate([t, np.cos(ang), -np.sin(ang)], axis=-1)
    zemb = jnp.asarray(np.pad(zemb, ((0, 0), (0, P - HY_EMB))), F32)
    min_decay = math.log(HY_TARGET) / HY_LONG_DECAY_PCT
    max_decay = math.log(HY_TARGET) / HY_SHORT_DECAY_PCT
    deltas = np.abs(np.linspace(min_decay, max_decay, HY_WIDTH))
    dec = jnp.asarray(np.exp(-t * deltas[None, :]), F32)
    pad = P - HY_FFN
    w1p = jnp.pad(w1, ((0, P - HY_EMB), (0, pad)))
    b1p = jnp.pad(b1, (0, pad)).reshape(1, P)
    w2p = jnp.pad(w2, ((0, pad), (0, pad)))
    b2p = jnp.pad(b2, (0, pad)).reshape(1, P)
    w3p = jnp.pad(w3, ((0, pad), (0, 0)))
    frp = jnp.pad(freq, ((0, 0), (0, pad)))
    ncg = HY_WIDTH // P
    const = lambda shape: pl.BlockSpec(shape, lambda o, c: tuple(0 for _ in shape))
    kshape = jax.ShapeDtypeStruct((HY_ORDER, nlag, T, HY_WIDTH), F32)
    kspec = pl.BlockSpec((1, nlag, T, P), lambda o, c: (o, 0, 0, c))
    return pl.pallas_call(
        functools.partial(_filt_kernel, L=L),
        grid=(HY_ORDER, ncg),
        in_specs=[const((L, P)), const((P, P)), const((1, P)), const((P, P)), const((1, P)), const((2, P)),
                  pl.BlockSpec((P, P), lambda o, c: (0, o * 2 * ncg + c)),
                  pl.BlockSpec((P, P), lambda o, c: (0, o * 2 * ncg + ncg + c)),
                  pl.BlockSpec((L, P), lambda o, c: (0, c)),
                  pl.BlockSpec((1, 1, P), lambda o, c: (o, 0, c)),
                  const((2 * T, T))],
        out_specs=[kspec, kspec],
        out_shape=[kshape, kshape],
        scratch_shapes=[pltpu.VMEM((2, L, P), BF16), pltpu.VMEM((2, L, P), F32),
                        pltpu.VMEM((2, nb, 2 * T, P), F32)],
        compiler_params=_params(("arbitrary", "arbitrary")),
        name="hyena_filters",
    )(zemb, w1p, b1p, w2p, b2p, frp, w3p, w3p, dec, bias.reshape(HY_ORDER, 1, HY_WIDTH), fwd)


CONV_ROWS = 128


def _conv_block(x_ref, t0, w, b, left):
    L, C = x_ref.shape
    T = CONV_ROWS
    blk = x_ref[t0:t0 + T, :]
    prev = x_ref[t0 - SUBLANES:t0, :] if t0 > 0 else jnp.zeros((SUBLANES, C), F32)
    nxt = x_ref[t0 + T:t0 + T + SUBLANES, :] if t0 + T < L else jnp.zeros((SUBLANES, C), F32)
    ext = jnp.concatenate([prev, blk, nxt], axis=0)
    n = T + 2 * SUBLANES
    acc = jnp.broadcast_to(b, (T, C))
    for tap in range(w.shape[0]):
        s = tap - left
        sh = blk if s == 0 else pltpu.roll(ext, (-s) % n, axis=0)[SUBLANES:SUBLANES + T]
        acc = acc + w[tap:tap + 1, :] * sh
    return acc


HY_LANES_ROWS = 2 * 2048
HY_MAX_BATCH = 8


def _toeplitz(k, u, mul):
    n = len(u)
    if n == 1:
        return [mul(k[0], u[0])]
    h = n // 2
    add = lambda a, b: tuple(x + y for x, y in zip(a, b))
    sub = lambda a, b: tuple(x - y for x, y in zip(a, b))
    t0, t1, t2 = k[h:h + n - 1], k[0:n - 1], k[n:2 * n - 1]
    p = _toeplitz(t0, [add(a, b) for a, b in zip(u[:h], u[h:])], mul)
    q = _toeplitz([sub(a, b) for a, b in zip(t1, t0)], u[h:], mul)
    r = _toeplitz([sub(a, b) for a, b in zip(t2, t0)], u[:h], mul)
    return [add(a, b) for a, b in zip(p, q)] + [add(a, b) for a, b in zip(p, r)]


def _hy_kernel(v_ref, x1_ref, x2_ref, g_ref, wv_ref, w1_ref, w2_ref, bv_ref, b1_ref, b2_ref,
               kre_ref, kim_ref, fwd_ref, inv_ref, y_ref,
               x1c_ref, x2c_ref, u_ref, U_ref, Y_ref, *, L):
    T = fwd_ref.shape[1]
    P = LANES
    nb = L // T
    NBT = v_ref.shape[0]
    left = (HY_SHORT - 1) // 2
    def block_spectrum(j, r0):
        U_ref[j] = jnp.dot(fwd_ref[...], u_ref[pl.ds(r0, T), :], preferred_element_type=F32)

    for j in range(nb):
        for t0 in range(j * T, (j + 1) * T, CONV_ROWS):
            rows = slice(t0, t0 + CONV_ROWS)
            for p in range(NBT):
                u_ref[rows, p * P:(p + 1) * P] = _conv_block(v_ref.at[p], t0, wv_ref[...], bv_ref[...], left).astype(BF16)
                x1c_ref[p, rows, :] = _conv_block(x1_ref.at[p], t0, w1_ref[...], b1_ref[...], left)
                x2c_ref[p, rows, :] = _conv_block(x2_ref.at[p], t0, w2_ref[...], b2_ref[...], left)
        block_spectrum(j, j * T)

    row0 = lax.broadcasted_iota(jnp.int32, (SUBLANES, NBT * P), 0) == 0
    tile = lambda x: jnp.concatenate([x] * NBT, axis=1)

    def cmul(k, u):
        kr, ki = tile(k[0]), tile(k[1])
        return (kr * u[0] - ki * u[1], kr * u[1] + ki * u[0])

    def rmul(k, u):
        return (tile(k[0]) * u[0], tile(k[1]) * u[1])

    def spectra_products(o):
        def load(r, rows):
            k = [(kre_ref[o, l, pl.ds(r, rows), :], kim_ref[o, l, pl.ds(r, rows), :]) for l in range(2 * nb - 1)]
            u = [(U_ref[j, pl.ds(r, rows), :], U_ref[j, pl.ds(T + r, rows), :]) for j in range(nb)]
            return k, u

        def chunk(rc, carry):
            r = pl.multiple_of(rc * SUBLANES, SUBLANES)
            y = _toeplitz(*load(r, SUBLANES), cmul)
            for i in range(nb):
                Y_ref[i, pl.ds(r, SUBLANES), :] = y[i][0]
                Y_ref[i, pl.ds(T + r, SUBLANES), :] = y[i][1]
            return carry

        lax.fori_loop(0, T // SUBLANES, chunk, 0, unroll=2)
        y = _toeplitz(*load(0, SUBLANES), rmul)
        for i in range(nb):
            Y_ref[i, 0:SUBLANES, :] = jnp.where(row0, y[i][0], Y_ref[i, 0:SUBLANES, :])
            Y_ref[i, T:T + SUBLANES, :] = jnp.where(row0, y[i][1], Y_ref[i, T:T + SUBLANES, :])

    def long_conv_block(i):
        return _dot1(inv_ref[...], Y_ref[i])

    spectra_products(0)

    def order0(i, carry):
        r0 = pl.multiple_of(i * T, T)
        y = long_conv_block(i)
        for p in range(NBT):
            z1 = x1c_ref[p, pl.ds(r0, T), :] * y[:, p * P:(p + 1) * P]
            u_ref[pl.ds(r0, T), p * P:(p + 1) * P] = z1.astype(BF16)
        return carry

    lax.fori_loop(0, nb, order0, 0, unroll=2 if nb % 2 == 0 else 1)
    for j in range(nb):
        block_spectrum(j, j * T)
    spectra_products(1)

    def order1(i, carry):
        r0 = pl.multiple_of(i * T, T)
        y = long_conv_block(i)
        for p in range(NBT):
            out = x2c_ref[p, pl.ds(r0, T), :] * y[:, p * P:(p + 1) * P]
            y_ref[p, pl.ds(r0, T), :] = (out * _silu(g_ref[p, pl.ds(r0, T), :])).astype(BF16)
        return carry

    lax.fori_loop(0, nb, order1, 0, unroll=2 if nb % 2 == 0 else 1)


def _hyena(z, conv_w, conv_b, kre, kim, fwd, inv):
    B, L, _ = z.shape
    NBT = min(B, max(2, HY_LANES_ROWS // L), HY_MAX_BATCH)
    assert B % NBT == 0
    T = fwd.shape[1]
    nb = L // T
    nlag = 2 * nb - 1
    P = LANES
    ncg = HY_WIDTH // P
    col = lambda off: pl.BlockSpec((NBT, L, P), lambda c, b, off=off: (b, 0, off + c))
    wspec = lambda part: pl.BlockSpec((HY_SHORT, P), lambda c, b, part=part: (0, part * ncg + c))
    bspec = lambda part: pl.BlockSpec((1, P), lambda c, b, part=part: (0, part * ncg + c))
    kspec = pl.BlockSpec((HY_ORDER, nlag, T, P), lambda c, b: (0, 0, 0, c))
    fspec = pl.BlockSpec((2 * T, T), lambda c, b: (0, 0))
    ispec = pl.BlockSpec((T, 2 * T), lambda c, b: (0, 0))
    cb = conv_b.reshape(1, 3 * HY_WIDTH)
    return pl.pallas_call(
        functools.partial(_hy_kernel, L=L),
        grid=(ncg, B // NBT),
        in_specs=[col(_HV0), col(_HX10), col(_HX20), col(_GH0),
                  wspec(0), wspec(1), wspec(2), bspec(0), bspec(1), bspec(2),
                  kspec, kspec, fspec, ispec],
        out_specs=pl.BlockSpec((NBT, L, P), lambda c, b: (b, 0, c)),
        out_shape=jax.ShapeDtypeStruct((B, L, HY_WIDTH), BF16),
        scratch_shapes=[pltpu.VMEM((NBT, L, P), F32), pltpu.VMEM((NBT, L, P), F32),
                        pltpu.VMEM((L, NBT * P), BF16),
                        pltpu.VMEM((nb, 2 * T, NBT * P), F32), pltpu.VMEM((nb, 2 * T, NBT * P), F32)],
        compiler_params=_params(("arbitrary", "arbitrary")),
        name="hyena",
    )(z, z, z, z, conv_w, conv_w, conv_w, cb, cb, cb, kre, kim, fwd, inv)


LRU_TILE = 256
LRU_GROUPS = LRU_TILE // LANES
LRU_ROWS = 128


def _lru_kernel(*refs, L, has_state):
    it = iter(refs)
    x_ref, g_ref, cw_ref, cb_ref, wg_ref, gb_ref, lam_ref = (next(it) for _ in range(7))
    h0_ref = next(it) if has_state else None
    y_ref, sN_ref, xp_ref, A_ref, B_ref, H_ref, C_ref = (next(it) for _ in range(7))

    S = L // SUBLANES
    W = LRU_TILE
    G = LRU_GROUPS
    R = LRU_ROWS
    left = LRU_CONV // 2
    pad = left * SUBLANES

    sub = lax.broadcasted_iota(jnp.int32, (SUBLANES, W), 0)
    for k in range(left):
        tail = pltpu.roll(x_ref[0, L - pad + k * SUBLANES:L - pad + (k + 1) * SUBLANES, :], 1, axis=0)
        xp_ref[k * SUBLANES:(k + 1) * SUBLANES, :] = jnp.where(sub == 0, 0.0, tail)
    for k in range(LRU_CONV - 1 - left):
        head = pltpu.roll(x_ref[0, k * SUBLANES:(k + 1) * SUBLANES, :], SUBLANES - 1, axis=0)
        xp_ref[pad + L + k * SUBLANES:pad + L + (k + 1) * SUBLANES, :] = jnp.where(sub == SUBLANES - 1, 0.0, head)

    def copy(c, carry):
        r0 = pl.multiple_of(c * R, R)
        xp_ref[pl.ds(pad + r0, R), :] = x_ref[0, pl.ds(r0, R), :]
        return carry

    lax.fori_loop(0, L // R, copy, 0)

    csp = [(0.5 * LRU_C) * _softplus(-lam_ref[d]) for d in (0, 1)]

    def coeffs(c, carry):
        r0 = pl.multiple_of(c * R, R)
        u = jnp.broadcast_to(cb_ref[...], (R, W))
        for k in range(LRU_CONV):
            u = u + cw_ref[k:k + 1, :] * xp_ref[pl.ds(r0 + k * SUBLANES, R), :]
        half = jnp.dot(u.astype(BF16), wg_ref[0], preferred_element_type=F32) + gb_ref[0]
        hu = 0.5 * u
        for d in (0, 1):
            t_r = jnp.tanh(half[:, (2 * d) * W:(2 * d + 1) * W])
            t_i = jnp.tanh(half[:, (2 * d + 1) * W:(2 * d + 2) * W])
            nl = csp[d] * t_r + csp[d]
            a = jnp.exp(-nl)
            s2 = jnp.tanh(nl) * (a * a + 1.0)
            b = jnp.where(s2 > 0.0, s2 * lax.rsqrt(s2), 0.0) * (t_i * hu + hu)
            for gi in range(G):
                lanes = slice(gi * LANES, (gi + 1) * LANES)
                A_ref[d, gi, pl.ds(r0, R), :] = a[:, lanes]
                B_ref[d, gi, pl.ds(r0, R), :] = b[:, lanes]
        return carry

    lax.fori_loop(0, L // R, coeffs, 0, unroll=2)

    def scan_body(t, carry):
        out = []
        for d in (0, 1):
            r8 = pl.multiple_of((t if d == 0 else S - 1 - t) * SUBLANES, SUBLANES)
            for gi in range(G):
                h, acc = carry[d * G + gi]
                a = A_ref[d, gi, pl.ds(r8, SUBLANES), :]
                h = a * h + B_ref[d, gi, pl.ds(r8, SUBLANES), :]
                acc = acc * a
                H_ref[d, gi, pl.ds(r8, SUBLANES), :] = h
                C_ref[d, gi, pl.ds(r8, SUBLANES), :] = acc
                out.append((h, acc))
        return tuple(out)

    init = tuple((jnp.zeros((SUBLANES, LANES), F32), jnp.ones((SUBLANES, LANES), F32))
                 for _ in range(2 * G))
    lax.fori_loop(0, S, scan_body, init, unroll=4)

    hin = [None] * (2 * G)
    for d in (0, 1):
        last = (S - 1 if d == 0 else 0) * SUBLANES
        for gi in range(G):
            hl = H_ref[d, gi, last:last + SUBLANES, :]
            ac = C_ref[d, gi, last:last + SUBLANES, :]
            if has_state:
                h = h0_ref[0, d:d + 1, gi * LANES:(gi + 1) * LANES]
            else:
                h = jnp.zeros((1, LANES), F32)
            rows = [None] * SUBLANES
            for j in (range(SUBLANES) if d == 0 else range(SUBLANES - 1, -1, -1)):
                rows[j] = h
                h = hl[j:j + 1, :] + ac[j:j + 1, :] * h
            hin[d * G + gi] = jnp.concatenate([jnp.concatenate(rows, axis=0)] * (R // SUBLANES), axis=0)
            sN_ref[0, d:d + 1, gi * LANES:(gi + 1) * LANES] = h

    def combine(c, carry):
        r0 = pl.multiple_of(c * R, R)
        cols = []
        for gi in range(G):
            tot = jnp.zeros((R, LANES), F32)
            for d in (0, 1):
                tot = tot + (H_ref[d, gi, pl.ds(r0, R), :] + C_ref[d, gi, pl.ds(r0, R), :] * hin[d * G + gi])
            cols.append(tot)
        y_ref[0, pl.ds(r0, R), :] = jnp.concatenate(cols, axis=1) * _silu(g_ref[0, pl.ds(r0, R), :])
        return carry

    lax.fori_loop(0, L // R, combine, 0)


def _rglru(z4, conv_w, conv_b, gate_w, gate_b, lam, h0):
    B, S, _, _ = z4.shape
    L = S * SUBLANES
    has_state = h0 is not None
    W = LRU_TILE
    nh = LRU_WIDTH // W
    bpt = W // LRU_BLOCK_DIM
    gw = gate_w.reshape(2, 2, nh, bpt, LRU_BLOCK_DIM, LRU_BLOCK_DIM)
    eye = jnp.eye(bpt, dtype=F32)
    dense = (0.5 * jnp.einsum('dkhnij,nm->hnidkmj', gw, eye)).reshape(nh, W, 4 * W).astype(BF16)
    gb = 0.5 * gate_b.reshape(2, 2, nh, W).transpose(2, 0, 1, 3).reshape(nh, 1, 4 * W)
    zr = z4.reshape(B, L, D_LRU)
    off = lambda base: pl.BlockSpec((1, L, W), lambda b, h, base=base: (b, 0, base + h))
    in_specs = [off(0), off(LRU_WIDTH // W),
                pl.BlockSpec((LRU_CONV, W), lambda b, h: (0, h)),
                pl.BlockSpec((1, W), lambda b, h: (0, h)),
                pl.BlockSpec((1, W, 4 * W), lambda b, h: (h, 0, 0)),
                pl.BlockSpec((1, 1, 4 * W), lambda b, h: (h, 0, 0)),
                pl.BlockSpec((2, 1, W), lambda b, h: (0, 0, h))]
    args = [zr, zr, conv_w, conv_b.reshape(1, LRU_WIDTH), dense, gb, lam.reshape(2, 1, LRU_WIDTH)]
    if has_state:
        in_specs.append(pl.BlockSpec((1, 2, W), lambda b, h: (b, 0, h)))
        args.append(h0)
    y, s = pl.pallas_call(
        functools.partial(_lru_kernel, L=L, has_state=has_state),
        grid=(B, nh),
        in_specs=in_specs,
        out_specs=[pl.BlockSpec((1, L, W), lambda b, h: (b, 0, h)),
                   pl.BlockSpec((1, 2, W), lambda b, h: (b, 0, h))],
        out_shape=[jax.ShapeDtypeStruct((B, L, LRU_WIDTH), F32),
                   jax.ShapeDtypeStruct((B, 2, LRU_WIDTH), F32)],
        scratch_shapes=[pltpu.VMEM((L + (LRU_CONV - 1) * SUBLANES, W), F32)]
        + [pltpu.VMEM((2, LRU_GROUPS, L, LANES), F32) for _ in range(4)],
        compiler_params=_params(("arbitrary", "arbitrary")),
        name="rglru",
    )(*args)
    return y.reshape(B, S, SUBLANES, LRU_WIDTH), s


def _rope_tables(L):
    rows = L // GRID_W
    row = np.repeat(np.arange(rows, dtype=np.float64), GRID_W)
    col = np.tile(np.arange(GRID_W, dtype=np.float64), rows)
    n_f = RET_HEAD_DIM // 4
    inv = ROPE_BASE ** (-np.arange(n_f, dtype=np.float64) / n_f)
    ang = np.concatenate([row[:, None] * inv[None], col[:, None] * inv[None]], axis=-1)
    cos, sin = np.cos(ang), np.sin(ang)
    return (jnp.asarray(np.concatenate([cos, cos], axis=-1), F32),
            jnp.asarray(np.concatenate([-sin, sin], axis=-1), F32))


def kernel(x_prompt, x_sample, state_ret, state_lru, c, c_ctx, norm_g, ada_w, ada_b, w_in, ret_decay_logit, hy_conv_w, hy_conv_b, hy_ffn_w1, hy_ffn_b1, hy_ffn_w2, hy_ffn_b2, hy_ffn_w3, hy_freq, hy_bias, lru_conv_w, lru_conv_b, lru_gate_w, lru_gate_b, lru_lambda, w_out, final_g):
    Bp, Lp, _ = x_prompt.shape
    Bs, Ls, _ = x_sample.shape
    assert Bs + 1 <= SUBLANES

    cvec = jnp.zeros((SUBLANES, D_MODEL), F32).at[:Bs].set(c).at[Bs].set(c_ctx)
    mod = _modulation(cvec, ada_w, ada_b)
    rope = _rope_tables(Ls)
    dft = {L: _dft_matrices(min(DFT_MAX, L)) for L in {Lp, Ls}}
    w_in_b = w_in.astype(BF16)
    w_out_b = w_out.astype(BF16)
    fg = final_g.reshape(1, D_MODEL)

    xc, xl = x_prompt, x_sample
    new_state_ret, new_lru = None, []
    for l in range(DEPTH):
        shift, scale, gate = (mod[l, :, i * D_MODEL:(i + 1) * D_MODEL] for i in range(3))
        g = norm_g[l].reshape(1, D_MODEL)
        final = l == DEPTH - 1
        filt = {}
        for L in sorted({Lp, Ls}):
            filt[L] = _hyena_filter_spectra(L, hy_ffn_w1[l], hy_ffn_b1[l], hy_ffn_w2[l], hy_ffn_b2[l],
                                            hy_ffn_w3[l], hy_freq[l], hy_bias[l], dft[L][0])

        def layer(x, sel, rope_t, ret_s0, lru_s0, ret_states):
            sh, sc, gt = (m[sel][:, None, :] for m in (shift, scale, gate))
            z, z_lru = _in_proj(x, sh, sc, g, w_in_b, l)
            y_ret, *s_ret = _retention(z, ret_decay_logit[l], rope_t, ret_s0, l, ret_states)
            kre, kim = filt[x.shape[1]]
            y_hy = _hyena(z, hy_conv_w[l], hy_conv_b[l], kre, kim, *dft[x.shape[1]])
            y_lru, s_lru = _rglru(z_lru, lru_conv_w[l], lru_conv_b[l], lru_gate_w[l], lru_gate_b[l],
                                  lru_lambda[l], lru_s0)
            return _out_proj(y_ret, y_hy, y_lru, x, gt, w_out_b, l, fg, final), s_ret, s_lru

        xc, (new_state_ret,), ls = layer(xc, slice(Bs, Bs + 1), None, None, None,
                                         "new" if l == 0 else new_state_ret)
        new_lru.append(ls)
        xl, _, _ = layer(xl, slice(0, Bs), rope, state_ret, state_lru[:, l], None)

    new_state_lru = jnp.stack(new_lru, axis=1).astype(x_prompt.dtype)
    return (xc, xl, new_state_ret.astype(x_prompt.dtype), new_state_lru)
```

```python
import functools
import math

import numpy as np
import jax
import jax.numpy as jnp
from jax import lax
from jax.experimental import pallas as pl
from jax.experimental.pallas import tpu as pltpu

F32 = jnp.float32
BF16 = jnp.bfloat16
HI = lax.Precision.HIGHEST

D_MODEL = 1024
DEPTH = 2
GRID_W = 64
EPS = 1e-6
RET_HEADS = 4
RET_HEAD_DIM = 128
RET_WIDTH = RET_HEADS * RET_HEAD_DIM
ROPE_BASE = 10000.0
HY_WIDTH = 512
HY_ORDER = 2
HY_SHORT = 3
HY_EMB = 33
HY_FFN = 64
HY_SHORT_DECAY_PCT = 0.3
HY_LONG_DECAY_PCT = 1.5
HY_TARGET = 1e-2
LRU_WIDTH = 512
LRU_BLOCKS = 8
LRU_BLOCK_DIM = LRU_WIDTH // LRU_BLOCKS
LRU_CONV = 4
LRU_C = 8.0
D_MIX = RET_WIDTH + HY_WIDTH + LRU_WIDTH
D_IN = 4 * RET_WIDTH + 4 * HY_WIDTH + 2 * LRU_WIDTH

LANES = 128
SUBLANES = 8
DFT_MAX = 512
ROW_TILE = 256
VMEM_LIMIT = 56 * 1024 * 1024

_Q0, _K0, _V0, _GR0 = 0, 4, 8, 12
_HV0, _HX10, _HX20, _GH0 = 16, 20, 24, 28


def _sigmoid(x):
    return 0.5 * jnp.tanh(0.5 * x) + 0.5


def _silu(x):
    return x * _sigmoid(x)


def _softplus(x):
    return jnp.maximum(x, 0.0) + jnp.log1p(jnp.exp(-jnp.abs(x)))


def _params(sem):
    return pltpu.CompilerParams(dimension_semantics=sem, vmem_limit_bytes=VMEM_LIMIT)


def _mod_kernel(c_ref, w_ref, b_ref, o_ref):
    s_hi, s_lo = _split_bf16(_silu(c_ref[...]))
    w_hi, w_lo = _split_bf16(w_ref[0])
    o_ref[0] = (jnp.dot(s_hi, w_hi, preferred_element_type=F32)
                + jnp.dot(s_lo, w_hi, preferred_element_type=F32)
                + jnp.dot(s_hi, w_lo, preferred_element_type=F32)) + b_ref[0]


def _modulation(cvec, ada_w, ada_b):
    tn = 768
    return pl.pallas_call(
        _mod_kernel,
        grid=(DEPTH, 3 * D_MODEL // tn),
        in_specs=[pl.BlockSpec((SUBLANES, D_MODEL), lambda l, j: (0, 0)),
                  pl.BlockSpec((1, D_MODEL, tn), lambda l, j: (l, 0, j)),
                  pl.BlockSpec((1, 1, tn), lambda l, j: (l, 0, j))],
        out_specs=pl.BlockSpec((1, SUBLANES, tn), lambda l, j: (l, 0, j)),
        out_shape=jax.ShapeDtypeStruct((DEPTH, SUBLANES, 3 * D_MODEL), F32),
        compiler_params=_params(("arbitrary", "arbitrary")),
        name="modulation",
    )(cvec, ada_w, ada_b.reshape(DEPTH, 1, 3 * D_MODEL))


D_MAIN = 4 * RET_WIDTH + 4 * HY_WIDTH
D_LRU = 2 * LRU_WIDTH


def _in_kernel(x_ref, sh_ref, sc_ref, g_ref, w_ref, o_ref, ol_ref):
    _, nseg, rows, _ = x_ref.shape
    x = x_ref[0].reshape(nseg * rows, D_MODEL)
    ms = jnp.mean(x * x, axis=-1, keepdims=True)
    y = x * lax.rsqrt(ms + EPS) * g_ref[...]
    h = (y * (1.0 + sc_ref[0]) + sh_ref[0]).astype(BF16)
    tn = 1024
    for n in range(D_MAIN // tn):
        z = jnp.dot(h, w_ref[:, n * tn:(n + 1) * tn], preferred_element_type=F32)
        o_ref[0, :, :, n * tn:(n + 1) * tn] = z.reshape(nseg, rows, tn)
    zl = jnp.dot(h, w_ref[:, D_MAIN:D_IN], preferred_element_type=F32)
    for j in range(nseg):
        ol_ref[0, :, j, :] = zl[j * rows:(j + 1) * rows]


def _in_proj(x, shift, scale, g, w_bf16, layer):
    B, L, _ = x.shape
    S = L // SUBLANES
    rows = min(ROW_TILE // SUBLANES, S)
    per_batch = shift.shape[0] > 1
    mod_map = (lambda b, i: (b, 0, 0)) if per_batch else (lambda b, i: (0, 0, 0))
    seg = lambda C: pl.BlockSpec((1, SUBLANES, rows, C), lambda b, i: (b, 0, i, 0))
    z, z_lru = pl.pallas_call(
        _in_kernel,
        grid=(B, S // rows),
        in_specs=[seg(D_MODEL),
                  pl.BlockSpec((1, 1, D_MODEL), mod_map),
                  pl.BlockSpec((1, 1, D_MODEL), mod_map),
                  pl.BlockSpec((1, D_MODEL), lambda b, i: (0, 0)),
                  pl.BlockSpec((None, D_MODEL, D_IN), lambda b, i: (layer, 0, 0),
                               pipeline_mode=pl.Buffered(1))],
        out_specs=[seg(D_MAIN), pl.BlockSpec((1, rows, SUBLANES, D_LRU), lambda b, i: (b, i, 0, 0))],
        out_shape=[jax.ShapeDtypeStruct((B, SUBLANES, S, D_MAIN), F32),
                   jax.ShapeDtypeStruct((B, S, SUBLANES, D_LRU), F32)],
        compiler_params=_params(("arbitrary", "arbitrary")),
        name="in_proj",
    )(x.reshape(B, SUBLANES, S, D_MODEL), shift, scale, g, w_bf16)
    return z.reshape(B, L, D_MAIN), z_lru


OUT_ROWS = 1024


def _out_kernel(yr_ref, yh_ref, yl_ref, x_ref, gate_ref, w_ref, fg_ref, o_ref, *, final):
    nb, nseg, rows, _ = x_ref.shape
    n = nb * nseg * rows
    acc = jnp.dot(yr_ref[...].reshape(n, RET_WIDTH), w_ref[0:RET_WIDTH], preferred_element_type=F32)
    acc = acc + jnp.dot(yh_ref[...].reshape(n, HY_WIDTH), w_ref[RET_WIDTH:RET_WIDTH + HY_WIDTH],
                        preferred_element_type=F32)
    yl = jnp.concatenate([yl_ref[bi, :, j, :] for bi in range(nb) for j in range(nseg)], axis=0)
    acc = acc + jnp.dot(yl.astype(BF16), w_ref[RET_WIDTH + HY_WIDTH:D_MIX], preferred_element_type=F32)
    x = x_ref[...].reshape(n, D_MODEL) + gate_ref[0] * acc
    if final:
        ms = jnp.mean(x * x, axis=-1, keepdims=True)
        x = x * lax.rsqrt(ms + EPS) * fg_ref[...]
    o_ref[...] = x.reshape(nb, nseg, rows, D_MODEL)


def _out_proj(y_ret, y_hy, y_lru, x, gate, w_bf16, layer, final_g, final):
    B, L, _ = x.shape
    S = L // SUBLANES
    rows = min(OUT_ROWS // SUBLANES, S)
    per_batch = gate.shape[0] > 1
    mod_map = (lambda b, i: (b, 0, 0)) if per_batch else (lambda b, i: (0, 0, 0))
    NB = 1 if per_batch else min(B, max(1, OUT_ROWS // (SUBLANES * rows)))
    assert B % NB == 0
    seg = lambda C: pl.BlockSpec((NB, SUBLANES, rows, C), lambda b, i: (b, 0, i, 0))
    view = lambda a: a.reshape(B, SUBLANES, S, a.shape[-1])
    out = pl.pallas_call(
        functools.partial(_out_kernel, final=final),
        grid=(B // NB, S // rows),
        in_specs=[seg(RET_WIDTH), seg(HY_WIDTH),
                  pl.BlockSpec((NB, rows, SUBLANES, LRU_WIDTH), lambda b, i: (b, i, 0, 0)),
                  seg(D_MODEL),
                  pl.BlockSpec((1, 1, D_MODEL), mod_map),
                  pl.BlockSpec((None, D_MIX, D_MODEL), lambda b, i: (layer, 0, 0),
                               pipeline_mode=pl.Buffered(1)),
                  pl.BlockSpec((1, D_MODEL), lambda b, i: (0, 0))],
        out_specs=seg(D_MODEL),
        out_shape=jax.ShapeDtypeStruct((B, SUBLANES, S, D_MODEL), F32),
        compiler_params=_params(("arbitrary", "arbitrary")),
        name="out_proj",
    )(view(y_ret), view(y_hy), y_lru, view(x), gate, w_bf16, final_g)
    return out.reshape(B, L, D_MODEL)


RET_HEAD_ROWS = 2 * 2048
RET_BLOCK = 256


def _ret_kernel(*refs, L, use_rope, has_state, emit_state, chained, layer):
    it = iter(refs)
    q_ref, k_ref, v_ref, g_ref = next(it), next(it), next(it), next(it)
    cos_ref = sin_ref = s0_ref = sN_ref = None
    if use_rope:
        cos_ref, sin_ref = next(it), next(it)
    lg_ref = next(it)
    if has_state:
        s0_ref = next(it)
    if chained:
        next(it)
    y_ref = next(it)
    if emit_state:
        sN_ref = next(it)
    R_ref, M_ref, Z_ref, D_ref, KV_ref, kb_ref = (next(it) for _ in range(6))

    C = min(RET_BLOCK, L)
    DH = RET_HEAD_DIM
    HP = q_ref.shape[-1] // DH
    n = L // C
    cross = has_state or n > 1
    kscale = DH ** -0.5

    @pl.when(pl.program_id(1) == 0)
    def _():
        ii = lax.broadcasted_iota(jnp.int32, (C, C), 0)
        jj = lax.broadcasted_iota(jnp.int32, (C, C), 1)
        diff = (ii - jj).astype(F32)
        ri = lax.broadcasted_iota(jnp.int32, (C, DH), 0).astype(F32)
        for hh in range(HP):
            lgf = -_softplus(-lg_ref[0, hh])
            lgb = -_softplus(-lg_ref[1, hh])
            M_ref[hh] = (jnp.where(diff >= 0, jnp.exp(lgf[:, :C] * jnp.maximum(diff, 0.0)), 0.0)
                         + jnp.where(diff <= 0, jnp.exp(lgb[:, :C] * jnp.maximum(-diff, 0.0)), 0.0))
            lf, lb = lgf[:, :DH], lgb[:, :DH]
            Z_ref[hh, 0] = jnp.concatenate([jnp.exp(lf * (C - 1.0 - ri)), jnp.exp(lb * ri)], axis=1)
            Z_ref[hh, 1] = jnp.concatenate([jnp.exp(lf * (ri + 1.0)), jnp.exp(lb * (C - ri))], axis=1)
            D_ref[hh] = jnp.exp(jnp.concatenate([jnp.broadcast_to(lf * C, (DH, DH)),
                                                 jnp.broadcast_to(lb * C, (DH, DH))], axis=0))

    def rotary(x, r0):
        if not use_rope:
            return x
        return x * cos_ref[pl.ds(r0, C), :] + pltpu.roll(x, 64, axis=1) * sin_ref[pl.ds(r0, C), :]

    def chunk_kv(c, carry):
        r0 = pl.multiple_of(c * C, C)
        for hh in range(HP):
            lanes = slice(hh * DH, (hh + 1) * DH)
            k = rotary(k_ref[0, pl.ds(r0, C), lanes] * kscale, r0)
            kb_ref[hh, pl.ds(r0, C), :] = k.astype(BF16)
            kz = (jnp.concatenate([k, k], axis=1) * Z_ref[hh, 0]).astype(BF16)
            vb = v_ref[0, pl.ds(r0, C), lanes].astype(BF16)
            KV_ref[hh, c] = lax.dot_general(kz, vb, (((0,), (0,)), ((), ())), preferred_element_type=F32)
        return carry

    lax.fori_loop(0, n, chunk_kv, 0, unroll=2 if n % 2 == 0 else 1)

    for d in (0, 1):
        for hh in range(HP):
            R_ref[d, hh] = s0_ref[0, d, hh] if has_state else jnp.zeros((DH, DH), F32)

    def states(t, carry):
        for d in (0, 1):
            c = t if d == 0 else n - 1 - t
            rows = slice(d * DH, (d + 1) * DH)
            for hh in range(HP):
                R = R_ref[d, hh]
                R_ref[d, hh] = D_ref[hh, rows, :] * R + KV_ref[hh, c, rows, :]
                KV_ref[hh, c, rows, :] = R
        return carry

    lax.fori_loop(0, n, states, 0)
    if emit_state:
        for l in range(sN_ref.shape[1]):
            if chained or l == layer:
                for d in (0, 1):
                    for hh in range(HP):
                        sN_ref[0, l, d, hh] = R_ref[d, hh]
            else:
                sN_ref[0, l] = jnp.zeros(sN_ref.shape[2:], F32)

    def chunk_out(c, carry):
        r0 = pl.multiple_of(c * C, C)
        cols = []
        for hh in range(HP):
            lanes = slice(hh * DH, (hh + 1) * DH)
            q = rotary(q_ref[0, pl.ds(r0, C), lanes], r0)
            vb = v_ref[0, pl.ds(r0, C), lanes].astype(BF16)
            s = lax.dot_general(q.astype(BF16), kb_ref[hh, pl.ds(r0, C), :], (((1,), (1,)), ((), ())),
                                preferred_element_type=F32) * M_ref[hh]
            o = jnp.dot(s.astype(BF16), vb, preferred_element_type=F32)
            if cross:
                qx = (jnp.concatenate([q, q], axis=1) * Z_ref[hh, 1]).astype(BF16)
                o = o + jnp.dot(qx, KV_ref[hh, c].astype(BF16), preferred_element_type=F32)
            cols.append(o * lax.rsqrt(jnp.mean(o * o, axis=-1, keepdims=True) + EPS))
        g = g_ref[0, pl.ds(r0, C), :]
        y_ref[0, pl.ds(r0, C), :] = (jnp.concatenate(cols, axis=1) * _silu(g)).astype(BF16)
        return carry

    lax.fori_loop(0, n, chunk_out, 0, unroll=2 if n % 2 == 0 else 1)


def _retention(z, decay_logit, rope, state0, layer, states):
    B, L, _ = z.shape
    use_rope = rope is not None
    has_state = state0 is not None
    emit_state = states is not None
    chained = emit_state and not isinstance(states, str)
    H = RET_HEADS
    HP = min(H, max(1, RET_HEAD_ROWS // L))
    W = HP * RET_HEAD_DIM
    col = lambda off: pl.BlockSpec((1, L, W), lambda h, b, off=off: (b, 0, off * LANES // W + h))
    in_specs = [col(_Q0), col(_K0), col(_V0), col(_GR0)]
    args = [z, z, z, z]
    if use_rope:
        in_specs += [pl.BlockSpec((L, LANES), lambda h, b: (0, 0))] * 2
        args += list(rope)
    C = min(RET_BLOCK, L)
    in_specs.append(pl.BlockSpec((2, HP, 1, RET_BLOCK), lambda h, b: (0, h, 0, 0)))
    args.append(jnp.broadcast_to(decay_logit[:, :, None, None], (2, H, 1, RET_BLOCK)))
    if has_state:
        in_specs.append(pl.BlockSpec((1, None, 2, HP, RET_HEAD_DIM, RET_HEAD_DIM),
                                     lambda h, b: (b, layer, 0, h, 0, 0)))
        args.append(state0)
    out_specs = [pl.BlockSpec((1, L, W), lambda h, b: (b, 0, h))]
    out_shape = [jax.ShapeDtypeStruct((B, L, RET_WIDTH), BF16)]
    aliases = {}
    if emit_state:
        nl, l0 = (1, layer) if chained else (DEPTH, 0)
        out_specs.append(pl.BlockSpec((1, nl, 2, HP, RET_HEAD_DIM, RET_HEAD_DIM),
                                      lambda h, b: (b, l0, 0, h, 0, 0)))
        out_shape.append(jax.ShapeDtypeStruct((B, DEPTH, 2, H, RET_HEAD_DIM, RET_HEAD_DIM), F32))
        if chained:
            aliases = {len(args): 1}
            in_specs.append(pl.BlockSpec(memory_space=pl.ANY))
            args.append(states)
    return pl.pallas_call(
        functools.partial(_ret_kernel, L=L, use_rope=use_rope, has_state=has_state,
                          emit_state=emit_state, chained=chained, layer=layer),
        grid=(H // HP, B),
        in_specs=in_specs,
        out_specs=out_specs,
        out_shape=out_shape,
        input_output_aliases=aliases,
        scratch_shapes=[pltpu.VMEM((2, HP, RET_HEAD_DIM, RET_HEAD_DIM), F32),
                        pltpu.VMEM((HP, C, C), F32),
                        pltpu.VMEM((HP, 2, C, 2 * RET_HEAD_DIM), F32),
                        pltpu.VMEM((HP, 2 * RET_HEAD_DIM, RET_HEAD_DIM), F32),
                        pltpu.VMEM((HP, L // C, 2 * RET_HEAD_DIM, RET_HEAD_DIM), F32),
                        pltpu.VMEM((HP, L, RET_HEAD_DIM), BF16)],
        compiler_params=_params(("arbitrary", "arbitrary")),
        name="retention",
    )(*args)


def _dft_matrices(T):
    N = 2 * T
    k = np.arange(T, dtype=np.float64)[:, None]
    m = np.arange(T, dtype=np.float64)[None, :]
    ang = 2.0 * np.pi * k * m / N
    fwd = np.concatenate([np.cos(ang), -np.sin(ang)], axis=0)
    fwd[T, :] = (-1.0) ** np.arange(T)
    ck = np.full((T,), 2.0)
    ck[0] = 1.0
    inv_re = (np.cos(ang) * ck[:, None]).T / N
    inv_im = (-2.0 * np.sin(ang)).T / N
    inv_im[:, 0] = ((-1.0) ** np.arange(T)) / N
    inv = np.concatenate([inv_re, inv_im], axis=1)
    return jnp.asarray(fwd, F32).astype(BF16), jnp.asarray(inv, F32).astype(BF16)


def _split_bf16(x):
    hi = x.astype(BF16)
    return hi, (x - hi.astype(F32)).astype(BF16)


def _dot1(a, b):
    return jnp.dot(a, b.astype(BF16), preferred_element_type=F32)


def _filt_kernel(zemb_ref, w1_ref, b1_ref, w2_ref, b2_ref, fr_ref, w3f_ref, w3b_ref, dec_ref,
                 bias_ref, fwd_ref, kre_ref, kim_ref, hid_ref, hn_ref, F_ref, *, L):
    T = fwd_ref.shape[1]
    nb = L // T

    @pl.when((pl.program_id(0) == 0) & (pl.program_id(1) == 0))
    def _():
        z1 = jnp.dot(zemb_ref[...], w1_ref[...], precision=HI, preferred_element_type=F32) + b1_ref[...]
        h1 = jnp.sin(fr_ref[0:1, :] * z1)
        z2 = jnp.dot(h1, w2_ref[...], precision=HI, preferred_element_type=F32) + b2_ref[...]
        hid = jnp.sin(fr_ref[1:2, :] * z2)
        hid_ref[0], hid_ref[1] = _split_bf16(hid)

    w3 = jnp.concatenate([w3f_ref[...], w3b_ref[...]], axis=1)
    w_hi, w_lo = _split_bf16(w3)
    h = (jnp.dot(hid_ref[0], w_hi, preferred_element_type=F32)
         + jnp.dot(hid_ref[1], w_hi, preferred_element_type=F32)
         + jnp.dot(hid_ref[0], w_lo, preferred_element_type=F32))
    h = h * jnp.concatenate([dec_ref[...]] * 2, axis=1)
    h = h / (jnp.sum(jnp.abs(h), axis=0, keepdims=True) + EPS)
    for dr in (0, 1):
        hn_ref[dr] = h[:, dr * LANES:(dr + 1) * LANES]
    for blk in range(nb):
        F = _dot1(fwd_ref[...], h[blk * T:(blk + 1) * T])
        for dr in (0, 1):
            F_ref[dr, blk] = F[:, dr * LANES:(dr + 1) * LANES]

    row = lax.broadcasted_iota(jnp.int32, (T, LANES), 0)
    sgn = jnp.where((row & 1) == 0, 1.0, -1.0).astype(F32)
    row0 = row == 0
    bias = bias_ref[0]
    for d in range(-(nb - 1), nb):
        re = jnp.zeros((T, LANES), F32)
        im = jnp.zeros((T, LANES), F32)
        nyq = jnp.zeros((1, LANES), F32)
        if d >= 0:
            F = F_ref[0, d]
            re, im, nyq = re + F[:T], im + F[T:], nyq + F[T:T + 1]
        if d >= 1:
            F = F_ref[0, d - 1]
            p0 = hn_ref[0, (d - 1) * T:(d - 1) * T + 1, :]
            re, im, nyq = re + sgn * (F[:T] - p0), im + sgn * F[T:], nyq + (F[T:T + 1] - p0)
        e = -d
        if e >= 0:
            F = F_ref[1, e]
            re, im, nyq = re + F[:T], im - F[T:], nyq + F[T:T + 1]
        if e >= 1:
            F = F_ref[1, e - 1]
            p0 = hn_ref[1, (e - 1) * T:(e - 1) * T + 1, :]
            re, im, nyq = re + sgn * (F[:T] - p0), im - sgn * F[T:], nyq + (F[T:T + 1] - p0)
        if d == 0:
            re, nyq = re + bias, nyq + bias
        kre_ref[0, d + nb - 1] = re
        kim_ref[0, d + nb - 1] = jnp.where(row0, nyq, im)


def _hyena_filter_spectra(L, w1, b1, w2, b2, w3, freq, bias, fwd):
    T = fwd.shape[1]
    nb = L // T
    nlag = 2 * nb - 1
    P = LANES
    t = np.linspace(0.0, 1.0, L)[:, None]
    n_bands = (HY_EMB - 1) // 2
    f = np.linspace(1e-4, n_bands - 1, n_bands)
    ang = (2.0 * math.pi / L) * np.arange(L)[:, None] * f[None, :]
    zemb = np.concatenate([t, np.cos(ang), -np.sin(ang)], axis=-1)
    zemb = jnp.asarray(np.pad(zemb, ((0, 0), (0, P - HY_EMB))), F32)
    min_decay = math.log(HY_TARGET) / HY_LONG_DECAY_PCT
    max_decay = math.log(HY_TARGET) / HY_SHORT_DECAY_PCT
    deltas = np.abs(np.linspace(min_decay, max_decay, HY_WIDTH))
    dec = jnp.asarray(np.exp(-t * deltas[None, :]), F32)
    pad = P - HY_FFN
    w1p = jnp.pad(w1, ((0, P - HY_EMB), (0, pad)))
    b1p = jnp.pad(b1, (0, pad)).reshape(1, P)
    w2p = jnp.pad(w2, ((0, pad), (0, pad)))
    b2p = jnp.pad(b2, (0, pad)).reshape(1, P)
    w3p = jnp.pad(w3, ((0, pad), (0, 0)))
    frp = jnp.pad(freq, ((0, 0), (0, pad)))
    ncg = HY_WIDTH // P
    const = lambda shape: pl.BlockSpec(shape, lambda o, c: tuple(0 for _ in shape))
    kshape = jax.ShapeDtypeStruct((HY_ORDER, nlag, T, HY_WIDTH), F32)
    kspec = pl.BlockSpec((1, nlag, T, P), lambda o, c: (o, 0, 0, c))
    return pl.pallas_call(
        functools.partial(_filt_kernel, L=L),
        grid=(HY_ORDER, ncg),
        in_specs=[const((L, P)), const((P, P)), const((1, P)), const((P, P)), const((1, P)), const((2, P)),
                  pl.BlockSpec((P, P), lambda o, c: (0, o * 2 * ncg + c)),
                  pl.BlockSpec((P, P), lambda o, c: (0, o * 2 * ncg + ncg + c)),
                  pl.BlockSpec((L, P), lambda o, c: (0, c)),
                  pl.BlockSpec((1, 1, P), lambda o, c: (o, 0, c)),
                  const((2 * T, T))],
        out_specs=[kspec, kspec],
        out_shape=[kshape, kshape],
        scratch_shapes=[pltpu.VMEM((2, L, P), BF16), pltpu.VMEM((2, L, P), F32),
                        pltpu.VMEM((2, nb, 2 * T, P), F32)],
        compiler_params=_params(("arbitrary", "arbitrary")),
        name="hyena_filters",
    )(zemb, w1p, b1p, w2p, b2p, frp, w3p, w3p, dec, bias.reshape(HY_ORDER, 1, HY_WIDTH), fwd)


CONV_ROWS = 128


def _conv_block(x_ref, t0, w, b, left):
    L, C = x_ref.shape
    T = CONV_ROWS
    blk = x_ref[t0:t0 + T, :]
    prev = x_ref[t0 - SUBLANES:t0, :] if t0 > 0 else jnp.zeros((SUBLANES, C), F32)
    nxt = x_ref[t0 + T:t0 + T + SUBLANES, :] if t0 + T < L else jnp.zeros((SUBLANES, C), F32)
    ext = jnp.concatenate([prev, blk, nxt], axis=0)
    n = T + 2 * SUBLANES
    acc = jnp.broadcast_to(b, (T, C))
    for tap in range(w.shape[0]):
        s = tap - left
        sh = blk if s == 0 else pltpu.roll(ext, (-s) % n, axis=0)[SUBLANES:SUBLANES + T]
        acc = acc + w[tap:tap + 1, :] * sh
    return acc


HY_LANES_ROWS = 2 * 2048
HY_MAX_BATCH = 8


def _toeplitz(k, u, mul):
    n = len(u)
    if n == 1:
        return [mul(k[0], u[0])]
    h = n // 2
    add = lambda a, b: tuple(x + y for x, y in zip(a, b))
    sub = lambda a, b: tuple(x - y for x, y in zip(a, b))
    t0, t1, t2 = k[h:h + n - 1], k[0:n - 1], k[n:2 * n - 1]
    p = _toeplitz(t0, [add(a, b) for a, b in zip(u[:h], u[h:])], mul)
    q = _toeplitz([sub(a, b) for a, b in zip(t1, t0)], u[h:], mul)
    r = _toeplitz([sub(a, b) for a, b in zip(t2, t0)], u[:h], mul)
    return [add(a, b) for a, b in zip(p, q)] + [add(a, b) for a, b in zip(p, r)]


def _hy_kernel(v_ref, x1_ref, x2_ref, g_ref, wv_ref, w1_ref, w2_ref, bv_ref, b1_ref, b2_ref,
               kre_ref, kim_ref, fwd_ref, inv_ref, y_ref,
               x1c_ref, x2c_ref, u_ref, U_ref, Y_ref, *, L):
    T = fwd_ref.shape[1]
    P = LANES
    nb = L // T
    NBT = v_ref.shape[0]
    left = (HY_SHORT - 1) // 2
    def block_spectrum(j, r0):
        U_ref[j] = jnp.dot(fwd_ref[...], u_ref[pl.ds(r0, T), :], preferred_element_type=F32)

    for j in range(nb):
        for t0 in range(j * T, (j + 1) * T, CONV_ROWS):
            rows = slice(t0, t0 + CONV_ROWS)
            for p in range(NBT):
                u_ref[rows, p * P:(p + 1) * P] = _conv_block(v_ref.at[p], t0, wv_ref[...], bv_ref[...], left).astype(BF16)
                x1c_ref[p, rows, :] = _conv_block(x1_ref.at[p], t0, w1_ref[...], b1_ref[...], left)
                x2c_ref[p, rows, :] = _conv_block(x2_ref.at[p], t0, w2_ref[...], b2_ref[...], left)
        block_spectrum(j, j * T)

    row0 = lax.broadcasted_iota(jnp.int32, (SUBLANES, NBT * P), 0) == 0
    tile = lambda x: jnp.concatenate([x] * NBT, axis=1)

    def cmul(k, u):
        kr, ki = tile(k[0]), tile(k[1])
        return (kr * u[0] - ki * u[1], kr * u[1] + ki * u[0])

    def rmul(k, u):
        return (tile(k[0]) * u[0], tile(k[1]) * u[1])

    def spectra_products(o):
        def load(r, rows):
            k = [(kre_ref[o, l, pl.ds(r, rows), :], kim_ref[o, l, pl.ds(r, rows), :]) for l in range(2 * nb - 1)]
            u = [(U_ref[j, pl.ds(r, rows), :], U_ref[j, pl.ds(T + r, rows), :]) for j in range(nb)]
            return k, u

        def chunk(rc, carry):
            r = pl.multiple_of(rc * SUBLANES, SUBLANES)
            y = _toeplitz(*load(r, SUBLANES), cmul)
            for i in range(nb):
                Y_ref[i, pl.ds(r, SUBLANES), :] = y[i][0]
                Y_ref[i, pl.ds(T + r, SUBLANES), :] = y[i][1]
            return carry

        lax.fori_loop(0, T // SUBLANES, chunk, 0, unroll=2)
        y = _toeplitz(*load(0, SUBLANES), rmul)
        for i in range(nb):
            Y_ref[i, 0:SUBLANES, :] = jnp.where(row0, y[i][0], Y_ref[i, 0:SUBLANES, :])
            Y_ref[i, T:T + SUBLANES, :] = jnp.where(row0, y[i][1], Y_ref[i, T:T + SUBLANES, :])

    def long_conv_block(i):
        return _dot1(inv_ref[...], Y_ref[i])

    spectra_products(0)

    def order0(i, carry):
        r0 = pl.multiple_of(i * T, T)
        y = long_conv_block(i)
        for p in range(NBT):
            z1 = x1c_ref[p, pl.ds(r0, T), :] * y[:, p * P:(p + 1) * P]
            u_ref[pl.ds(r0, T), p * P:(p + 1) * P] = z1.astype(BF16)
        return carry

    lax.fori_loop(0, nb, order0, 0, unroll=2 if nb % 2 == 0 else 1)
    for j in range(nb):
        block_spectrum(j, j * T)
    spectra_products(1)

    def order1(i, carry):
        r0 = pl.multiple_of(i * T, T)
        y = long_conv_block(i)
        for p in range(NBT):
            out = x2c_ref[p, pl.ds(r0, T), :] * y[:, p * P:(p + 1) * P]
            y_ref[p, pl.ds(r0, T), :] = (out * _silu(g_ref[p, pl.ds(r0, T), :])).astype(BF16)
        return carry

    lax.fori_loop(0, nb, order1, 0, unroll=2 if nb % 2 == 0 else 1)


def _hyena(z, conv_w, conv_b, kre, kim, fwd, inv):
    B, L, _ = z.shape
    NBT = min(B, max(2, HY_LANES_ROWS // L), HY_MAX_BATCH)
    assert B % NBT == 0
    T = fwd.shape[1]
    nb = L // T
    nlag = 2 * nb - 1
    P = LANES
    ncg = HY_WIDTH // P
    col = lambda off: pl.BlockSpec((NBT, L, P), lambda c, b, off=off: (b, 0, off + c))
    wspec = lambda part: pl.BlockSpec((HY_SHORT, P), lambda c, b, part=part: (0, part * ncg + c))
    bspec = lambda part: pl.BlockSpec((1, P), lambda c, b, part=part: (0, part * ncg + c))
    kspec = pl.BlockSpec((HY_ORDER, nlag, T, P), lambda c, b: (0, 0, 0, c))
    fspec = pl.BlockSpec((2 * T, T), lambda c, b: (0, 0))
    ispec = pl.BlockSpec((T, 2 * T), lambda c, b: (0, 0))
    cb = conv_b.reshape(1, 3 * HY_WIDTH)
    return pl.pallas_call(
        functools.partial(_hy_kernel, L=L),
        grid=(ncg, B // NBT),
        in_specs=[col(_HV0), col(_HX10), col(_HX20), col(_GH0),
                  wspec(0), wspec(1), wspec(2), bspec(0), bspec(1), bspec(2),
                  kspec, kspec, fspec, ispec],
        out_specs=pl.BlockSpec((NBT, L, P), lambda c, b: (b, 0, c)),
        out_shape=jax.ShapeDtypeStruct((B, L, HY_WIDTH), BF16),
        scratch_shapes=[pltpu.VMEM((NBT, L, P), F32), pltpu.VMEM((NBT, L, P), F32),
                        pltpu.VMEM((L, NBT * P), BF16),
                        pltpu.VMEM((nb, 2 * T, NBT * P), F32), pltpu.VMEM((nb, 2 * T, NBT * P), F32)],
        compiler_params=_params(("arbitrary", "arbitrary")),
        name="hyena",
    )(z, z, z, z, conv_w, conv_w, conv_w, cb, cb, cb, kre, kim, fwd, inv)


LRU_TILE = 256
LRU_GROUPS = LRU_TILE // LANES
LRU_ROWS = 128
LRU_STEP_ROWS = 1024


def _lru_kernel(*refs, L, has_state):
    for bi in range(refs[0].shape[0]):
        _lru_one(*refs, L=L, has_state=has_state, bi=bi)


def _lru_one(*refs, L, has_state, bi):
    it = iter(refs)
    x_ref, g_ref, cw_ref, cb_ref, wg_ref, gb_ref, lam_ref = (next(it) for _ in range(7))
    h0_ref = next(it) if has_state else None
    y_ref, sN_ref, xp_ref, A_ref, B_ref, H_ref, C_ref = (next(it) for _ in range(7))

    S = L // SUBLANES
    W = LRU_TILE
    G = LRU_GROUPS
    R = LRU_ROWS
    left = LRU_CONV // 2
    pad = left * SUBLANES

    sub = lax.broadcasted_iota(jnp.int32, (SUBLANES, W), 0)
    for k in range(left):
        tail = pltpu.roll(x_ref[bi, L - pad + k * SUBLANES:L - pad + (k + 1) * SUBLANES, :], 1, axis=0)
        xp_ref[k * SUBLANES:(k + 1) * SUBLANES, :] = jnp.where(sub == 0, 0.0, tail)
    for k in range(LRU_CONV - 1 - left):
        head = pltpu.roll(x_ref[bi, k * SUBLANES:(k + 1) * SUBLANES, :], SUBLANES - 1, axis=0)
        xp_ref[pad + L + k * SUBLANES:pad + L + (k + 1) * SUBLANES, :] = jnp.where(sub == SUBLANES - 1, 0.0, head)

    def copy(c, carry):
        r0 = pl.multiple_of(c * R, R)
        xp_ref[pl.ds(pad + r0, R), :] = x_ref[bi, pl.ds(r0, R), :]
        return carry

    lax.fori_loop(0, L // R, copy, 0)

    csp = [(0.5 * LRU_C) * _softplus(-lam_ref[d]) for d in (0, 1)]

    def coeffs(c, carry):
        r0 = pl.multiple_of(c * R, R)
        u = jnp.broadcast_to(cb_ref[...], (R, W))
        for k in range(LRU_CONV):
            u = u + cw_ref[k:k + 1, :] * xp_ref[pl.ds(r0 + k * SUBLANES, R), :]
        half = jnp.dot(u.astype(BF16), wg_ref[0], preferred_element_type=F32) + gb_ref[0]
        hu = 0.5 * u
        for d in (0, 1):
            t_r = jnp.tanh(half[:, (2 * d) * W:(2 * d + 1) * W])
            t_i = jnp.tanh(half[:, (2 * d + 1) * W:(2 * d + 2) * W])
            nl = csp[d] * t_r + csp[d]
            a = jnp.exp(-nl)
            s2 = jnp.tanh(nl) * (a * a + 1.0)
            b = jnp.where(s2 > 0.0, s2 * lax.rsqrt(s2), 0.0) * (t_i * hu + hu)
            for gi in range(G):
                lanes = slice(gi * LANES, (gi + 1) * LANES)
                A_ref[d, gi, pl.ds(r0, R), :] = a[:, lanes]
                B_ref[d, gi, pl.ds(r0, R), :] = b[:, lanes]
        return carry

    lax.fori_loop(0, L // R, coeffs, 0, unroll=2)

    def scan_body(t, carry):
        out = []
        for d in (0, 1):
            r8 = pl.multiple_of((t if d == 0 else S - 1 - t) * SUBLANES, SUBLANES)
            for gi in range(G):
                h, acc = carry[d * G + gi]
                a = A_ref[d, gi, pl.ds(r8, SUBLANES), :]
                h = a * h + B_ref[d, gi, pl.ds(r8, SUBLANES), :]
                acc = acc * a
                H_ref[d, gi, pl.ds(r8, SUBLANES), :] = h
                C_ref[d, gi, pl.ds(r8, SUBLANES), :] = acc
                out.append((h, acc))
        return tuple(out)

    init = tuple((jnp.zeros((SUBLANES, LANES), F32), jnp.ones((SUBLANES, LANES), F32))
                 for _ in range(2 * G))
    lax.fori_loop(0, S, scan_body, init, unroll=4)

    hin = [None] * (2 * G)
    for d in (0, 1):
        last = (S - 1 if d == 0 else 0) * SUBLANES
        for gi in range(G):
            hl = H_ref[d, gi, last:last + SUBLANES, :]
            ac = C_ref[d, gi, last:last + SUBLANES, :]
            if has_state:
                h = h0_ref[bi, d:d + 1, gi * LANES:(gi + 1) * LANES]
            else:
                h = jnp.zeros((1, LANES), F32)
            rows = [None] * SUBLANES
            for j in (range(SUBLANES) if d == 0 else range(SUBLANES - 1, -1, -1)):
                rows[j] = h
                h = hl[j:j + 1, :] + ac[j:j + 1, :] * h
            hin[d * G + gi] = jnp.concatenate([jnp.concatenate(rows, axis=0)] * (R // SUBLANES), axis=0)
            sN_ref[bi, d:d + 1, gi * LANES:(gi + 1) * LANES] = h

    def combine(c, carry):
        r0 = pl.multiple_of(c * R, R)
        cols = []
        for gi in range(G):
            tot = jnp.zeros((R, LANES), F32)
            for d in (0, 1):
                tot = tot + (H_ref[d, gi, pl.ds(r0, R), :] + C_ref[d, gi, pl.ds(r0, R), :] * hin[d * G + gi])
            cols.append(tot)
        y_ref[bi, pl.ds(r0, R), :] = jnp.concatenate(cols, axis=1) * _silu(g_ref[bi, pl.ds(r0, R), :])
        return carry

    lax.fori_loop(0, L // R, combine, 0)


def _rglru(z4, conv_w, conv_b, gate_w, gate_b, lam, h0):
    B, S, _, _ = z4.shape
    L = S * SUBLANES
    has_state = h0 is not None
    W = LRU_TILE
    nh = LRU_WIDTH // W
    bpt = W // LRU_BLOCK_DIM
    gw = gate_w.reshape(2, 2, nh, bpt, LRU_BLOCK_DIM, LRU_BLOCK_DIM)
    eye = jnp.eye(bpt, dtype=F32)
    dense = (0.5 * jnp.einsum('dkhnij,nm->hnidkmj', gw, eye)).reshape(nh, W, 4 * W).astype(BF16)
    gb = 0.5 * gate_b.reshape(2, 2, nh, W).transpose(2, 0, 1, 3).reshape(nh, 1, 4 * W)
    zr = z4.reshape(B, L, D_LRU)
    NB = min(B, max(1, LRU_STEP_ROWS // L))
    assert B % NB == 0
    off = lambda base: pl.BlockSpec((NB, L, W), lambda h, b, base=base: (b, 0, base + h))
    in_specs = [off(0), off(LRU_WIDTH // W),
                pl.BlockSpec((LRU_CONV, W), lambda h, b: (0, h)),
                pl.BlockSpec((1, W), lambda h, b: (0, h)),
                pl.BlockSpec((1, W, 4 * W), lambda h, b: (h, 0, 0)),
                pl.BlockSpec((1, 1, 4 * W), lambda h, b: (h, 0, 0)),
                pl.BlockSpec((2, 1, W), lambda h, b: (0, 0, h))]
    args = [zr, zr, conv_w, conv_b.reshape(1, LRU_WIDTH), dense, gb, lam.reshape(2, 1, LRU_WIDTH)]
    if has_state:
        in_specs.append(pl.BlockSpec((NB, 2, W), lambda h, b: (b, 0, h)))
        args.append(h0)
    y, s = pl.pallas_call(
        functools.partial(_lru_kernel, L=L, has_state=has_state),
        grid=(nh, B // NB),
        in_specs=in_specs,
        out_specs=[pl.BlockSpec((NB, L, W), lambda h, b: (b, 0, h)),
                   pl.BlockSpec((NB, 2, W), lambda h, b: (b, 0, h))],
        out_shape=[jax.ShapeDtypeStruct((B, L, LRU_WIDTH), F32),
                   jax.ShapeDtypeStruct((B, 2, LRU_WIDTH), F32)],
        scratch_shapes=[pltpu.VMEM((L + (LRU_CONV - 1) * SUBLANES, W), F32)]
        + [pltpu.VMEM((2, LRU_GROUPS, L, LANES), F32) for _ in range(4)],
        compiler_params=_params(("arbitrary", "arbitrary")),
        name="rglru",
    )(*args)
    return y.reshape(B, S, SUBLANES, LRU_WIDTH), s


def _rope_tables(L):
    rows = L // GRID_W
    row = np.repeat(np.arange(rows, dtype=np.float64), GRID_W)
    col = np.tile(np.arange(GRID_W, dtype=np.float64), rows)
    n_f = RET_HEAD_DIM // 4
    inv = ROPE_BASE ** (-np.arange(n_f, dtype=np.float64) / n_f)
    ang = np.concatenate([row[:, None] * inv[None], col[:, None] * inv[None]], axis=-1)
    cos, sin = np.cos(ang), np.sin(ang)
    return (jnp.asarray(np.concatenate([cos, cos], axis=-1), F32),
            jnp.asarray(np.concatenate([-sin, sin], axis=-1), F32))


def kernel(x_prompt, x_sample, state_ret, state_lru, c, c_ctx, norm_g, ada_w, ada_b, w_in, ret_decay_logit, hy_conv_w, hy_conv_b, hy_ffn_w1, hy_ffn_b1, hy_ffn_w2, hy_ffn_b2, hy_ffn_w3, hy_freq, hy_bias, lru_conv_w, lru_conv_b, lru_gate_w, lru_gate_b, lru_lambda, w_out, final_g):
    Bp, Lp, _ = x_prompt.shape
    Bs, Ls, _ = x_sample.shape
    assert Bs + 1 <= SUBLANES

    cvec = jnp.zeros((SUBLANES, D_MODEL), F32).at[:Bs].set(c).at[Bs].set(c_ctx)
    mod = _modulation(cvec, ada_w, ada_b)
    rope = _rope_tables(Ls)
    dft = {L: _dft_matrices(min(DFT_MAX, L)) for L in {Lp, Ls}}
    w_in_b = w_in.astype(BF16)
    w_out_b = w_out.astype(BF16)
    fg = final_g.reshape(1, D_MODEL)

    xc, xl = x_prompt, x_sample
    new_state_ret, new_lru = None, []
    for l in range(DEPTH):
        shift, scale, gate = (mod[l, :, i * D_MODEL:(i + 1) * D_MODEL] for i in range(3))
        g = norm_g[l].reshape(1, D_MODEL)
        final = l == DEPTH - 1
        filt = {}
        for L in sorted({Lp, Ls}):
            filt[L] = _hyena_filter_spectra(L, hy_ffn_w1[l], hy_ffn_b1[l], hy_ffn_w2[l], hy_ffn_b2[l],
                                            hy_ffn_w3[l], hy_freq[l], hy_bias[l], dft[L][0])

        def layer(x, sel, rope_t, ret_s0, lru_s0, ret_states):
            sh, sc, gt = (m[sel][:, None, :] for m in (shift, scale, gate))
            z, z_lru = _in_proj(x, sh, sc, g, w_in_b, l)
            y_ret, *s_ret = _retention(z, ret_decay_logit[l], rope_t, ret_s0, l, ret_states)
            kre, kim = filt[x.shape[1]]
            y_hy = _hyena(z, hy_conv_w[l], hy_conv_b[l], kre, kim, *dft[x.shape[1]])
            y_lru, s_lru = _rglru(z_lru, lru_conv_w[l], lru_conv_b[l], lru_gate_w[l], lru_gate_b[l],
                                  lru_lambda[l], lru_s0)
            return _out_proj(y_ret, y_hy, y_lru, x, gt, w_out_b, l, fg, final), s_ret, s_lru

        xc, (new_state_ret,), ls = layer(xc, slice(Bs, Bs + 1), None, None, None,
                                         "new" if l == 0 else new_state_ret)
        new_lru.append(ls)
        xl, _, _ = layer(xl, slice(0, Bs), rope, state_ret, state_lru[:, l], None)

    new_state_lru = jnp.stack(new_lru, axis=1).astype(x_prompt.dtype)
    return (xc, xl, new_state_ret.astype(x_prompt.dtype), new_state_lru)
```

```python
import functools
import math

import numpy as np
import jax
import jax.numpy as jnp
from jax import lax
from jax.experimental import pallas as pl
from jax.experimental.pallas import tpu as pltpu

F32 = jnp.float32
BF16 = jnp.bfloat16
HI = lax.Precision.HIGHEST

D_MODEL = 1024
DEPTH = 2
GRID_W = 64
EPS = 1e-6
RET_HEADS = 4
RET_HEAD_DIM = 128
RET_WIDTH = RET_HEADS * RET_HEAD_DIM
ROPE_BASE = 10000.0
HY_WIDTH = 512
HY_ORDER = 2
HY_SHORT = 3
HY_EMB = 33
HY_FFN = 64
HY_SHORT_DECAY_PCT = 0.3
HY_LONG_DECAY_PCT = 1.5
HY_TARGET = 1e-2
LRU_WIDTH = 512
LRU_BLOCKS = 8
LRU_BLOCK_DIM = LRU_WIDTH // LRU_BLOCKS
LRU_CONV = 4
LRU_C = 8.0
D_MIX = RET_WIDTH + HY_WIDTH + LRU_WIDTH
D_IN = 4 * RET_WIDTH + 4 * HY_WIDTH + 2 * LRU_WIDTH

LANES = 128
SUBLANES = 8
DFT_MAX = 512
ROW_TILE = 256
IN_ROWS_MAX = 512
VMEM_LIMIT = 56 * 1024 * 1024

_Q0, _K0, _V0, _GR0 = 0, 4, 8, 12
_HV0, _HX10, _HX20, _GH0 = 16, 20, 24, 28


def _sigmoid(x):
    return 0.5 * jnp.tanh(0.5 * x) + 0.5


def _silu(x):
    return x * _sigmoid(x)


def _softplus(x):
    return jnp.maximum(x, 0.0) + jnp.log1p(jnp.exp(-jnp.abs(x)))


def _params(sem):
    return pltpu.CompilerParams(dimension_semantics=sem, vmem_limit_bytes=VMEM_LIMIT)


def _mod_kernel(c_ref, w_ref, b_ref, o_ref):
    s_hi, s_lo = _split_bf16(_silu(c_ref[...]))
    w_hi, w_lo = _split_bf16(w_ref[0])
    o_ref[0] = (jnp.dot(s_hi, w_hi, preferred_element_type=F32)
                + jnp.dot(s_lo, w_hi, preferred_element_type=F32)
                + jnp.dot(s_hi, w_lo, preferred_element_type=F32)) + b_ref[0]


def _modulation(cvec, ada_w, ada_b):
    tn = 768
    return pl.pallas_call(
        _mod_kernel,
        grid=(DEPTH, 3 * D_MODEL // tn),
        in_specs=[pl.BlockSpec((SUBLANES, D_MODEL), lambda l, j: (0, 0)),
                  pl.BlockSpec((1, D_MODEL, tn), lambda l, j: (l, 0, j)),
                  pl.BlockSpec((1, 1, tn), lambda l, j: (l, 0, j))],
        out_specs=pl.BlockSpec((1, SUBLANES, tn), lambda l, j: (l, 0, j)),
        out_shape=jax.ShapeDtypeStruct((DEPTH, SUBLANES, 3 * D_MODEL), F32),
        compiler_params=_params(("arbitrary", "arbitrary")),
        name="modulation",
    )(cvec, ada_w, ada_b.reshape(DEPTH, 1, 3 * D_MODEL))


D_MAIN = 4 * RET_WIDTH + 4 * HY_WIDTH
D_LRU = 2 * LRU_WIDTH


def _in_kernel(x_ref, sh_ref, sc_ref, g_ref, w_ref, o_ref, ol_ref):
    nb, nseg, rows, _ = x_ref.shape
    x = x_ref[...].reshape(nb * nseg * rows, D_MODEL)
    ms = jnp.mean(x * x, axis=-1, keepdims=True)
    y = x * lax.rsqrt(ms + EPS) * g_ref[...]
    h = (y * (1.0 + sc_ref[0]) + sh_ref[0]).astype(BF16)
    tn = 1024
    for n in range(D_MAIN // tn):
        z = jnp.dot(h, w_ref[:, n * tn:(n + 1) * tn], preferred_element_type=F32)
        o_ref[:, :, :, n * tn:(n + 1) * tn] = z.reshape(nb, nseg, rows, tn)
    zl = jnp.dot(h, w_ref[:, D_MAIN:D_IN], preferred_element_type=F32)
    for bi in range(nb):
        for j in range(nseg):
            r0 = (bi * nseg + j) * rows
            ol_ref[bi, :, j, :] = zl[r0:r0 + rows]


def _in_proj(x, shift, scale, g, w_bf16, layer):
    B, L, _ = x.shape
    S = L // SUBLANES
    rows = min(ROW_TILE // SUBLANES, S)
    per_batch = shift.shape[0] > 1
    mod_map = (lambda b, i: (b, 0, 0)) if per_batch else (lambda b, i: (0, 0, 0))
    NB = 1 if per_batch else min(B, max(1, IN_ROWS_MAX // (SUBLANES * rows)))
    assert B % NB == 0
    seg = lambda C: pl.BlockSpec((NB, SUBLANES, rows, C), lambda b, i: (b, 0, i, 0))
    z, z_lru = pl.pallas_call(
        _in_kernel,
        grid=(B // NB, S // rows),
        in_specs=[seg(D_MODEL),
                  pl.BlockSpec((1, 1, D_MODEL), mod_map),
                  pl.BlockSpec((1, 1, D_MODEL), mod_map),
                  pl.BlockSpec((1, D_MODEL), lambda b, i: (0, 0)),
                  pl.BlockSpec((None, D_MODEL, D_IN), lambda b, i: (layer, 0, 0),
                               pipeline_mode=pl.Buffered(1))],
        out_specs=[seg(D_MAIN), pl.BlockSpec((NB, rows, SUBLANES, D_LRU), lambda b, i: (b, i, 0, 0))],
        out_shape=[jax.ShapeDtypeStruct((B, SUBLANES, S, D_MAIN), F32),
                   jax.ShapeDtypeStruct((B, S, SUBLANES, D_LRU), F32)],
        compiler_params=_params(("arbitrary", "arbitrary")),
        name="in_proj",
    )(x.reshape(B, SUBLANES, S, D_MODEL), shift, scale, g, w_bf16)
    return z.reshape(B, L, D_MAIN), z_lru


OUT_ROWS = 1024


def _out_kernel(yr_ref, yh_ref, yl_ref, x_ref, gate_ref, w_ref, fg_ref, o_ref, *, final):
    nb, nseg, rows, _ = x_ref.shape
    n = nb * nseg * rows
    acc = jnp.dot(yr_ref[...].reshape(n, RET_WIDTH), w_ref[0:RET_WIDTH], preferred_element_type=F32)
    acc = acc + jnp.dot(yh_ref[...].reshape(n, HY_WIDTH), w_ref[RET_WIDTH:RET_WIDTH + HY_WIDTH],
                        preferred_element_type=F32)
    yl = jnp.concatenate([yl_ref[bi, :, j, :] for bi in range(nb) for j in range(nseg)], axis=0)
    acc = acc + jnp.dot(yl.astype(BF16), w_ref[RET_WIDTH + HY_WIDTH:D_MIX], preferred_element_type=F32)
    x = x_ref[...].reshape(n, D_MODEL) + gate_ref[0] * acc
    if final:
        ms = jnp.mean(x * x, axis=-1, keepdims=True)
        x = x * lax.rsqrt(ms + EPS) * fg_ref[...]
    o_ref[...] = x.reshape(nb, nseg, rows, D_MODEL)


def _out_proj(y_ret, y_hy, y_lru, x, gate, w_bf16, layer, final_g, final):
    B, L, _ = x.shape
    S = L // SUBLANES
    rows = min(OUT_ROWS // SUBLANES, S)
    per_batch = gate.shape[0] > 1
    mod_map = (lambda b, i: (b, 0, 0)) if per_batch else (lambda b, i: (0, 0, 0))
    NB = 1 if per_batch else min(B, max(1, OUT_ROWS // (SUBLANES * rows)))
    assert B % NB == 0
    seg = lambda C: pl.BlockSpec((NB, SUBLANES, rows, C), lambda b, i: (b, 0, i, 0))
    view = lambda a: a.reshape(B, SUBLANES, S, a.shape[-1])
    out = pl.pallas_call(
        functools.partial(_out_kernel, final=final),
        grid=(B // NB, S // rows),
        in_specs=[seg(RET_WIDTH), seg(HY_WIDTH),
                  pl.BlockSpec((NB, rows, SUBLANES, LRU_WIDTH), lambda b, i: (b, i, 0, 0)),
                  seg(D_MODEL),
                  pl.BlockSpec((1, 1, D_MODEL), mod_map),
                  pl.BlockSpec((None, D_MIX, D_MODEL), lambda b, i: (layer, 0, 0),
                               pipeline_mode=pl.Buffered(1)),
                  pl.BlockSpec((1, D_MODEL), lambda b, i: (0, 0))],
        out_specs=seg(D_MODEL),
        out_shape=jax.ShapeDtypeStruct((B, SUBLANES, S, D_MODEL), F32),
        compiler_params=_params(("arbitrary", "arbitrary")),
        name="out_proj",
    )(view(y_ret), view(y_hy), y_lru, view(x), gate, w_bf16, final_g)
    return out.reshape(B, L, D_MODEL)


RET_HEAD_ROWS = 2 * 2048
RET_BLOCK = 256


def _ret_kernel(*refs, L, use_rope, has_state, emit_state, chained, layer):
    it = iter(refs)
    q_ref, k_ref, v_ref, g_ref = next(it), next(it), next(it), next(it)
    cos_ref = sin_ref = s0_ref = sN_ref = None
    if use_rope:
        cos_ref, sin_ref = next(it), next(it)
    lg_ref = next(it)
    if has_state:
        s0_ref = next(it)
    if chained:
        next(it)
    y_ref = next(it)
    if emit_state:
        sN_ref = next(it)
    R_ref, M_ref, Z_ref, D_ref, KV_ref, kb_ref = (next(it) for _ in range(6))

    C = min(RET_BLOCK, L)
    DH = RET_HEAD_DIM
    HP = q_ref.shape[-1] // DH
    n = L // C
    cross = has_state or n > 1
    kscale = DH ** -0.5

    @pl.when(pl.program_id(1) == 0)
    def _():
        ii = lax.broadcasted_iota(jnp.int32, (C, C), 0)
        jj = lax.broadcasted_iota(jnp.int32, (C, C), 1)
        diff = (ii - jj).astype(F32)
        ri = lax.broadcasted_iota(jnp.int32, (C, DH), 0).astype(F32)
        for hh in range(HP):
            lgf = -_softplus(-lg_ref[0, hh])
            lgb = -_softplus(-lg_ref[1, hh])
            M_ref[hh] = (jnp.where(diff >= 0, jnp.exp(lgf[:, :C] * jnp.maximum(diff, 0.0)), 0.0)
                         + jnp.where(diff <= 0, jnp.exp(lgb[:, :C] * jnp.maximum(-diff, 0.0)), 0.0))
            lf, lb = lgf[:, :DH], lgb[:, :DH]
            Z_ref[hh, 0] = jnp.concatenate([jnp.exp(lf * (C - 1.0 - ri)), jnp.exp(lb * ri)], axis=1)
            Z_ref[hh, 1] = jnp.concatenate([jnp.exp(lf * (ri + 1.0)), jnp.exp(lb * (C - ri))], axis=1)
            D_ref[hh] = jnp.exp(jnp.concatenate([jnp.broadcast_to(lf * C, (DH, DH)),
                                                 jnp.broadcast_to(lb * C, (DH, DH))], axis=0))

    def rotary(x, r0):
        if not use_rope:
            return x
        return x * cos_ref[pl.ds(r0, C), :] + pltpu.roll(x, 64, axis=1) * sin_ref[pl.ds(r0, C), :]

    def chunk_kv(c, carry):
        r0 = pl.multiple_of(c * C, C)
        for hh in range(HP):
            lanes = slice(hh * DH, (hh + 1) * DH)
            k = rotary(k_ref[0, pl.ds(r0, C), lanes] * kscale, r0)
            kb_ref[hh, pl.ds(r0, C), :] = k.astype(BF16)
            kz = (jnp.concatenate([k, k], axis=1) * Z_ref[hh, 0]).astype(BF16)
            vb = v_ref[0, pl.ds(r0, C), lanes].astype(BF16)
            KV_ref[hh, c] = lax.dot_general(kz, vb, (((0,), (0,)), ((), ())), preferred_element_type=F32)
        return carry

    lax.fori_loop(0, n, chunk_kv, 0, unroll=2 if n % 2 == 0 else 1)

    for d in (0, 1):
        for hh in range(HP):
            R_ref[d, hh] = s0_ref[0, d, hh] if has_state else jnp.zeros((DH, DH), F32)

    def states(t, carry):
        for d in (0, 1):
            c = t if d == 0 else n - 1 - t
            rows = slice(d * DH, (d + 1) * DH)
            for hh in range(HP):
                R = R_ref[d, hh]
                R_ref[d, hh] = D_ref[hh, rows, :] * R + KV_ref[hh, c, rows, :]
                KV_ref[hh, c, rows, :] = R
        return carry

    lax.fori_loop(0, n, states, 0)
    if emit_state:
        for l in range(sN_ref.shape[1]):
            if chained or l == layer:
                for d in (0, 1):
                    for hh in range(HP):
                        sN_ref[0, l, d, hh] = R_ref[d, hh]
            else:
                sN_ref[0, l] = jnp.zeros(sN_ref.shape[2:], F32)

    def chunk_out(c, carry):
        r0 = pl.multiple_of(c * C, C)
        cols = []
        for hh in range(HP):
            lanes = slice(hh * DH, (hh + 1) * DH)
            q = rotary(q_ref[0, pl.ds(r0, C), lanes], r0)
            vb = v_ref[0, pl.ds(r0, C), lanes].astype(BF16)
            s = lax.dot_general(q.astype(BF16), kb_ref[hh, pl.ds(r0, C), :], (((1,), (1,)), ((), ())),
                                preferred_element_type=F32) * M_ref[hh]
            o = jnp.dot(s.astype(BF16), vb, preferred_element_type=F32)
            if cross:
                qx = (jnp.concatenate([q, q], axis=1) * Z_ref[hh, 1]).astype(BF16)
                o = o + jnp.dot(qx, KV_ref[hh, c].astype(BF16), preferred_element_type=F32)
            cols.append(o * lax.rsqrt(jnp.mean(o * o, axis=-1, keepdims=True) + EPS))
        g = g_ref[0, pl.ds(r0, C), :]
        y_ref[0, pl.ds(r0, C), :] = (jnp.concatenate(cols, axis=1) * _silu(g)).astype(BF16)
        return carry

    lax.fori_loop(0, n, chunk_out, 0, unroll=2 if n % 2 == 0 else 1)


def _retention(z, decay_logit, rope, state0, layer, states):
    B, L, _ = z.shape
    use_rope = rope is not None
    has_state = state0 is not None
    emit_state = states is not None
    chained = emit_state and not isinstance(states, str)
    H = RET_HEADS
    HP = min(H, max(1, RET_HEAD_ROWS // L))
    W = HP * RET_HEAD_DIM
    col = lambda off: pl.BlockSpec((1, L, W), lambda h, b, off=off: (b, 0, off * LANES // W + h))
    in_specs = [col(_Q0), col(_K0), col(_V0), col(_GR0)]
    args = [z, z, z, z]
    if use_rope:
        in_specs += [pl.BlockSpec((L, LANES), lambda h, b: (0, 0))] * 2
        args += list(rope)
    C = min(RET_BLOCK, L)
    in_specs.append(pl.BlockSpec((2, HP, 1, RET_BLOCK), lambda h, b: (0, h, 0, 0)))
    args.append(jnp.broadcast_to(decay_logit[:, :, None, None], (2, H, 1, RET_BLOCK)))
    if has_state:
        in_specs.append(pl.BlockSpec((1, None, 2, HP, RET_HEAD_DIM, RET_HEAD_DIM),
                                     lambda h, b: (b, layer, 0, h, 0, 0)))
        args.append(state0)
    out_specs = [pl.BlockSpec((1, L, W), lambda h, b: (b, 0, h))]
    out_shape = [jax.ShapeDtypeStruct((B, L, RET_WIDTH), BF16)]
    aliases = {}
    if emit_state:
        nl, l0 = (1, layer) if chained else (DEPTH, 0)
        out_specs.append(pl.BlockSpec((1, nl, 2, HP, RET_HEAD_DIM, RET_HEAD_DIM),
                                      lambda h, b: (b, l0, 0, h, 0, 0)))
        out_shape.append(jax.ShapeDtypeStruct((B, DEPTH, 2, H, RET_HEAD_DIM, RET_HEAD_DIM), F32))
        if chained:
            aliases = {len(args): 1}
            in_specs.append(pl.BlockSpec(memory_space=pl.ANY))
            args.append(states)
    return pl.pallas_call(
        functools.partial(_ret_kernel, L=L, use_rope=use_rope, has_state=has_state,
                          emit_state=emit_state, chained=chained, layer=layer),
        grid=(H // HP, B),
        in_specs=in_specs,
        out_specs=out_specs,
        out_shape=out_shape,
        input_output_aliases=aliases,
        scratch_shapes=[pltpu.VMEM((2, HP, RET_HEAD_DIM, RET_HEAD_DIM), F32),
                        pltpu.VMEM((HP, C, C), F32),
                        pltpu.VMEM((HP, 2, C, 2 * RET_HEAD_DIM), F32),
                        pltpu.VMEM((HP, 2 * RET_HEAD_DIM, RET_HEAD_DIM), F32),
                        pltpu.VMEM((HP, L // C, 2 * RET_HEAD_DIM, RET_HEAD_DIM), F32),
                        pltpu.VMEM((HP, L, RET_HEAD_DIM), BF16)],
        compiler_params=_params(("arbitrary", "arbitrary")),
        name="retention",
    )(*args)


def _dft_matrices(T):
    N = 2 * T
    k = np.arange(T, dtype=np.float64)[:, None]
    m = np.arange(T, dtype=np.float64)[None, :]
    ang = 2.0 * np.pi * k * m / N
    fwd = np.concatenate([np.cos(ang), -np.sin(ang)], axis=0)
    fwd[T, :] = (-1.0) ** np.arange(T)
    ck = np.full((T,), 2.0)
    ck[0] = 1.0
    inv_re = (np.cos(ang) * ck[:, None]).T / N
    inv_im = (-2.0 * np.sin(ang)).T / N
    inv_im[:, 0] = ((-1.0) ** np.arange(T)) / N
    inv = np.concatenate([inv_re, inv_im], axis=1)
    return jnp.asarray(fwd, F32).astype(BF16), jnp.asarray(inv, F32).astype(BF16)


def _split_bf16(x):
    hi = x.astype(BF16)
    return hi, (x - hi.astype(F32)).astype(BF16)


def _dot1(a, b):
    return jnp.dot(a, b.astype(BF16), preferred_element_type=F32)


def _filt_kernel(zemb_ref, w1_ref, b1_ref, w2_ref, b2_ref, fr_ref, w3f_ref, w3b_ref, dec_ref,
                 bias_ref, fwd_ref, kre_ref, kim_ref, hid_ref, hn_ref, F_ref, *, L):
    T = fwd_ref.shape[1]
    nb = L // T

    @pl.when((pl.program_id(0) == 0) & (pl.program_id(1) == 0))
    def _():
        z1 = jnp.dot(zemb_ref[...], w1_ref[...], precision=HI, preferred_element_type=F32) + b1_ref[...]
        h1 = jnp.sin(fr_ref[0:1, :] * z1)
        z2 = jnp.dot(h1, w2_ref[...], precision=HI, preferred_element_type=F32) + b2_ref[...]
        hid = jnp.sin(fr_ref[1:2, :] * z2)
        hid_ref[0], hid_ref[1] = _split_bf16(hid)

    w3 = jnp.concatenate([w3f_ref[...], w3b_ref[...]], axis=1)
    w_hi, w_lo = _split_bf16(w3)
    h = (jnp.dot(hid_ref[0], w_hi, preferred_element_type=F32)
         + jnp.dot(hid_ref[1], w_hi, preferred_element_type=F32)
         + jnp.dot(hid_ref[0], w_lo, preferred_element_type=F32))
    h = h * jnp.concatenate([dec_ref[...]] * 2, axis=1)
    h = h / (jnp.sum(jnp.abs(h), axis=0, keepdims=True) + EPS)
    for dr in (0, 1):
        hn_ref[dr] = h[:, dr * LANES:(dr + 1) * LANES]
    for blk in range(nb):
        F = _dot1(fwd_ref[...], h[blk * T:(blk + 1) * T])
        for dr in (0, 1):
            F_ref[dr, blk] = F[:, dr * LANES:(dr + 1) * LANES]

    row = lax.broadcasted_iota(jnp.int32, (T, LANES), 0)
    sgn = jnp.where((row & 1) == 0, 1.0, -1.0).astype(F32)
    row0 = row == 0
    bias = bias_ref[0]
    for d in range(-(nb - 1), nb):
        re = jnp.zeros((T, LANES), F32)
        im = jnp.zeros((T, LANES), F32)
        nyq = jnp.zeros((1, LANES), F32)
        if d >= 0:
            F = F_ref[0, d]
            re, im, nyq = re + F[:T], im + F[T:], nyq + F[T:T + 1]
        if d >= 1:
            F = F_ref[0, d - 1]
            p0 = hn_ref[0, (d - 1) * T:(d - 1) * T + 1, :]
            re, im, nyq = re + sgn * (F[:T] - p0), im + sgn * F[T:], nyq + (F[T:T + 1] - p0)
        e = -d
        if e >= 0:
            F = F_ref[1, e]
            re, im, nyq = re + F[:T], im - F[T:], nyq + F[T:T + 1]
        if e >= 1:
            F = F_ref[1, e - 1]
            p0 = hn_ref[1, (e - 1) * T:(e - 1) * T + 1, :]
            re, im, nyq = re + sgn * (F[:T] - p0), im - sgn * F[T:], nyq + (F[T:T + 1] - p0)
        if d == 0:
            re, nyq = re + bias, nyq + bias
        kre_ref[0, d + nb - 1] = re
        kim_ref[0, d + nb - 1] = jnp.where(row0, nyq, im)


def _hyena_filter_spectra(L, w1, b1, w2, b2, w3, freq, bias, fwd):
    T = fwd.shape[1]
    nb = L // T
    nlag = 2 * nb - 1
    P = LANES
    t = np.linspace(0.0, 1.0, L)[:, None]
    n_bands = (HY_EMB - 1) // 2
    f = np.linspace(1e-4, n_bands - 1, n_bands)
    ang = (2.0 * math.pi / L) * np.arange(L)[:, None] * f[None, :]
    zemb = np.concatenate([t, np.cos(ang), -np.sin(ang)], axis=-1)
    zemb = jnp.asarray(np.pad(zemb, ((0, 0), (0, P - HY_EMB))), F32)
    min_decay = math.log(HY_TARGET) / HY_LONG_DECAY_PCT
    max_decay = math.log(HY_TARGET) / HY_SHORT_DECAY_PCT
    deltas = np.abs(np.linspace(min_decay, max_decay, HY_WIDTH))
    dec = jnp.asarray(np.exp(-t * deltas[None, :]), F32)
    pad = P - HY_FFN
    w1p = jnp.pad(w1, ((0, P - HY_EMB), (0, pad)))
    b1p = jnp.pad(b1, (0, pad)).reshape(1, P)
    w2p = jnp.pad(w2, ((0, pad), (0, pad)))
    b2p = jnp.pad(b2, (0, pad)).reshape(1, P)
    w3p = jnp.pad(w3, ((0, pad), (0, 0)))
    frp = jnp.pad(freq, ((0, 0), (0, pad)))
    ncg = HY_WIDTH // P
    const = lambda shape: pl.BlockSpec(shape, lambda o, c: tuple(0 for _ in shape))
    kshape = jax.ShapeDtypeStruct((HY_ORDER, nlag, T, HY_WIDTH), F32)
    kspec = pl.BlockSpec((1, nlag, T, P), lambda o, c: (o, 0, 0, c))
    return pl.pallas_call(
        functools.partial(_filt_kernel, L=L),
        grid=(HY_ORDER, ncg),
        in_specs=[const((L, P)), const((P, P)), const((1, P)), const((P, P)), const((1, P)), const((2, P)),
                  pl.BlockSpec((P, P), lambda o, c: (0, o * 2 * ncg + c)),
                  pl.BlockSpec((P, P), lambda o, c: (0, o * 2 * ncg + ncg + c)),
                  pl.BlockSpec((L, P), lambda o, c: (0, c)),
                  pl.BlockSpec((1, 1, P), lambda o, c: (o, 0, c)),
                  const((2 * T, T))],
        out_specs=[kspec, kspec],
        out_shape=[kshape, kshape],
        scratch_shapes=[pltpu.VMEM((2, L, P), BF16), pltpu.VMEM((2, L, P), F32),
                        pltpu.VMEM((2, nb, 2 * T, P), F32)],
        compiler_params=_params(("arbitrary", "arbitrary")),
        name="hyena_filters",
    )(zemb, w1p, b1p, w2p, b2p, frp, w3p, w3p, dec, bias.reshape(HY_ORDER, 1, HY_WIDTH), fwd)


CONV_ROWS = 128


def _conv_block(x_ref, t0, w, b, left):
    L, C = x_ref.shape
    T = CONV_ROWS
    blk = x_ref[t0:t0 + T, :]
    prev = x_ref[t0 - SUBLANES:t0, :] if t0 > 0 else jnp.zeros((SUBLANES, C), F32)
    nxt = x_ref[t0 + T:t0 + T + SUBLANES, :] if t0 + T < L else jnp.zeros((SUBLANES, C), F32)
    ext = jnp.concatenate([prev, blk, nxt], axis=0)
    n = T + 2 * SUBLANES
    acc = jnp.broadcast_to(b, (T, C))
    for tap in range(w.shape[0]):
        s = tap - left
        sh = blk if s == 0 else pltpu.roll(ext, (-s) % n, axis=0)[SUBLANES:SUBLANES + T]
        acc = acc + w[tap:tap + 1, :] * sh
    return acc


HY_LANES_ROWS = 2 * 2048
HY_MAX_BATCH = 8


def _toeplitz(k, u, mul):
    n = len(u)
    if n == 1:
        return [mul(k[0], u[0])]
    h = n // 2
    add = lambda a, b: tuple(x + y for x, y in zip(a, b))
    sub = lambda a, b: tuple(x - y for x, y in zip(a, b))
    t0, t1, t2 = k[h:h + n - 1], k[0:n - 1], k[n:2 * n - 1]
    p = _toeplitz(t0, [add(a, b) for a, b in zip(u[:h], u[h:])], mul)
    q = _toeplitz([sub(a, b) for a, b in zip(t1, t0)], u[h:], mul)
    r = _toeplitz([sub(a, b) for a, b in zip(t2, t0)], u[:h], mul)
    return [add(a, b) for a, b in zip(p, q)] + [add(a, b) for a, b in zip(p, r)]


def _hy_kernel(v_ref, x1_ref, x2_ref, g_ref, wv_ref, w1_ref, w2_ref, bv_ref, b1_ref, b2_ref,
               kre_ref, kim_ref, fwd_ref, inv_ref, y_ref,
               x1c_ref, x2c_ref, u_ref, U_ref, Y_ref, *, L):
    T = fwd_ref.shape[1]
    P = LANES
    nb = L // T
    NBT = v_ref.shape[0]
    left = (HY_SHORT - 1) // 2
    def block_spectrum(j, r0):
        U_ref[j] = jnp.dot(fwd_ref[...], u_ref[pl.ds(r0, T), :], preferred_element_type=F32)

    for j in range(nb):
        for t0 in range(j * T, (j + 1) * T, CONV_ROWS):
            rows = slice(t0, t0 + CONV_ROWS)
            for p in range(NBT):
                u_ref[rows, p * P:(p + 1) * P] = _conv_block(v_ref.at[p], t0, wv_ref[...], bv_ref[...], left).astype(BF16)
                x1c_ref[p, rows, :] = _conv_block(x1_ref.at[p], t0, w1_ref[...], b1_ref[...], left)
                x2c_ref[p, rows, :] = _conv_block(x2_ref.at[p], t0, w2_ref[...], b2_ref[...], left)
        block_spectrum(j, j * T)

    row0 = lax.broadcasted_iota(jnp.int32, (SUBLANES, NBT * P), 0) == 0
    tile = lambda x: jnp.concatenate([x] * NBT, axis=1)

    def cmul(k, u):
        kr, ki = tile(k[0]), tile(k[1])
        return (kr * u[0] - ki * u[1], kr * u[1] + ki * u[0])

    def rmul(k, u):
        return (tile(k[0]) * u[0], tile(k[1]) * u[1])

    def spectra_products(o):
        def load(r, rows):
            k = [(kre_ref[o, l, pl.ds(r, rows), :], kim_ref[o, l, pl.ds(r, rows), :]) for l in range(2 * nb - 1)]
            u = [(U_ref[j, pl.ds(r, rows), :], U_ref[j, pl.ds(T + r, rows), :]) for j in range(nb)]
            return k, u

        def chunk(rc, carry):
            r = pl.multiple_of(rc * SUBLANES, SUBLANES)
            y = _toeplitz(*load(r, SUBLANES), cmul)
            for i in range(nb):
                Y_ref[i, pl.ds(r, SUBLANES), :] = y[i][0]
                Y_ref[i, pl.ds(T + r, SUBLANES), :] = y[i][1]
            return carry

        lax.fori_loop(0, T // SUBLANES, chunk, 0, unroll=2)
        y = _toeplitz(*load(0, SUBLANES), rmul)
        for i in range(nb):
            Y_ref[i, 0:SUBLANES, :] = jnp.where(row0, y[i][0], Y_ref[i, 0:SUBLANES, :])
            Y_ref[i, T:T + SUBLANES, :] = jnp.where(row0, y[i][1], Y_ref[i, T:T + SUBLANES, :])

    def long_conv_block(i):
        return _dot1(inv_ref[...], Y_ref[i])

    spectra_products(0)

    def order0(i, carry):
        r0 = pl.multiple_of(i * T, T)
        y = long_conv_block(i)
        for p in range(NBT):
            z1 = x1c_ref[p, pl.ds(r0, T), :] * y[:, p * P:(p + 1) * P]
            u_ref[pl.ds(r0, T), p * P:(p + 1) * P] = z1.astype(BF16)
        return carry

    lax.fori_loop(0, nb, order0, 0, unroll=2 if nb % 2 == 0 else 1)
    for j in range(nb):
        block_spectrum(j, j * T)
    spectra_products(1)

    def order1(i, carry):
        r0 = pl.multiple_of(i * T, T)
        y = long_conv_block(i)
        for p in range(NBT):
            out = x2c_ref[p, pl.ds(r0, T), :] * y[:, p * P:(p + 1) * P]
            y_ref[p, pl.ds(r0, T), :] = (out * _silu(g_ref[p, pl.ds(r0, T), :])).astype(BF16)
        return carry

    lax.fori_loop(0, nb, order1, 0, unroll=2 if nb % 2 == 0 else 1)


def _hyena(z, conv_w, conv_b, kre, kim, fwd, inv):
    B, L, _ = z.shape
    NBT = min(B, max(2, HY_LANES_ROWS // L), HY_MAX_BATCH)
    assert B % NBT == 0
    T = fwd.shape[1]
    nb = L // T
    nlag = 2 * nb - 1
    P = LANES
    ncg = HY_WIDTH // P
    col = lambda off: pl.BlockSpec((NBT, L, P), lambda c, b, off=off: (b, 0, off + c))
    wspec = lambda part: pl.BlockSpec((HY_SHORT, P), lambda c, b, part=part: (0, part * ncg + c))
    bspec = lambda part: pl.BlockSpec((1, P), lambda c, b, part=part: (0, part * ncg + c))
    kspec = pl.BlockSpec((HY_ORDER, nlag, T, P), lambda c, b: (0, 0, 0, c))
    fspec = pl.BlockSpec((2 * T, T), lambda c, b: (0, 0))
    ispec = pl.BlockSpec((T, 2 * T), lambda c, b: (0, 0))
    cb = conv_b.reshape(1, 3 * HY_WIDTH)
    return pl.pallas_call(
        functools.partial(_hy_kernel, L=L),
        grid=(ncg, B // NBT),
        in_specs=[col(_HV0), col(_HX10), col(_HX20), col(_GH0),
                  wspec(0), wspec(1), wspec(2), bspec(0), bspec(1), bspec(2),
                  kspec, kspec, fspec, ispec],
        out_specs=pl.BlockSpec((NBT, L, P), lambda c, b: (b, 0, c)),
        out_shape=jax.ShapeDtypeStruct((B, L, HY_WIDTH), BF16),
        scratch_shapes=[pltpu.VMEM((NBT, L, P), F32), pltpu.VMEM((NBT, L, P), F32),
                        pltpu.VMEM((L, NBT * P), BF16),
                        pltpu.VMEM((nb, 2 * T, NBT * P), F32), pltpu.VMEM((nb, 2 * T, NBT * P), F32)],
        compiler_params=_params(("arbitrary", "arbitrary")),
        name="hyena",
    )(z, z, z, z, conv_w, conv_w, conv_w, cb, cb, cb, kre, kim, fwd, inv)


LRU_TILE = 256
LRU_GROUPS = LRU_TILE // LANES
LRU_ROWS = 128
LRU_STEP_ROWS = 1024


def _lru_kernel(*refs, L, has_state):
    for bi in range(refs[0].shape[0]):
        _lru_one(*refs, L=L, has_state=has_state, bi=bi)


def _lru_one(*refs, L, has_state, bi):
    it = iter(refs)
    x_ref, g_ref, cw_ref, cb_ref, wg_ref, gb_ref, lam_ref = (next(it) for _ in range(7))
    h0_ref = next(it) if has_state else None
    y_ref, sN_ref, xp_ref, A_ref, B_ref, H_ref, C_ref = (next(it) for _ in range(7))

    S = L // SUBLANES
    W = LRU_TILE
    G = LRU_GROUPS
    R = LRU_ROWS
    left = LRU_CONV // 2
    pad = left * SUBLANES

    sub = lax.broadcasted_iota(jnp.int32, (SUBLANES, W), 0)
    for k in range(left):
        tail = pltpu.roll(x_ref[bi, L - pad + k * SUBLANES:L - pad + (k + 1) * SUBLANES, :], 1, axis=0)
        xp_ref[k * SUBLANES:(k + 1) * SUBLANES, :] = jnp.where(sub == 0, 0.0, tail)
    for k in range(LRU_CONV - 1 - left):
        head = pltpu.roll(x_ref[bi, k * SUBLANES:(k + 1) * SUBLANES, :], SUBLANES - 1, axis=0)
        xp_ref[pad + L + k * SUBLANES:pad + L + (k + 1) * SUBLANES, :] = jnp.where(sub == SUBLANES - 1, 0.0, head)

    def copy(c, carry):
        r0 = pl.multiple_of(c * R, R)
        xp_ref[pl.ds(pad + r0, R), :] = x_ref[bi, pl.ds(r0, R), :]
        return carry

    lax.fori_loop(0, L // R, copy, 0)

    csp = [(0.5 * LRU_C) * _softplus(-lam_ref[d]) for d in (0, 1)]

    def coeffs(c, carry):
        r0 = pl.multiple_of(c * R, R)
        u = jnp.broadcast_to(cb_ref[...], (R, W))
        for k in range(LRU_CONV):
            u = u + cw_ref[k:k + 1, :] * xp_ref[pl.ds(r0 + k * SUBLANES, R), :]
        half = jnp.dot(u.astype(BF16), wg_ref[0], preferred_element_type=F32) + gb_ref[0]
        hu = 0.5 * u
        for d in (0, 1):
            t_r = jnp.tanh(half[:, (2 * d) * W:(2 * d + 1) * W])
            t_i = jnp.tanh(half[:, (2 * d + 1) * W:(2 * d + 2) * W])
            nl = csp[d] * t_r + csp[d]
            a = jnp.exp(-nl)
            s2 = jnp.tanh(nl) * (a * a + 1.0)
            b = jnp.where(s2 > 0.0, s2 * lax.rsqrt(s2), 0.0) * (t_i * hu + hu)
            for gi in range(G):
                lanes = slice(gi * LANES, (gi + 1) * LANES)
                A_ref[d, gi, pl.ds(r0, R), :] = a[:, lanes]
                B_ref[d, gi, pl.ds(r0, R), :] = b[:, lanes]
        return carry

    lax.fori_loop(0, L // R, coeffs, 0, unroll=2)

    def scan_body(t, carry):
        out = []
        for d in (0, 1):
            r8 = pl.multiple_of((t if d == 0 else S - 1 - t) * SUBLANES, SUBLANES)
            for gi in range(G):
                h, acc = carry[d * G + gi]
                a = A_ref[d, gi, pl.ds(r8, SUBLANES), :]
                h = a * h + B_ref[d, gi, pl.ds(r8, SUBLANES), :]
                acc = acc * a
                H_ref[d, gi, pl.ds(r8, SUBLANES), :] = h
                C_ref[d, gi, pl.ds(r8, SUBLANES), :] = acc
                out.append((h, acc))
        return tuple(out)

    init = tuple((jnp.zeros((SUBLANES, LANES), F32), jnp.ones((SUBLANES, LANES), F32))
                 for _ in range(2 * G))
    lax.fori_loop(0, S, scan_body, init, unroll=4)

    hin = [None] * (2 * G)
    for d in (0, 1):
        last = (S - 1 if d == 0 else 0) * SUBLANES
        for gi in range(G):
            hl = H_ref[d, gi, last:last + SUBLANES, :]
            ac = C_ref[d, gi, last:last + SUBLANES, :]
            if has_state:
                h = h0_ref[bi, d:d + 1, gi * LANES:(gi + 1) * LANES]
            else:
                h = jnp.zeros((1, LANES), F32)
            rows = [None] * SUBLANES
            for j in (range(SUBLANES) if d == 0 else range(SUBLANES - 1, -1, -1)):
                rows[j] = h
                h = hl[j:j + 1, :] + ac[j:j + 1, :] * h
            hin[d * G + gi] = jnp.concatenate([jnp.concatenate(rows, axis=0)] * (R // SUBLANES), axis=0)
            sN_ref[bi, d:d + 1, gi * LANES:(gi + 1) * LANES] = h

    def combine(c, carry):
        r0 = pl.multiple_of(c * R, R)
        cols = []
        for gi in range(G):
            tot = jnp.zeros((R, LANES), F32)
            for d in (0, 1):
                tot = tot + (H_ref[d, gi, pl.ds(r0, R), :] + C_ref[d, gi, pl.ds(r0, R), :] * hin[d * G + gi])
            cols.append(tot)
        y_ref[bi, pl.ds(r0, R), :] = jnp.concatenate(cols, axis=1) * _silu(g_ref[bi, pl.ds(r0, R), :])
        return carry

    lax.fori_loop(0, L // R, combine, 0)


def _rglru(z4, conv_w, conv_b, gate_w, gate_b, lam, h0):
    B, S, _, _ = z4.shape
    L = S * SUBLANES
    has_state = h0 is not None
    W = LRU_TILE
    nh = LRU_WIDTH // W
    bpt = W // LRU_BLOCK_DIM
    gw = gate_w.reshape(2, 2, nh, bpt, LRU_BLOCK_DIM, LRU_BLOCK_DIM)
    eye = jnp.eye(bpt, dtype=F32)
    dense = (0.5 * jnp.einsum('dkhnij,nm->hnidkmj', gw, eye)).reshape(nh, W, 4 * W).astype(BF16)
    gb = 0.5 * gate_b.reshape(2, 2, nh, W).transpose(2, 0, 1, 3).reshape(nh, 1, 4 * W)
    zr = z4.reshape(B, L, D_LRU)
    NB = min(B, max(1, LRU_STEP_ROWS // L))
    assert B % NB == 0
    off = lambda base: pl.BlockSpec((NB, L, W), lambda h, b, base=base: (b, 0, base + h))
    in_specs = [off(0), off(LRU_WIDTH // W),
                pl.BlockSpec((LRU_CONV, W), lambda h, b: (0, h)),
                pl.BlockSpec((1, W), lambda h, b: (0, h)),
                pl.BlockSpec((1, W, 4 * W), lambda h, b: (h, 0, 0)),
                pl.BlockSpec((1, 1, 4 * W), lambda h, b: (h, 0, 0)),
                pl.BlockSpec((2, 1, W), lambda h, b: (0, 0, h))]
    args = [zr, zr, conv_w, conv_b.reshape(1, LRU_WIDTH), dense, gb, lam.reshape(2, 1, LRU_WIDTH)]
    if has_state:
        in_specs.append(pl.BlockSpec((NB, 2, W), lambda h, b: (b, 0, h)))
        args.append(h0)
    y, s = pl.pallas_call(
        functools.partial(_lru_kernel, L=L, has_state=has_state),
        grid=(nh, B // NB),
        in_specs=in_specs,
        out_specs=[pl.BlockSpec((NB, L, W), lambda h, b: (b, 0, h)),
                   pl.BlockSpec((NB, 2, W), lambda h, b: (b, 0, h))],
        out_shape=[jax.ShapeDtypeStruct((B, L, LRU_WIDTH), F32),
                   jax.ShapeDtypeStruct((B, 2, LRU_WIDTH), F32)],
        scratch_shapes=[pltpu.VMEM((L + (LRU_CONV - 1) * SUBLANES, W), F32)]
        + [pltpu.VMEM((2, LRU_GROUPS, L, LANES), F32) for _ in range(4)],
        compiler_params=_params(("arbitrary", "arbitrary")),
        name="rglru",
    )(*args)
    return y.reshape(B, S, SUBLANES, LRU_WIDTH), s


def _rope_tables(L):
    rows = L // GRID_W
    row = np.repeat(np.arange(rows, dtype=np.float64), GRID_W)
    col = np.tile(np.arange(GRID_W, dtype=np.float64), rows)
    n_f = RET_HEAD_DIM // 4
    inv = ROPE_BASE ** (-np.arange(n_f, dtype=np.float64) / n_f)
    ang = np.concatenate([row[:, None] * inv[None], col[:, None] * inv[None]], axis=-1)
    cos, sin = np.cos(ang), np.sin(ang)
    return (jnp.asarray(np.concatenate([cos, cos], axis=-1), F32),
            jnp.asarray(np.concatenate([-sin, sin], axis=-1), F32))


def kernel(x_prompt, x_sample, state_ret, state_lru, c, c_ctx, norm_g, ada_w, ada_b, w_in, ret_decay_logit, hy_conv_w, hy_conv_b, hy_ffn_w1, hy_ffn_b1, hy_ffn_w2, hy_ffn_b2, hy_ffn_w3, hy_freq, hy_bias, lru_conv_w, lru_conv_b, lru_gate_w, lru_gate_b, lru_lambda, w_out, final_g):
    Bp, Lp, _ = x_prompt.shape
    Bs, Ls, _ = x_sample.shape
    assert Bs + 1 <= SUBLANES

    cvec = jnp.zeros((SUBLANES, D_MODEL), F32).at[:Bs].set(c).at[Bs].set(c_ctx)
    mod = _modulation(cvec, ada_w, ada_b)
    rope = _rope_tables(Ls)
    dft = {L: _dft_matrices(min(DFT_MAX, L)) for L in {Lp, Ls}}
    w_in_b = w_in.astype(BF16)
    w_out_b = w_out.astype(BF16)
    fg = final_g.reshape(1, D_MODEL)

    xc, xl = x_prompt, x_sample
    new_state_ret, new_lru = None, []
    for l in range(DEPTH):
        shift, scale, gate = (mod[l, :, i * D_MODEL:(i + 1) * D_MODEL] for i in range(3))
        g = norm_g[l].reshape(1, D_MODEL)
        final = l == DEPTH - 1
        filt = {}
        for L in sorted({Lp, Ls}):
            filt[L] = _hyena_filter_spectra(L, hy_ffn_w1[l], hy_ffn_b1[l], hy_ffn_w2[l], hy_ffn_b2[l],
                                            hy_ffn_w3[l], hy_freq[l], hy_bias[l], dft[L][0])

        def layer(x, sel, rope_t, ret_s0, lru_s0, ret_states):
            sh, sc, gt = (m[sel][:, None, :] for m in (shift, scale, gate))
            z, z_lru = _in_proj(x, sh, sc, g, w_in_b, l)
            y_ret, *s_ret = _retention(z, ret_decay_logit[l], rope_t, ret_s0, l, ret_states)
            kre, kim = filt[x.shape[1]]
            y_hy = _hyena(z, hy_conv_w[l], hy_conv_b[l], kre, kim, *dft[x.shape[1]])
            y_lru, s_lru = _rglru(z_lru, lru_conv_w[l], lru_conv_b[l], lru_gate_w[l], lru_gate_b[l],
                                  lru_lambda[l], lru_s0)
            return _out_proj(y_ret, y_hy, y_lru, x, gt, w_out_b, l, fg, final), s_ret, s_lru

        xc, (new_state_ret,), ls = layer(xc, slice(Bs, Bs + 1), None, None, None,
                                         "new" if l == 0 else new_state_ret)
        new_lru.append(ls)
        xl, _, _ = layer(xl, slice(0, Bs), rope, state_ret, state_lru[:, l], None)

    new_state_lru = jnp.stack(new_lru, axis=1).astype(x_prompt.dtype)
    return (xc, xl, new_state_ret.astype(x_prompt.dtype), new_state_lru)
```

```python
import functools
import math

import numpy as np
import jax
import jax.numpy as jnp
from jax import lax
from jax.experimental import pallas as pl
from jax.experimental.pallas import tpu as pltpu

F32 = jnp.float32
BF16 = jnp.bfloat16
HI = lax.Precision.HIGHEST

D_MODEL = 1024
DEPTH = 2
GRID_W = 64
EPS = 1e-6
RET_HEADS = 4
RET_HEAD_DIM = 128
RET_WIDTH = RET_HEADS * RET_HEAD_DIM
ROPE_BASE = 10000.0
HY_WIDTH = 512
HY_ORDER = 2
HY_SHORT = 3
HY_EMB = 33
HY_FFN = 64
HY_SHORT_DECAY_PCT = 0.3
HY_LONG_DECAY_PCT = 1.5
HY_TARGET = 1e-2
LRU_WIDTH = 512
LRU_BLOCKS = 8
LRU_BLOCK_DIM = LRU_WIDTH // LRU_BLOCKS
LRU_CONV = 4
LRU_C = 8.0
D_MIX = RET_WIDTH + HY_WIDTH + LRU_WIDTH
D_IN = 4 * RET_WIDTH + 4 * HY_WIDTH + 2 * LRU_WIDTH

LANES = 128
SUBLANES = 8
DFT_MAX = 512
ROW_TILE = 256
VMEM_LIMIT = 56 * 1024 * 1024

_Q0, _K0, _V0, _GR0 = 0, 4, 8, 12
_HV0, _HX10, _HX20, _GH0 = 16, 20, 24, 28


def _sigmoid(x):
    return 0.5 * jnp.tanh(0.5 * x) + 0.5


def _silu(x):
    return x * _sigmoid(x)


def _softplus(x):
    return jnp.maximum(x, 0.0) + jnp.log1p(jnp.exp(-jnp.abs(x)))


def _params(sem):
    return pltpu.CompilerParams(dimension_semantics=sem, vmem_limit_bytes=VMEM_LIMIT)


def _mod_kernel(c_ref, w_ref, b_ref, o_ref):
    s_hi, s_lo = _split_bf16(_silu(c_ref[...]))
    w_hi, w_lo = _split_bf16(w_ref[0])
    o_ref[0] = (jnp.dot(s_hi, w_hi, preferred_element_type=F32)
                + jnp.dot(s_lo, w_hi, preferred_element_type=F32)
                + jnp.dot(s_hi, w_lo, preferred_element_type=F32)) + b_ref[0]


def _modulation(cvec, ada_w, ada_b):
    tn = 768
    return pl.pallas_call(
        _mod_kernel,
        grid=(DEPTH, 3 * D_MODEL // tn),
        in_specs=[pl.BlockSpec((SUBLANES, D_MODEL), lambda l, j: (0, 0)),
                  pl.BlockSpec((1, D_MODEL, tn), lambda l, j: (l, 0, j)),
                  pl.BlockSpec((1, 1, tn), lambda l, j: (l, 0, j))],
        out_specs=pl.BlockSpec((1, SUBLANES, tn), lambda l, j: (l, 0, j)),
        out_shape=jax.ShapeDtypeStruct((DEPTH, SUBLANES, 3 * D_MODEL), F32),
        compiler_params=_params(("arbitrary", "arbitrary")),
        name="modulation",
    )(cvec, ada_w, ada_b.reshape(DEPTH, 1, 3 * D_MODEL))


D_MAIN = 4 * RET_WIDTH + 4 * HY_WIDTH
D_LRU = 2 * LRU_WIDTH


def _in_kernel(x_ref, sh_ref, sc_ref, g_ref, w_ref, o_ref, ol_ref):
    _, nseg, rows, _ = x_ref.shape
    x = x_ref[0].reshape(nseg * rows, D_MODEL)
    ms = jnp.mean(x * x, axis=-1, keepdims=True)
    y = x * lax.rsqrt(ms + EPS) * g_ref[...]
    h = (y * (1.0 + sc_ref[0]) + sh_ref[0]).astype(BF16)
    tn = 1024
    for n in range(D_MAIN // tn):
        z = jnp.dot(h, w_ref[:, n * tn:(n + 1) * tn], preferred_element_type=F32)
        o_ref[0, :, :, n * tn:(n + 1) * tn] = z.reshape(nseg, rows, tn)
    zl = jnp.dot(h, w_ref[:, D_MAIN:D_IN], preferred_element_type=F32)
    for j in range(nseg):
        ol_ref[0, :, j, :] = zl[j * rows:(j + 1) * rows]


def _in_proj(x, shift, scale, g, w_bf16, layer):
    B, L, _ = x.shape
    S = L // SUBLANES
    rows = min(ROW_TILE // SUBLANES, S)
    per_batch = shift.shape[0] > 1
    mod_map = (lambda b, i: (b, 0, 0)) if per_batch else (lambda b, i: (0, 0, 0))
    seg = lambda C: pl.BlockSpec((1, SUBLANES, rows, C), lambda b, i: (b, 0, i, 0))
    z, z_lru = pl.pallas_call(
        _in_kernel,
        grid=(B, S // rows),
        in_specs=[seg(D_MODEL),
                  pl.BlockSpec((1, 1, D_MODEL), mod_map),
                  pl.BlockSpec((1, 1, D_MODEL), mod_map),
                  pl.BlockSpec((1, D_MODEL), lambda b, i: (0, 0)),
                  pl.BlockSpec((None, D_MODEL, D_IN), lambda b, i: (layer, 0, 0),
                               pipeline_mode=pl.Buffered(1))],
        out_specs=[seg(D_MAIN), pl.BlockSpec((1, rows, SUBLANES, D_LRU), lambda b, i: (b, i, 0, 0))],
        out_shape=[jax.ShapeDtypeStruct((B, SUBLANES, S, D_MAIN), F32),
                   jax.ShapeDtypeStruct((B, S, SUBLANES, D_LRU), F32)],
        compiler_params=_params(("arbitrary", "arbitrary")),
        name="in_proj",
    )(x.reshape(B, SUBLANES, S, D_MODEL), shift, scale, g, w_bf16)
    return z.reshape(B, L, D_MAIN), z_lru


OUT_ROWS = 1024


def _out_kernel(yr_ref, yh_ref, yl_ref, x_ref, gate_ref, w_ref, fg_ref, o_ref, *, final):
    nb, nseg, rows, _ = x_ref.shape
    n = nb * nseg * rows
    acc = jnp.dot(yr_ref[...].reshape(n, RET_WIDTH), w_ref[0:RET_WIDTH], preferred_element_type=F32)
    acc = acc + jnp.dot(yh_ref[...].reshape(n, HY_WIDTH), w_ref[RET_WIDTH:RET_WIDTH + HY_WIDTH],
                        preferred_element_type=F32)
    yl = jnp.concatenate([yl_ref[bi, :, j, :] for bi in range(nb) for j in range(nseg)], axis=0)
    acc = acc + jnp.dot(yl.astype(BF16), w_ref[RET_WIDTH + HY_WIDTH:D_MIX], preferred_element_type=F32)
    x = x_ref[...].reshape(n, D_MODEL) + gate_ref[0] * acc
    if final:
        ms = jnp.mean(x * x, axis=-1, keepdims=True)
        x = x * lax.rsqrt(ms + EPS) * fg_ref[...]
    o_ref[...] = x.reshape(nb, nseg, rows, D_MODEL)


def _out_proj(y_ret, y_hy, y_lru, x, gate, w_bf16, layer, final_g, final):
    B, L, _ = x.shape
    S = L // SUBLANES
    rows = min(OUT_ROWS // SUBLANES, S)
    per_batch = gate.shape[0] > 1
    mod_map = (lambda b, i: (b, 0, 0)) if per_batch else (lambda b, i: (0, 0, 0))
    NB = 1 if per_batch else min(B, max(1, OUT_ROWS // (SUBLANES * rows)))
    assert B % NB == 0
    seg = lambda C: pl.BlockSpec((NB, SUBLANES, rows, C), lambda b, i: (b, 0, i, 0))
    view = lambda a: a.reshape(B, SUBLANES, S, a.shape[-1])
    out = pl.pallas_call(
        functools.partial(_out_kernel, final=final),
        grid=(B // NB, S // rows),
        in_specs=[seg(RET_WIDTH), seg(HY_WIDTH),
                  pl.BlockSpec((NB, rows, SUBLANES, LRU_WIDTH), lambda b, i: (b, i, 0, 0)),
                  seg(D_MODEL),
                  pl.BlockSpec((1, 1, D_MODEL), mod_map),
                  pl.BlockSpec((None, D_MIX, D_MODEL), lambda b, i: (layer, 0, 0),
                               pipeline_mode=pl.Buffered(1)),
                  pl.BlockSpec((1, D_MODEL), lambda b, i: (0, 0))],
        out_specs=seg(D_MODEL),
        out_shape=jax.ShapeDtypeStruct((B, SUBLANES, S, D_MODEL), F32),
        compiler_params=_params(("arbitrary", "arbitrary")),
        name="out_proj",
    )(view(y_ret), view(y_hy), y_lru, view(x), gate, w_bf16, final_g)
    return out.reshape(B, L, D_MODEL)


RET_HEAD_ROWS = 2 * 2048
RET_BLOCK = 256


def _ret_kernel(*refs, L, use_rope, has_state, emit_state, chained, layer):
    it = iter(refs)
    q_ref, k_ref, v_ref, g_ref = next(it), next(it), next(it), next(it)
    cos_ref = sin_ref = s0_ref = sN_ref = None
    if use_rope:
        cos_ref, sin_ref = next(it), next(it)
    lg_ref = next(it)
    if has_state:
        s0_ref = next(it)
    if chained:
        next(it)
    y_ref = next(it)
    if emit_state:
        sN_ref = next(it)
    R_ref, M_ref, Z_ref, D_ref, KV_ref, kb_ref = (next(it) for _ in range(6))

    C = min(RET_BLOCK, L)
    DH = RET_HEAD_DIM
    HP = q_ref.shape[-1] // DH
    n = L // C
    cross = has_state or n > 1
    kscale = DH ** -0.5

    @pl.when(pl.program_id(1) == 0)
    def _():
        ii = lax.broadcasted_iota(jnp.int32, (C, C), 0)
        jj = lax.broadcasted_iota(jnp.int32, (C, C), 1)
        diff = (ii - jj).astype(F32)
        ri = lax.broadcasted_iota(jnp.int32, (C, DH), 0).astype(F32)
        for hh in range(HP):
            lgf = -_softplus(-lg_ref[0, hh])
            lgb = -_softplus(-lg_ref[1, hh])
            M_ref[hh] = (jnp.where(diff >= 0, jnp.exp(lgf[:, :C] * jnp.maximum(diff, 0.0)), 0.0)
                         + jnp.where(diff <= 0, jnp.exp(lgb[:, :C] * jnp.maximum(-diff, 0.0)), 0.0))
            lf, lb = lgf[:, :DH], lgb[:, :DH]
            Z_ref[hh, 0] = jnp.concatenate([jnp.exp(lf * (C - 1.0 - ri)), jnp.exp(lb * ri)], axis=1)
            Z_ref[hh, 1] = jnp.concatenate([jnp.exp(lf * (ri + 1.0)), jnp.exp(lb * (C - ri))], axis=1)
            D_ref[hh] = jnp.exp(jnp.concatenate([jnp.broadcast_to(lf * C, (DH, DH)),
                                                 jnp.broadcast_to(lb * C, (DH, DH))], axis=0))

    def rotary(x, r0):
        if not use_rope:
            return x
        return x * cos_ref[pl.ds(r0, C), :] + pltpu.roll(x, 64, axis=1) * sin_ref[pl.ds(r0, C), :]

    def chunk_kv(c, carry):
        r0 = pl.multiple_of(c * C, C)
        for hh in range(HP):
            lanes = slice(hh * DH, (hh + 1) * DH)
            k = rotary(k_ref[0, pl.ds(r0, C), lanes] * kscale, r0)
            kb_ref[hh, pl.ds(r0, C), :] = k.astype(BF16)
            kz = (jnp.concatenate([k, k], axis=1) * Z_ref[hh, 0]).astype(BF16)
            vb = v_ref[0, pl.ds(r0, C), lanes].astype(BF16)
            KV_ref[hh, c] = lax.dot_general(kz, vb, (((0,), (0,)), ((), ())), preferred_element_type=F32)
        return carry

    lax.fori_loop(0, n, chunk_kv, 0, unroll=2 if n % 2 == 0 else 1)

    for d in (0, 1):
        for hh in range(HP):
            R_ref[d, hh] = s0_ref[0, d, hh] if has_state else jnp.zeros((DH, DH), F32)

    def states(t, carry):
        for d in (0, 1):
            c = t if d == 0 else n - 1 - t
            rows = slice(d * DH, (d + 1) * DH)
            for hh in range(HP):
                R = R_ref[d, hh]
                R_ref[d, hh] = D_ref[hh, rows, :] * R + KV_ref[hh, c, rows, :]
                KV_ref[hh, c, rows, :] = R
        return carry

    lax.fori_loop(0, n, states, 0)
    if emit_state:
        for l in range(sN_ref.shape[1]):
            if chained or l == layer:
                for d in (0, 1):
                    for hh in range(HP):
                        sN_ref[0, l, d, hh] = R_ref[d, hh]
            else:
                sN_ref[0, l] = jnp.zeros(sN_ref.shape[2:], F32)

    def chunk_out(c, carry):
        r0 = pl.multiple_of(c * C, C)
        cols = []
        for hh in range(HP):
            lanes = slice(hh * DH, (hh + 1) * DH)
            q = rotary(q_ref[0, pl.ds(r0, C), lanes], r0)
            vb = v_ref[0, pl.ds(r0, C), lanes].astype(BF16)
            s = lax.dot_general(q.astype(BF16), kb_ref[hh, pl.ds(r0, C), :], (((1,), (1,)), ((), ())),
                                preferred_element_type=F32) * M_ref[hh]
            o = jnp.dot(s.astype(BF16), vb, preferred_element_type=F32)
            if cross:
                qx = (jnp.concatenate([q, q], axis=1) * Z_ref[hh, 1]).astype(BF16)
                o = o + jnp.dot(qx, KV_ref[hh, c].astype(BF16), preferred_element_type=F32)
            cols.append(o * lax.rsqrt(jnp.mean(o * o, axis=-1, keepdims=True) + EPS))
        g = g_ref[0, pl.ds(r0, C), :]
        y_ref[0, pl.ds(r0, C), :] = (jnp.concatenate(cols, axis=1) * _silu(g)).astype(BF16)
        return carry

    lax.fori_loop(0, n, chunk_out, 0, unroll=2 if n % 2 == 0 else 1)


def _retention(z, decay_logit, rope, state0, layer, states):
    B, L, _ = z.shape
    use_rope = rope is not None
    has_state = state0 is not None
    emit_state = states is not None
    chained = emit_state and not isinstance(states, str)
    H = RET_HEADS
    HP = min(H, max(1, RET_HEAD_ROWS // L))
    W = HP * RET_HEAD_DIM
    col = lambda off: pl.BlockSpec((1, L, W), lambda h, b, off=off: (b, 0, off * LANES // W + h))
    in_specs = [col(_Q0), col(_K0), col(_V0), col(_GR0)]
    args = [z, z, z, z]
    if use_rope:
        in_specs += [pl.BlockSpec((L, LANES), lambda h, b: (0, 0))] * 2
        args += list(rope)
    C = min(RET_BLOCK, L)
    in_specs.append(pl.BlockSpec((2, HP, 1, RET_BLOCK), lambda h, b: (0, h, 0, 0)))
    args.append(jnp.broadcast_to(decay_logit[:, :, None, None], (2, H, 1, RET_BLOCK)))
    if has_state:
        in_specs.append(pl.BlockSpec((1, None, 2, HP, RET_HEAD_DIM, RET_HEAD_DIM),
                                     lambda h, b: (b, layer, 0, h, 0, 0)))
        args.append(state0)
    out_specs = [pl.BlockSpec((1, L, W), lambda h, b: (b, 0, h))]
    out_shape = [jax.ShapeDtypeStruct((B, L, RET_WIDTH), BF16)]
    aliases = {}
    if emit_state:
        nl, l0 = (1, layer) if chained else (DEPTH, 0)
        out_specs.append(pl.BlockSpec((1, nl, 2, HP, RET_HEAD_DIM, RET_HEAD_DIM),
                                      lambda h, b: (b, l0, 0, h, 0, 0)))
        out_shape.append(jax.ShapeDtypeStruct((B, DEPTH, 2, H, RET_HEAD_DIM, RET_HEAD_DIM), F32))
        if chained:
            aliases = {len(args): 1}
            in_specs.append(pl.BlockSpec(memory_space=pl.ANY))
            args.append(states)
    return pl.pallas_call(
        functools.partial(_ret_kernel, L=L, use_rope=use_rope, has_state=has_state,
                          emit_state=emit_state, chained=chained, layer=layer),
        grid=(H // HP, B),
        in_specs=in_specs,
        out_specs=out_specs,
        out_shape=out_shape,
        input_output_aliases=aliases,
        scratch_shapes=[pltpu.VMEM((2, HP, RET_HEAD_DIM, RET_HEAD_DIM), F32),
                        pltpu.VMEM((HP, C, C), F32),
                        pltpu.VMEM((HP, 2, C, 2 * RET_HEAD_DIM), F32),
                        pltpu.VMEM((HP, 2 * RET_HEAD_DIM, RET_HEAD_DIM), F32),
                        pltpu.VMEM((HP, L // C, 2 * RET_HEAD_DIM, RET_HEAD_DIM), F32),
                        pltpu.VMEM((HP, L, RET_HEAD_DIM), BF16)],
        compiler_params=_params(("arbitrary", "arbitrary")),
        name="retention",
    )(*args)


def _dft_matrices(T):
    N = 2 * T
    k = np.arange(T, dtype=np.float64)[:, None]
    m = np.arange(T, dtype=np.float64)[None, :]
    ang = 2.0 * np.pi * k * m / N
    fwd = np.concatenate([np.cos(ang), -np.sin(ang)], axis=0)
    fwd[T, :] = (-1.0) ** np.arange(T)
    ck = np.full((T,), 2.0)
    ck[0] = 1.0
    inv_re = (np.cos(ang) * ck[:, None]).T / N
    inv_im = (-2.0 * np.sin(ang)).T / N
    inv_im[:, 0] = ((-1.0) ** np.arange(T)) / N
    inv = np.concatenate([inv_re, inv_im], axis=1)
    return jnp.asarray(fwd, F32).astype(BF16), jnp.asarray(inv, F32).astype(BF16)


def _split_bf16(x):
    hi = x.astype(BF16)
    return hi, (x - hi.astype(F32)).astype(BF16)


def _dot1(a, b):
    return jnp.dot(a, b.astype(BF16), preferred_element_type=F32)


def _filt_kernel(zemb_ref, w1_ref, b1_ref, w2_ref, b2_ref, fr_ref, w3f_ref, w3b_ref, dec_ref,
                 bias_ref, fwd_ref, kre_ref, kim_ref, hid_ref, hn_ref, F_ref, *, L):
    T = fwd_ref.shape[1]
    nb = L // T

    @pl.when((pl.program_id(0) == 0) & (pl.program_id(1) == 0))
    def _():
        z1 = jnp.dot(zemb_ref[...], w1_ref[...], precision=HI, preferred_element_type=F32) + b1_ref[...]
        h1 = jnp.sin(fr_ref[0:1, :] * z1)
        z2 = jnp.dot(h1, w2_ref[...], precision=HI, preferred_element_type=F32) + b2_ref[...]
        hid = jnp.sin(fr_ref[1:2, :] * z2)
        hid_ref[0], hid_ref[1] = _split_bf16(hid)

    w3 = jnp.concatenate([w3f_ref[...], w3b_ref[...]], axis=1)
    w_hi, w_lo = _split_bf16(w3)
    h = (jnp.dot(hid_ref[0], w_hi, preferred_element_type=F32)
         + jnp.dot(hid_ref[1], w_hi, preferred_element_type=F32)
         + jnp.dot(hid_ref[0], w_lo, preferred_element_type=F32))
    h = h * jnp.concatenate([dec_ref[...]] * 2, axis=1)
    h = h / (jnp.sum(jnp.abs(h), axis=0, keepdims=True) + EPS)
    for dr in (0, 1):
        hn_ref[dr] = h[:, dr * LANES:(dr + 1) * LANES]
    for blk in range(nb):
        F = _dot1(fwd_ref[...], h[blk * T:(blk + 1) * T])
        for dr in (0, 1):
            F_ref[dr, blk] = F[:, dr * LANES:(dr + 1) * LANES]

    row = lax.broadcasted_iota(jnp.int32, (T, LANES), 0)
    sgn = jnp.where((row & 1) == 0, 1.0, -1.0).astype(F32)
    row0 = row == 0
    bias = bias_ref[0]
    for d in range(-(nb - 1), nb):
        re = jnp.zeros((T, LANES), F32)
        im = jnp.zeros((T, LANES), F32)
        nyq = jnp.zeros((1, LANES), F32)
        if d >= 0:
            F = F_ref[0, d]
            re, im, nyq = re + F[:T], im + F[T:], nyq + F[T:T + 1]
        if d >= 1:
            F = F_ref[0, d - 1]
            p0 = hn_ref[0, (d - 1) * T:(d - 1) * T + 1, :]
            re, im, nyq = re + sgn * (F[:T] - p0), im + sgn * F[T:], nyq + (F[T:T + 1] - p0)
        e = -d
        if e >= 0:
            F = F_ref[1, e]
            re, im, nyq = re + F[:T], im - F[T:], nyq + F[T:T + 1]
        if e >= 1:
            F = F_ref[1, e - 1]
            p0 = hn_ref[1, (e - 1) * T:(e - 1) * T + 1, :]
            re, im, nyq = re + sgn * (F[:T] - p0), im - sgn * F[T:], nyq + (F[T:T + 1] - p0)
        if d == 0:
            re, nyq = re + bias, nyq + bias
        kre_ref[0, d + nb - 1] = re
        kim_ref[0, d + nb - 1] = jnp.where(row0, nyq, im)


def _hyena_filter_spectra(L, w1, b1, w2, b2, w3, freq, bias, fwd):
    T = fwd.shape[1]
    nb = L // T
    nlag = 2 * nb - 1
    P = LANES
    t = np.linspace(0.0, 1.0, L)[:, None]
    n_bands = (HY_EMB - 1) // 2
    f = np.linspace(1e-4, n_bands - 1, n_bands)
    ang = (2.0 * math.pi / L) * np.arange(L)[:, None] * f[None, :]
    zemb = np.concatenate([t, np.cos(ang), -np.sin(ang)], axis=-1)
    zemb = jnp.asarray(np.pad(zemb, ((0, 0), (0, P - HY_EMB))), F32)
    min_decay = math.log(HY_TARGET) / HY_LONG_DECAY_PCT
    max_decay = math.log(HY_TARGET) / HY_SHORT_DECAY_PCT
    deltas = np.abs(np.linspace(min_decay, max_decay, HY_WIDTH))
    dec = jnp.asarray(np.exp(-t * deltas[None, :]), F32)
    pad = P - HY_FFN
    w1p = jnp.pad(w1, ((0, P - HY_EMB), (0, pad)))
    b1p = jnp.pad(b1, (0, pad)).reshape(1, P)
    w2p = jnp.pad(w2, ((0, pad), (0, pad)))
    b2p = jnp.pad(b2, (0, pad)).reshape(1, P)
    w3p = jnp.pad(w3, ((0, pad), (0, 0)))
    frp = jnp.pad(freq, ((0, 0), (0, pad)))
    ncg = HY_WIDTH // P
    const = lambda shape: pl.BlockSpec(shape, lambda o, c: tuple(0 for _ in shape))
    kshape = jax.ShapeDtypeStruct((HY_ORDER, nlag, T, HY_WIDTH), F32)
    kspec = pl.BlockSpec((1, nlag, T, P), lambda o, c: (o, 0, 0, c))
    return pl.pallas_call(
        functools.partial(_filt_kernel, L=L),
        grid=(HY_ORDER, ncg),
        in_specs=[const((L, P)), const((P, P)), const((1, P)), const((P, P)), const((1, P)), const((2, P)),
                  pl.BlockSpec((P, P), lambda o, c: (0, o * 2 * ncg + c)),
                  pl.BlockSpec((P, P), lambda o, c: (0, o * 2 * ncg + ncg + c)),
                  pl.BlockSpec((L, P), lambda o, c: (0, c)),
                  pl.BlockSpec((1, 1, P), lambda o, c: (o, 0, c)),
                  const((2 * T, T))],
        out_specs=[kspec, kspec],
        out_shape=[kshape, kshape],
        scratch_shapes=[pltpu.VMEM((2, L, P), BF16), pltpu.VMEM((2, L, P), F32),
                        pltpu.VMEM((2, nb, 2 * T, P), F32)],
        compiler_params=_params(("arbitrary", "arbitrary")),
        name="hyena_filters",
    )(zemb, w1p, b1p, w2p, b2p, frp, w3p, w3p, dec, bias.reshape(HY_ORDER, 1, HY_WIDTH), fwd)


CONV_ROWS = 128


def _conv_block(x_ref, t0, w, b, left):
    L, C = x_ref.shape
    T = CONV_ROWS
    blk = x_ref[t0:t0 + T, :]
    prev = x_ref[t0 - SUBLANES:t0, :] if t0 > 0 else jnp.zeros((SUBLANES, C), F32)
    nxt = x_ref[t0 + T:t0 + T + SUBLANES, :] if t0 + T < L else jnp.zeros((SUBLANES, C), F32)
    ext = jnp.concatenate([prev, blk, nxt], axis=0)
    n = T + 2 * SUBLANES
    acc = jnp.broadcast_to(b, (T, C))
    for tap in range(w.shape[0]):
        s = tap - left
        sh = blk if s == 0 else pltpu.roll(ext, (-s) % n, axis=0)[SUBLANES:SUBLANES + T]
        acc = acc + w[tap:tap + 1, :] * sh
    return acc


HY_LANES_ROWS = 2 * 2048
HY_MAX_BATCH = 8


def _toeplitz(k, u, mul):
    n = len(u)
    if n == 1:
        return [mul(k[0], u[0])]
    h = n // 2
    add = lambda a, b: tuple(x + y for x, y in zip(a, b))
    sub = lambda a, b: tuple(x - y for x, y in zip(a, b))
    t0, t1, t2 = k[h:h + n - 1], k[0:n - 1], k[n:2 * n - 1]
    p = _toeplitz(t0, [add(a, b) for a, b in zip(u[:h], u[h:])], mul)
    q = _toeplitz([sub(a, b) for a, b in zip(t1, t0)], u[h:], mul)
    r = _toeplitz([sub(a, b) for a, b in zip(t2, t0)], u[:h], mul)
    return [add(a, b) for a, b in zip(p, q)] + [add(a, b) for a, b in zip(p, r)]


def _hy_kernel(v_ref, x1_ref, x2_ref, g_ref, wv_ref, w1_ref, w2_ref, bv_ref, b1_ref, b2_ref,
               kre_ref, kim_ref, fwd_ref, inv_ref, y_ref,
               x1c_ref, x2c_ref, u_ref, U_ref, Y_ref, *, L):
    T = fwd_ref.shape[1]
    nb = L // T
    NBT, _, P = v_ref.shape
    left = (HY_SHORT - 1) // 2
    def block_spectrum(j, r0):
        U_ref[j] = jnp.dot(fwd_ref[...], u_ref[pl.ds(r0, T), :], preferred_element_type=F32)

    for j in range(nb):
        for t0 in range(j * T, (j + 1) * T, CONV_ROWS):
            rows = slice(t0, t0 + CONV_ROWS)
            for p in range(NBT):
                u_ref[rows, p * P:(p + 1) * P] = _conv_block(v_ref.at[p], t0, wv_ref[...], bv_ref[...], left).astype(BF16)
                x1c_ref[p, rows, :] = _conv_block(x1_ref.at[p], t0, w1_ref[...], b1_ref[...], left)
                x2c_ref[p, rows, :] = _conv_block(x2_ref.at[p], t0, w2_ref[...], b2_ref[...], left)
        block_spectrum(j, j * T)

    row0 = lax.broadcasted_iota(jnp.int32, (SUBLANES, NBT * P), 0) == 0
    tile = lambda x: jnp.concatenate([x] * NBT, axis=1)

    def cmul(k, u):
        kr, ki = tile(k[0]), tile(k[1])
        return (kr * u[0] - ki * u[1], kr * u[1] + ki * u[0])

    def rmul(k, u):
        return (tile(k[0]) * u[0], tile(k[1]) * u[1])

    def spectra_products(o):
        def load(r, rows):
            k = [(kre_ref[o, l, pl.ds(r, rows), :], kim_ref[o, l, pl.ds(r, rows), :]) for l in range(2 * nb - 1)]
            u = [(U_ref[j, pl.ds(r, rows), :], U_ref[j, pl.ds(T + r, rows), :]) for j in range(nb)]
            return k, u

        def chunk(rc, carry):
            r = pl.multiple_of(rc * SUBLANES, SUBLANES)
            y = _toeplitz(*load(r, SUBLANES), cmul)
            for i in range(nb):
                Y_ref[i, pl.ds(r, SUBLANES), :] = y[i][0]
                Y_ref[i, pl.ds(T + r, SUBLANES), :] = y[i][1]
            return carry

        lax.fori_loop(0, T // SUBLANES, chunk, 0, unroll=2)
        y = _toeplitz(*load(0, SUBLANES), rmul)
        for i in range(nb):
            Y_ref[i, 0:SUBLANES, :] = jnp.where(row0, y[i][0], Y_ref[i, 0:SUBLANES, :])
            Y_ref[i, T:T + SUBLANES, :] = jnp.where(row0, y[i][1], Y_ref[i, T:T + SUBLANES, :])

    def long_conv_block(i):
        return _dot1(inv_ref[...], Y_ref[i])

    spectra_products(0)

    def order0(i, carry):
        r0 = pl.multiple_of(i * T, T)
        y = long_conv_block(i)
        for p in range(NBT):
            z1 = x1c_ref[p, pl.ds(r0, T), :] * y[:, p * P:(p + 1) * P]
            u_ref[pl.ds(r0, T), p * P:(p + 1) * P] = z1.astype(BF16)
        return carry

    lax.fori_loop(0, nb, order0, 0, unroll=2 if nb % 2 == 0 else 1)
    for j in range(nb):
        block_spectrum(j, j * T)
    spectra_products(1)

    def order1(i, carry):
        r0 = pl.multiple_of(i * T, T)
        y = long_conv_block(i)
        for p in range(NBT):
            out = x2c_ref[p, pl.ds(r0, T), :] * y[:, p * P:(p + 1) * P]
            y_ref[p, pl.ds(r0, T), :] = (out * _silu(g_ref[p, pl.ds(r0, T), :])).astype(BF16)
        return carry

    lax.fori_loop(0, nb, order1, 0, unroll=2 if nb % 2 == 0 else 1)


def _hyena(z, conv_w, conv_b, kre, kim, fwd, inv):
    B, L, _ = z.shape
    NBT = min(B, max(2, HY_LANES_ROWS // L), HY_MAX_BATCH)
    assert B % NBT == 0
    T = fwd.shape[1]
    nb = L // T
    nlag = 2 * nb - 1
    P = LANES * (2 if 2 * NBT * L <= HY_LANES_ROWS else 1)
    ncg = HY_WIDTH // P
    col = lambda off: pl.BlockSpec((NBT, L, P), lambda c, b, off=off: (b, 0, off * LANES // P + c))
    wspec = lambda part: pl.BlockSpec((HY_SHORT, P), lambda c, b, part=part: (0, part * ncg + c))
    bspec = lambda part: pl.BlockSpec((1, P), lambda c, b, part=part: (0, part * ncg + c))
    kspec = pl.BlockSpec((HY_ORDER, nlag, T, P), lambda c, b: (0, 0, 0, c))
    fspec = pl.BlockSpec((2 * T, T), lambda c, b: (0, 0))
    ispec = pl.BlockSpec((T, 2 * T), lambda c, b: (0, 0))
    cb = conv_b.reshape(1, 3 * HY_WIDTH)
    return pl.pallas_call(
        functools.partial(_hy_kernel, L=L),
        grid=(ncg, B // NBT),
        in_specs=[col(_HV0), col(_HX10), col(_HX20), col(_GH0),
                  wspec(0), wspec(1), wspec(2), bspec(0), bspec(1), bspec(2),
                  kspec, kspec, fspec, ispec],
        out_specs=pl.BlockSpec((NBT, L, P), lambda c, b: (b, 0, c)),
        out_shape=jax.ShapeDtypeStruct((B, L, HY_WIDTH), BF16),
        scratch_shapes=[pltpu.VMEM((NBT, L, P), F32), pltpu.VMEM((NBT, L, P), F32),
                        pltpu.VMEM((L, NBT * P), BF16),
                        pltpu.VMEM((nb, 2 * T, NBT * P), F32), pltpu.VMEM((nb, 2 * T, NBT * P), F32)],
        compiler_params=_params(("arbitrary", "arbitrary")),
        name="hyena",
    )(z, z, z, z, conv_w, conv_w, conv_w, cb, cb, cb, kre, kim, fwd, inv)


LRU_TILE = 256
LRU_GROUPS = LRU_TILE // LANES
LRU_ROWS = 128
LRU_STEP_ROWS = 1024


def _lru_kernel(*refs, L, has_state):
    for bi in range(refs[0].shape[0]):
        _lru_one(*refs, L=L, has_state=has_state, bi=bi)


def _lru_one(*refs, L, has_state, bi):
    it = iter(refs)
    x_ref, g_ref, cw_ref, cb_ref, wg_ref, gb_ref, lam_ref = (next(it) for _ in range(7))
    h0_ref = next(it) if has_state else None
    y_ref, sN_ref, xp_ref, A_ref, B_ref, H_ref, C_ref = (next(it) for _ in range(7))

    S = L // SUBLANES
    W = LRU_TILE
    G = LRU_GROUPS
    R = LRU_ROWS
    left = LRU_CONV // 2
    pad = left * SUBLANES

    sub = lax.broadcasted_iota(jnp.int32, (SUBLANES, W), 0)
    for k in range(left):
        tail = pltpu.roll(x_ref[bi, L - pad + k * SUBLANES:L - pad + (k + 1) * SUBLANES, :], 1, axis=0)
        xp_ref[k * SUBLANES:(k + 1) * SUBLANES, :] = jnp.where(sub == 0, 0.0, tail)
    for k in range(LRU_CONV - 1 - left):
        head = pltpu.roll(x_ref[bi, k * SUBLANES:(k + 1) * SUBLANES, :], SUBLANES - 1, axis=0)
        xp_ref[pad + L + k * SUBLANES:pad + L + (k + 1) * SUBLANES, :] = jnp.where(sub == SUBLANES - 1, 0.0, head)

    def copy(c, carry):
        r0 = pl.multiple_of(c * R, R)
        xp_ref[pl.ds(pad + r0, R), :] = x_ref[bi, pl.ds(r0, R), :]
        return carry

    lax.fori_loop(0, L // R, copy, 0)

    csp = [(0.5 * LRU_C) * _softplus(-lam_ref[d]) for d in (0, 1)]

    def coeffs(c, carry):
        r0 = pl.multiple_of(c * R, R)
        u = jnp.broadcast_to(cb_ref[...], (R, W))
        for k in range(LRU_CONV):
            u = u + cw_ref[k:k + 1, :] * xp_ref[pl.ds(r0 + k * SUBLANES, R), :]
        half = jnp.dot(u.astype(BF16), wg_ref[0], preferred_element_type=F32) + gb_ref[0]
        hu = 0.5 * u
        for d in (0, 1):
            t_r = jnp.tanh(half[:, (2 * d) * W:(2 * d + 1) * W])
            t_i = jnp.tanh(half[:, (2 * d + 1) * W:(2 * d + 2) * W])
            nl = csp[d] * t_r + csp[d]
            a = jnp.exp(-nl)
            s2 = jnp.tanh(nl) * (a * a + 1.0)
            b = jnp.where(s2 > 0.0, s2 * lax.rsqrt(s2), 0.0) * (t_i * hu + hu)
            for gi in range(G):
                lanes = slice(gi * LANES, (gi + 1) * LANES)
                A_ref[d, gi, pl.ds(r0, R), :] = a[:, lanes]
                B_ref[d, gi, pl.ds(r0, R), :] = b[:, lanes]
        return carry

    lax.fori_loop(0, L // R, coeffs, 0, unroll=2)

    def scan_body(t, carry):
        out = []
        for d in (0, 1):
            r8 = pl.multiple_of((t if d == 0 else S - 1 - t) * SUBLANES, SUBLANES)
            for gi in range(G):
                h, acc = carry[d * G + gi]
                a = A_ref[d, gi, pl.ds(r8, SUBLANES), :]
                h = a * h + B_ref[d, gi, pl.ds(r8, SUBLANES), :]
                acc = acc * a
                H_ref[d, gi, pl.ds(r8, SUBLANES), :] = h
                C_ref[d, gi, pl.ds(r8, SUBLANES), :] = acc
                out.append((h, acc))
        return tuple(out)

    init = tuple((jnp.zeros((SUBLANES, LANES), F32), jnp.ones((SUBLANES, LANES), F32))
                 for _ in range(2 * G))
    lax.fori_loop(0, S, scan_body, init, unroll=4)

    hin = [None] * (2 * G)
    for d in (0, 1):
        last = (S - 1 if d == 0 else 0) * SUBLANES
        for gi in range(G):
            hl = H_ref[d, gi, last:last + SUBLANES, :]
            ac = C_ref[d, gi, last:last + SUBLANES, :]
            if has_state:
                h = h0_ref[bi, d:d + 1, gi * LANES:(gi + 1) * LANES]
            else:
                h = jnp.zeros((1, LANES), F32)
            rows = [None] * SUBLANES
            for j in (range(SUBLANES) if d == 0 else range(SUBLANES - 1, -1, -1)):
                rows[j] = h
                h = hl[j:j + 1, :] + ac[j:j + 1, :] * h
            hin[d * G + gi] = jnp.concatenate([jnp.concatenate(rows, axis=0)] * (R // SUBLANES), axis=0)
            sN_ref[bi, d:d + 1, gi * LANES:(gi + 1) * LANES] = h

    def combine(c, carry):
        r0 = pl.multiple_of(c * R, R)
        cols = []
        for gi in range(G):
            tot = jnp.zeros((R, LANES), F32)
            for d in (0, 1):
                tot = tot + (H_ref[d, gi, pl.ds(r0, R), :] + C_ref[d, gi, pl.ds(r0, R), :] * hin[d * G + gi])
            cols.append(tot)
        y_ref[bi, pl.ds(r0, R), :] = jnp.concatenate(cols, axis=1) * _silu(g_ref[bi, pl.ds(r0, R), :])
        return carry

    lax.fori_loop(0, L // R, combine, 0)


def _rglru(z4, conv_w, conv_b, gate_w, gate_b, lam, h0):
    B, S, _, _ = z4.shape
    L = S * SUBLANES
    has_state = h0 is not None
    W = LRU_TILE
    nh = LRU_WIDTH // W
    bpt = W // LRU_BLOCK_DIM
    gw = gate_w.reshape(2, 2, nh, bpt, LRU_BLOCK_DIM, LRU_BLOCK_DIM)
    eye = jnp.eye(bpt, dtype=F32)
    dense = (0.5 * jnp.einsum('dkhnij,nm->hnidkmj', gw, eye)).reshape(nh, W, 4 * W).astype(BF16)
    gb = 0.5 * gate_b.reshape(2, 2, nh, W).transpose(2, 0, 1, 3).reshape(nh, 1, 4 * W)
    zr = z4.reshape(B, L, D_LRU)
    NB = min(B, max(1, LRU_STEP_ROWS // L))
    assert B % NB == 0
    off = lambda base: pl.BlockSpec((NB, L, W), lambda h, b, base=base: (b, 0, base + h))
    in_specs = [off(0), off(LRU_WIDTH // W),
                pl.BlockSpec((LRU_CONV, W), lambda h, b: (0, h)),
                pl.BlockSpec((1, W), lambda h, b: (0, h)),
                pl.BlockSpec((1, W, 4 * W), lambda h, b: (h, 0, 0)),
                pl.BlockSpec((1, 1, 4 * W), lambda h, b: (h, 0, 0)),
                pl.BlockSpec((2, 1, W), lambda h, b: (0, 0, h))]
    args = [zr, zr, conv_w, conv_b.reshape(1, LRU_WIDTH), dense, gb, lam.reshape(2, 1, LRU_WIDTH)]
    if has_state:
        in_specs.append(pl.BlockSpec((NB, 2, W), lambda h, b: (b, 0, h)))
        args.append(h0)
    y, s = pl.pallas_call(
        functools.partial(_lru_kernel, L=L, has_state=has_state),
        grid=(nh, B // NB),
        in_specs=in_specs,
        out_specs=[pl.BlockSpec((NB, L, W), lambda h, b: (b, 0, h)),
                   pl.BlockSpec((NB, 2, W), lambda h, b: (b, 0, h))],
        out_shape=[jax.ShapeDtypeStruct((B, L, LRU_WIDTH), F32),
                   jax.ShapeDtypeStruct((B, 2, LRU_WIDTH), F32)],
        scratch_shapes=[pltpu.VMEM((L + (LRU_CONV - 1) * SUBLANES, W), F32)]
        + [pltpu.VMEM((2, LRU_GROUPS, L, LANES), F32) for _ in range(4)],
        compiler_params=_params(("arbitrary", "arbitrary")),
        name="rglru",
    )(*args)
    return y.reshape(B, S, SUBLANES, LRU_WIDTH), s


def _rope_tables(L):
    rows = L // GRID_W
    row = np.repeat(np.arange(rows, dtype=np.float64), GRID_W)
    col = np.tile(np.arange(GRID_W, dtype=np.float64), rows)
    n_f = RET_HEAD_DIM // 4
    inv = ROPE_BASE ** (-np.arange(n_f, dtype=np.float64) / n_f)
    ang = np.concatenate([row[:, None] * inv[None], col[:, None] * inv[None]], axis=-1)
    cos, sin = np.cos(ang), np.sin(ang)
    return (jnp.asarray(np.concatenate([cos, cos], axis=-1), F32),
            jnp.asarray(np.concatenate([-sin, sin], axis=-1), F32))


def kernel(x_prompt, x_sample, state_ret, state_lru, c, c_ctx, norm_g, ada_w, ada_b, w_in, ret_decay_logit, hy_conv_w, hy_conv_b, hy_ffn_w1, hy_ffn_b1, hy_ffn_w2, hy_ffn_b2, hy_ffn_w3, hy_freq, hy_bias, lru_conv_w, lru_conv_b, lru_gate_w, lru_gate_b, lru_lambda, w_out, final_g):
    Bp, Lp, _ = x_prompt.shape
    Bs, Ls, _ = x_sample.shape
    assert Bs + 1 <= SUBLANES

    cvec = jnp.zeros((SUBLANES, D_MODEL), F32).at[:Bs].set(c).at[Bs].set(c_ctx)
    mod = _modulation(cvec, ada_w, ada_b)
    rope = _rope_tables(Ls)
    dft = {L: _dft_matrices(min(DFT_MAX, L)) for L in {Lp, Ls}}
    w_in_b = w_in.astype(BF16)
    w_out_b = w_out.astype(BF16)
    fg = final_g.reshape(1, D_MODEL)

    xc, xl = x_prompt, x_sample
    new_state_ret, new_lru = None, []
    for l in range(DEPTH):
        shift, scale, gate = (mod[l, :, i * D_MODEL:(i + 1) * D_MODEL] for i in range(3))
        g = norm_g[l].reshape(1, D_MODEL)
        final = l == DEPTH - 1
        filt = {}
        for L in sorted({Lp, Ls}):
            filt[L] = _hyena_filter_spectra(L, hy_ffn_w1[l], hy_ffn_b1[l], hy_ffn_w2[l], hy_ffn_b2[l],
                                            hy_ffn_w3[l], hy_freq[l], hy_bias[l], dft[L][0])

        def layer(x, sel, rope_t, ret_s0, lru_s0, ret_states):
            sh, sc, gt = (m[sel][:, None, :] for m in (shift, scale, gate))
            z, z_lru = _in_proj(x, sh, sc, g, w_in_b, l)
            y_ret, *s_ret = _retention(z, ret_decay_logit[l], rope_t, ret_s0, l, ret_states)
            kre, kim = filt[x.shape[1]]
            y_hy = _hyena(z, hy_conv_w[l], hy_conv_b[l], kre, kim, *dft[x.shape[1]])
            y_lru, s_lru = _rglru(z_lru, lru_conv_w[l], lru_conv_b[l], lru_gate_w[l], lru_gate_b[l],
                                  lru_lambda[l], lru_s0)
            return _out_proj(y_ret, y_hy, y_lru, x, gt, w_out_b, l, fg, final), s_ret, s_lru

        xc, (new_state_ret,), ls = layer(xc, slice(Bs, Bs + 1), None, None, None,
                                         "new" if l == 0 else new_state_ret)
        new_lru.append(ls)
        xl, _, _ = layer(xl, slice(0, Bs), rope, state_ret, state_lru[:, l], None)

    new_state_lru = jnp.stack(new_lru, axis=1).astype(x_prompt.dtype)
    return (xc, xl, new_state_ret.astype(x_prompt.dtype), new_state_lru)
```

```python
import functools
import math

import numpy as np
import jax
import jax.numpy as jnp
from jax import lax
from jax.experimental import pallas as pl
from jax.experimental.pallas import tpu as pltpu

F32 = jnp.float32
BF16 = jnp.bfloat16
HI = lax.Precision.HIGHEST

D_MODEL = 1024
DEPTH = 2
GRID_W = 64
EPS = 1e-6
RET_HEADS = 4
RET_HEAD_DIM = 128
RET_WIDTH = RET_HEADS * RET_HEAD_DIM
ROPE_BASE = 10000.0
HY_WIDTH = 512
HY_ORDER = 2
HY_SHORT = 3
HY_EMB = 33
HY_FFN = 64
HY_SHORT_DECAY_PCT = 0.3
HY_LONG_DECAY_PCT = 1.5
HY_TARGET = 1e-2
LRU_WIDTH = 512
LRU_BLOCKS = 8
LRU_BLOCK_DIM = LRU_WIDTH // LRU_BLOCKS
LRU_CONV = 4
LRU_C = 8.0
D_MIX = RET_WIDTH + HY_WIDTH + LRU_WIDTH
D_IN = 4 * RET_WIDTH + 4 * HY_WIDTH + 2 * LRU_WIDTH

LANES = 128
SUBLANES = 8
DFT_MAX = 512
ROW_TILE = 256
VMEM_LIMIT = 56 * 1024 * 1024

_Q0, _K0, _V0, _GR0 = 0, 4, 8, 12
_HV0, _HX10, _HX20, _GH0 = 16, 20, 24, 28


def _sigmoid(x):
    return 0.5 * jnp.tanh(0.5 * x) + 0.5


def _silu(x):
    return x * _sigmoid(x)


def _softplus(x):
    return jnp.maximum(x, 0.0) + jnp.log1p(jnp.exp(-jnp.abs(x)))


def _params(sem):
    return pltpu.CompilerParams(dimension_semantics=sem, vmem_limit_bytes=VMEM_LIMIT)


def _mod_kernel(c_ref, w_ref, b_ref, o_ref):
    s_hi, s_lo = _split_bf16(_silu(c_ref[...]))
    w_hi, w_lo = _split_bf16(w_ref[0])
    o_ref[0] = (jnp.dot(s_hi, w_hi, preferred_element_type=F32)
                + jnp.dot(s_lo, w_hi, preferred_element_type=F32)
                + jnp.dot(s_hi, w_lo, preferred_element_type=F32)) + b_ref[0]


def _modulation(cvec, ada_w, ada_b):
    tn = 768
    return pl.pallas_call(
        _mod_kernel,
        grid=(DEPTH, 3 * D_MODEL // tn),
        in_specs=[pl.BlockSpec((SUBLANES, D_MODEL), lambda l, j: (0, 0)),
                  pl.BlockSpec((1, D_MODEL, tn), lambda l, j: (l, 0, j)),
                  pl.BlockSpec((1, 1, tn), lambda l, j: (l, 0, j))],
        out_specs=pl.BlockSpec((1, SUBLANES, tn), lambda l, j: (l, 0, j)),
        out_shape=jax.ShapeDtypeStruct((DEPTH, SUBLANES, 3 * D_MODEL), F32),
        compiler_params=_params(("arbitrary", "arbitrary")),
        name="modulation",
    )(cvec, ada_w, ada_b.reshape(DEPTH, 1, 3 * D_MODEL))


D_MAIN = 4 * RET_WIDTH + 4 * HY_WIDTH
D_LRU = 2 * LRU_WIDTH


def _in_kernel(x_ref, sh_ref, sc_ref, g_ref, w_ref, o_ref, ol_ref):
    _, nseg, rows, _ = x_ref.shape
    x = x_ref[0].reshape(nseg * rows, D_MODEL)
    ms = jnp.mean(x * x, axis=-1, keepdims=True)
    y = x * lax.rsqrt(ms + EPS) * g_ref[...]
    h = (y * (1.0 + sc_ref[0]) + sh_ref[0]).astype(BF16)
    tn = 1024
    for n in range(D_MAIN // tn):
        z = jnp.dot(h, w_ref[:, n * tn:(n + 1) * tn], preferred_element_type=F32)
        o_ref[0, :, :, n * tn:(n + 1) * tn] = z.reshape(nseg, rows, tn)
    zl = jnp.dot(h, w_ref[:, D_MAIN:D_IN], preferred_element_type=F32)
    for j in range(nseg):
        ol_ref[0, :, j, :] = zl[j * rows:(j + 1) * rows]


def _in_proj(x, shift, scale, g, w_bf16, layer):
    B, L, _ = x.shape
    S = L // SUBLANES
    rows = min(ROW_TILE // SUBLANES, S)
    per_batch = shift.shape[0] > 1
    mod_map = (lambda b, i: (b, 0, 0)) if per_batch else (lambda b, i: (0, 0, 0))
    seg = lambda C: pl.BlockSpec((1, SUBLANES, rows, C), lambda b, i: (b, 0, i, 0))
    z, z_lru = pl.pallas_call(
        _in_kernel,
        grid=(B, S // rows),
        in_specs=[seg(D_MODEL),
                  pl.BlockSpec((1, 1, D_MODEL), mod_map),
                  pl.BlockSpec((1, 1, D_MODEL), mod_map),
                  pl.BlockSpec((1, D_MODEL), lambda b, i: (0, 0)),
                  pl.BlockSpec((None, D_MODEL, D_IN), lambda b, i: (layer, 0, 0),
                               pipeline_mode=pl.Buffered(1))],
        out_specs=[seg(D_MAIN), pl.BlockSpec((1, rows, SUBLANES, D_LRU), lambda b, i: (b, i, 0, 0))],
        out_shape=[jax.ShapeDtypeStruct((B, SUBLANES, S, D_MAIN), F32),
                   jax.ShapeDtypeStruct((B, S, SUBLANES, D_LRU), F32)],
        compiler_params=_params(("arbitrary", "arbitrary")),
        name="in_proj",
    )(x.reshape(B, SUBLANES, S, D_MODEL), shift, scale, g, w_bf16)
    return z.reshape(B, L, D_MAIN), z_lru


OUT_ROWS = 1024


def _out_kernel(yr_ref, yh_ref, yl_ref, x_ref, gate_ref, w_ref, fg_ref, o_ref, *, final):
    nb, nseg, rows, _ = x_ref.shape
    n = nb * nseg * rows
    acc = jnp.dot(yr_ref[...].reshape(n, RET_WIDTH), w_ref[0:RET_WIDTH], preferred_element_type=F32)
    acc = acc + jnp.dot(yh_ref[...].reshape(n, HY_WIDTH), w_ref[RET_WIDTH:RET_WIDTH + HY_WIDTH],
                        preferred_element_type=F32)
    yl = jnp.concatenate([yl_ref[bi, :, j, :] for bi in range(nb) for j in range(nseg)], axis=0)
    acc = acc + jnp.dot(yl.astype(BF16), w_ref[RET_WIDTH + HY_WIDTH:D_MIX], preferred_element_type=F32)
    x = x_ref[...].reshape(n, D_MODEL) + gate_ref[0] * acc
    if final:
        ms = jnp.mean(x * x, axis=-1, keepdims=True)
        x = x * lax.rsqrt(ms + EPS) * fg_ref[...]
    o_ref[...] = x.reshape(nb, nseg, rows, D_MODEL)


def _out_proj(y_ret, y_hy, y_lru, x, gate, w_bf16, layer, final_g, final):
    B, L, _ = x.shape
    S = L // SUBLANES
    rows = min(OUT_ROWS // SUBLANES, S)
    per_batch = gate.shape[0] > 1
    mod_map = (lambda b, i: (b, 0, 0)) if per_batch else (lambda b, i: (0, 0, 0))
    NB = 1 if per_batch else min(B, max(1, OUT_ROWS // (SUBLANES * rows)))
    assert B % NB == 0
    seg = lambda C: pl.BlockSpec((NB, SUBLANES, rows, C), lambda b, i: (b, 0, i, 0))
    view = lambda a: a.reshape(B, SUBLANES, S, a.shape[-1])
    out = pl.pallas_call(
        functools.partial(_out_kernel, final=final),
        grid=(B // NB, S // rows),
        in_specs=[seg(RET_WIDTH), seg(HY_WIDTH),
                  pl.BlockSpec((NB, rows, SUBLANES, LRU_WIDTH), lambda b, i: (b, i, 0, 0)),
                  seg(D_MODEL),
                  pl.BlockSpec((1, 1, D_MODEL), mod_map),
                  pl.BlockSpec((None, D_MIX, D_MODEL), lambda b, i: (layer, 0, 0),
                               pipeline_mode=pl.Buffered(1)),
                  pl.BlockSpec((1, D_MODEL), lambda b, i: (0, 0))],
        out_specs=seg(D_MODEL),
        out_shape=jax.ShapeDtypeStruct((B, SUBLANES, S, D_MODEL), F32),
        compiler_params=_params(("arbitrary", "arbitrary")),
        name="out_proj",
    )(view(y_ret), view(y_hy), y_lru, view(x), gate, w_bf16, final_g)
    return out.reshape(B, L, D_MODEL)


RET_HEAD_ROWS = 2 * 2048
RET_BLOCK = 256


def _ret_kernel(*refs, L, use_rope, has_state, emit_state, chained, layer):
    it = iter(refs)
    q_ref, k_ref, v_ref, g_ref = next(it), next(it), next(it), next(it)
    cos_ref = sin_ref = s0_ref = sN_ref = None
    if use_rope:
        cos_ref, sin_ref = next(it), next(it)
    lg_ref = next(it)
    if has_state:
        s0_ref = next(it)
    if chained:
        next(it)
    y_ref = next(it)
    if emit_state:
        sN_ref = next(it)
    R_ref, M_ref, Z_ref, D_ref, KV_ref, kb_ref = (next(it) for _ in range(6))

    C = min(RET_BLOCK, L)
    DH = RET_HEAD_DIM
    HP = q_ref.shape[-1] // DH
    n = L // C
    cross = has_state or n > 1
    kscale = DH ** -0.5

    @pl.when(pl.program_id(1) == 0)
    def _():
        ii = lax.broadcasted_iota(jnp.int32, (C, C), 0)
        jj = lax.broadcasted_iota(jnp.int32, (C, C), 1)
        diff = (ii - jj).astype(F32)
        ri = lax.broadcasted_iota(jnp.int32, (C, DH), 0).astype(F32)
        for hh in range(HP):
            lgf = -_softplus(-lg_ref[0, hh])
            lgb = -_softplus(-lg_ref[1, hh])
            M_ref[hh] = (jnp.where(diff >= 0, jnp.exp(lgf[:, :C] * jnp.maximum(diff, 0.0)), 0.0)
                         + jnp.where(diff <= 0, jnp.exp(lgb[:, :C] * jnp.maximum(-diff, 0.0)), 0.0))
            lf, lb = lgf[:, :DH], lgb[:, :DH]
            Z_ref[hh, 0] = jnp.concatenate([jnp.exp(lf * (C - 1.0 - ri)), jnp.exp(lb * ri)], axis=1)
            Z_ref[hh, 1] = jnp.concatenate([jnp.exp(lf * (ri + 1.0)), jnp.exp(lb * (C - ri))], axis=1)
            D_ref[hh] = jnp.exp(jnp.concatenate([jnp.broadcast_to(lf * C, (DH, DH)),
                                                 jnp.broadcast_to(lb * C, (DH, DH))], axis=0))

    def rotary(x, r0):
        if not use_rope:
            return x
        return x * cos_ref[pl.ds(r0, C), :] + pltpu.roll(x, 64, axis=1) * sin_ref[pl.ds(r0, C), :]

    def chunk_kv(c, carry):
        r0 = pl.multiple_of(c * C, C)
        for hh in range(HP):
            lanes = slice(hh * DH, (hh + 1) * DH)
            k = rotary(k_ref[0, pl.ds(r0, C), lanes] * kscale, r0)
            kb_ref[hh, pl.ds(r0, C), :] = k.astype(BF16)
            kz = (jnp.concatenate([k, k], axis=1) * Z_ref[hh, 0]).astype(BF16)
            vb = v_ref[0, pl.ds(r0, C), lanes].astype(BF16)
            KV_ref[hh, c] = lax.dot_general(kz, vb, (((0,), (0,)), ((), ())), preferred_element_type=F32)
        return carry

    lax.fori_loop(0, n, chunk_kv, 0, unroll=4 if n % 4 == 0 else (2 if n % 2 == 0 else 1))

    for d in (0, 1):
        for hh in range(HP):
            R_ref[d, hh] = s0_ref[0, d, hh] if has_state else jnp.zeros((DH, DH), F32)

    def states(t, carry):
        for d in (0, 1):
            c = t if d == 0 else n - 1 - t
            rows = slice(d * DH, (d + 1) * DH)
            for hh in range(HP):
                R = R_ref[d, hh]
                R_ref[d, hh] = D_ref[hh, rows, :] * R + KV_ref[hh, c, rows, :]
                KV_ref[hh, c, rows, :] = R
        return carry

    lax.fori_loop(0, n, states, 0)
    if emit_state:
        for l in range(sN_ref.shape[1]):
            if chained or l == layer:
                for d in (0, 1):
                    for hh in range(HP):
                        sN_ref[0, l, d, hh] = R_ref[d, hh]
            else:
                sN_ref[0, l] = jnp.zeros(sN_ref.shape[2:], F32)

    def chunk_out(c, carry):
        r0 = pl.multiple_of(c * C, C)
        cols = []
        for hh in range(HP):
            lanes = slice(hh * DH, (hh + 1) * DH)
            q = rotary(q_ref[0, pl.ds(r0, C), lanes], r0)
            vb = v_ref[0, pl.ds(r0, C), lanes].astype(BF16)
            s = lax.dot_general(q.astype(BF16), kb_ref[hh, pl.ds(r0, C), :], (((1,), (1,)), ((), ())),
                                preferred_element_type=F32) * M_ref[hh]
            o = jnp.dot(s.astype(BF16), vb, preferred_element_type=F32)
            if cross:
                qx = (jnp.concatenate([q, q], axis=1) * Z_ref[hh, 1]).astype(BF16)
                o = o + jnp.dot(qx, KV_ref[hh, c].astype(BF16), preferred_element_type=F32)
            cols.append(o * lax.rsqrt(jnp.mean(o * o, axis=-1, keepdims=True) + EPS))
        g = g_ref[0, pl.ds(r0, C), :]
        y_ref[0, pl.ds(r0, C), :] = (jnp.concatenate(cols, axis=1) * _silu(g)).astype(BF16)
        return carry

    lax.fori_loop(0, n, chunk_out, 0, unroll=4 if n % 4 == 0 else (2 if n % 2 == 0 else 1))


def _retention(z, decay_logit, rope, state0, layer, states):
    B, L, _ = z.shape
    use_rope = rope is not None
    has_state = state0 is not None
    emit_state = states is not None
    chained = emit_state and not isinstance(states, str)
    H = RET_HEADS
    HP = min(H, max(1, RET_HEAD_ROWS // L))
    W = HP * RET_HEAD_DIM
    col = lambda off: pl.BlockSpec((1, L, W), lambda h, b, off=off: (b, 0, off * LANES // W + h))
    in_specs = [col(_Q0), col(_K0), col(_V0), col(_GR0)]
    args = [z, z, z, z]
    if use_rope:
        in_specs += [pl.BlockSpec((L, LANES), lambda h, b: (0, 0))] * 2
        args += list(rope)
    C = min(RET_BLOCK, L)
    in_specs.append(pl.BlockSpec((2, HP, 1, RET_BLOCK), lambda h, b: (0, h, 0, 0)))
    args.append(jnp.broadcast_to(decay_logit[:, :, None, None], (2, H, 1, RET_BLOCK)))
    if has_state:
        in_specs.append(pl.BlockSpec((1, None, 2, HP, RET_HEAD_DIM, RET_HEAD_DIM),
                                     lambda h, b: (b, layer, 0, h, 0, 0)))
        args.append(state0)
    out_specs = [pl.BlockSpec((1, L, W), lambda h, b: (b, 0, h))]
    out_shape = [jax.ShapeDtypeStruct((B, L, RET_WIDTH), BF16)]
    aliases = {}
    if emit_state:
        nl, l0 = (1, layer) if chained else (DEPTH, 0)
        out_specs.append(pl.BlockSpec((1, nl, 2, HP, RET_HEAD_DIM, RET_HEAD_DIM),
                                      lambda h, b: (b, l0, 0, h, 0, 0)))
        out_shape.append(jax.ShapeDtypeStruct((B, DEPTH, 2, H, RET_HEAD_DIM, RET_HEAD_DIM), F32))
        if chained:
            aliases = {len(args): 1}
            in_specs.append(pl.BlockSpec(memory_space=pl.ANY))
            args.append(states)
    return pl.pallas_call(
        functools.partial(_ret_kernel, L=L, use_rope=use_rope, has_state=has_state,
                          emit_state=emit_state, chained=chained, layer=layer),
        grid=(H // HP, B),
        in_specs=in_specs,
        out_specs=out_specs,
        out_shape=out_shape,
        input_output_aliases=aliases,
        scratch_shapes=[pltpu.VMEM((2, HP, RET_HEAD_DIM, RET_HEAD_DIM), F32),
                        pltpu.VMEM((HP, C, C), F32),
                        pltpu.VMEM((HP, 2, C, 2 * RET_HEAD_DIM), F32),
                        pltpu.VMEM((HP, 2 * RET_HEAD_DIM, RET_HEAD_DIM), F32),
                        pltpu.VMEM((HP, L // C, 2 * RET_HEAD_DIM, RET_HEAD_DIM), F32),
                        pltpu.VMEM((HP, L, RET_HEAD_DIM), BF16)],
        compiler_params=_params(("arbitrary", "arbitrary")),
        name="retention",
    )(*args)


def _dft_matrices(T):
    N = 2 * T
    k = np.arange(T, dtype=np.float64)[:, None]
    m = np.arange(T, dtype=np.float64)[None, :]
    ang = 2.0 * np.pi * k * m / N
    fwd = np.concatenate([np.cos(ang), -np.sin(ang)], axis=0)
    fwd[T, :] = (-1.0) ** np.arange(T)
    ck = np.full((T,), 2.0)
    ck[0] = 1.0
    inv_re = (np.cos(ang) * ck[:, None]).T / N
    inv_im = (-2.0 * np.sin(ang)).T / N
    inv_im[:, 0] = ((-1.0) ** np.arange(T)) / N
    inv = np.concatenate([inv_re, inv_im], axis=1)
    return jnp.asarray(fwd, F32).astype(BF16), jnp.asarray(inv, F32).astype(BF16)


def _split_bf16(x):
    hi = x.astype(BF16)
    return hi, (x - hi.astype(F32)).astype(BF16)


def _dot1(a, b):
    return jnp.dot(a, b.astype(BF16), preferred_element_type=F32)


def _filt_kernel(zemb_ref, w1_ref, b1_ref, w2_ref, b2_ref, fr_ref, w3f_ref, w3b_ref, dec_ref,
                 bias_ref, fwd_ref, kre_ref, kim_ref, hid_ref, hn_ref, F_ref, *, L):
    T = fwd_ref.shape[1]
    nb = L // T

    @pl.when((pl.program_id(0) == 0) & (pl.program_id(1) == 0))
    def _():
        z1 = jnp.dot(zemb_ref[...], w1_ref[...], precision=HI, preferred_element_type=F32) + b1_ref[...]
        h1 = jnp.sin(fr_ref[0:1, :] * z1)
        z2 = jnp.dot(h1, w2_ref[...], precision=HI, preferred_element_type=F32) + b2_ref[...]
        hid = jnp.sin(fr_ref[1:2, :] * z2)
        hid_ref[0], hid_ref[1] = _split_bf16(hid)

    w3 = jnp.concatenate([w3f_ref[...], w3b_ref[...]], axis=1)
    w_hi, w_lo = _split_bf16(w3)
    h = (jnp.dot(hid_ref[0], w_hi, preferred_element_type=F32)
         + jnp.dot(hid_ref[1], w_hi, preferred_element_type=F32)
         + jnp.dot(hid_ref[0], w_lo, preferred_element_type=F32))
    h = h * jnp.concatenate([dec_ref[...]] * 2, axis=1)
    h = h / (jnp.sum(jnp.abs(h), axis=0, keepdims=True) + EPS)
    for dr in (0, 1):
        hn_ref[dr] = h[:, dr * LANES:(dr + 1) * LANES]
    for blk in range(nb):
        F = _dot1(fwd_ref[...], h[blk * T:(blk + 1) * T])
        for dr in (0, 1):
            F_ref[dr, blk] = F[:, dr * LANES:(dr + 1) * LANES]

    row = lax.broadcasted_iota(jnp.int32, (T, LANES), 0)
    sgn = jnp.where((row & 1) == 0, 1.0, -1.0).astype(F32)
    row0 = row == 0
    bias = bias_ref[0]
    for d in range(-(nb - 1), nb):
        re = jnp.zeros((T, LANES), F32)
        im = jnp.zeros((T, LANES), F32)
        nyq = jnp.zeros((1, LANES), F32)
        if d >= 0:
            F = F_ref[0, d]
            re, im, nyq = re + F[:T], im + F[T:], nyq + F[T:T + 1]
        if d >= 1:
            F = F_ref[0, d - 1]
            p0 = hn_ref[0, (d - 1) * T:(d - 1) * T + 1, :]
            re, im, nyq = re + sgn * (F[:T] - p0), im + sgn * F[T:], nyq + (F[T:T + 1] - p0)
        e = -d
        if e >= 0:
            F = F_ref[1, e]
            re, im, nyq = re + F[:T], im - F[T:], nyq + F[T:T + 1]
        if e >= 1:
            F = F_ref[1, e - 1]
            p0 = hn_ref[1, (e - 1) * T:(e - 1) * T + 1, :]
            re, im, nyq = re + sgn * (F[:T] - p0), im - sgn * F[T:], nyq + (F[T:T + 1] - p0)
        if d == 0:
            re, nyq = re + bias, nyq + bias
        kre_ref[0, d + nb - 1] = re
        kim_ref[0, d + nb - 1] = jnp.where(row0, nyq, im)


def _hyena_filter_spectra(L, w1, b1, w2, b2, w3, freq, bias, fwd):
    T = fwd.shape[1]
    nb = L // T
    nlag = 2 * nb - 1
    P = LANES
    t = np.linspace(0.0, 1.0, L)[:, None]
    n_bands = (HY_EMB - 1) // 2
    f = np.linspace(1e-4, n_bands - 1, n_bands)
    ang = (2.0 * math.pi / L) * np.arange(L)[:, None] * f[None, :]
    zemb = np.concatenate([t, np.cos(ang), -np.sin(ang)], axis=-1)
    zemb = jnp.asarray(np.pad(zemb, ((0, 0), (0, P - HY_EMB))), F32)
    min_decay = math.log(HY_TARGET) / HY_LONG_DECAY_PCT
    max_decay = math.log(HY_TARGET) / HY_SHORT_DECAY_PCT
    deltas = np.abs(np.linspace(min_decay, max_decay, HY_WIDTH))
    dec = jnp.asarray(np.exp(-t * deltas[None, :]), F32)
    pad = P - HY_FFN
    w1p = jnp.pad(w1, ((0, P - HY_EMB), (0, pad)))
    b1p = jnp.pad(b1, (0, pad)).reshape(1, P)
    w2p = jnp.pad(w2, ((0, pad), (0, pad)))
    b2p = jnp.pad(b2, (0, pad)).reshape(1, P)
    w3p = jnp.pad(w3, ((0, pad), (0, 0)))
    frp = jnp.pad(freq, ((0, 0), (0, pad)))
    ncg = HY_WIDTH // P
    const = lambda shape: pl.BlockSpec(shape, lambda o, c: tuple(0 for _ in shape))
    kshape = jax.ShapeDtypeStruct((HY_ORDER, nlag, T, HY_WIDTH), F32)
    kspec = pl.BlockSpec((1, nlag, T, P), lambda o, c: (o, 0, 0, c))
    return pl.pallas_call(
        functools.partial(_filt_kernel, L=L),
        grid=(HY_ORDER, ncg),
        in_specs=[const((L, P)), const((P, P)), const((1, P)), const((P, P)), const((1, P)), const((2, P)),
                  pl.BlockSpec((P, P), lambda o, c: (0, o * 2 * ncg + c)),
                  pl.BlockSpec((P, P), lambda o, c: (0, o * 2 * ncg + ncg + c)),
                  pl.BlockSpec((L, P), lambda o, c: (0, c)),
                  pl.BlockSpec((1, 1, P), lambda o, c: (o, 0, c)),
                  const((2 * T, T))],
        out_specs=[kspec, kspec],
        out_shape=[kshape, kshape],
        scratch_shapes=[pltpu.VMEM((2, L, P), BF16), pltpu.VMEM((2, L, P), F32),
                        pltpu.VMEM((2, nb, 2 * T, P), F32)],
        compiler_params=_params(("arbitrary", "arbitrary")),
        name="hyena_filters",
    )(zemb, w1p, b1p, w2p, b2p, frp, w3p, w3p, dec, bias.reshape(HY_ORDER, 1, HY_WIDTH), fwd)


CONV_ROWS = 128


def _conv_block(x_ref, t0, w, b, left):
    L, C = x_ref.shape
    T = CONV_ROWS
    blk = x_ref[t0:t0 + T, :]
    prev = x_ref[t0 - SUBLANES:t0, :] if t0 > 0 else jnp.zeros((SUBLANES, C), F32)
    nxt = x_ref[t0 + T:t0 + T + SUBLANES, :] if t0 + T < L else jnp.zeros((SUBLANES, C), F32)
    ext = jnp.concatenate([prev, blk, nxt], axis=0)
    n = T + 2 * SUBLANES
    acc = jnp.broadcast_to(b, (T, C))
    for tap in range(w.shape[0]):
        s = tap - left
        sh = blk if s == 0 else pltpu.roll(ext, (-s) % n, axis=0)[SUBLANES:SUBLANES + T]
        acc = acc + w[tap:tap + 1, :] * sh
    return acc


HY_LANES_ROWS = 2 * 2048
HY_MAX_BATCH = 8


def _toeplitz(k, u, mul):
    n = len(u)
    if n == 1:
        return [mul(k[0], u[0])]
    h = n // 2
    add = lambda a, b: tuple(x + y for x, y in zip(a, b))
    sub = lambda a, b: tuple(x - y for x, y in zip(a, b))
    t0, t1, t2 = k[h:h + n - 1], k[0:n - 1], k[n:2 * n - 1]
    p = _toeplitz(t0, [add(a, b) for a, b in zip(u[:h], u[h:])], mul)
    q = _toeplitz([sub(a, b) for a, b in zip(t1, t0)], u[h:], mul)
    r = _toeplitz([sub(a, b) for a, b in zip(t2, t0)], u[:h], mul)
    return [add(a, b) for a, b in zip(p, q)] + [add(a, b) for a, b in zip(p, r)]


def _hy_kernel(v_ref, x1_ref, x2_ref, g_ref, wv_ref, w1_ref, w2_ref, bv_ref, b1_ref, b2_ref,
               kre_ref, kim_ref, fwd_ref, inv_ref, y_ref,
               x1c_ref, x2c_ref, u_ref, U_ref, Y_ref, *, L):
    T = fwd_ref.shape[1]
    P = LANES
    nb = L // T
    NBT = v_ref.shape[0]
    left = (HY_SHORT - 1) // 2
    def block_spectrum(j, r0):
        U_ref[j] = jnp.dot(fwd_ref[...], u_ref[pl.ds(r0, T), :], preferred_element_type=F32)

    for j in range(nb):
        for t0 in range(j * T, (j + 1) * T, CONV_ROWS):
            rows = slice(t0, t0 + CONV_ROWS)
            for p in range(NBT):
                u_ref[rows, p * P:(p + 1) * P] = _conv_block(v_ref.at[p], t0, wv_ref[...], bv_ref[...], left).astype(BF16)
                x1c_ref[p, rows, :] = _conv_block(x1_ref.at[p], t0, w1_ref[...], b1_ref[...], left)
                x2c_ref[p, rows, :] = _conv_block(x2_ref.at[p], t0, w2_ref[...], b2_ref[...], left)
        block_spectrum(j, j * T)

    row0 = lax.broadcasted_iota(jnp.int32, (SUBLANES, NBT * P), 0) == 0
    tile = lambda x: jnp.concatenate([x] * NBT, axis=1)

    def cmul(k, u):
        kr, ki = tile(k[0]), tile(k[1])
        return (kr * u[0] - ki * u[1], kr * u[1] + ki * u[0])

    def rmul(k, u):
        return (tile(k[0]) * u[0], tile(k[1]) * u[1])

    def spectra_products(o):
        def load(r, rows):
            k = [(kre_ref[o, l, pl.ds(r, rows), :], kim_ref[o, l, pl.ds(r, rows), :]) for l in range(2 * nb - 1)]
            u = [(U_ref[j, pl.ds(r, rows), :], U_ref[j, pl.ds(T + r, rows), :]) for j in range(nb)]
            return k, u

        def chunk(rc, carry):
            r = pl.multiple_of(rc * SUBLANES, SUBLANES)
            y = _toeplitz(*load(r, SUBLANES), cmul)
            for i in range(nb):
                Y_ref[i, pl.ds(r, SUBLANES), :] = y[i][0]
                Y_ref[i, pl.ds(T + r, SUBLANES), :] = y[i][1]
            return carry

        lax.fori_loop(0, T // SUBLANES, chunk, 0, unroll=2)
        y = _toeplitz(*load(0, SUBLANES), rmul)
        for i in range(nb):
            Y_ref[i, 0:SUBLANES, :] = jnp.where(row0, y[i][0], Y_ref[i, 0:SUBLANES, :])
            Y_ref[i, T:T + SUBLANES, :] = jnp.where(row0, y[i][1], Y_ref[i, T:T + SUBLANES, :])

    def long_conv_block(i):
        return _dot1(inv_ref[...], Y_ref[i])

    spectra_products(0)

    def order0(i, carry):
        r0 = pl.multiple_of(i * T, T)
        y = long_conv_block(i)
        for p in range(NBT):
            z1 = x1c_ref[p, pl.ds(r0, T), :] * y[:, p * P:(p + 1) * P]
            u_ref[pl.ds(r0, T), p * P:(p + 1) * P] = z1.astype(BF16)
        return carry

    lax.fori_loop(0, nb, order0, 0, unroll=4 if nb % 4 == 0 else 1)
    for j in range(nb):
        block_spectrum(j, j * T)
    spectra_products(1)

    def order1(i, carry):
        r0 = pl.multiple_of(i * T, T)
        y = long_conv_block(i)
        for p in range(NBT):
            out = x2c_ref[p, pl.ds(r0, T), :] * y[:, p * P:(p + 1) * P]
            y_ref[p, pl.ds(r0, T), :] = (out * _silu(g_ref[p, pl.ds(r0, T), :])).astype(BF16)
        return carry

    lax.fori_loop(0, nb, order1, 0, unroll=4 if nb % 4 == 0 else 1)


def _hyena(z, conv_w, conv_b, kre, kim, fwd, inv):
    B, L, _ = z.shape
    NBT = min(B, max(2, HY_LANES_ROWS // L), HY_MAX_BATCH)
    assert B % NBT == 0
    T = fwd.shape[1]
    nb = L // T
    nlag = 2 * nb - 1
    P = LANES
    ncg = HY_WIDTH // P
    col = lambda off: pl.BlockSpec((NBT, L, P), lambda c, b, off=off: (b, 0, off + c))
    wspec = lambda part: pl.BlockSpec((HY_SHORT, P), lambda c, b, part=part: (0, part * ncg + c))
    bspec = lambda part: pl.BlockSpec((1, P), lambda c, b, part=part: (0, part * ncg + c))
    kspec = pl.BlockSpec((HY_ORDER, nlag, T, P), lambda c, b: (0, 0, 0, c))
    fspec = pl.BlockSpec((2 * T, T), lambda c, b: (0, 0))
    ispec = pl.BlockSpec((T, 2 * T), lambda c, b: (0, 0))
    cb = conv_b.reshape(1, 3 * HY_WIDTH)
    return pl.pallas_call(
        functools.partial(_hy_kernel, L=L),
        grid=(ncg, B // NBT),
        in_specs=[col(_HV0), col(_HX10), col(_HX20), col(_GH0),
                  wspec(0), wspec(1), wspec(2), bspec(0), bspec(1), bspec(2),
                  kspec, kspec, fspec, ispec],
        out_specs=pl.BlockSpec((NBT, L, P), lambda c, b: (b, 0, c)),
        out_shape=jax.ShapeDtypeStruct((B, L, HY_WIDTH), BF16),
        scratch_shapes=[pltpu.VMEM((NBT, L, P), F32), pltpu.VMEM((NBT, L, P), F32),
                        pltpu.VMEM((L, NBT * P), BF16),
                        pltpu.VMEM((nb, 2 * T, NBT * P), F32), pltpu.VMEM((nb, 2 * T, NBT * P), F32)],
        compiler_params=_params(("arbitrary", "arbitrary")),
        name="hyena",
    )(z, z, z, z, conv_w, conv_w, conv_w, cb, cb, cb, kre, kim, fwd, inv)


LRU_TILE = 256
LRU_GROUPS = LRU_TILE // LANES
LRU_ROWS = 128
LRU_STEP_ROWS = 1024


def _lru_kernel(*refs, L, has_state):
    for bi in range(refs[0].shape[0]):
        _lru_one(*refs, L=L, has_state=has_state, bi=bi)


def _lru_one(*refs, L, has_state, bi):
    it = iter(refs)
    x_ref, g_ref, cw_ref, cb_ref, wg_ref, gb_ref, lam_ref = (next(it) for _ in range(7))
    h0_ref = next(it) if has_state else None
    y_ref, sN_ref, xp_ref, A_ref, B_ref, H_ref, C_ref = (next(it) for _ in range(7))

    S = L // SUBLANES
    W = LRU_TILE
    G = LRU_GROUPS
    R = LRU_ROWS
    left = LRU_CONV // 2
    pad = left * SUBLANES

    sub = lax.broadcasted_iota(jnp.int32, (SUBLANES, W), 0)
    for k in range(left):
        tail = pltpu.roll(x_ref[bi, L - pad + k * SUBLANES:L - pad + (k + 1) * SUBLANES, :], 1, axis=0)
        xp_ref[k * SUBLANES:(k + 1) * SUBLANES, :] = jnp.where(sub == 0, 0.0, tail)
    for k in range(LRU_CONV - 1 - left):
        head = pltpu.roll(x_ref[bi, k * SUBLANES:(k + 1) * SUBLANES, :], SUBLANES - 1, axis=0)
        xp_ref[pad + L + k * SUBLANES:pad + L + (k + 1) * SUBLANES, :] = jnp.where(sub == SUBLANES - 1, 0.0, head)

    def copy(c, carry):
        r0 = pl.multiple_of(c * R, R)
        xp_ref[pl.ds(pad + r0, R), :] = x_ref[bi, pl.ds(r0, R), :]
        return carry

    lax.fori_loop(0, L // R, copy, 0)

    csp = [(0.5 * LRU_C) * _softplus(-lam_ref[d]) for d in (0, 1)]

    def coeffs(c, carry):
        r0 = pl.multiple_of(c * R, R)
        u = jnp.broadcast_to(cb_ref[...], (R, W))
        for k in range(LRU_CONV):
            u = u + cw_ref[k:k + 1, :] * xp_ref[pl.ds(r0 + k * SUBLANES, R), :]
        half = jnp.dot(u.astype(BF16), wg_ref[0], preferred_element_type=F32) + gb_ref[0]
        hu = 0.5 * u
        for d in (0, 1):
            t_r = jnp.tanh(half[:, (2 * d) * W:(2 * d + 1) * W])
            t_i = jnp.tanh(half[:, (2 * d + 1) * W:(2 * d + 2) * W])
            nl = csp[d] * t_r + csp[d]
            a = jnp.exp(-nl)
            s2 = jnp.tanh(nl) * (a * a + 1.0)
            b = jnp.where(s2 > 0.0, s2 * lax.rsqrt(s2), 0.0) * (t_i * hu + hu)
            for gi in range(G):
                lanes = slice(gi * LANES, (gi + 1) * LANES)
                A_ref[d, gi, pl.ds(r0, R), :] = a[:, lanes]
                B_ref[d, gi, pl.ds(r0, R), :] = b[:, lanes]
        return carry

    lax.fori_loop(0, L // R, coeffs, 0, unroll=2)

    def scan_body(t, carry):
        out = []
        for d in (0, 1):
            r8 = pl.multiple_of((t if d == 0 else S - 1 - t) * SUBLANES, SUBLANES)
            for gi in range(G):
                h, acc = carry[d * G + gi]
                a = A_ref[d, gi, pl.ds(r8, SUBLANES), :]
                h = a * h + B_ref[d, gi, pl.ds(r8, SUBLANES), :]
                acc = acc * a
                H_ref[d, gi, pl.ds(r8, SUBLANES), :] = h
                C_ref[d, gi, pl.ds(r8, SUBLANES), :] = acc
                out.append((h, acc))
        return tuple(out)

    init = tuple((jnp.zeros((SUBLANES, LANES), F32), jnp.ones((SUBLANES, LANES), F32))
                 for _ in range(2 * G))
    lax.fori_loop(0, S, scan_body, init, unroll=4)

    hin = [None] * (2 * G)
    for d in (0, 1):
        last = (S - 1 if d == 0 else 0) * SUBLANES
        for gi in range(G):
            hl = H_ref[d, gi, last:last + SUBLANES, :]
            ac = C_ref[d, gi, last:last + SUBLANES, :]
            if has_state:
                h = h0_ref[bi, d:d + 1, gi * LANES:(gi + 1) * LANES]
            else:
                h = jnp.zeros((1, LANES), F32)
            rows = [None] * SUBLANES
            for j in (range(SUBLANES) if d == 0 else range(SUBLANES - 1, -1, -1)):
                rows[j] = h
                h = hl[j:j + 1, :] + ac[j:j + 1, :] * h
            hin[d * G + gi] = jnp.concatenate([jnp.concatenate(rows, axis=0)] * (R // SUBLANES), axis=0)
            sN_ref[bi, d:d + 1, gi * LANES:(gi + 1) * LANES] = h

    def combine(c, carry):
        r0 = pl.multiple_of(c * R, R)
        cols = []
        for gi in range(G):
            tot = jnp.zeros((R, LANES), F32)
            for d in (0, 1):
                tot = tot + (H_ref[d, gi, pl.ds(r0, R), :] + C_ref[d, gi, pl.ds(r0, R), :] * hin[d * G + gi])
            cols.append(tot)
        y_ref[bi, pl.ds(r0, R), :] = jnp.concatenate(cols, axis=1) * _silu(g_ref[bi, pl.ds(r0, R), :])
        return carry

    lax.fori_loop(0, L // R, combine, 0)


def _rglru(z4, conv_w, conv_b, gate_w, gate_b, lam, h0):
    B, S, _, _ = z4.shape
    L = S * SUBLANES
    has_state = h0 is not None
    W = LRU_TILE
    nh = LRU_WIDTH // W
    bpt = W // LRU_BLOCK_DIM
    gw = gate_w.reshape(2, 2, nh, bpt, LRU_BLOCK_DIM, LRU_BLOCK_DIM)
    eye = jnp.eye(bpt, dtype=F32)
    dense = (0.5 * jnp.einsum('dkhnij,nm->hnidkmj', gw, eye)).reshape(nh, W, 4 * W).astype(BF16)
    gb = 0.5 * gate_b.reshape(2, 2, nh, W).transpose(2, 0, 1, 3).reshape(nh, 1, 4 * W)
    zr = z4.reshape(B, L, D_LRU)
    NB = min(B, max(1, LRU_STEP_ROWS // L))
    assert B % NB == 0
    off = lambda base: pl.BlockSpec((NB, L, W), lambda h, b, base=base: (b, 0, base + h))
    in_specs = [off(0), off(LRU_WIDTH // W),
                pl.BlockSpec((LRU_CONV, W), lambda h, b: (0, h)),
                pl.BlockSpec((1, W), lambda h, b: (0, h)),
                pl.BlockSpec((1, W, 4 * W), lambda h, b: (h, 0, 0)),
                pl.BlockSpec((1, 1, 4 * W), lambda h, b: (h, 0, 0)),
                pl.BlockSpec((2, 1, W), lambda h, b: (0, 0, h))]
    args = [zr, zr, conv_w, conv_b.reshape(1, LRU_WIDTH), dense, gb, lam.reshape(2, 1, LRU_WIDTH)]
    if has_state:
        in_specs.append(pl.BlockSpec((NB, 2, W), lambda h, b: (b, 0, h)))
        args.append(h0)
    y, s = pl.pallas_call(
        functools.partial(_lru_kernel, L=L, has_state=has_state),
        grid=(nh, B // NB),
        in_specs=in_specs,
        out_specs=[pl.BlockSpec((NB, L, W), lambda h, b: (b, 0, h)),
                   pl.BlockSpec((NB, 2, W), lambda h, b: (b, 0, h))],
        out_shape=[jax.ShapeDtypeStruct((B, L, LRU_WIDTH), F32),
                   jax.ShapeDtypeStruct((B, 2, LRU_WIDTH), F32)],
        scratch_shapes=[pltpu.VMEM((L + (LRU_CONV - 1) * SUBLANES, W), F32)]
        + [pltpu.VMEM((2, LRU_GROUPS, L, LANES), F32) for _ in range(4)],
        compiler_params=_params(("arbitrary", "arbitrary")),
        name="rglru",
    )(*args)
    return y.reshape(B, S, SUBLANES, LRU_WIDTH), s


def _rope_tables(L):
    rows = L // GRID_W
    row = np.repeat(np.arange(rows, dtype=np.float64), GRID_W)
    col = np.tile(np.arange(GRID_W, dtype=np.float64), rows)
    n_f = RET_HEAD_DIM // 4
    inv = ROPE_BASE ** (-np.arange(n_f, dtype=np.float64) / n_f)
    ang = np.concatenate([row[:, None] * inv[None], col[:, None] * inv[None]], axis=-1)
    cos, sin = np.cos(ang), np.sin(ang)
    return (jnp.asarray(np.concatenate([cos, cos], axis=-1), F32),
            jnp.asarray(np.concatenate([-sin, sin], axis=-1), F32))


def kernel(x_prompt, x_sample, state_ret, state_lru, c, c_ctx, norm_g, ada_w, ada_b, w_in, ret_decay_logit, hy_conv_w, hy_conv_b, hy_ffn_w1, hy_ffn_b1, hy_ffn_w2, hy_ffn_b2, hy_ffn_w3, hy_freq, hy_bias, lru_conv_w, lru_conv_b, lru_gate_w, lru_gate_b, lru_lambda, w_out, final_g):
    Bp, Lp, _ = x_prompt.shape
    Bs, Ls, _ = x_sample.shape
    assert Bs + 1 <= SUBLANES

    cvec = jnp.zeros((SUBLANES, D_MODEL), F32).at[:Bs].set(c).at[Bs].set(c_ctx)
    mod = _modulation(cvec, ada_w, ada_b)
    rope = _rope_tables(Ls)
    dft = {L: _dft_matrices(min(DFT_MAX, L)) for L in {Lp, Ls}}
    w_in_b = w_in.astype(BF16)
    w_out_b = w_out.astype(BF16)
    fg = final_g.reshape(1, D_MODEL)

    xc, xl = x_prompt, x_sample
    new_state_ret, new_lru = None, []
    for l in range(DEPTH):
        shift, scale, gate = (mod[l, :, i * D_MODEL:(i + 1) * D_MODEL] for i in range(3))
        g = norm_g[l].reshape(1, D_MODEL)
        final = l == DEPTH - 1
        filt = {}
        for L in sorted({Lp, Ls}):
            filt[L] = _hyena_filter_spectra(L, hy_ffn_w1[l], hy_ffn_b1[l], hy_ffn_w2[l], hy_ffn_b2[l],
                                            hy_ffn_w3[l], hy_freq[l], hy_bias[l], dft[L][0])

        def layer(x, sel, rope_t, ret_s0, lru_s0, ret_states):
            sh, sc, gt = (m[sel][:, None, :] for m in (shift, scale, gate))
            z, z_lru = _in_proj(x, sh, sc, g, w_in_b, l)
            y_ret, *s_ret = _retention(z, ret_decay_logit[l], rope_t, ret_s0, l, ret_states)
            kre, kim = filt[x.shape[1]]
            y_hy = _hyena(z, hy_conv_w[l], hy_conv_b[l], kre, kim, *dft[x.shape[1]])
            y_lru, s_lru = _rglru(z_lru, lru_conv_w[l], lru_conv_b[l], lru_gate_w[l], lru_gate_b[l],
                                  lru_lambda[l], lru_s0)
            return _out_proj(y_ret, y_hy, y_lru, x, gt, w_out_b, l, fg, final), s_ret, s_lru

        xc, (new_state_ret,), ls = layer(xc, slice(Bs, Bs + 1), None, None, None,
                                         "new" if l == 0 else new_state_ret)
        new_lru.append(ls)
        xl, _, _ = layer(xl, slice(0, Bs), rope, state_ret, state_lru[:, l], None)

    new_state_lru = jnp.stack(new_lru, axis=1).astype(x_prompt.dtype)
    return (xc, xl, new_state_ret.astype(x_prompt.dtype), new_state_lru)
```

```python
import functools
import math

import numpy as np
import jax
import jax.numpy as jnp
from jax import lax
from jax.experimental import pallas as pl
from jax.experimental.pallas import tpu as pltpu

F32 = jnp.float32
BF16 = jnp.bfloat16
HI = lax.Precision.HIGHEST

D_MODEL = 1024
DEPTH = 2
GRID_W = 64
EPS = 1e-6
RET_HEADS = 4
RET_HEAD_DIM = 128
RET_WIDTH = RET_HEADS * RET_HEAD_DIM
ROPE_BASE = 10000.0
HY_WIDTH = 512
HY_ORDER = 2
HY_SHORT = 3
HY_EMB = 33
HY_FFN = 64
HY_SHORT_DECAY_PCT = 0.3
HY_LONG_DECAY_PCT = 1.5
HY_TARGET = 1e-2
LRU_WIDTH = 512
LRU_BLOCKS = 8
LRU_BLOCK_DIM = LRU_WIDTH // LRU_BLOCKS
LRU_CONV = 4
LRU_C = 8.0
D_MIX = RET_WIDTH + HY_WIDTH + LRU_WIDTH
D_IN = 4 * RET_WIDTH + 4 * HY_WIDTH + 2 * LRU_WIDTH

LANES = 128
SUBLANES = 8
DFT_MAX = 512
ROW_TILE = 256
VMEM_LIMIT = 56 * 1024 * 1024

_Q0, _K0, _V0, _GR0 = 0, 4, 8, 12
_HV0, _HX10, _HX20, _GH0 = 16, 20, 24, 28


def _sigmoid(x):
    return 0.5 * jnp.tanh(0.5 * x) + 0.5


def _silu(x):
    return x * _sigmoid(x)


def _softplus(x):
    return jnp.maximum(x, 0.0) + jnp.log1p(jnp.exp(-jnp.abs(x)))


def _params(sem):
    return pltpu.CompilerParams(dimension_semantics=sem, vmem_limit_bytes=VMEM_LIMIT)


def _mod_kernel(c_ref, w_ref, b_ref, o_ref):
    s_hi, s_lo = _split_bf16(_silu(c_ref[...]))
    w_hi, w_lo = _split_bf16(w_ref[0])
    o_ref[0] = (jnp.dot(s_hi, w_hi, preferred_element_type=F32)
                + jnp.dot(s_lo, w_hi, preferred_element_type=F32)
                + jnp.dot(s_hi, w_lo, preferred_element_type=F32)) + b_ref[0]


def _modulation(cvec, ada_w, ada_b):
    tn = 768
    return pl.pallas_call(
        _mod_kernel,
        grid=(DEPTH, 3 * D_MODEL // tn),
        in_specs=[pl.BlockSpec((SUBLANES, D_MODEL), lambda l, j: (0, 0)),
                  pl.BlockSpec((1, D_MODEL, tn), lambda l, j: (l, 0, j)),
                  pl.BlockSpec((1, 1, tn), lambda l, j: (l, 0, j))],
        out_specs=pl.BlockSpec((1, SUBLANES, tn), lambda l, j: (l, 0, j)),
        out_shape=jax.ShapeDtypeStruct((DEPTH, SUBLANES, 3 * D_MODEL), F32),
        compiler_params=_params(("arbitrary", "arbitrary")),
        name="modulation",
    )(cvec, ada_w, ada_b.reshape(DEPTH, 1, 3 * D_MODEL))


D_MAIN = 4 * RET_WIDTH + 4 * HY_WIDTH
D_LRU = 2 * LRU_WIDTH


def _in_kernel(x_ref, sh_ref, sc_ref, g_ref, w_ref, o_ref, ol_ref):
    _, nseg, rows, _ = x_ref.shape
    x = x_ref[0].reshape(nseg * rows, D_MODEL)
    ms = jnp.mean(x * x, axis=-1, keepdims=True)
    y = x * lax.rsqrt(ms + EPS) * g_ref[...]
    h = (y * (1.0 + sc_ref[0]) + sh_ref[0]).astype(BF16)
    tn = 1024
    for n in range(D_MAIN // tn):
        z = jnp.dot(h, w_ref[:, n * tn:(n + 1) * tn], preferred_element_type=F32)
        o_ref[0, :, :, n * tn:(n + 1) * tn] = z.reshape(nseg, rows, tn)
    zl = jnp.dot(h, w_ref[:, D_MAIN:D_IN], preferred_element_type=F32)
    for j in range(nseg):
        ol_ref[0, :, j, :] = zl[j * rows:(j + 1) * rows]


def _in_proj(x, shift, scale, g, w_bf16, layer):
    B, L, _ = x.shape
    S = L // SUBLANES
    rows = min(ROW_TILE // SUBLANES, S)
    per_batch = shift.shape[0] > 1
    mod_map = (lambda b, i: (b, 0, 0)) if per_batch else (lambda b, i: (0, 0, 0))
    seg = lambda C: pl.BlockSpec((1, SUBLANES, rows, C), lambda b, i: (b, 0, i, 0))
    z, z_lru = pl.pallas_call(
        _in_kernel,
        grid=(B, S // rows),
        in_specs=[seg(D_MODEL),
                  pl.BlockSpec((1, 1, D_MODEL), mod_map),
                  pl.BlockSpec((1, 1, D_MODEL), mod_map),
                  pl.BlockSpec((1, D_MODEL), lambda b, i: (0, 0)),
                  pl.BlockSpec((None, D_MODEL, D_IN), lambda b, i: (layer, 0, 0),
                               pipeline_mode=pl.Buffered(1))],
        out_specs=[seg(D_MAIN), pl.BlockSpec((1, rows, SUBLANES, D_LRU), lambda b, i: (b, i, 0, 0))],
        out_shape=[jax.ShapeDtypeStruct((B, SUBLANES, S, D_MAIN), F32),
                   jax.ShapeDtypeStruct((B, S, SUBLANES, D_LRU), F32)],
        compiler_params=_params(("arbitrary", "arbitrary")),
        name="in_proj",
    )(x.reshape(B, SUBLANES, S, D_MODEL), shift, scale, g, w_bf16)
    return z.reshape(B, L, D_MAIN), z_lru


OUT_ROWS = 1024


def _out_kernel(yr_ref, yh_ref, yl_ref, x_ref, gate_ref, w_ref, fg_ref, o_ref, *, final):
    nb, nseg, rows, _ = x_ref.shape
    n = nb * nseg * rows
    acc = jnp.dot(yr_ref[...].reshape(n, RET_WIDTH), w_ref[0:RET_WIDTH], preferred_element_type=F32)
    acc = acc + jnp.dot(yh_ref[...].reshape(n, HY_WIDTH), w_ref[RET_WIDTH:RET_WIDTH + HY_WIDTH],
                        preferred_element_type=F32)
    yl = jnp.concatenate([yl_ref[bi, :, j, :] for bi in range(nb) for j in range(nseg)], axis=0)
    acc = acc + jnp.dot(yl.astype(BF16), w_ref[RET_WIDTH + HY_WIDTH:D_MIX], preferred_element_type=F32)
    x = x_ref[...].reshape(n, D_MODEL) + gate_ref[0] * acc
    if final:
        ms = jnp.mean(x * x, axis=-1, keepdims=True)
        x = x * lax.rsqrt(ms + EPS) * fg_ref[...]
    o_ref[...] = x.reshape(nb, nseg, rows, D_MODEL)


def _out_proj(y_ret, y_hy, y_lru, x, gate, w_bf16, layer, final_g, final):
    B, L, _ = x.shape
    S = L // SUBLANES
    rows = min(OUT_ROWS // SUBLANES, S)
    per_batch = gate.shape[0] > 1
    mod_map = (lambda b, i: (b, 0, 0)) if per_batch else (lambda b, i: (0, 0, 0))
    NB = 1 if per_batch else min(B, max(1, OUT_ROWS // (SUBLANES * rows)))
    assert B % NB == 0
    seg = lambda C: pl.BlockSpec((NB, SUBLANES, rows, C), lambda b, i: (b, 0, i, 0))
    view = lambda a: a.reshape(B, SUBLANES, S, a.shape[-1])
    out = pl.pallas_call(
        functools.partial(_out_kernel, final=final),
        grid=(B // NB, S // rows),
        in_specs=[seg(RET_WIDTH), seg(HY_WIDTH),
                  pl.BlockSpec((NB, rows, SUBLANES, LRU_WIDTH), lambda b, i: (b, i, 0, 0)),
                  seg(D_MODEL),
                  pl.BlockSpec((1, 1, D_MODEL), mod_map),
                  pl.BlockSpec((None, D_MIX, D_MODEL), lambda b, i: (layer, 0, 0),
                               pipeline_mode=pl.Buffered(1)),
                  pl.BlockSpec((1, D_MODEL), lambda b, i: (0, 0))],
        out_specs=seg(D_MODEL),
        out_shape=jax.ShapeDtypeStruct((B, SUBLANES, S, D_MODEL), F32),
        compiler_params=_params(("arbitrary", "arbitrary")),
        name="out_proj",
    )(view(y_ret), view(y_hy), y_lru, view(x), gate, w_bf16, final_g)
    return out.reshape(B, L, D_MODEL)


RET_HEAD_ROWS = 2 * 2048
RET_BLOCK = 256


def _ret_kernel(*refs, L, use_rope, has_state, emit_state, chained, layer):
    it = iter(refs)
    q_ref, k_ref, v_ref, g_ref = next(it), next(it), next(it), next(it)
    cos_ref = sin_ref = s0_ref = sN_ref = None
    if use_rope:
        cos_ref, sin_ref = next(it), next(it)
    lg_ref = next(it)
    if has_state:
        s0_ref = next(it)
    if chained:
        next(it)
    y_ref = next(it)
    if emit_state:
        sN_ref = next(it)
    R_ref, M_ref, Z_ref, D_ref, KV_ref, kb_ref = (next(it) for _ in range(6))

    C = min(RET_BLOCK, L)
    DH = RET_HEAD_DIM
    HP = q_ref.shape[-1] // DH
    n = L // C
    cross = has_state or n > 1
    kscale = DH ** -0.5

    @pl.when(pl.program_id(1) == 0)
    def _():
        ii = lax.broadcasted_iota(jnp.int32, (C, C), 0)
        jj = lax.broadcasted_iota(jnp.int32, (C, C), 1)
        diff = (ii - jj).astype(F32)
        ri = lax.broadcasted_iota(jnp.int32, (C, DH), 0).astype(F32)
        for hh in range(HP):
            lgf = -_softplus(-lg_ref[0, hh])
            lgb = -_softplus(-lg_ref[1, hh])
            M_ref[hh] = (jnp.where(diff >= 0, jnp.exp(lgf[:, :C] * jnp.maximum(diff, 0.0)), 0.0)
                         + jnp.where(diff <= 0, jnp.exp(lgb[:, :C] * jnp.maximum(-diff, 0.0)), 0.0))
            lf, lb = lgf[:, :DH], lgb[:, :DH]
            Z_ref[hh, 0] = jnp.concatenate([jnp.exp(lf * (C - 1.0 - ri)), jnp.exp(lb * ri)], axis=1)
            Z_ref[hh, 1] = jnp.concatenate([jnp.exp(lf * (ri + 1.0)), jnp.exp(lb * (C - ri))], axis=1)
            D_ref[hh] = jnp.exp(jnp.concatenate([jnp.broadcast_to(lf * C, (DH, DH)),
                                                 jnp.broadcast_to(lb * C, (DH, DH))], axis=0))

    def rotary(x, r0):
        if not use_rope:
            return x
        return x * cos_ref[pl.ds(r0, C), :] + pltpu.roll(x, 64, axis=1) * sin_ref[pl.ds(r0, C), :]

    def chunk_kv(c, carry):
        r0 = pl.multiple_of(c * C, C)
        for hh in range(HP):
            lanes = slice(hh * DH, (hh + 1) * DH)
            k = rotary(k_ref[0, pl.ds(r0, C), lanes] * kscale, r0)
            kb_ref[hh, pl.ds(r0, C), :] = k.astype(BF16)
            kz = (jnp.concatenate([k, k], axis=1) * Z_ref[hh, 0]).astype(BF16)
            vb = v_ref[0, pl.ds(r0, C), lanes].astype(BF16)
            KV_ref[hh, c] = lax.dot_general(kz, vb, (((0,), (0,)), ((), ())), preferred_element_type=F32)
        return carry

    lax.fori_loop(0, n, chunk_kv, 0, unroll=4 if n % 4 == 0 else (2 if n % 2 == 0 else 1))

    for d in (0, 1):
        for hh in range(HP):
            R_ref[d, hh] = s0_ref[0, d, hh] if has_state else jnp.zeros((DH, DH), F32)

    def states(t, carry):
        for d in (0, 1):
            c = t if d == 0 else n - 1 - t
            rows = slice(d * DH, (d + 1) * DH)
            for hh in range(HP):
                R = R_ref[d, hh]
                R_ref[d, hh] = D_ref[hh, rows, :] * R + KV_ref[hh, c, rows, :]
                KV_ref[hh, c, rows, :] = R
        return carry

    lax.fori_loop(0, n, states, 0)
    if emit_state:
        for l in range(sN_ref.shape[1]):
            if chained or l == layer:
                for d in (0, 1):
                    for hh in range(HP):
                        sN_ref[0, l, d, hh] = R_ref[d, hh]
            else:
                sN_ref[0, l] = jnp.zeros(sN_ref.shape[2:], F32)

    def chunk_out(c, carry):
        r0 = pl.multiple_of(c * C, C)
        cols = []
        for hh in range(HP):
            lanes = slice(hh * DH, (hh + 1) * DH)
            q = rotary(q_ref[0, pl.ds(r0, C), lanes], r0)
            vb = v_ref[0, pl.ds(r0, C), lanes].astype(BF16)
            s = lax.dot_general(q.astype(BF16), kb_ref[hh, pl.ds(r0, C), :], (((1,), (1,)), ((), ())),
                                preferred_element_type=F32) * M_ref[hh]
            o = jnp.dot(s.astype(BF16), vb, preferred_element_type=F32)
            if cross:
                qx = (jnp.concatenate([q, q], axis=1) * Z_ref[hh, 1]).astype(BF16)
                o = o + jnp.dot(qx, KV_ref[hh, c].astype(BF16), preferred_element_type=F32)
            cols.append(o * lax.rsqrt(jnp.mean(o * o, axis=-1, keepdims=True) + EPS))
        g = g_ref[0, pl.ds(r0, C), :]
        y_ref[0, pl.ds(r0, C), :] = (jnp.concatenate(cols, axis=1) * _silu(g)).astype(BF16)
        return carry

    lax.fori_loop(0, n, chunk_out, 0, unroll=4 if n % 4 == 0 else (2 if n % 2 == 0 else 1))


def _retention(z, decay_logit, rope, state0, layer, states):
    B, L, _ = z.shape
    use_rope = rope is not None
    has_state = state0 is not None
    emit_state = states is not None
    chained = emit_state and not isinstance(states, str)
    H = RET_HEADS
    HP = min(H, max(1, RET_HEAD_ROWS // L))
    W = HP * RET_HEAD_DIM
    col = lambda off: pl.BlockSpec((1, L, W), lambda h, b, off=off: (b, 0, off * LANES // W + h))
    in_specs = [col(_Q0), col(_K0), col(_V0), col(_GR0)]
    args = [z, z, z, z]
    if use_rope:
        in_specs += [pl.BlockSpec((L, LANES), lambda h, b: (0, 0))] * 2
        args += list(rope)
    C = min(RET_BLOCK, L)
    in_specs.append(pl.BlockSpec((2, HP, 1, RET_BLOCK), lambda h, b: (0, h, 0, 0)))
    args.append(jnp.broadcast_to(decay_logit[:, :, None, None], (2, H, 1, RET_BLOCK)))
    if has_state:
        in_specs.append(pl.BlockSpec((1, None, 2, HP, RET_HEAD_DIM, RET_HEAD_DIM),
                                     lambda h, b: (b, layer, 0, h, 0, 0)))
        args.append(state0)
    out_specs = [pl.BlockSpec((1, L, W), lambda h, b: (b, 0, h))]
    out_shape = [jax.ShapeDtypeStruct((B, L, RET_WIDTH), BF16)]
    aliases = {}
    if emit_state:
        nl, l0 = (1, layer) if chained else (DEPTH, 0)
        out_specs.append(pl.BlockSpec((1, nl, 2, HP, RET_HEAD_DIM, RET_HEAD_DIM),
                                      lambda h, b: (b, l0, 0, h, 0, 0)))
        out_shape.append(jax.ShapeDtypeStruct((B, DEPTH, 2, H, RET_HEAD_DIM, RET_HEAD_DIM), F32))
        if chained:
            aliases = {len(args): 1}
            in_specs.append(pl.BlockSpec(memory_space=pl.ANY))
            args.append(states)
    return pl.pallas_call(
        functools.partial(_ret_kernel, L=L, use_rope=use_rope, has_state=has_state,
                          emit_state=emit_state, chained=chained, layer=layer),
        grid=(H // HP, B),
        in_specs=in_specs,
        out_specs=out_specs,
        out_shape=out_shape,
        input_output_aliases=aliases,
        scratch_shapes=[pltpu.VMEM((2, HP, RET_HEAD_DIM, RET_HEAD_DIM), F32),
                        pltpu.VMEM((HP, C, C), F32),
                        pltpu.VMEM((HP, 2, C, 2 * RET_HEAD_DIM), F32),
                        pltpu.VMEM((HP, 2 * RET_HEAD_DIM, RET_HEAD_DIM), F32),
                        pltpu.VMEM((HP, L // C, 2 * RET_HEAD_DIM, RET_HEAD_DIM), F32),
                        pltpu.VMEM((HP, L, RET_HEAD_DIM), BF16)],
        compiler_params=_params(("arbitrary", "arbitrary")),
        name="retention",
    )(*args)


def _dft_matrices(T):
    N = 2 * T
    k = np.arange(T, dtype=np.float64)[:, None]
    m = np.arange(T, dtype=np.float64)[None, :]
    ang = 2.0 * np.pi * k * m / N
    fwd = np.concatenate([np.cos(ang), -np.sin(ang)], axis=0)
    fwd[T, :] = (-1.0) ** np.arange(T)
    ck = np.full((T,), 2.0)
    ck[0] = 1.0
    inv_re = (np.cos(ang) * ck[:, None]).T / N
    inv_im = (-2.0 * np.sin(ang)).T / N
    inv_im[:, 0] = ((-1.0) ** np.arange(T)) / N
    inv = np.concatenate([inv_re, inv_im], axis=1)
    return jnp.asarray(fwd, F32).astype(BF16), jnp.asarray(inv, F32).astype(BF16)


def _split_bf16(x):
    hi = x.astype(BF16)
    return hi, (x - hi.astype(F32)).astype(BF16)


def _dot1(a, b):
    return jnp.dot(a, b.astype(BF16), preferred_element_type=F32)


def _filt_kernel(zemb_ref, w1_ref, b1_ref, w2_ref, b2_ref, fr_ref, w3f_ref, w3b_ref, dec_ref,
                 bias_ref, fwd_ref, kre_ref, kim_ref, hid_ref, hn_ref, F_ref, *, L):
    T = fwd_ref.shape[1]
    nb = L // T

    @pl.when((pl.program_id(0) == 0) & (pl.program_id(1) == 0))
    def _():
        z1 = jnp.dot(zemb_ref[...], w1_ref[...], precision=HI, preferred_element_type=F32) + b1_ref[...]
        h1 = jnp.sin(fr_ref[0:1, :] * z1)
        z2 = jnp.dot(h1, w2_ref[...], precision=HI, preferred_element_type=F32) + b2_ref[...]
        hid = jnp.sin(fr_ref[1:2, :] * z2)
        hid_ref[0], hid_ref[1] = _split_bf16(hid)

    w3 = jnp.concatenate([w3f_ref[...], w3b_ref[...]], axis=1)
    w_hi, w_lo = _split_bf16(w3)
    h = (jnp.dot(hid_ref[0], w_hi, preferred_element_type=F32)
         + jnp.dot(hid_ref[1], w_hi, preferred_element_type=F32)
         + jnp.dot(hid_ref[0], w_lo, preferred_element_type=F32))
    h = h * jnp.concatenate([dec_ref[...]] * 2, axis=1)
    h = h / (jnp.sum(jnp.abs(h), axis=0, keepdims=True) + EPS)
    for dr in (0, 1):
        hn_ref[dr] = h[:, dr * LANES:(dr + 1) * LANES]
    for blk in range(nb):
        F = _dot1(fwd_ref[...], h[blk * T:(blk + 1) * T])
        for dr in (0, 1):
            F_ref[dr, blk] = F[:, dr * LANES:(dr + 1) * LANES]

    row = lax.broadcasted_iota(jnp.int32, (T, LANES), 0)
    sgn = jnp.where((row & 1) == 0, 1.0, -1.0).astype(F32)
    row0 = row == 0
    bias = bias_ref[0]
    for d in range(-(nb - 1), nb):
        re = jnp.zeros((T, LANES), F32)
        im = jnp.zeros((T, LANES), F32)
        nyq = jnp.zeros((1, LANES), F32)
        if d >= 0:
            F = F_ref[0, d]
            re, im, nyq = re + F[:T], im + F[T:], nyq + F[T:T + 1]
        if d >= 1:
            F = F_ref[0, d - 1]
            p0 = hn_ref[0, (d - 1) * T:(d - 1) * T + 1, :]
            re, im, nyq = re + sgn * (F[:T] - p0), im + sgn * F[T:], nyq + (F[T:T + 1] - p0)
        e = -d
        if e >= 0:
            F = F_ref[1, e]
            re, im, nyq = re + F[:T], im - F[T:], nyq + F[T:T + 1]
        if e >= 1:
            F = F_ref[1, e - 1]
            p0 = hn_ref[1, (e - 1) * T:(e - 1) * T + 1, :]
            re, im, nyq = re + sgn * (F[:T] - p0), im - sgn * F[T:], nyq + (F[T:T + 1] - p0)
        if d == 0:
            re, nyq = re + bias, nyq + bias
        kre_ref[0, d + nb - 1] = re
        kim_ref[0, d + nb - 1] = jnp.where(row0, nyq, im)


def _hyena_filter_spectra(L, w1, b1, w2, b2, w3, freq, bias, fwd):
    T = fwd.shape[1]
    nb = L // T
    nlag = 2 * nb - 1
    P = LANES
    t = np.linspace(0.0, 1.0, L)[:, None]
    n_bands = (HY_EMB - 1) // 2
    f = np.linspace(1e-4, n_bands - 1, n_bands)
    ang = (2.0 * math.pi / L) * np.arange(L)[:, None] * f[None, :]
    zemb = np.concatenate([t, np.cos(ang), -np.sin(ang)], axis=-1)
    zemb = jnp.asarray(np.pad(zemb, ((0, 0), (0, P - HY_EMB))), F32)
    min_decay = math.log(HY_TARGET) / HY_LONG_DECAY_PCT
    max_decay = math.log(HY_TARGET) / HY_SHORT_DECAY_PCT
    deltas = np.abs(np.linspace(min_decay, max_decay, HY_WIDTH))
    dec = jnp.asarray(np.exp(-t * deltas[None, :]), F32)
    pad = P - HY_FFN
    w1p = jnp.pad(w1, ((0, P - HY_EMB), (0, pad)))
    b1p = jnp.pad(b1, (0, pad)).reshape(1, P)
    w2p = jnp.pad(w2, ((0, pad), (0, pad)))
    b2p = jnp.pad(b2, (0, pad)).reshape(1, P)
    w3p = jnp.pad(w3, ((0, pad), (0, 0)))
    frp = jnp.pad(freq, ((0, 0), (0, pad)))
    ncg = HY_WIDTH // P
    const = lambda shape: pl.BlockSpec(shape, lambda o, c: tuple(0 for _ in shape))
    kshape = jax.ShapeDtypeStruct((HY_ORDER, nlag, T, HY_WIDTH), F32)
    kspec = pl.BlockSpec((1, nlag, T, P), lambda o, c: (o, 0, 0, c))
    return pl.pallas_call(
        functools.partial(_filt_kernel, L=L),
        grid=(HY_ORDER, ncg),
        in_specs=[const((L, P)), const((P, P)), const((1, P)), const((P, P)), const((1, P)), const((2, P)),
                  pl.BlockSpec((P, P), lambda o, c: (0, o * 2 * ncg + c)),
                  pl.BlockSpec((P, P), lambda o, c: (0, o * 2 * ncg + ncg + c)),
                  pl.BlockSpec((L, P), lambda o, c: (0, c)),
                  pl.BlockSpec((1, 1, P), lambda o, c: (o, 0, c)),
                  const((2 * T, T))],
        out_specs=[kspec, kspec],
        out_shape=[kshape, kshape],
        scratch_shapes=[pltpu.VMEM((2, L, P), BF16), pltpu.VMEM((2, L, P), F32),
                        pltpu.VMEM((2, nb, 2 * T, P), F32)],
        compiler_params=_params(("arbitrary", "arbitrary")),
        name="hyena_filters",
    )(zemb, w1p, b1p, w2p, b2p, frp, w3p, w3p, dec, bias.reshape(HY_ORDER, 1, HY_WIDTH), fwd)


CONV_ROWS = 128


def _conv_block(x_ref, t0, w, b, left):
    L, C = x_ref.shape
    T = CONV_ROWS
    blk = x_ref[t0:t0 + T, :]
    prev = x_ref[t0 - SUBLANES:t0, :] if t0 > 0 else jnp.zeros((SUBLANES, C), F32)
    nxt = x_ref[t0 + T:t0 + T + SUBLANES, :] if t0 + T < L else jnp.zeros((SUBLANES, C), F32)
    ext = jnp.concatenate([prev, blk, nxt], axis=0)
    n = T + 2 * SUBLANES
    acc = jnp.broadcast_to(b, (T, C))
    for tap in range(w.shape[0]):
        s = tap - left
        sh = blk if s == 0 else pltpu.roll(ext, (-s) % n, axis=0)[SUBLANES:SUBLANES + T]
        acc = acc + w[tap:tap + 1, :] * sh
    return acc


HY_LANES_ROWS = 2 * 2048
HY_MAX_BATCH = 8


def _toeplitz(k, u, mul):
    n = len(u)
    if n == 1:
        return [mul(k[0], u[0])]
    h = n // 2
    add = lambda a, b: tuple(x + y for x, y in zip(a, b))
    sub = lambda a, b: tuple(x - y for x, y in zip(a, b))
    t0, t1, t2 = k[h:h + n - 1], k[0:n - 1], k[n:2 * n - 1]
    p = _toeplitz(t0, [add(a, b) for a, b in zip(u[:h], u[h:])], mul)
    q = _toeplitz([sub(a, b) for a, b in zip(t1, t0)], u[h:], mul)
    r = _toeplitz([sub(a, b) for a, b in zip(t2, t0)], u[:h], mul)
    return [add(a, b) for a, b in zip(p, q)] + [add(a, b) for a, b in zip(p, r)]


def _hy_kernel(v_ref, x1_ref, x2_ref, g_ref, wv_ref, w1_ref, w2_ref, bv_ref, b1_ref, b2_ref,
               kre_ref, kim_ref, fwd_ref, inv_ref, y_ref,
               x1c_ref, x2c_ref, u_ref, U_ref, Y_ref, *, L):
    T = fwd_ref.shape[1]
    P = LANES
    nb = L // T
    NBT = v_ref.shape[0]
    left = (HY_SHORT - 1) // 2
    def block_spectrum(j, r0):
        U_ref[j] = jnp.dot(fwd_ref[...], u_ref[pl.ds(r0, T), :], preferred_element_type=F32)

    for j in range(nb):
        for t0 in range(j * T, (j + 1) * T, CONV_ROWS):
            rows = slice(t0, t0 + CONV_ROWS)
            for p in range(NBT):
                u_ref[rows, p * P:(p + 1) * P] = _conv_block(v_ref.at[p], t0, wv_ref[...], bv_ref[...], left).astype(BF16)
                x1c_ref[p, rows, :] = _conv_block(x1_ref.at[p], t0, w1_ref[...], b1_ref[...], left)
                x2c_ref[p, rows, :] = _conv_block(x2_ref.at[p], t0, w2_ref[...], b2_ref[...], left)
        block_spectrum(j, j * T)

    row0 = lax.broadcasted_iota(jnp.int32, (SUBLANES, NBT * P), 0) == 0
    tile = lambda x: jnp.concatenate([x] * NBT, axis=1)

    def cmul(k, u):
        kr, ki = tile(k[0]), tile(k[1])
        return (kr * u[0] - ki * u[1], kr * u[1] + ki * u[0])

    def rmul(k, u):
        return (tile(k[0]) * u[0], tile(k[1]) * u[1])

    def spectra_products(o):
        def load(r, rows):
            k = [(kre_ref[o, l, pl.ds(r, rows), :], kim_ref[o, l, pl.ds(r, rows), :]) for l in range(2 * nb - 1)]
            u = [(U_ref[j, pl.ds(r, rows), :], U_ref[j, pl.ds(T + r, rows), :]) for j in range(nb)]
            return k, u

        def chunk(rc, carry):
            r = pl.multiple_of(rc * SUBLANES, SUBLANES)
            y = _toeplitz(*load(r, SUBLANES), cmul)
            for i in range(nb):
                Y_ref[i, pl.ds(r, SUBLANES), :] = y[i][0]
                Y_ref[i, pl.ds(T + r, SUBLANES), :] = y[i][1]
            return carry

        lax.fori_loop(0, T // SUBLANES, chunk, 0, unroll=4)
        y = _toeplitz(*load(0, SUBLANES), rmul)
        for i in range(nb):
            Y_ref[i, 0:SUBLANES, :] = jnp.where(row0, y[i][0], Y_ref[i, 0:SUBLANES, :])
            Y_ref[i, T:T + SUBLANES, :] = jnp.where(row0, y[i][1], Y_ref[i, T:T + SUBLANES, :])

    def long_conv_block(i):
        return _dot1(inv_ref[...], Y_ref[i])

    spectra_products(0)

    def order0(i, carry):
        r0 = pl.multiple_of(i * T, T)
        y = long_conv_block(i)
        for p in range(NBT):
            z1 = x1c_ref[p, pl.ds(r0, T), :] * y[:, p * P:(p + 1) * P]
            u_ref[pl.ds(r0, T), p * P:(p + 1) * P] = z1.astype(BF16)
        return carry

    lax.fori_loop(0, nb, order0, 0, unroll=4 if nb % 4 == 0 else 1)
    for j in range(nb):
        block_spectrum(j, j * T)
    spectra_products(1)

    def order1(i, carry):
        r0 = pl.multiple_of(i * T, T)
        y = long_conv_block(i)
        for p in range(NBT):
            out = x2c_ref[p, pl.ds(r0, T), :] * y[:, p * P:(p + 1) * P]
            y_ref[p, pl.ds(r0, T), :] = (out * _silu(g_ref[p, pl.ds(r0, T), :])).astype(BF16)
        return carry

    lax.fori_loop(0, nb, order1, 0, unroll=4 if nb % 4 == 0 else 1)


def _hyena(z, conv_w, conv_b, kre, kim, fwd, inv):
    B, L, _ = z.shape
    NBT = min(B, max(2, HY_LANES_ROWS // L), HY_MAX_BATCH)
    assert B % NBT == 0
    T = fwd.shape[1]
    nb = L // T
    nlag = 2 * nb - 1
    P = LANES
    ncg = HY_WIDTH // P
    col = lambda off: pl.BlockSpec((NBT, L, P), lambda c, b, off=off: (b, 0, off + c))
    wspec = lambda part: pl.BlockSpec((HY_SHORT, P), lambda c, b, part=part: (0, part * ncg + c))
    bspec = lambda part: pl.BlockSpec((1, P), lambda c, b, part=part: (0, part * ncg + c))
    kspec = pl.BlockSpec((HY_ORDER, nlag, T, P), lambda c, b: (0, 0, 0, c))
    fspec = pl.BlockSpec((2 * T, T), lambda c, b: (0, 0))
    ispec = pl.BlockSpec((T, 2 * T), lambda c, b: (0, 0))
    cb = conv_b.reshape(1, 3 * HY_WIDTH)
    return pl.pallas_call(
        functools.partial(_hy_kernel, L=L),
        grid=(ncg, B // NBT),
        in_specs=[col(_HV0), col(_HX10), col(_HX20), col(_GH0),
                  wspec(0), wspec(1), wspec(2), bspec(0), bspec(1), bspec(2),
                  kspec, kspec, fspec, ispec],
        out_specs=pl.BlockSpec((NBT, L, P), lambda c, b: (b, 0, c)),
        out_shape=jax.ShapeDtypeStruct((B, L, HY_WIDTH), BF16),
        scratch_shapes=[pltpu.VMEM((NBT, L, P), F32), pltpu.VMEM((NBT, L, P), F32),
                        pltpu.VMEM((L, NBT * P), BF16),
                        pltpu.VMEM((nb, 2 * T, NBT * P), F32), pltpu.VMEM((nb, 2 * T, NBT * P), F32)],
        compiler_params=_params(("arbitrary", "arbitrary")),
        name="hyena",
    )(z, z, z, z, conv_w, conv_w, conv_w, cb, cb, cb, kre, kim, fwd, inv)


LRU_TILE = 256
LRU_GROUPS = LRU_TILE // LANES
LRU_ROWS = 128
LRU_STEP_ROWS = 1024


def _lru_kernel(*refs, L, has_state):
    for bi in range(refs[0].shape[0]):
        _lru_one(*refs, L=L, has_state=has_state, bi=bi)


def _lru_one(*refs, L, has_state, bi):
    it = iter(refs)
    x_ref, g_ref, cw_ref, cb_ref, wg_ref, gb_ref, lam_ref = (next(it) for _ in range(7))
    h0_ref = next(it) if has_state else None
    y_ref, sN_ref, xp_ref, A_ref, B_ref, H_ref, C_ref = (next(it) for _ in range(7))

    S = L // SUBLANES
    W = LRU_TILE
    G = LRU_GROUPS
    R = LRU_ROWS
    left = LRU_CONV // 2
    pad = left * SUBLANES

    sub = lax.broadcasted_iota(jnp.int32, (SUBLANES, W), 0)
    for k in range(left):
        tail = pltpu.roll(x_ref[bi, L - pad + k * SUBLANES:L - pad + (k + 1) * SUBLANES, :], 1, axis=0)
        xp_ref[k * SUBLANES:(k + 1) * SUBLANES, :] = jnp.where(sub == 0, 0.0, tail)
    for k in range(LRU_CONV - 1 - left):
        head = pltpu.roll(x_ref[bi, k * SUBLANES:(k + 1) * SUBLANES, :], SUBLANES - 1, axis=0)
        xp_ref[pad + L + k * SUBLANES:pad + L + (k + 1) * SUBLANES, :] = jnp.where(sub == SUBLANES - 1, 0.0, head)

    def copy(c, carry):
        r0 = pl.multiple_of(c * R, R)
        xp_ref[pl.ds(pad + r0, R), :] = x_ref[bi, pl.ds(r0, R), :]
        return carry

    lax.fori_loop(0, L // R, copy, 0)

    csp = [(0.5 * LRU_C) * _softplus(-lam_ref[d]) for d in (0, 1)]

    def coeffs(c, carry):
        r0 = pl.multiple_of(c * R, R)
        u = jnp.broadcast_to(cb_ref[...], (R, W))
        for k in range(LRU_CONV):
            u = u + cw_ref[k:k + 1, :] * xp_ref[pl.ds(r0 + k * SUBLANES, R), :]
        half = jnp.dot(u.astype(BF16), wg_ref[0], preferred_element_type=F32) + gb_ref[0]
        hu = 0.5 * u
        for d in (0, 1):
            t_r = jnp.tanh(half[:, (2 * d) * W:(2 * d + 1) * W])
            t_i = jnp.tanh(half[:, (2 * d + 1) * W:(2 * d + 2) * W])
            nl = csp[d] * t_r + csp[d]
            a = jnp.exp(-nl)
            s2 = jnp.tanh(nl) * (a * a + 1.0)
            b = jnp.where(s2 > 0.0, s2 * lax.rsqrt(s2), 0.0) * (t_i * hu + hu)
            for gi in range(G):
                lanes = slice(gi * LANES, (gi + 1) * LANES)
                A_ref[d, gi, pl.ds(r0, R), :] = a[:, lanes]
                B_ref[d, gi, pl.ds(r0, R), :] = b[:, lanes]
        return carry

    lax.fori_loop(0, L // R, coeffs, 0, unroll=2)

    def scan_body(t, carry):
        out = []
        for d in (0, 1):
            r8 = pl.multiple_of((t if d == 0 else S - 1 - t) * SUBLANES, SUBLANES)
            for gi in range(G):
                h, acc = carry[d * G + gi]
                a = A_ref[d, gi, pl.ds(r8, SUBLANES), :]
                h = a * h + B_ref[d, gi, pl.ds(r8, SUBLANES), :]
                acc = acc * a
                H_ref[d, gi, pl.ds(r8, SUBLANES), :] = h
                C_ref[d, gi, pl.ds(r8, SUBLANES), :] = acc
                out.append((h, acc))
        return tuple(out)

    init = tuple((jnp.zeros((SUBLANES, LANES), F32), jnp.ones((SUBLANES, LANES), F32))
                 for _ in range(2 * G))
    lax.fori_loop(0, S, scan_body, init, unroll=8)

    hin = [None] * (2 * G)
    for d in (0, 1):
        last = (S - 1 if d == 0 else 0) * SUBLANES
        for gi in range(G):
            hl = H_ref[d, gi, last:last + SUBLANES, :]
            ac = C_ref[d, gi, last:last + SUBLANES, :]
            if has_state:
                h = h0_ref[bi, d:d + 1, gi * LANES:(gi + 1) * LANES]
            else:
                h = jnp.zeros((1, LANES), F32)
            rows = [None] * SUBLANES
            for j in (range(SUBLANES) if d == 0 else range(SUBLANES - 1, -1, -1)):
                rows[j] = h
                h = hl[j:j + 1, :] + ac[j:j + 1, :] * h
            hin[d * G + gi] = jnp.concatenate([jnp.concatenate(rows, axis=0)] * (R // SUBLANES), axis=0)
            sN_ref[bi, d:d + 1, gi * LANES:(gi + 1) * LANES] = h

    def combine(c, carry):
        r0 = pl.multiple_of(c * R, R)
        cols = []
        for gi in range(G):
            tot = jnp.zeros((R, LANES), F32)
            for d in (0, 1):
                tot = tot + (H_ref[d, gi, pl.ds(r0, R), :] + C_ref[d, gi, pl.ds(r0, R), :] * hin[d * G + gi])
            cols.append(tot)
        y_ref[bi, pl.ds(r0, R), :] = jnp.concatenate(cols, axis=1) * _silu(g_ref[bi, pl.ds(r0, R), :])
        return carry

    lax.fori_loop(0, L // R, combine, 0, unroll=2)


def _rglru(z4, conv_w, conv_b, gate_w, gate_b, lam, h0):
    B, S, _, _ = z4.shape
    L = S * SUBLANES
    has_state = h0 is not None
    W = LRU_TILE
    nh = LRU_WIDTH // W
    bpt = W // LRU_BLOCK_DIM
    gw = gate_w.reshape(2, 2, nh, bpt, LRU_BLOCK_DIM, LRU_BLOCK_DIM)
    eye = jnp.eye(bpt, dtype=F32)
    dense = (0.5 * jnp.einsum('dkhnij,nm->hnidkmj', gw, eye)).reshape(nh, W, 4 * W).astype(BF16)
    gb = 0.5 * gate_b.reshape(2, 2, nh, W).transpose(2, 0, 1, 3).reshape(nh, 1, 4 * W)
    zr = z4.reshape(B, L, D_LRU)
    NB = min(B, max(1, LRU_STEP_ROWS // L))
    assert B % NB == 0
    off = lambda base: pl.BlockSpec((NB, L, W), lambda h, b, base=base: (b, 0, base + h))
    in_specs = [off(0), off(LRU_WIDTH // W),
                pl.BlockSpec((LRU_CONV, W), lambda h, b: (0, h)),
                pl.BlockSpec((1, W), lambda h, b: (0, h)),
                pl.BlockSpec((1, W, 4 * W), lambda h, b: (h, 0, 0)),
                pl.BlockSpec((1, 1, 4 * W), lambda h, b: (h, 0, 0)),
                pl.BlockSpec((2, 1, W), lambda h, b: (0, 0, h))]
    args = [zr, zr, conv_w, conv_b.reshape(1, LRU_WIDTH), dense, gb, lam.reshape(2, 1, LRU_WIDTH)]
    if has_state:
        in_specs.append(pl.BlockSpec((NB, 2, W), lambda h, b: (b, 0, h)))
        args.append(h0)
    y, s = pl.pallas_call(
        functools.partial(_lru_kernel, L=L, has_state=has_state),
        grid=(nh, B // NB),
        in_specs=in_specs,
        out_specs=[pl.BlockSpec((NB, L, W), lambda h, b: (b, 0, h)),
                   pl.BlockSpec((NB, 2, W), lambda h, b: (b, 0, h))],
        out_shape=[jax.ShapeDtypeStruct((B, L, LRU_WIDTH), F32),
                   jax.ShapeDtypeStruct((B, 2, LRU_WIDTH), F32)],
        scratch_shapes=[pltpu.VMEM((L + (LRU_CONV - 1) * SUBLANES, W), F32)]
        + [pltpu.VMEM((2, LRU_GROUPS, L, LANES), F32) for _ in range(4)],
        compiler_params=_params(("arbitrary", "arbitrary")),
        name="rglru",
    )(*args)
    return y.reshape(B, S, SUBLANES, LRU_WIDTH), s


def _rope_tables(L):
    rows = L // GRID_W
    row = np.repeat(np.arange(rows, dtype=np.float64), GRID_W)
    col = np.tile(np.arange(GRID_W, dtype=np.float64), rows)
    n_f = RET_HEAD_DIM // 4
    inv = ROPE_BASE ** (-np.arange(n_f, dtype=np.float64) / n_f)
    ang = np.concatenate([row[:, None] * inv[None], col[:, None] * inv[None]], axis=-1)
    cos, sin = np.cos(ang), np.sin(ang)
    return (jnp.asarray(np.concatenate([cos, cos], axis=-1), F32),
            jnp.asarray(np.concatenate([-sin, sin], axis=-1), F32))


def kernel(x_prompt, x_sample, state_ret, state_lru, c, c_ctx, norm_g, ada_w, ada_b, w_in, ret_decay_logit, hy_conv_w, hy_conv_b, hy_ffn_w1, hy_ffn_b1, hy_ffn_w2, hy_ffn_b2, hy_ffn_w3, hy_freq, hy_bias, lru_conv_w, lru_conv_b, lru_gate_w, lru_gate_b, lru_lambda, w_out, final_g):
    Bp, Lp, _ = x_prompt.shape
    Bs, Ls, _ = x_sample.shape
    assert Bs + 1 <= SUBLANES

    cvec = jnp.zeros((SUBLANES, D_MODEL), F32).at[:Bs].set(c).at[Bs].set(c_ctx)
    mod = _modulation(cvec, ada_w, ada_b)
    rope = _rope_tables(Ls)
    dft = {L: _dft_matrices(min(DFT_MAX, L)) for L in {Lp, Ls}}
    w_in_b = w_in.astype(BF16)
    w_out_b = w_out.astype(BF16)
    fg = final_g.reshape(1, D_MODEL)

    xc, xl = x_prompt, x_sample
    new_state_ret, new_lru = None, []
    for l in range(DEPTH):
        shift, scale, gate = (mod[l, :, i * D_MODEL:(i + 1) * D_MODEL] for i in range(3))
        g = norm_g[l].reshape(1, D_MODEL)
        final = l == DEPTH - 1
        filt = {}
        for L in sorted({Lp, Ls}):
            filt[L] = _hyena_filter_spectra(L, hy_ffn_w1[l], hy_ffn_b1[l], hy_ffn_w2[l], hy_ffn_b2[l],
                                            hy_ffn_w3[l], hy_freq[l], hy_bias[l], dft[L][0])

        def layer(x, sel, rope_t, ret_s0, lru_s0, ret_states):
            sh, sc, gt = (m[sel][:, None, :] for m in (shift, scale, gate))
            z, z_lru = _in_proj(x, sh, sc, g, w_in_b, l)
            y_ret, *s_ret = _retention(z, ret_decay_logit[l], rope_t, ret_s0, l, ret_states)
            kre, kim = filt[x.shape[1]]
            y_hy = _hyena(z, hy_conv_w[l], hy_conv_b[l], kre, kim, *dft[x.shape[1]])
            y_lru, s_lru = _rglru(z_lru, lru_conv_w[l], lru_conv_b[l], lru_gate_w[l], lru_gate_b[l],
                                  lru_lambda[l], lru_s0)
            return _out_proj(y_ret, y_hy, y_lru, x, gt, w_out_b, l, fg, final), s_ret, s_lru

        xc, (new_state_ret,), ls = layer(xc, slice(Bs, Bs + 1), None, None, None,
                                         "new" if l == 0 else new_state_ret)
        new_lru.append(ls)
        xl, _, _ = layer(xl, slice(0, Bs), rope, state_ret, state_lru[:, l], None)

    new_state_lru = jnp.stack(new_lru, axis=1).astype(x_prompt.dtype)
    return (xc, xl, new_state_ret.astype(x_prompt.dtype), new_state_lru)
```

```python
import functools
import math

import numpy as np
import jax
import jax.numpy as jnp
from jax import lax
from jax.experimental import pallas as pl
from jax.experimental.pallas import tpu as pltpu

F32 = jnp.float32
BF16 = jnp.bfloat16
HI = lax.Precision.HIGHEST

D_MODEL = 1024
DEPTH = 2
GRID_W = 64
EPS = 1e-6
RET_HEADS = 4
RET_HEAD_DIM = 128
RET_WIDTH = RET_HEADS * RET_HEAD_DIM
ROPE_BASE = 10000.0
HY_WIDTH = 512
HY_ORDER = 2
HY_SHORT = 3
HY_EMB = 33
HY_FFN = 64
HY_SHORT_DECAY_PCT = 0.3
HY_LONG_DECAY_PCT = 1.5
HY_TARGET = 1e-2
LRU_WIDTH = 512
LRU_BLOCKS = 8
LRU_BLOCK_DIM = LRU_WIDTH // LRU_BLOCKS
LRU_CONV = 4
LRU_C = 8.0
D_MIX = RET_WIDTH + HY_WIDTH + LRU_WIDTH
D_IN = 4 * RET_WIDTH + 4 * HY_WIDTH + 2 * LRU_WIDTH

LANES = 128
SUBLANES = 8
DFT_MAX = 512
ROW_TILE = 256
VMEM_LIMIT = 56 * 1024 * 1024

_Q0, _K0, _V0, _GR0 = 0, 4, 8, 12
_HV0, _HX10, _HX20, _GH0 = 16, 20, 24, 28


def _sigmoid(x):
    return 0.5 * jnp.tanh(0.5 * x) + 0.5


def _silu(x):
    return x * _sigmoid(x)


def _softplus(x):
    return jnp.maximum(x, 0.0) + jnp.log1p(jnp.exp(-jnp.abs(x)))


def _params(sem):
    return pltpu.CompilerParams(dimension_semantics=sem, vmem_limit_bytes=VMEM_LIMIT)


def _mod_kernel(c_ref, w_ref, b_ref, o_ref):
    s_hi, s_lo = _split_bf16(_silu(c_ref[...]))
    w_hi, w_lo = _split_bf16(w_ref[0])
    o_ref[0] = (jnp.dot(s_hi, w_hi, preferred_element_type=F32)
                + jnp.dot(s_lo, w_hi, preferred_element_type=F32)
                + jnp.dot(s_hi, w_lo, preferred_element_type=F32)) + b_ref[0]


def _modulation(cvec, ada_w, ada_b):
    tn = 768
    return pl.pallas_call(
        _mod_kernel,
        grid=(DEPTH, 3 * D_MODEL // tn),
        in_specs=[pl.BlockSpec((SUBLANES, D_MODEL), lambda l, j: (0, 0)),
                  pl.BlockSpec((1, D_MODEL, tn), lambda l, j: (l, 0, j)),
                  pl.BlockSpec((1, 1, tn), lambda l, j: (l, 0, j))],
        out_specs=pl.BlockSpec((1, SUBLANES, tn), lambda l, j: (l, 0, j)),
        out_shape=jax.ShapeDtypeStruct((DEPTH, SUBLANES, 3 * D_MODEL), F32),
        compiler_params=_params(("arbitrary", "arbitrary")),
        name="modulation",
    )(cvec, ada_w, ada_b.reshape(DEPTH, 1, 3 * D_MODEL))


D_MAIN = 4 * RET_WIDTH + 4 * HY_WIDTH
D_LRU = 2 * LRU_WIDTH


def _in_kernel(x_ref, sh_ref, sc_ref, g_ref, w_ref, o_ref, ol_ref):
    _, nseg, rows, _ = x_ref.shape
    x = x_ref[0].reshape(nseg * rows, D_MODEL)
    ms = jnp.mean(x * x, axis=-1, keepdims=True)
    y = x * lax.rsqrt(ms + EPS) * g_ref[...]
    h = (y * (1.0 + sc_ref[0]) + sh_ref[0]).astype(BF16)
    tn = 1024
    for n in range(D_MAIN // tn):
        z = jnp.dot(h, w_ref[:, n * tn:(n + 1) * tn], preferred_element_type=F32)
        o_ref[0, :, :, n * tn:(n + 1) * tn] = z.reshape(nseg, rows, tn)
    zl = jnp.dot(h, w_ref[:, D_MAIN:D_IN], preferred_element_type=F32)
    for j in range(nseg):
        ol_ref[0, :, j, :] = zl[j * rows:(j + 1) * rows]


def _in_proj(x, shift, scale, g, w_bf16, layer):
    B, L, _ = x.shape
    S = L // SUBLANES
    rows = min(ROW_TILE // SUBLANES, S)
    per_batch = shift.shape[0] > 1
    mod_map = (lambda b, i: (b, 0, 0)) if per_batch else (lambda b, i: (0, 0, 0))
    seg = lambda C: pl.BlockSpec((1, SUBLANES, rows, C), lambda b, i: (b, 0, i, 0))
    z, z_lru = pl.pallas_call(
        _in_kernel,
        grid=(B, S // rows),
        in_specs=[seg(D_MODEL),
                  pl.BlockSpec((1, 1, D_MODEL), mod_map),
                  pl.BlockSpec((1, 1, D_MODEL), mod_map),
                  pl.BlockSpec((1, D_MODEL), lambda b, i: (0, 0)),
                  pl.BlockSpec((None, D_MODEL, D_IN), lambda b, i: (layer, 0, 0),
                               pipeline_mode=pl.Buffered(1))],
        out_specs=[seg(D_MAIN), pl.BlockSpec((1, rows, SUBLANES, D_LRU), lambda b, i: (b, i, 0, 0))],
        out_shape=[jax.ShapeDtypeStruct((B, SUBLANES, S, D_MAIN), F32),
                   jax.ShapeDtypeStruct((B, S, SUBLANES, D_LRU), F32)],
        compiler_params=_params(("arbitrary", "arbitrary")),
        name="in_proj",
    )(x.reshape(B, SUBLANES, S, D_MODEL), shift, scale, g, w_bf16)
    return z.reshape(B, L, D_MAIN), z_lru


OUT_ROWS = 1024


def _out_kernel(yr_ref, yh_ref, yl_ref, x_ref, gate_ref, w_ref, fg_ref, o_ref, *, final):
    nb, nseg, rows, _ = x_ref.shape
    n = nb * nseg * rows
    acc = jnp.dot(yr_ref[...].reshape(n, RET_WIDTH), w_ref[0:RET_WIDTH], preferred_element_type=F32)
    acc = acc + jnp.dot(yh_ref[...].reshape(n, HY_WIDTH), w_ref[RET_WIDTH:RET_WIDTH + HY_WIDTH],
                        preferred_element_type=F32)
    yl = jnp.concatenate([yl_ref[bi, :, j, :] for bi in range(nb) for j in range(nseg)], axis=0)
    acc = acc + jnp.dot(yl.astype(BF16), w_ref[RET_WIDTH + HY_WIDTH:D_MIX], preferred_element_type=F32)
    x = x_ref[...].reshape(n, D_MODEL) + gate_ref[0] * acc
    if final:
        ms = jnp.mean(x * x, axis=-1, keepdims=True)
        x = x * lax.rsqrt(ms + EPS) * fg_ref[...]
    o_ref[...] = x.reshape(nb, nseg, rows, D_MODEL)


def _out_proj(y_ret, y_hy, y_lru, x, gate, w_bf16, layer, final_g, final):
    B, L, _ = x.shape
    S = L // SUBLANES
    rows = min(OUT_ROWS // SUBLANES, S)
    per_batch = gate.shape[0] > 1
    mod_map = (lambda b, i: (b, 0, 0)) if per_batch else (lambda b, i: (0, 0, 0))
    NB = 1 if per_batch else min(B, max(1, OUT_ROWS // (SUBLANES * rows)))
    assert B % NB == 0
    seg = lambda C: pl.BlockSpec((NB, SUBLANES, rows, C), lambda b, i: (b, 0, i, 0))
    view = lambda a: a.reshape(B, SUBLANES, S, a.shape[-1])
    out = pl.pallas_call(
        functools.partial(_out_kernel, final=final),
        grid=(B // NB, S // rows),
        in_specs=[seg(RET_WIDTH), seg(HY_WIDTH),
                  pl.BlockSpec((NB, rows, SUBLANES, LRU_WIDTH), lambda b, i: (b, i, 0, 0)),
                  seg(D_MODEL),
                  pl.BlockSpec((1, 1, D_MODEL), mod_map),
                  pl.BlockSpec((None, D_MIX, D_MODEL), lambda b, i: (layer, 0, 0),
                               pipeline_mode=pl.Buffered(1)),
                  pl.BlockSpec((1, D_MODEL), lambda b, i: (0, 0))],
        out_specs=seg(D_MODEL),
        out_shape=jax.ShapeDtypeStruct((B, SUBLANES, S, D_MODEL), F32),
        compiler_params=_params(("arbitrary", "arbitrary")),
        name="out_proj",
    )(view(y_ret), view(y_hy), y_lru, view(x), gate, w_bf16, final_g)
    return out.reshape(B, L, D_MODEL)


RET_HEAD_ROWS = 2 * 2048
RET_BLOCK = 256


def _ret_kernel(*refs, L, use_rope, has_state, emit_state, chained, layer):
    it = iter(refs)
    q_ref, k_ref, v_ref, g_ref = next(it), next(it), next(it), next(it)
    cos_ref = sin_ref = s0_ref = sN_ref = None
    if use_rope:
        cos_ref, sin_ref = next(it), next(it)
    lg_ref = next(it)
    if has_state:
        s0_ref = next(it)
    if chained:
        next(it)
    y_ref = next(it)
    if emit_state:
        sN_ref = next(it)
    R_ref, M_ref, Z_ref, D_ref, KV_ref, kb_ref = (next(it) for _ in range(6))

    C = min(RET_BLOCK, L)
    DH = RET_HEAD_DIM
    HP = q_ref.shape[-1] // DH
    n = L // C
    cross = has_state or n > 1
    kscale = DH ** -0.5

    @pl.when(pl.program_id(1) == 0)
    def _():
        ii = lax.broadcasted_iota(jnp.int32, (C, C), 0)
        jj = lax.broadcasted_iota(jnp.int32, (C, C), 1)
        diff = (ii - jj).astype(F32)
        ri = lax.broadcasted_iota(jnp.int32, (C, DH), 0).astype(F32)
        for hh in range(HP):
            lgf = -_softplus(-lg_ref[0, hh])
            lgb = -_softplus(-lg_ref[1, hh])
            M_ref[hh] = (jnp.where(diff >= 0, jnp.exp(lgf[:, :C] * jnp.maximum(diff, 0.0)), 0.0)
                         + jnp.where(diff <= 0, jnp.exp(lgb[:, :C] * jnp.maximum(-diff, 0.0)), 0.0))
            lf, lb = lgf[:, :DH], lgb[:, :DH]
            Z_ref[hh, 0] = jnp.concatenate([jnp.exp(lf * (C - 1.0 - ri)), jnp.exp(lb * ri)], axis=1)
            Z_ref[hh, 1] = jnp.concatenate([jnp.exp(lf * (ri + 1.0)), jnp.exp(lb * (C - ri))], axis=1)
            D_ref[hh] = jnp.exp(jnp.concatenate([jnp.broadcast_to(lf * C, (DH, DH)),
                                                 jnp.broadcast_to(lb * C, (DH, DH))], axis=0))

    def rotary(x, r0):
        if not use_rope:
            return x
        return x * cos_ref[pl.ds(r0, C), :] + pltpu.roll(x, 64, axis=1) * sin_ref[pl.ds(r0, C), :]

    def chunk_kv(c, carry):
        r0 = pl.multiple_of(c * C, C)
        for hh in range(HP):
            lanes = slice(hh * DH, (hh + 1) * DH)
            k = rotary(k_ref[0, pl.ds(r0, C), lanes] * kscale, r0)
            kb_ref[hh, pl.ds(r0, C), :] = k.astype(BF16)
            kz = (jnp.concatenate([k, k], axis=1) * Z_ref[hh, 0]).astype(BF16)
            vb = v_ref[0, pl.ds(r0, C), lanes].astype(BF16)
            KV_ref[hh, c] = lax.dot_general(kz, vb, (((0,), (0,)), ((), ())), preferred_element_type=F32)
        return carry

    lax.fori_loop(0, n, chunk_kv, 0, unroll=4 if n % 4 == 0 else (2 if n % 2 == 0 else 1))

    for d in (0, 1):
        for hh in range(HP):
            R_ref[d, hh] = s0_ref[0, d, hh] if has_state else jnp.zeros((DH, DH), F32)

    def states(t, carry):
        for d in (0, 1):
            c = t if d == 0 else n - 1 - t
            rows = slice(d * DH, (d + 1) * DH)
            for hh in range(HP):
                R = R_ref[d, hh]
                R_ref[d, hh] = D_ref[hh, rows, :] * R + KV_ref[hh, c, rows, :]
                KV_ref[hh, c, rows, :] = R
        return carry

    lax.fori_loop(0, n, states, 0)
    if emit_state:
        for l in range(sN_ref.shape[1]):
            if chained or l == layer:
                for d in (0, 1):
                    for hh in range(HP):
                        sN_ref[0, l, d, hh] = R_ref[d, hh]
            else:
                sN_ref[0, l] = jnp.zeros(sN_ref.shape[2:], F32)

    def chunk_out(c, carry):
        r0 = pl.multiple_of(c * C, C)
        cols = []
        for hh in range(HP):
            lanes = slice(hh * DH, (hh + 1) * DH)
            q = rotary(q_ref[0, pl.ds(r0, C), lanes], r0)
            vb = v_ref[0, pl.ds(r0, C), lanes].astype(BF16)
            s = lax.dot_general(q.astype(BF16), kb_ref[hh, pl.ds(r0, C), :], (((1,), (1,)), ((), ())),
                                preferred_element_type=F32) * M_ref[hh]
            o = jnp.dot(s.astype(BF16), vb, preferred_element_type=F32)
            if cross:
                qx = (jnp.concatenate([q, q], axis=1) * Z_ref[hh, 1]).astype(BF16)
                o = o + jnp.dot(qx, KV_ref[hh, c].astype(BF16), preferred_element_type=F32)
            cols.append(o * lax.rsqrt(jnp.mean(o * o, axis=-1, keepdims=True) + EPS))
        g = g_ref[0, pl.ds(r0, C), :]
        y_ref[0, pl.ds(r0, C), :] = (jnp.concatenate(cols, axis=1) * _silu(g)).astype(BF16)
        return carry

    lax.fori_loop(0, n, chunk_out, 0, unroll=4 if n % 4 == 0 else (2 if n % 2 == 0 else 1))


def _retention(z, decay_logit, rope, state0, layer, states):
    B, L, _ = z.shape
    use_rope = rope is not None
    has_state = state0 is not None
    emit_state = states is not None
    chained = emit_state and not isinstance(states, str)
    H = RET_HEADS
    HP = min(H, max(1, RET_HEAD_ROWS // L))
    W = HP * RET_HEAD_DIM
    col = lambda off: pl.BlockSpec((1, L, W), lambda h, b, off=off: (b, 0, off * LANES // W + h))
    in_specs = [col(_Q0), col(_K0), col(_V0), col(_GR0)]
    args = [z, z, z, z]
    if use_rope:
        in_specs += [pl.BlockSpec((L, LANES), lambda h, b: (0, 0))] * 2
        args += list(rope)
    C = min(RET_BLOCK, L)
    in_specs.append(pl.BlockSpec((2, HP, 1, RET_BLOCK), lambda h, b: (0, h, 0, 0)))
    args.append(jnp.broadcast_to(decay_logit[:, :, None, None], (2, H, 1, RET_BLOCK)))
    if has_state:
        in_specs.append(pl.BlockSpec((1, None, 2, HP, RET_HEAD_DIM, RET_HEAD_DIM),
                                     lambda h, b: (b, layer, 0, h, 0, 0)))
        args.append(state0)
    out_specs = [pl.BlockSpec((1, L, W), lambda h, b: (b, 0, h))]
    out_shape = [jax.ShapeDtypeStruct((B, L, RET_WIDTH), BF16)]
    aliases = {}
    if emit_state:
        nl, l0 = (1, layer) if chained else (DEPTH, 0)
        out_specs.append(pl.BlockSpec((1, nl, 2, HP, RET_HEAD_DIM, RET_HEAD_DIM),
                                      lambda h, b: (b, l0, 0, h, 0, 0)))
        out_shape.append(jax.ShapeDtypeStruct((B, DEPTH, 2, H, RET_HEAD_DIM, RET_HEAD_DIM), F32))
        if chained:
            aliases = {len(args): 1}
            in_specs.append(pl.BlockSpec(memory_space=pl.ANY))
            args.append(states)
    return pl.pallas_call(
        functools.partial(_ret_kernel, L=L, use_rope=use_rope, has_state=has_state,
                          emit_state=emit_state, chained=chained, layer=layer),
        grid=(H // HP, B),
        in_specs=in_specs,
        out_specs=out_specs,
        out_shape=out_shape,
        input_output_aliases=aliases,
        scratch_shapes=[pltpu.VMEM((2, HP, RET_HEAD_DIM, RET_HEAD_DIM), F32),
                        pltpu.VMEM((HP, C, C), F32),
                        pltpu.VMEM((HP, 2, C, 2 * RET_HEAD_DIM), F32),
                        pltpu.VMEM((HP, 2 * RET_HEAD_DIM, RET_HEAD_DIM), F32),
                        pltpu.VMEM((HP, L // C, 2 * RET_HEAD_DIM, RET_HEAD_DIM), F32),
                        pltpu.VMEM((HP, L, RET_HEAD_DIM), BF16)],
        compiler_params=_params(("arbitrary", "arbitrary")),
        name="retention",
    )(*args)


def _dft_matrices(T):
    N = 2 * T
    k = np.arange(T, dtype=np.float64)[:, None]
    m = np.arange(T, dtype=np.float64)[None, :]
    ang = 2.0 * np.pi * k * m / N
    fwd = np.concatenate([np.cos(ang), -np.sin(ang)], axis=0)
    fwd[T, :] = (-1.0) ** np.arange(T)
    ck = np.full((T,), 2.0)
    ck[0] = 1.0
    inv_re = (np.cos(ang) * ck[:, None]).T / N
    inv_im = (-2.0 * np.sin(ang)).T / N
    inv_im[:, 0] = ((-1.0) ** np.arange(T)) / N
    inv = np.concatenate([inv_re, inv_im], axis=1)
    return jnp.asarray(fwd, F32).astype(BF16), jnp.asarray(inv, F32).astype(BF16)


def _split_bf16(x):
    hi = x.astype(BF16)
    return hi, (x - hi.astype(F32)).astype(BF16)


def _dot1(a, b):
    return jnp.dot(a, b.astype(BF16), preferred_element_type=F32)


def _filt_kernel(zemb_ref, w1_ref, b1_ref, w2_ref, b2_ref, fr_ref, w3f_ref, w3b_ref, dec_ref,
                 bias_ref, fwd_ref, kre_ref, kim_ref, hid_ref, hn_ref, F_ref, *, L):
    T = fwd_ref.shape[1]
    nb = L // T

    @pl.when((pl.program_id(0) == 0) & (pl.program_id(1) == 0))
    def _():
        z1 = jnp.dot(zemb_ref[...], w1_ref[...], precision=HI, preferred_element_type=F32) + b1_ref[...]
        h1 = jnp.sin(fr_ref[0:1, :] * z1)
        z2 = jnp.dot(h1, w2_ref[...], precision=HI, preferred_element_type=F32) + b2_ref[...]
        hid = jnp.sin(fr_ref[1:2, :] * z2)
        hid_ref[0], hid_ref[1] = _split_bf16(hid)

    w3 = jnp.concatenate([w3f_ref[...], w3b_ref[...]], axis=1)
    w_hi, w_lo = _split_bf16(w3)
    h = (jnp.dot(hid_ref[0], w_hi, preferred_element_type=F32)
         + jnp.dot(hid_ref[1], w_hi, preferred_element_type=F32)
         + jnp.dot(hid_ref[0], w_lo, preferred_element_type=F32))
    h = h * jnp.concatenate([dec_ref[...]] * 2, axis=1)
    h = h / (jnp.sum(jnp.abs(h), axis=0, keepdims=True) + EPS)
    for dr in (0, 1):
        hn_ref[dr] = h[:, dr * LANES:(dr + 1) * LANES]
    for blk in range(nb):
        F = _dot1(fwd_ref[...], h[blk * T:(blk + 1) * T])
        for dr in (0, 1):
            F_ref[dr, blk] = F[:, dr * LANES:(dr + 1) * LANES]

    row = lax.broadcasted_iota(jnp.int32, (T, LANES), 0)
    sgn = jnp.where((row & 1) == 0, 1.0, -1.0).astype(F32)
    row0 = row == 0
    bias = bias_ref[0]
    for d in range(-(nb - 1), nb):
        re = jnp.zeros((T, LANES), F32)
        im = jnp.zeros((T, LANES), F32)
        nyq = jnp.zeros((1, LANES), F32)
        if d >= 0:
            F = F_ref[0, d]
            re, im, nyq = re + F[:T], im + F[T:], nyq + F[T:T + 1]
        if d >= 1:
            F = F_ref[0, d - 1]
            p0 = hn_ref[0, (d - 1) * T:(d - 1) * T + 1, :]
            re, im, nyq = re + sgn * (F[:T] - p0), im + sgn * F[T:], nyq + (F[T:T + 1] - p0)
        e = -d
        if e >= 0:
            F = F_ref[1, e]
            re, im, nyq = re + F[:T], im - F[T:], nyq + F[T:T + 1]
        if e >= 1:
            F = F_ref[1, e - 1]
            p0 = hn_ref[1, (e - 1) * T:(e - 1) * T + 1, :]
            re, im, nyq = re + sgn * (F[:T] - p0), im - sgn * F[T:], nyq + (F[T:T + 1] - p0)
        if d == 0:
            re, nyq = re + bias, nyq + bias
        kre_ref[0, d + nb - 1] = re
        kim_ref[0, d + nb - 1] = jnp.where(row0, nyq, im)


def _hyena_filter_spectra(L, w1, b1, w2, b2, w3, freq, bias, fwd):
    T = fwd.shape[1]
    nb = L // T
    nlag = 2 * nb - 1
    P = LANES
    t = np.linspace(0.0, 1.0, L)[:, None]
    n_bands = (HY_EMB - 1) // 2
    f = np.linspace(1e-4, n_bands - 1, n_bands)
    ang = (2.0 * math.pi / L) * np.arange(L)[:, None] * f[None, :]
    zemb = np.concatenate([t, np.cos(ang), -np.sin(ang)], axis=-1)
    zemb = jnp.asarray(np.pad(zemb, ((0, 0), (0, P - HY_EMB))), F32)
    min_decay = math.log(HY_TARGET) / HY_LONG_DECAY_PCT
    max_decay = math.log(HY_TARGET) / HY_SHORT_DECAY_PCT
    deltas = np.abs(np.linspace(min_decay, max_decay, HY_WIDTH))
    dec = jnp.asarray(np.exp(-t * deltas[None, :]), F32)
    pad = P - HY_FFN
    w1p = jnp.pad(w1, ((0, P - HY_EMB), (0, pad)))
    b1p = jnp.pad(b1, (0, pad)).reshape(1, P)
    w2p = jnp.pad(w2, ((0, pad), (0, pad)))
    b2p = jnp.pad(b2, (0, pad)).reshape(1, P)
    w3p = jnp.pad(w3, ((0, pad), (0, 0)))
    frp = jnp.pad(freq, ((0, 0), (0, pad)))
    ncg = HY_WIDTH // P
    const = lambda shape: pl.BlockSpec(shape, lambda o, c: tuple(0 for _ in shape))
    kshape = jax.ShapeDtypeStruct((HY_ORDER, nlag, T, HY_WIDTH), F32)
    kspec = pl.BlockSpec((1, nlag, T, P), lambda o, c: (o, 0, 0, c))
    return pl.pallas_call(
        functools.partial(_filt_kernel, L=L),
        grid=(HY_ORDER, ncg),
        in_specs=[const((L, P)), const((P, P)), const((1, P)), const((P, P)), const((1, P)), const((2, P)),
                  pl.BlockSpec((P, P), lambda o, c: (0, o * 2 * ncg + c)),
                  pl.BlockSpec((P, P), lambda o, c: (0, o * 2 * ncg + ncg + c)),
                  pl.BlockSpec((L, P), lambda o, c: (0, c)),
                  pl.BlockSpec((1, 1, P), lambda o, c: (o, 0, c)),
                  const((2 * T, T))],
        out_specs=[kspec, kspec],
        out_shape=[kshape, kshape],
        scratch_shapes=[pltpu.VMEM((2, L, P), BF16), pltpu.VMEM((2, L, P), F32),
                        pltpu.VMEM((2, nb, 2 * T, P), F32)],
        compiler_params=_params(("arbitrary", "arbitrary")),
        name="hyena_filters",
    )(zemb, w1p, b1p, w2p, b2p, frp, w3p, w3p, dec, bias.reshape(HY_ORDER, 1, HY_WIDTH), fwd)


CONV_ROWS = 128


def _conv_block(x_ref, t0, w, b, left):
    L, C = x_ref.shape
    T = CONV_ROWS
    blk = x_ref[t0:t0 + T, :]
    prev = x_ref[t0 - SUBLANES:t0, :] if t0 > 0 else jnp.zeros((SUBLANES, C), F32)
    nxt = x_ref[t0 + T:t0 + T + SUBLANES, :] if t0 + T < L else jnp.zeros((SUBLANES, C), F32)
    ext = jnp.concatenate([prev, blk, nxt], axis=0)
    n = T + 2 * SUBLANES
    acc = jnp.broadcast_to(b, (T, C))
    for tap in range(w.shape[0]):
        s = tap - left
        sh = blk if s == 0 else pltpu.roll(ext, (-s) % n, axis=0)[SUBLANES:SUBLANES + T]
        acc = acc + w[tap:tap + 1, :] * sh
    return acc


HY_LANES_ROWS = 2 * 2048
HY_MAX_BATCH = 16


def _toeplitz(k, u, mul):
    n = len(u)
    if n == 1:
        return [mul(k[0], u[0])]
    h = n // 2
    add = lambda a, b: tuple(x + y for x, y in zip(a, b))
    sub = lambda a, b: tuple(x - y for x, y in zip(a, b))
    t0, t1, t2 = k[h:h + n - 1], k[0:n - 1], k[n:2 * n - 1]
    p = _toeplitz(t0, [add(a, b) for a, b in zip(u[:h], u[h:])], mul)
    q = _toeplitz([sub(a, b) for a, b in zip(t1, t0)], u[h:], mul)
    r = _toeplitz([sub(a, b) for a, b in zip(t2, t0)], u[:h], mul)
    return [add(a, b) for a, b in zip(p, q)] + [add(a, b) for a, b in zip(p, r)]


def _hy_kernel(v_ref, x1_ref, x2_ref, g_ref, wv_ref, w1_ref, w2_ref, bv_ref, b1_ref, b2_ref,
               kre_ref, kim_ref, fwd_ref, inv_ref, y_ref,
               x1c_ref, x2c_ref, u_ref, U_ref, Y_ref, *, L):
    T = fwd_ref.shape[1]
    P = LANES
    nb = L // T
    NBT = v_ref.shape[0]
    left = (HY_SHORT - 1) // 2
    def block_spectrum(j, r0):
        U_ref[j] = jnp.dot(fwd_ref[...], u_ref[pl.ds(r0, T), :], preferred_element_type=F32)

    for j in range(nb):
        for t0 in range(j * T, (j + 1) * T, CONV_ROWS):
            rows = slice(t0, t0 + CONV_ROWS)
            for p in range(NBT):
                u_ref[rows, p * P:(p + 1) * P] = _conv_block(v_ref.at[p], t0, wv_ref[...], bv_ref[...], left).astype(BF16)
                x1c_ref[p, rows, :] = _conv_block(x1_ref.at[p], t0, w1_ref[...], b1_ref[...], left)
                x2c_ref[p, rows, :] = _conv_block(x2_ref.at[p], t0, w2_ref[...], b2_ref[...], left)
        block_spectrum(j, j * T)

    row0 = lax.broadcasted_iota(jnp.int32, (SUBLANES, NBT * P), 0) == 0
    tile = lambda x: jnp.concatenate([x] * NBT, axis=1)

    def cmul(k, u):
        kr, ki = tile(k[0]), tile(k[1])
        return (kr * u[0] - ki * u[1], kr * u[1] + ki * u[0])

    def rmul(k, u):
        return (tile(k[0]) * u[0], tile(k[1]) * u[1])

    def spectra_products(o):
        def load(r, rows):
            k = [(kre_ref[o, l, pl.ds(r, rows), :], kim_ref[o, l, pl.ds(r, rows), :]) for l in range(2 * nb - 1)]
            u = [(U_ref[j, pl.ds(r, rows), :], U_ref[j, pl.ds(T + r, rows), :]) for j in range(nb)]
            return k, u

        def chunk(rc, carry):
            r = pl.multiple_of(rc * SUBLANES, SUBLANES)
            y = _toeplitz(*load(r, SUBLANES), cmul)
            for i in range(nb):
                Y_ref[i, pl.ds(r, SUBLANES), :] = y[i][0]
                Y_ref[i, pl.ds(T + r, SUBLANES), :] = y[i][1]
            return carry

        lax.fori_loop(0, T // SUBLANES, chunk, 0, unroll=4)
        y = _toeplitz(*load(0, SUBLANES), rmul)
        for i in range(nb):
            Y_ref[i, 0:SUBLANES, :] = jnp.where(row0, y[i][0], Y_ref[i, 0:SUBLANES, :])
            Y_ref[i, T:T + SUBLANES, :] = jnp.where(row0, y[i][1], Y_ref[i, T:T + SUBLANES, :])

    def long_conv_block(i):
        return _dot1(inv_ref[...], Y_ref[i])

    spectra_products(0)

    def order0(i, carry):
        r0 = pl.multiple_of(i * T, T)
        y = long_conv_block(i)
        for p in range(NBT):
            z1 = x1c_ref[p, pl.ds(r0, T), :] * y[:, p * P:(p + 1) * P]
            u_ref[pl.ds(r0, T), p * P:(p + 1) * P] = z1.astype(BF16)
        return carry

    lax.fori_loop(0, nb, order0, 0, unroll=4 if nb % 4 == 0 else 1)
    for j in range(nb):
        block_spectrum(j, j * T)
    spectra_products(1)

    def order1(i, carry):
        r0 = pl.multiple_of(i * T, T)
        y = long_conv_block(i)
        for p in range(NBT):
            out = x2c_ref[p, pl.ds(r0, T), :] * y[:, p * P:(p + 1) * P]
            y_ref[p, pl.ds(r0, T), :] = (out * _silu(g_ref[p, pl.ds(r0, T), :])).astype(BF16)
        return carry

    lax.fori_loop(0, nb, order1, 0, unroll=4 if nb % 4 == 0 else 1)


def _hyena(z, conv_w, conv_b, kre, kim, fwd, inv):
    B, L, _ = z.shape
    NBT = min(B, max(2, HY_LANES_ROWS // L), HY_MAX_BATCH)
    assert B % NBT == 0
    T = fwd.shape[1]
    nb = L // T
    nlag = 2 * nb - 1
    P = LANES
    ncg = HY_WIDTH // P
    col = lambda off: pl.BlockSpec((NBT, L, P), lambda c, b, off=off: (b, 0, off + c))
    wspec = lambda part: pl.BlockSpec((HY_SHORT, P), lambda c, b, part=part: (0, part * ncg + c))
    bspec = lambda part: pl.BlockSpec((1, P), lambda c, b, part=part: (0, part * ncg + c))
    kspec = pl.BlockSpec((HY_ORDER, nlag, T, P), lambda c, b: (0, 0, 0, c))
    fspec = pl.BlockSpec((2 * T, T), lambda c, b: (0, 0))
    ispec = pl.BlockSpec((T, 2 * T), lambda c, b: (0, 0))
    cb = conv_b.reshape(1, 3 * HY_WIDTH)
    return pl.pallas_call(
        functools.partial(_hy_kernel, L=L),
        grid=(ncg, B // NBT),
        in_specs=[col(_HV0), col(_HX10), col(_HX20), col(_GH0),
                  wspec(0), wspec(1), wspec(2), bspec(0), bspec(1), bspec(2),
                  kspec, kspec, fspec, ispec],
        out_specs=pl.BlockSpec((NBT, L, P), lambda c, b: (b, 0, c)),
        out_shape=jax.ShapeDtypeStruct((B, L, HY_WIDTH), BF16),
        scratch_shapes=[pltpu.VMEM((NBT, L, P), F32), pltpu.VMEM((NBT, L, P), F32),
                        pltpu.VMEM((L, NBT * P), BF16),
                        pltpu.VMEM((nb, 2 * T, NBT * P), F32), pltpu.VMEM((nb, 2 * T, NBT * P), F32)],
        compiler_params=_params(("arbitrary", "arbitrary")),
        name="hyena",
    )(z, z, z, z, conv_w, conv_w, conv_w, cb, cb, cb, kre, kim, fwd, inv)


LRU_TILE = 256
LRU_GROUPS = LRU_TILE // LANES
LRU_ROWS = 128
LRU_STEP_ROWS = 1024


def _lru_kernel(*refs, L, has_state):
    for bi in range(refs[0].shape[0]):
        _lru_one(*refs, L=L, has_state=has_state, bi=bi)


def _lru_one(*refs, L, has_state, bi):
    it = iter(refs)
    x_ref, g_ref, cw_ref, cb_ref, wg_ref, gb_ref, lam_ref = (next(it) for _ in range(7))
    h0_ref = next(it) if has_state else None
    y_ref, sN_ref, xp_ref, A_ref, B_ref, H_ref, C_ref = (next(it) for _ in range(7))

    S = L // SUBLANES
    W = LRU_TILE
    G = LRU_GROUPS
    R = LRU_ROWS
    left = LRU_CONV // 2
    pad = left * SUBLANES

    sub = lax.broadcasted_iota(jnp.int32, (SUBLANES, W), 0)
    for k in range(left):
        tail = pltpu.roll(x_ref[bi, L - pad + k * SUBLANES:L - pad + (k + 1) * SUBLANES, :], 1, axis=0)
        xp_ref[k * SUBLANES:(k + 1) * SUBLANES, :] = jnp.where(sub == 0, 0.0, tail)
    for k in range(LRU_CONV - 1 - left):
        head = pltpu.roll(x_ref[bi, k * SUBLANES:(k + 1) * SUBLANES, :], SUBLANES - 1, axis=0)
        xp_ref[pad + L + k * SUBLANES:pad + L + (k + 1) * SUBLANES, :] = jnp.where(sub == SUBLANES - 1, 0.0, head)

    def copy(c, carry):
        r0 = pl.multiple_of(c * R, R)
        xp_ref[pl.ds(pad + r0, R), :] = x_ref[bi, pl.ds(r0, R), :]
        return carry

    lax.fori_loop(0, L // R, copy, 0)

    csp = [(0.5 * LRU_C) * _softplus(-lam_ref[d]) for d in (0, 1)]

    def coeffs(c, carry):
        r0 = pl.multiple_of(c * R, R)
        u = jnp.broadcast_to(cb_ref[...], (R, W))
        for k in range(LRU_CONV):
            u = u + cw_ref[k:k + 1, :] * xp_ref[pl.ds(r0 + k * SUBLANES, R), :]
        half = jnp.dot(u.astype(BF16), wg_ref[0], preferred_element_type=F32) + gb_ref[0]
        hu = 0.5 * u
        for d in (0, 1):
            t_r = jnp.tanh(half[:, (2 * d) * W:(2 * d + 1) * W])
            t_i = jnp.tanh(half[:, (2 * d + 1) * W:(2 * d + 2) * W])
            nl = csp[d] * t_r + csp[d]
            a = jnp.exp(-nl)
            s2 = jnp.tanh(nl) * (a * a + 1.0)
            b = jnp.where(s2 > 0.0, s2 * lax.rsqrt(s2), 0.0) * (t_i * hu + hu)
            for gi in range(G):
                lanes = slice(gi * LANES, (gi + 1) * LANES)
                A_ref[d, gi, pl.ds(r0, R), :] = a[:, lanes]
                B_ref[d, gi, pl.ds(r0, R), :] = b[:, lanes]
        return carry

    lax.fori_loop(0, L // R, coeffs, 0, unroll=4 if (L // R) % 4 == 0 else 2)

    def scan_body(t, carry):
        out = []
        for d in (0, 1):
            r8 = pl.multiple_of((t if d == 0 else S - 1 - t) * SUBLANES, SUBLANES)
            for gi in range(G):
                h, acc = carry[d * G + gi]
                a = A_ref[d, gi, pl.ds(r8, SUBLANES), :]
                h = a * h + B_ref[d, gi, pl.ds(r8, SUBLANES), :]
                acc = acc * a
                H_ref[d, gi, pl.ds(r8, SUBLANES), :] = h
                C_ref[d, gi, pl.ds(r8, SUBLANES), :] = acc
                out.append((h, acc))
        return tuple(out)

    init = tuple((jnp.zeros((SUBLANES, LANES), F32), jnp.ones((SUBLANES, LANES), F32))
                 for _ in range(2 * G))
    lax.fori_loop(0, S, scan_body, init, unroll=8)

    hin = [None] * (2 * G)
    for d in (0, 1):
        last = (S - 1 if d == 0 else 0) * SUBLANES
        for gi in range(G):
            hl = H_ref[d, gi, last:last + SUBLANES, :]
            ac = C_ref[d, gi, last:last + SUBLANES, :]
            if has_state:
                h = h0_ref[bi, d:d + 1, gi * LANES:(gi + 1) * LANES]
            else:
                h = jnp.zeros((1, LANES), F32)
            rows = [None] * SUBLANES
            for j in (range(SUBLANES) if d == 0 else range(SUBLANES - 1, -1, -1)):
                rows[j] = h
                h = hl[j:j + 1, :] + ac[j:j + 1, :] * h
            hin[d * G + gi] = jnp.concatenate([jnp.concatenate(rows, axis=0)] * (R // SUBLANES), axis=0)
            sN_ref[bi, d:d + 1, gi * LANES:(gi + 1) * LANES] = h

    def combine(c, carry):
        r0 = pl.multiple_of(c * R, R)
        cols = []
        for gi in range(G):
            tot = jnp.zeros((R, LANES), F32)
            for d in (0, 1):
                tot = tot + (H_ref[d, gi, pl.ds(r0, R), :] + C_ref[d, gi, pl.ds(r0, R), :] * hin[d * G + gi])
            cols.append(tot)
        y_ref[bi, pl.ds(r0, R), :] = jnp.concatenate(cols, axis=1) * _silu(g_ref[bi, pl.ds(r0, R), :])
        return carry

    lax.fori_loop(0, L // R, combine, 0, unroll=2)


def _rglru(z4, conv_w, conv_b, gate_w, gate_b, lam, h0):
    B, S, _, _ = z4.shape
    L = S * SUBLANES
    has_state = h0 is not None
    W = LRU_TILE
    nh = LRU_WIDTH // W
    bpt = W // LRU_BLOCK_DIM
    gw = gate_w.reshape(2, 2, nh, bpt, LRU_BLOCK_DIM, LRU_BLOCK_DIM)
    eye = jnp.eye(bpt, dtype=F32)
    dense = (0.5 * jnp.einsum('dkhnij,nm->hnidkmj', gw, eye)).reshape(nh, W, 4 * W).astype(BF16)
    gb = 0.5 * gate_b.reshape(2, 2, nh, W).transpose(2, 0, 1, 3).reshape(nh, 1, 4 * W)
    zr = z4.reshape(B, L, D_LRU)
    NB = min(B, max(1, LRU_STEP_ROWS // L))
    assert B % NB == 0
    off = lambda base: pl.BlockSpec((NB, L, W), lambda h, b, base=base: (b, 0, base + h))
    in_specs = [off(0), off(LRU_WIDTH // W),
                pl.BlockSpec((LRU_CONV, W), lambda h, b: (0, h)),
                pl.BlockSpec((1, W), lambda h, b: (0, h)),
                pl.BlockSpec((1, W, 4 * W), lambda h, b: (h, 0, 0)),
                pl.BlockSpec((1, 1, 4 * W), lambda h, b: (h, 0, 0)),
                pl.BlockSpec((2, 1, W), lambda h, b: (0, 0, h))]
    args = [zr, zr, conv_w, conv_b.reshape(1, LRU_WIDTH), dense, gb, lam.reshape(2, 1, LRU_WIDTH)]
    if has_state:
        in_specs.append(pl.BlockSpec((NB, 2, W), lambda h, b: (b, 0, h)))
        args.append(h0)
    y, s = pl.pallas_call(
        functools.partial(_lru_kernel, L=L, has_state=has_state),
        grid=(nh, B // NB),
        in_specs=in_specs,
        out_specs=[pl.BlockSpec((NB, L, W), lambda h, b: (b, 0, h)),
                   pl.BlockSpec((NB, 2, W), lambda h, b: (b, 0, h))],
        out_shape=[jax.ShapeDtypeStruct((B, L, LRU_WIDTH), F32),
                   jax.ShapeDtypeStruct((B, 2, LRU_WIDTH), F32)],
        scratch_shapes=[pltpu.VMEM((L + (LRU_CONV - 1) * SUBLANES, W), F32)]
        + [pltpu.VMEM((2, LRU_GROUPS, L, LANES), F32) for _ in range(4)],
        compiler_params=_params(("arbitrary", "arbitrary")),
        name="rglru",
    )(*args)
    return y.reshape(B, S, SUBLANES, LRU_WIDTH), s


def _rope_tables(L):
    rows = L // GRID_W
    row = np.repeat(np.arange(rows, dtype=np.float64), GRID_W)
    col = np.tile(np.arange(GRID_W, dtype=np.float64), rows)
    n_f = RET_HEAD_DIM // 4
    inv = ROPE_BASE ** (-np.arange(n_f, dtype=np.float64) / n_f)
    ang = np.concatenate([row[:, None] * inv[None], col[:, None] * inv[None]], axis=-1)
    cos, sin = np.cos(ang), np.sin(ang)
    return (jnp.asarray(np.concatenate([cos, cos], axis=-1), F32),
            jnp.asarray(np.concatenate([-sin, sin], axis=-1), F32))


def kernel(x_prompt, x_sample, state_ret, state_lru, c, c_ctx, norm_g, ada_w, ada_b, w_in, ret_decay_logit, hy_conv_w, hy_conv_b, hy_ffn_w1, hy_ffn_b1, hy_ffn_w2, hy_ffn_b2, hy_ffn_w3, hy_freq, hy_bias, lru_conv_w, lru_conv_b, lru_gate_w, lru_gate_b, lru_lambda, w_out, final_g):
    Bp, Lp, _ = x_prompt.shape
    Bs, Ls, _ = x_sample.shape
    assert Bs + 1 <= SUBLANES

    cvec = jnp.zeros((SUBLANES, D_MODEL), F32).at[:Bs].set(c).at[Bs].set(c_ctx)
    mod = _modulation(cvec, ada_w, ada_b)
    rope = _rope_tables(Ls)
    dft = {L: _dft_matrices(min(DFT_MAX, L)) for L in {Lp, Ls}}
    w_in_b = w_in.astype(BF16)
    w_out_b = w_out.astype(BF16)
    fg = final_g.reshape(1, D_MODEL)

    xc, xl = x_prompt, x_sample
    new_state_ret, new_lru = None, []
    for l in range(DEPTH):
        shift, scale, gate = (mod[l, :, i * D_MODEL:(i + 1) * D_MODEL] for i in range(3))
        g = norm_g[l].reshape(1, D_MODEL)
        final = l == DEPTH - 1
        filt = {}
        for L in sorted({Lp, Ls}):
            filt[L] = _hyena_filter_spectra(L, hy_ffn_w1[l], hy_ffn_b1[l], hy_ffn_w2[l], hy_ffn_b2[l],
                                            hy_ffn_w3[l], hy_freq[l], hy_bias[l], dft[L][0])

        def layer(x, sel, rope_t, ret_s0, lru_s0, ret_states):
            sh, sc, gt = (m[sel][:, None, :] for m in (shift, scale, gate))
            z, z_lru = _in_proj(x, sh, sc, g, w_in_b, l)
            y_ret, *s_ret = _retention(z, ret_decay_logit[l], rope_t, ret_s0, l, ret_states)
            kre, kim = filt[x.shape[1]]
            y_hy = _hyena(z, hy_conv_w[l], hy_conv_b[l], kre, kim, *dft[x.shape[1]])
            y_lru, s_lru = _rglru(z_lru, lru_conv_w[l], lru_conv_b[l], lru_gate_w[l], lru_gate_b[l],
                                  lru_lambda[l], lru_s0)
            return _out_proj(y_ret, y_hy, y_lru, x, gt, w_out_b, l, fg, final), s_ret, s_lru

        xc, (new_state_ret,), ls = layer(xc, slice(Bs, Bs + 1), None, None, None,
                                         "new" if l == 0 else new_state_ret)
        new_lru.append(ls)
        xl, _, _ = layer(xl, slice(0, Bs), rope, state_ret, state_lru[:, l], None)

    new_state_lru = jnp.stack(new_lru, axis=1).astype(x_prompt.dtype)
    return (xc, xl, new_state_ret.astype(x_prompt.dtype), new_state_lru)
```

```python
import functools
import math

import numpy as np
import jax
import jax.numpy as jnp
from jax import lax
from jax.experimental import pallas as pl
from jax.experimental.pallas import tpu as pltpu

F32 = jnp.float32
BF16 = jnp.bfloat16
HI = lax.Precision.HIGHEST

D_MODEL = 1024
DEPTH = 2
GRID_W = 64
EPS = 1e-6
RET_HEADS = 4
RET_HEAD_DIM = 128
RET_WIDTH = RET_HEADS * RET_HEAD_DIM
ROPE_BASE = 10000.0
HY_WIDTH = 512
HY_ORDER = 2
HY_SHORT = 3
HY_EMB = 33
HY_FFN = 64
HY_SHORT_DECAY_PCT = 0.3
HY_LONG_DECAY_PCT = 1.5
HY_TARGET = 1e-2
LRU_WIDTH = 512
LRU_BLOCKS = 8
LRU_BLOCK_DIM = LRU_WIDTH // LRU_BLOCKS
LRU_CONV = 4
LRU_C = 8.0
D_MIX = RET_WIDTH + HY_WIDTH + LRU_WIDTH
D_IN = 4 * RET_WIDTH + 4 * HY_WIDTH + 2 * LRU_WIDTH

LANES = 128
SUBLANES = 8
DFT_MAX = 512
ROW_TILE = 256
VMEM_LIMIT = 56 * 1024 * 1024

_Q0, _K0, _V0, _GR0 = 0, 4, 8, 12
_HV0, _HX10, _HX20, _GH0 = 16, 20, 24, 28


def _sigmoid(x):
    return 0.5 * jnp.tanh(0.5 * x) + 0.5


def _silu(x):
    return x * _sigmoid(x)


def _softplus(x):
    return jnp.maximum(x, 0.0) + jnp.log1p(jnp.exp(-jnp.abs(x)))


def _params(sem):
    return pltpu.CompilerParams(dimension_semantics=sem, vmem_limit_bytes=VMEM_LIMIT)


def _mod_kernel(c_ref, w_ref, b_ref, o_ref):
    s_hi, s_lo = _split_bf16(_silu(c_ref[...]))
    w_hi, w_lo = _split_bf16(w_ref[0])
    o_ref[0] = (jnp.dot(s_hi, w_hi, preferred_element_type=F32)
                + jnp.dot(s_lo, w_hi, preferred_element_type=F32)
                + jnp.dot(s_hi, w_lo, preferred_element_type=F32)) + b_ref[0]


def _modulation(cvec, ada_w, ada_b):
    tn = 768
    return pl.pallas_call(
        _mod_kernel,
        grid=(DEPTH, 3 * D_MODEL // tn),
        in_specs=[pl.BlockSpec((SUBLANES, D_MODEL), lambda l, j: (0, 0)),
                  pl.BlockSpec((1, D_MODEL, tn), lambda l, j: (l, 0, j)),
                  pl.BlockSpec((1, 1, tn), lambda l, j: (l, 0, j))],
        out_specs=pl.BlockSpec((1, SUBLANES, tn), lambda l, j: (l, 0, j)),
        out_shape=jax.ShapeDtypeStruct((DEPTH, SUBLANES, 3 * D_MODEL), F32),
        compiler_params=_params(("arbitrary", "arbitrary")),
        name="modulation",
    )(cvec, ada_w, ada_b.reshape(DEPTH, 1, 3 * D_MODEL))


D_MAIN = 4 * RET_WIDTH + 4 * HY_WIDTH
D_LRU = 2 * LRU_WIDTH


def _in_kernel(x_ref, sh_ref, sc_ref, g_ref, w_ref, o_ref, ol_ref):
    _, nseg, rows, _ = x_ref.shape
    x = x_ref[0].reshape(nseg * rows, D_MODEL)
    ms = jnp.mean(x * x, axis=-1, keepdims=True)
    y = x * lax.rsqrt(ms + EPS) * g_ref[...]
    h = (y * (1.0 + sc_ref[0]) + sh_ref[0]).astype(BF16)
    tn = 1024
    for n in range(D_MAIN // tn):
        z = jnp.dot(h, w_ref[:, n * tn:(n + 1) * tn], preferred_element_type=F32)
        o_ref[0, :, :, n * tn:(n + 1) * tn] = z.reshape(nseg, rows, tn)
    zl = jnp.dot(h, w_ref[:, D_MAIN:D_IN], preferred_element_type=F32)
    for j in range(nseg):
        ol_ref[0, :, j, :] = zl[j * rows:(j + 1) * rows]


def _in_proj(x, shift, scale, g, w_bf16, layer):
    B, L, _ = x.shape
    S = L // SUBLANES
    rows = min(ROW_TILE // SUBLANES, S)
    per_batch = shift.shape[0] > 1
    mod_map = (lambda b, i: (b, 0, 0)) if per_batch else (lambda b, i: (0, 0, 0))
    seg = lambda C: pl.BlockSpec((1, SUBLANES, rows, C), lambda b, i: (b, 0, i, 0))
    z, z_lru = pl.pallas_call(
        _in_kernel,
        grid=(B, S // rows),
        in_specs=[seg(D_MODEL),
                  pl.BlockSpec((1, 1, D_MODEL), mod_map),
                  pl.BlockSpec((1, 1, D_MODEL), mod_map),
                  pl.BlockSpec((1, D_MODEL), lambda b, i: (0, 0)),
                  pl.BlockSpec((None, D_MODEL, D_IN), lambda b, i: (layer, 0, 0),
                               pipeline_mode=pl.Buffered(1))],
        out_specs=[seg(D_MAIN), pl.BlockSpec((1, rows, SUBLANES, D_LRU), lambda b, i: (b, i, 0, 0))],
        out_shape=[jax.ShapeDtypeStruct((B, SUBLANES, S, D_MAIN), F32),
                   jax.ShapeDtypeStruct((B, S, SUBLANES, D_LRU), F32)],
        compiler_params=_params(("arbitrary", "arbitrary")),
        name="in_proj",
    )(x.reshape(B, SUBLANES, S, D_MODEL), shift, scale, g, w_bf16)
    return z.reshape(B, L, D_MAIN), z_lru


OUT_ROWS = 1024


def _out_kernel(yr_ref, yh_ref, yl_ref, x_ref, gate_ref, w_ref, fg_ref, o_ref, *, final):
    nb, nseg, rows, _ = x_ref.shape
    n = nb * nseg * rows
    acc = jnp.dot(yr_ref[...].reshape(n, RET_WIDTH), w_ref[0:RET_WIDTH], preferred_element_type=F32)
    acc = acc + jnp.dot(yh_ref[...].reshape(n, HY_WIDTH), w_ref[RET_WIDTH:RET_WIDTH + HY_WIDTH],
                        preferred_element_type=F32)
    yl = jnp.concatenate([yl_ref[bi, :, j, :] for bi in range(nb) for j in range(nseg)], axis=0)
    acc = acc + jnp.dot(yl.astype(BF16), w_ref[RET_WIDTH + HY_WIDTH:D_MIX], preferred_element_type=F32)
    x = x_ref[...].reshape(n, D_MODEL) + gate_ref[0] * acc
    if final:
        ms = jnp.mean(x * x, axis=-1, keepdims=True)
        x = x * lax.rsqrt(ms + EPS) * fg_ref[...]
    o_ref[...] = x.reshape(nb, nseg, rows, D_MODEL)


def _out_proj(y_ret, y_hy, y_lru, x, gate, w_bf16, layer, final_g, final):
    B, L, _ = x.shape
    S = L // SUBLANES
    rows = min(OUT_ROWS // SUBLANES, S)
    per_batch = gate.shape[0] > 1
    mod_map = (lambda b, i: (b, 0, 0)) if per_batch else (lambda b, i: (0, 0, 0))
    NB = 1 if per_batch else min(B, max(1, OUT_ROWS // (SUBLANES * rows)))
    assert B % NB == 0
    seg = lambda C: pl.BlockSpec((NB, SUBLANES, rows, C), lambda b, i: (b, 0, i, 0))
    view = lambda a: a.reshape(B, SUBLANES, S, a.shape[-1])
    out = pl.pallas_call(
        functools.partial(_out_kernel, final=final),
        grid=(B // NB, S // rows),
        in_specs=[seg(RET_WIDTH), seg(HY_WIDTH),
                  pl.BlockSpec((NB, rows, SUBLANES, LRU_WIDTH), lambda b, i: (b, i, 0, 0)),
                  seg(D_MODEL),
                  pl.BlockSpec((1, 1, D_MODEL), mod_map),
                  pl.BlockSpec((None, D_MIX, D_MODEL), lambda b, i: (layer, 0, 0),
                               pipeline_mode=pl.Buffered(1)),
                  pl.BlockSpec((1, D_MODEL), lambda b, i: (0, 0))],
        out_specs=seg(D_MODEL),
        out_shape=jax.ShapeDtypeStruct((B, SUBLANES, S, D_MODEL), F32),
        compiler_params=_params(("arbitrary", "arbitrary")),
        name="out_proj",
    )(view(y_ret), view(y_hy), y_lru, view(x), gate, w_bf16, final_g)
    return out.reshape(B, L, D_MODEL)


RET_HEAD_ROWS = 2 * 2048
RET_BLOCK = 256


def _ret_kernel(*refs, L, use_rope, has_state, emit_state, chained, layer):
    it = iter(refs)
    q_ref, k_ref, v_ref, g_ref = next(it), next(it), next(it), next(it)
    cos_ref = sin_ref = s0_ref = sN_ref = None
    if use_rope:
        cos_ref, sin_ref = next(it), next(it)
    lg_ref = next(it)
    if has_state:
        s0_ref = next(it)
    if chained:
        next(it)
    y_ref = next(it)
    if emit_state:
        sN_ref = next(it)
    R_ref, M_ref, Z_ref, D_ref, KV_ref, kb_ref = (next(it) for _ in range(6))

    C = min(RET_BLOCK, L)
    DH = RET_HEAD_DIM
    HP = q_ref.shape[-1] // DH
    n = L // C
    cross = has_state or n > 1
    kscale = DH ** -0.5

    @pl.when(pl.program_id(1) == 0)
    def _():
        ii = lax.broadcasted_iota(jnp.int32, (C, C), 0)
        jj = lax.broadcasted_iota(jnp.int32, (C, C), 1)
        diff = (ii - jj).astype(F32)
        ri = lax.broadcasted_iota(jnp.int32, (C, DH), 0).astype(F32)
        for hh in range(HP):
            lgf = -_softplus(-lg_ref[0, hh])
            lgb = -_softplus(-lg_ref[1, hh])
            M_ref[hh] = (jnp.where(diff >= 0, jnp.exp(lgf[:, :C] * jnp.maximum(diff, 0.0)), 0.0)
                         + jnp.where(diff <= 0, jnp.exp(lgb[:, :C] * jnp.maximum(-diff, 0.0)), 0.0))
            lf, lb = lgf[:, :DH], lgb[:, :DH]
            Z_ref[hh, 0] = jnp.concatenate([jnp.exp(lf * (C - 1.0 - ri)), jnp.exp(lb * ri)], axis=1)
            Z_ref[hh, 1] = jnp.concatenate([jnp.exp(lf * (ri + 1.0)), jnp.exp(lb * (C - ri))], axis=1)
            D_ref[hh] = jnp.exp(jnp.concatenate([jnp.broadcast_to(lf * C, (DH, DH)),
                                                 jnp.broadcast_to(lb * C, (DH, DH))], axis=0))

    def rotary(x, r0):
        if not use_rope:
            return x
        return x * cos_ref[pl.ds(r0, C), :] + pltpu.roll(x, 64, axis=1) * sin_ref[pl.ds(r0, C), :]

    def chunk_kv(c, carry):
        r0 = pl.multiple_of(c * C, C)
        for hh in range(HP):
            lanes = slice(hh * DH, (hh + 1) * DH)
            k = rotary(k_ref[0, pl.ds(r0, C), lanes] * kscale, r0)
            kb_ref[hh, pl.ds(r0, C), :] = k.astype(BF16)
            kz = (jnp.concatenate([k, k], axis=1) * Z_ref[hh, 0]).astype(BF16)
            vb = v_ref[0, pl.ds(r0, C), lanes].astype(BF16)
            KV_ref[hh, c] = lax.dot_general(kz, vb, (((0,), (0,)), ((), ())), preferred_element_type=F32)
        return carry

    lax.fori_loop(0, n, chunk_kv, 0, unroll=4 if n % 4 == 0 else (2 if n % 2 == 0 else 1))

    for d in (0, 1):
        for hh in range(HP):
            R_ref[d, hh] = s0_ref[0, d, hh] if has_state else jnp.zeros((DH, DH), F32)

    def states(t, carry):
        for d in (0, 1):
            c = t if d == 0 else n - 1 - t
            rows = slice(d * DH, (d + 1) * DH)
            for hh in range(HP):
                R = R_ref[d, hh]
                R_ref[d, hh] = D_ref[hh, rows, :] * R + KV_ref[hh, c, rows, :]
                KV_ref[hh, c, rows, :] = R
        return carry

    lax.fori_loop(0, n, states, 0)
    if emit_state:
        for l in range(sN_ref.shape[1]):
            if chained or l == layer:
                for d in (0, 1):
                    for hh in range(HP):
                        sN_ref[0, l, d, hh] = R_ref[d, hh]
            else:
                sN_ref[0, l] = jnp.zeros(sN_ref.shape[2:], F32)

    def chunk_out(c, carry):
        r0 = pl.multiple_of(c * C, C)
        cols = []
        for hh in range(HP):
            lanes = slice(hh * DH, (hh + 1) * DH)
            q = rotary(q_ref[0, pl.ds(r0, C), lanes], r0)
            vb = v_ref[0, pl.ds(r0, C), lanes].astype(BF16)
            s = lax.dot_general(q.astype(BF16), kb_ref[hh, pl.ds(r0, C), :], (((1,), (1,)), ((), ())),
                                preferred_element_type=F32) * M_ref[hh]
            o = jnp.dot(s.astype(BF16), vb, preferred_element_type=F32)
            if cross:
                qx = (jnp.concatenate([q, q], axis=1) * Z_ref[hh, 1]).astype(BF16)
                o = o + jnp.dot(qx, KV_ref[hh, c].astype(BF16), preferred_element_type=F32)
            cols.append(o * lax.rsqrt(jnp.mean(o * o, axis=-1, keepdims=True) + EPS))
        g = g_ref[0, pl.ds(r0, C), :]
        y_ref[0, pl.ds(r0, C), :] = (jnp.concatenate(cols, axis=1) * _silu(g)).astype(BF16)
        return carry

    lax.fori_loop(0, n, chunk_out, 0, unroll=4 if n % 4 == 0 else (2 if n % 2 == 0 else 1))


def _retention(z, decay_logit, rope, state0, layer, states):
    B, L, _ = z.shape
    use_rope = rope is not None
    has_state = state0 is not None
    emit_state = states is not None
    chained = emit_state and not isinstance(states, str)
    H = RET_HEADS
    HP = min(H, max(1, RET_HEAD_ROWS // L))
    W = HP * RET_HEAD_DIM
    col = lambda off: pl.BlockSpec((1, L, W), lambda h, b, off=off: (b, 0, off * LANES // W + h))
    in_specs = [col(_Q0), col(_K0), col(_V0), col(_GR0)]
    args = [z, z, z, z]
    if use_rope:
        in_specs += [pl.BlockSpec((L, LANES), lambda h, b: (0, 0))] * 2
        args += list(rope)
    C = min(RET_BLOCK, L)
    in_specs.append(pl.BlockSpec((2, HP, 1, RET_BLOCK), lambda h, b: (0, h, 0, 0)))
    args.append(jnp.broadcast_to(decay_logit[:, :, None, None], (2, H, 1, RET_BLOCK)))
    if has_state:
        in_specs.append(pl.BlockSpec((1, None, 2, HP, RET_HEAD_DIM, RET_HEAD_DIM),
                                     lambda h, b: (b, layer, 0, h, 0, 0)))
        args.append(state0)
    out_specs = [pl.BlockSpec((1, L, W), lambda h, b: (b, 0, h))]
    out_shape = [jax.ShapeDtypeStruct((B, L, RET_WIDTH), BF16)]
    aliases = {}
    if emit_state:
        nl, l0 = (1, layer) if chained else (DEPTH, 0)
        out_specs.append(pl.BlockSpec((1, nl, 2, HP, RET_HEAD_DIM, RET_HEAD_DIM),
                                      lambda h, b: (b, l0, 0, h, 0, 0)))
        out_shape.append(jax.ShapeDtypeStruct((B, DEPTH, 2, H, RET_HEAD_DIM, RET_HEAD_DIM), F32))
        if chained:
            aliases = {len(args): 1}
            in_specs.append(pl.BlockSpec(memory_space=pl.ANY))
            args.append(states)
    return pl.pallas_call(
        functools.partial(_ret_kernel, L=L, use_rope=use_rope, has_state=has_state,
                          emit_state=emit_state, chained=chained, layer=layer),
        grid=(H // HP, B),
        in_specs=in_specs,
        out_specs=out_specs,
        out_shape=out_shape,
        input_output_aliases=aliases,
        scratch_shapes=[pltpu.VMEM((2, HP, RET_HEAD_DIM, RET_HEAD_DIM), F32),
                        pltpu.VMEM((HP, C, C), F32),
                        pltpu.VMEM((HP, 2, C, 2 * RET_HEAD_DIM), F32),
                        pltpu.VMEM((HP, 2 * RET_HEAD_DIM, RET_HEAD_DIM), F32),
                        pltpu.VMEM((HP, L // C, 2 * RET_HEAD_DIM, RET_HEAD_DIM), F32),
                        pltpu.VMEM((HP, L, RET_HEAD_DIM), BF16)],
        compiler_params=_params(("arbitrary", "arbitrary")),
        name="retention",
    )(*args)


def _dft_matrices(T):
    N = 2 * T
    k = np.arange(T, dtype=np.float64)[:, None]
    m = np.arange(T, dtype=np.float64)[None, :]
    ang = 2.0 * np.pi * k * m / N
    fwd = np.concatenate([np.cos(ang), -np.sin(ang)], axis=0)
    fwd[T, :] = (-1.0) ** np.arange(T)
    ck = np.full((T,), 2.0)
    ck[0] = 1.0
    inv_re = (np.cos(ang) * ck[:, None]).T / N
    inv_im = (-2.0 * np.sin(ang)).T / N
    inv_im[:, 0] = ((-1.0) ** np.arange(T)) / N
    inv = np.concatenate([inv_re, inv_im], axis=1)
    return jnp.asarray(fwd, F32).astype(BF16), jnp.asarray(inv, F32).astype(BF16)


def _split_bf16(x):
    hi = x.astype(BF16)
    return hi, (x - hi.astype(F32)).astype(BF16)


def _dot1(a, b):
    return jnp.dot(a, b.astype(BF16), preferred_element_type=F32)


def _filt_kernel(zemb_ref, w1_ref, b1_ref, w2_ref, b2_ref, fr_ref, w3f_ref, w3b_ref, dec_ref,
                 bias_ref, fwd_ref, kre_ref, kim_ref, hid_ref, hn_ref, F_ref, *, L):
    T = fwd_ref.shape[1]
    nb = L // T

    @pl.when((pl.program_id(0) == 0) & (pl.program_id(1) == 0))
    def _():
        z1 = jnp.dot(zemb_ref[...], w1_ref[...], precision=HI, preferred_element_type=F32) + b1_ref[...]
        h1 = jnp.sin(fr_ref[0:1, :] * z1)
        z2 = jnp.dot(h1, w2_ref[...], precision=HI, preferred_element_type=F32) + b2_ref[...]
        hid = jnp.sin(fr_ref[1:2, :] * z2)
        hid_ref[0], hid_ref[1] = _split_bf16(hid)

    w3 = jnp.concatenate([w3f_ref[...], w3b_ref[...]], axis=1)
    w_hi, w_lo = _split_bf16(w3)
    h = (jnp.dot(hid_ref[0], w_hi, preferred_element_type=F32)
         + jnp.dot(hid_ref[1], w_hi, preferred_element_type=F32)
         + jnp.dot(hid_ref[0], w_lo, preferred_element_type=F32))
    h = h * jnp.concatenate([dec_ref[...]] * 2, axis=1)
    h = h / (jnp.sum(jnp.abs(h), axis=0, keepdims=True) + EPS)
    for dr in (0, 1):
        hn_ref[dr] = h[:, dr * LANES:(dr + 1) * LANES]
    for blk in range(nb):
        F = _dot1(fwd_ref[...], h[blk * T:(blk + 1) * T])
        for dr in (0, 1):
            F_ref[dr, blk] = F[:, dr * LANES:(dr + 1) * LANES]

    row = lax.broadcasted_iota(jnp.int32, (T, LANES), 0)
    sgn = jnp.where((row & 1) == 0, 1.0, -1.0).astype(F32)
    row0 = row == 0
    bias = bias_ref[0]
    for d in range(-(nb - 1), nb):
        re = jnp.zeros((T, LANES), F32)
        im = jnp.zeros((T, LANES), F32)
        nyq = jnp.zeros((1, LANES), F32)
        if d >= 0:
            F = F_ref[0, d]
            re, im, nyq = re + F[:T], im + F[T:], nyq + F[T:T + 1]
        if d >= 1:
            F = F_ref[0, d - 1]
            p0 = hn_ref[0, (d - 1) * T:(d - 1) * T + 1, :]
            re, im, nyq = re + sgn * (F[:T] - p0), im + sgn * F[T:], nyq + (F[T:T + 1] - p0)
        e = -d
        if e >= 0:
            F = F_ref[1, e]
            re, im, nyq = re + F[:T], im - F[T:], nyq + F[T:T + 1]
        if e >= 1:
            F = F_ref[1, e - 1]
            p0 = hn_ref[1, (e - 1) * T:(e - 1) * T + 1, :]
            re, im, nyq = re + sgn * (F[:T] - p0), im - sgn * F[T:], nyq + (F[T:T + 1] - p0)
        if d == 0:
            re, nyq = re + bias, nyq + bias
        kre_ref[0, d + nb - 1] = re
        kim_ref[0, d + nb - 1] = jnp.where(row0, nyq, im)


def _hyena_filter_spectra(L, w1, b1, w2, b2, w3, freq, bias, fwd):
    T = fwd.shape[1]
    nb = L // T
    nlag = 2 * nb - 1
    P = LANES
    t = np.linspace(0.0, 1.0, L)[:, None]
    n_bands = (HY_EMB - 1) // 2
    f = np.linspace(1e-4, n_bands - 1, n_bands)
    ang = (2.0 * math.pi / L) * np.arange(L)[:, None] * f[None, :]
    zemb = np.concatenate([t, np.cos(ang), -np.sin(ang)], axis=-1)
    zemb = jnp.asarray(np.pad(zemb, ((0, 0), (0, P - HY_EMB))), F32)
    min_decay = math.log(HY_TARGET) / HY_LONG_DECAY_PCT
    max_decay = math.log(HY_TARGET) / HY_SHORT_DECAY_PCT
    deltas = np.abs(np.linspace(min_decay, max_decay, HY_WIDTH))
    dec = jnp.asarray(np.exp(-t * deltas[None, :]), F32)
    pad = P - HY_FFN
    w1p = jnp.pad(w1, ((0, P - HY_EMB), (0, pad)))
    b1p = jnp.pad(b1, (0, pad)).reshape(1, P)
    w2p = jnp.pad(w2, ((0, pad), (0, pad)))
    b2p = jnp.pad(b2, (0, pad)).reshape(1, P)
    w3p = jnp.pad(w3, ((0, pad), (0, 0)))
    frp = jnp.pad(freq, ((0, 0), (0, pad)))
    ncg = HY_WIDTH // P
    const = lambda shape: pl.BlockSpec(shape, lambda o, c: tuple(0 for _ in shape))
    kshape = jax.ShapeDtypeStruct((HY_ORDER, nlag, T, HY_WIDTH), F32)
    kspec = pl.BlockSpec((1, nlag, T, P), lambda o, c: (o, 0, 0, c))
    return pl.pallas_call(
        functools.partial(_filt_kernel, L=L),
        grid=(HY_ORDER, ncg),
        in_specs=[const((L, P)), const((P, P)), const((1, P)), const((P, P)), const((1, P)), const((2, P)),
                  pl.BlockSpec((P, P), lambda o, c: (0, o * 2 * ncg + c)),
                  pl.BlockSpec((P, P), lambda o, c: (0, o * 2 * ncg + ncg + c)),
                  pl.BlockSpec((L, P), lambda o, c: (0, c)),
                  pl.BlockSpec((1, 1, P), lambda o, c: (o, 0, c)),
                  const((2 * T, T))],
        out_specs=[kspec, kspec],
        out_shape=[kshape, kshape],
        scratch_shapes=[pltpu.VMEM((2, L, P), BF16), pltpu.VMEM((2, L, P), F32),
                        pltpu.VMEM((2, nb, 2 * T, P), F32)],
        compiler_params=_params(("arbitrary", "arbitrary")),
        name="hyena_filters",
    )(zemb, w1p, b1p, w2p, b2p, frp, w3p, w3p, dec, bias.reshape(HY_ORDER, 1, HY_WIDTH), fwd)


CONV_ROWS = 128


def _conv_block(x_ref, t0, w, b, left):
    L, C = x_ref.shape
    T = CONV_ROWS
    blk = x_ref[t0:t0 + T, :]
    prev = x_ref[t0 - SUBLANES:t0, :] if t0 > 0 else jnp.zeros((SUBLANES, C), F32)
    nxt = x_ref[t0 + T:t0 + T + SUBLANES, :] if t0 + T < L else jnp.zeros((SUBLANES, C), F32)
    ext = jnp.concatenate([prev, blk, nxt], axis=0)
    n = T + 2 * SUBLANES
    acc = jnp.broadcast_to(b, (T, C))
    for tap in range(w.shape[0]):
        s = tap - left
        sh = blk if s == 0 else pltpu.roll(ext, (-s) % n, axis=0)[SUBLANES:SUBLANES + T]
        acc = acc + w[tap:tap + 1, :] * sh
    return acc


HY_LANES_ROWS = 2 * 2048
HY_MAX_BATCH = 16


def _toeplitz(k, u, mul):
    n = len(u)
    if n == 1:
        return [mul(k[0], u[0])]
    h = n // 2
    add = lambda a, b: tuple(x + y for x, y in zip(a, b))
    sub = lambda a, b: tuple(x - y for x, y in zip(a, b))
    t0, t1, t2 = k[h:h + n - 1], k[0:n - 1], k[n:2 * n - 1]
    p = _toeplitz(t0, [add(a, b) for a, b in zip(u[:h], u[h:])], mul)
    q = _toeplitz([sub(a, b) for a, b in zip(t1, t0)], u[h:], mul)
    r = _toeplitz([sub(a, b) for a, b in zip(t2, t0)], u[:h], mul)
    return [add(a, b) for a, b in zip(p, q)] + [add(a, b) for a, b in zip(p, r)]


def _hy_kernel(v_ref, x1_ref, x2_ref, g_ref, wv_ref, w1_ref, w2_ref, bv_ref, b1_ref, b2_ref,
               kre_ref, kim_ref, fwd_ref, inv_ref, y_ref,
               x1c_ref, x2c_ref, u_ref, U_ref, Y_ref, *, L):
    T = fwd_ref.shape[1]
    P = LANES
    nb = L // T
    NBT = v_ref.shape[0]
    left = (HY_SHORT - 1) // 2
    def block_spectrum(j, r0):
        U_ref[j] = jnp.dot(fwd_ref[...], u_ref[pl.ds(r0, T), :], preferred_element_type=F32)

    for j in range(nb):
        for t0 in range(j * T, (j + 1) * T, CONV_ROWS):
            rows = slice(t0, t0 + CONV_ROWS)
            for p in range(NBT):
                u_ref[rows, p * P:(p + 1) * P] = _conv_block(v_ref.at[p], t0, wv_ref[...], bv_ref[...], left).astype(BF16)
                x1c_ref[p, rows, :] = _conv_block(x1_ref.at[p], t0, w1_ref[...], b1_ref[...], left)
                x2c_ref[p, rows, :] = _conv_block(x2_ref.at[p], t0, w2_ref[...], b2_ref[...], left)
        block_spectrum(j, j * T)

    row0 = lax.broadcasted_iota(jnp.int32, (SUBLANES, NBT * P), 0) == 0
    tile = lambda x: jnp.concatenate([x] * NBT, axis=1)

    def cmul(k, u):
        kr, ki = tile(k[0]), tile(k[1])
        return (kr * u[0] - ki * u[1], kr * u[1] + ki * u[0])

    def rmul(k, u):
        return (tile(k[0]) * u[0], tile(k[1]) * u[1])

    def spectra_products(o):
        def load(r, rows):
            k = [(kre_ref[o, l, pl.ds(r, rows), :], kim_ref[o, l, pl.ds(r, rows), :]) for l in range(2 * nb - 1)]
            u = [(U_ref[j, pl.ds(r, rows), :], U_ref[j, pl.ds(T + r, rows), :]) for j in range(nb)]
            return k, u

        def chunk(rc, carry):
            r = pl.multiple_of(rc * SUBLANES, SUBLANES)
            y = _toeplitz(*load(r, SUBLANES), cmul)
            for i in range(nb):
                Y_ref[i, pl.ds(r, SUBLANES), :] = y[i][0]
                Y_ref[i, pl.ds(T + r, SUBLANES), :] = y[i][1]
            return carry

        lax.fori_loop(0, T // SUBLANES, chunk, 0, unroll=8)
        y = _toeplitz(*load(0, SUBLANES), rmul)
        for i in range(nb):
            Y_ref[i, 0:SUBLANES, :] = jnp.where(row0, y[i][0], Y_ref[i, 0:SUBLANES, :])
            Y_ref[i, T:T + SUBLANES, :] = jnp.where(row0, y[i][1], Y_ref[i, T:T + SUBLANES, :])

    def long_conv_block(i):
        return _dot1(inv_ref[...], Y_ref[i])

    spectra_products(0)

    def order0(i, carry):
        r0 = pl.multiple_of(i * T, T)
        y = long_conv_block(i)
        for p in range(NBT):
            z1 = x1c_ref[p, pl.ds(r0, T), :] * y[:, p * P:(p + 1) * P]
            u_ref[pl.ds(r0, T), p * P:(p + 1) * P] = z1.astype(BF16)
        return carry

    lax.fori_loop(0, nb, order0, 0, unroll=4 if nb % 4 == 0 else 1)
    for j in range(nb):
        block_spectrum(j, j * T)
    spectra_products(1)

    def order1(i, carry):
        r0 = pl.multiple_of(i * T, T)
        y = long_conv_block(i)
        for p in range(NBT):
            out = x2c_ref[p, pl.ds(r0, T), :] * y[:, p * P:(p + 1) * P]
            y_ref[p, pl.ds(r0, T), :] = (out * _silu(g_ref[p, pl.ds(r0, T), :])).astype(BF16)
        return carry

    lax.fori_loop(0, nb, order1, 0, unroll=4 if nb % 4 == 0 else 1)


def _hyena(z, conv_w, conv_b, kre, kim, fwd, inv):
    B, L, _ = z.shape
    NBT = min(B, max(2, HY_LANES_ROWS // L), HY_MAX_BATCH)
    assert B % NBT == 0
    T = fwd.shape[1]
    nb = L // T
    nlag = 2 * nb - 1
    P = LANES
    ncg = HY_WIDTH // P
    col = lambda off: pl.BlockSpec((NBT, L, P), lambda c, b, off=off: (b, 0, off + c))
    wspec = lambda part: pl.BlockSpec((HY_SHORT, P), lambda c, b, part=part: (0, part * ncg + c))
    bspec = lambda part: pl.BlockSpec((1, P), lambda c, b, part=part: (0, part * ncg + c))
    kspec = pl.BlockSpec((HY_ORDER, nlag, T, P), lambda c, b: (0, 0, 0, c))
    fspec = pl.BlockSpec((2 * T, T), lambda c, b: (0, 0))
    ispec = pl.BlockSpec((T, 2 * T), lambda c, b: (0, 0))
    cb = conv_b.reshape(1, 3 * HY_WIDTH)
    return pl.pallas_call(
        functools.partial(_hy_kernel, L=L),
        grid=(ncg, B // NBT),
        in_specs=[col(_HV0), col(_HX10), col(_HX20), col(_GH0),
                  wspec(0), wspec(1), wspec(2), bspec(0), bspec(1), bspec(2),
                  kspec, kspec, fspec, ispec],
        out_specs=pl.BlockSpec((NBT, L, P), lambda c, b: (b, 0, c)),
        out_shape=jax.ShapeDtypeStruct((B, L, HY_WIDTH), BF16),
        scratch_shapes=[pltpu.VMEM((NBT, L, P), F32), pltpu.VMEM((NBT, L, P), F32),
                        pltpu.VMEM((L, NBT * P), BF16),
                        pltpu.VMEM((nb, 2 * T, NBT * P), F32), pltpu.VMEM((nb, 2 * T, NBT * P), F32)],
        compiler_params=_params(("arbitrary", "arbitrary")),
        name="hyena",
    )(z, z, z, z, conv_w, conv_w, conv_w, cb, cb, cb, kre, kim, fwd, inv)


LRU_TILE = 256
LRU_GROUPS = LRU_TILE // LANES
LRU_ROWS = 128
LRU_STEP_ROWS = 1024


def _lru_kernel(*refs, L, has_state):
    for bi in range(refs[0].shape[0]):
        _lru_one(*refs, L=L, has_state=has_state, bi=bi)


def _lru_one(*refs, L, has_state, bi):
    it = iter(refs)
    x_ref, g_ref, cw_ref, cb_ref, wg_ref, gb_ref, lam_ref = (next(it) for _ in range(7))
    h0_ref = next(it) if has_state else None
    y_ref, sN_ref, xp_ref, A_ref, B_ref, H_ref, C_ref = (next(it) for _ in range(7))

    S = L // SUBLANES
    W = LRU_TILE
    G = LRU_GROUPS
    R = LRU_ROWS
    left = LRU_CONV // 2
    pad = left * SUBLANES

    sub = lax.broadcasted_iota(jnp.int32, (SUBLANES, W), 0)
    for k in range(left):
        tail = pltpu.roll(x_ref[bi, L - pad + k * SUBLANES:L - pad + (k + 1) * SUBLANES, :], 1, axis=0)
        xp_ref[k * SUBLANES:(k + 1) * SUBLANES, :] = jnp.where(sub == 0, 0.0, tail)
    for k in range(LRU_CONV - 1 - left):
        head = pltpu.roll(x_ref[bi, k * SUBLANES:(k + 1) * SUBLANES, :], SUBLANES - 1, axis=0)
        xp_ref[pad + L + k * SUBLANES:pad + L + (k + 1) * SUBLANES, :] = jnp.where(sub == SUBLANES - 1, 0.0, head)

    def copy(c, carry):
        r0 = pl.multiple_of(c * R, R)
        xp_ref[pl.ds(pad + r0, R), :] = x_ref[bi, pl.ds(r0, R), :]
        return carry

    lax.fori_loop(0, L // R, copy, 0)

    csp = [(0.5 * LRU_C) * _softplus(-lam_ref[d]) for d in (0, 1)]

    def coeffs(c, carry):
        r0 = pl.multiple_of(c * R, R)
        u = jnp.broadcast_to(cb_ref[...], (R, W))
        for k in range(LRU_CONV):
            u = u + cw_ref[k:k + 1, :] * xp_ref[pl.ds(r0 + k * SUBLANES, R), :]
        half = jnp.dot(u.astype(BF16), wg_ref[0], preferred_element_type=F32) + gb_ref[0]
        hu = 0.5 * u
        for d in (0, 1):
            t_r = jnp.tanh(half[:, (2 * d) * W:(2 * d + 1) * W])
            t_i = jnp.tanh(half[:, (2 * d + 1) * W:(2 * d + 2) * W])
            nl = csp[d] * t_r + csp[d]
            a = jnp.exp(-nl)
            s2 = jnp.tanh(nl) * (a * a + 1.0)
            b = jnp.where(s2 > 0.0, s2 * lax.rsqrt(s2), 0.0) * (t_i * hu + hu)
            for gi in range(G):
                lanes = slice(gi * LANES, (gi + 1) * LANES)
                A_ref[d, gi, pl.ds(r0, R), :] = a[:, lanes]
                B_ref[d, gi, pl.ds(r0, R), :] = b[:, lanes]
        return carry

    lax.fori_loop(0, L // R, coeffs, 0, unroll=4 if (L // R) % 4 == 0 else 2)

    def scan_body(t, carry):
        out = []
        for d in (0, 1):
            r8 = pl.multiple_of((t if d == 0 else S - 1 - t) * SUBLANES, SUBLANES)
            for gi in range(G):
                h, acc = carry[d * G + gi]
                a = A_ref[d, gi, pl.ds(r8, SUBLANES), :]
                h = a * h + B_ref[d, gi, pl.ds(r8, SUBLANES), :]
                acc = acc * a
                H_ref[d, gi, pl.ds(r8, SUBLANES), :] = h
                C_ref[d, gi, pl.ds(r8, SUBLANES), :] = acc
                out.append((h, acc))
        return tuple(out)

    init = tuple((jnp.zeros((SUBLANES, LANES), F32), jnp.ones((SUBLANES, LANES), F32))
                 for _ in range(2 * G))
    lax.fori_loop(0, S, scan_body, init, unroll=8)

    hin = [None] * (2 * G)
    for d in (0, 1):
        last = (S - 1 if d == 0 else 0) * SUBLANES
        for gi in range(G):
            hl = H_ref[d, gi, last:last + SUBLANES, :]
            ac = C_ref[d, gi, last:last + SUBLANES, :]
            if has_state:
                h = h0_ref[bi, d:d + 1, gi * LANES:(gi + 1) * LANES]
            else:
                h = jnp.zeros((1, LANES), F32)
            rows = [None] * SUBLANES
            for j in (range(SUBLANES) if d == 0 else range(SUBLANES - 1, -1, -1)):
                rows[j] = h
                h = hl[j:j + 1, :] + ac[j:j + 1, :] * h
            hin[d * G + gi] = jnp.concatenate([jnp.concatenate(rows, axis=0)] * (R // SUBLANES), axis=0)
            sN_ref[bi, d:d + 1, gi * LANES:(gi + 1) * LANES] = h

    def combine(c, carry):
        r0 = pl.multiple_of(c * R, R)
        cols = []
        for gi in range(G):
            tot = jnp.zeros((R, LANES), F32)
            for d in (0, 1):
                tot = tot + (H_ref[d, gi, pl.ds(r0, R), :] + C_ref[d, gi, pl.ds(r0, R), :] * hin[d * G + gi])
            cols.append(tot)
        y_ref[bi, pl.ds(r0, R), :] = jnp.concatenate(cols, axis=1) * _silu(g_ref[bi, pl.ds(r0, R), :])
        return carry

    lax.fori_loop(0, L // R, combine, 0, unroll=2)


def _rglru(z4, conv_w, conv_b, gate_w, gate_b, lam, h0):
    B, S, _, _ = z4.shape
    L = S * SUBLANES
    has_state = h0 is not None
    W = LRU_TILE
    nh = LRU_WIDTH // W
    bpt = W // LRU_BLOCK_DIM
    gw = gate_w.reshape(2, 2, nh, bpt, LRU_BLOCK_DIM, LRU_BLOCK_DIM)
    eye = jnp.eye(bpt, dtype=F32)
    dense = (0.5 * jnp.einsum('dkhnij,nm->hnidkmj', gw, eye)).reshape(nh, W, 4 * W).astype(BF16)
    gb = 0.5 * gate_b.reshape(2, 2, nh, W).transpose(2, 0, 1, 3).reshape(nh, 1, 4 * W)
    zr = z4.reshape(B, L, D_LRU)
    NB = min(B, max(1, LRU_STEP_ROWS // L))
    assert B % NB == 0
    off = lambda base: pl.BlockSpec((NB, L, W), lambda h, b, base=base: (b, 0, base + h))
    in_specs = [off(0), off(LRU_WIDTH // W),
                pl.BlockSpec((LRU_CONV, W), lambda h, b: (0, h)),
                pl.BlockSpec((1, W), lambda h, b: (0, h)),
                pl.BlockSpec((1, W, 4 * W), lambda h, b: (h, 0, 0)),
                pl.BlockSpec((1, 1, 4 * W), lambda h, b: (h, 0, 0)),
                pl.BlockSpec((2, 1, W), lambda h, b: (0, 0, h))]
    args = [zr, zr, conv_w, conv_b.reshape(1, LRU_WIDTH), dense, gb, lam.reshape(2, 1, LRU_WIDTH)]
    if has_state:
        in_specs.append(pl.BlockSpec((NB, 2, W), lambda h, b: (b, 0, h)))
        args.append(h0)
    y, s = pl.pallas_call(
        functools.partial(_lru_kernel, L=L, has_state=has_state),
        grid=(nh, B // NB),
        in_specs=in_specs,
        out_specs=[pl.BlockSpec((NB, L, W), lambda h, b: (b, 0, h)),
                   pl.BlockSpec((NB, 2, W), lambda h, b: (b, 0, h))],
        out_shape=[jax.ShapeDtypeStruct((B, L, LRU_WIDTH), F32),
                   jax.ShapeDtypeStruct((B, 2, LRU_WIDTH), F32)],
        scratch_shapes=[pltpu.VMEM((L + (LRU_CONV - 1) * SUBLANES, W), F32)]
        + [pltpu.VMEM((2, LRU_GROUPS, L, LANES), F32) for _ in range(4)],
        compiler_params=_params(("arbitrary", "arbitrary")),
        name="rglru",
    )(*args)
    return y.reshape(B, S, SUBLANES, LRU_WIDTH), s


def _rope_tables(L):
    rows = L // GRID_W
    row = np.repeat(np.arange(rows, dtype=np.float64), GRID_W)
    col = np.tile(np.arange(GRID_W, dtype=np.float64), rows)
    n_f = RET_HEAD_DIM // 4
    inv = ROPE_BASE ** (-np.arange(n_f, dtype=np.float64) / n_f)
    ang = np.concatenate([row[:, None] * inv[None], col[:, None] * inv[None]], axis=-1)
    cos, sin = np.cos(ang), np.sin(ang)
    return (jnp.asarray(np.concatenate([cos, cos], axis=-1), F32),
            jnp.asarray(np.concatenate([-sin, sin], axis=-1), F32))


def kernel(x_prompt, x_sample, state_ret, state_lru, c, c_ctx, norm_g, ada_w, ada_b, w_in, ret_decay_logit, hy_conv_w, hy_conv_b, hy_ffn_w1, hy_ffn_b1, hy_ffn_w2, hy_ffn_b2, hy_ffn_w3, hy_freq, hy_bias, lru_conv_w, lru_conv_b, lru_gate_w, lru_gate_b, lru_lambda, w_out, final_g):
    Bp, Lp, _ = x_prompt.shape
    Bs, Ls, _ = x_sample.shape
    assert Bs + 1 <= SUBLANES

    cvec = jnp.zeros((SUBLANES, D_MODEL), F32).at[:Bs].set(c).at[Bs].set(c_ctx)
    mod = _modulation(cvec, ada_w, ada_b)
    rope = _rope_tables(Ls)
    dft = {L: _dft_matrices(min(DFT_MAX, L)) for L in {Lp, Ls}}
    w_in_b = w_in.astype(BF16)
    w_out_b = w_out.astype(BF16)
    fg = final_g.reshape(1, D_MODEL)

    xc, xl = x_prompt, x_sample
    new_state_ret, new_lru = None, []
    for l in range(DEPTH):
        shift, scale, gate = (mod[l, :, i * D_MODEL:(i + 1) * D_MODEL] for i in range(3))
        g = norm_g[l].reshape(1, D_MODEL)
        final = l == DEPTH - 1
        filt = {}
        for L in sorted({Lp, Ls}):
            filt[L] = _hyena_filter_spectra(L, hy_ffn_w1[l], hy_ffn_b1[l], hy_ffn_w2[l], hy_ffn_b2[l],
                                            hy_ffn_w3[l], hy_freq[l], hy_bias[l], dft[L][0])

        def layer(x, sel, rope_t, ret_s0, lru_s0, ret_states):
            sh, sc, gt = (m[sel][:, None, :] for m in (shift, scale, gate))
            z, z_lru = _in_proj(x, sh, sc, g, w_in_b, l)
            y_ret, *s_ret = _retention(z, ret_decay_logit[l], rope_t, ret_s0, l, ret_states)
            kre, kim = filt[x.shape[1]]
            y_hy = _hyena(z, hy_conv_w[l], hy_conv_b[l], kre, kim, *dft[x.shape[1]])
            y_lru, s_lru = _rglru(z_lru, lru_conv_w[l], lru_conv_b[l], lru_gate_w[l], lru_gate_b[l],
                                  lru_lambda[l], lru_s0)
            return _out_proj(y_ret, y_hy, y_lru, x, gt, w_out_b, l, fg, final), s_ret, s_lru

        xc, (new_state_ret,), ls = layer(xc, slice(Bs, Bs + 1), None, None, None,
                                         "new" if l == 0 else new_state_ret)
        new_lru.append(ls)
        xl, _, _ = layer(xl, slice(0, Bs), rope, state_ret, state_lru[:, l], None)

    new_state_lru = jnp.stack(new_lru, axis=1).astype(x_prompt.dtype)
    return (xc, xl, new_state_ret.astype(x_prompt.dtype), new_state_lru)
```
